```python
import math
import jax, jax.numpy as jnp
from jax import lax
import numpy as np

D_MODEL = 1024
BATCH = 16
SEQ = 2048
DEPTH = 1

CHUNK = 64
Q_BLOCK = 128
SB_HEADS = 16
SB_HEAD_DIM = 64
SB_WIDTH = SB_HEADS * SB_HEAD_DIM
CV_WIDTH = D_MODEL
CV_KERNEL = 31
LN_EPS = 1e-5
DEEPNORM_ALPHA = (2.0 * DEPTH) ** 0.25
DEEPNORM_BETA = (8.0 * DEPTH) ** -0.25
SPLIT_SIZES = (SB_WIDTH, SB_WIDTH, SB_WIDTH, SB_WIDTH, CV_WIDTH, CV_WIDTH, CV_WIDTH, D_MODEL, D_MODEL)
IN_WIDTH = sum(SPLIT_SIZES)
SPLIT_POINTS = tuple(int(i) for i in np.cumsum(SPLIT_SIZES)[:-1])

kernel_name = "stickbreak_conformer_gated_hybrid"


def _layer_norm(x, g, b):
    xf = x.astype(jnp.float32)
    mu = jnp.mean(xf, axis=-1, keepdims=True)
    var = jnp.mean(jnp.square(xf - mu), axis=-1, keepdims=True)
    y = (xf - mu) * lax.rsqrt(var + LN_EPS) * g.astype(jnp.float32) + b.astype(jnp.float32)
    return y.astype(x.dtype)


def _stick_breaking_attention(q, k, v):
    seq = q.shape[1]
    scale = 1.0 / math.sqrt(q.shape[-1])
    qf = q.astype(jnp.float32) * scale
    kf = k.astype(jnp.float32)
    vf = v.astype(jnp.float32)
    outs = []
    for start in range(0, seq, Q_BLOCK):
        end = start + Q_BLOCK
        logits = jnp.einsum('bqhd,bkhd->bhqk', qf[:, start:end], kf[:, :end])
        t_idx = start + jnp.arange(Q_BLOCK)[:, None]
        s_idx = jnp.arange(end)[None, :]
        mask = s_idx < t_idx
        log_not_beta = jnp.where(mask, jax.nn.log_sigmoid(-logits), 0.0)
        later = lax.cumsum(log_not_beta, axis=3, reverse=True) - log_not_beta
        weights = jnp.where(mask, jnp.exp(jax.nn.log_sigmoid(logits) + later), 0.0)
        outs.append(jnp.einsum('bhqk,bkhd->bqhd', weights, vf[:, :end]))
    return jnp.concatenate(outs, axis=1).astype(q.dtype)


def _causal_depthwise_conv(u, w, b):
    kw = w.shape[0]
    y = lax.conv_general_dilated(
        u, w[:, None, :].astype(u.dtype), window_strides=(1,), padding=[(kw - 1, 0)],
        dimension_numbers=('NWC', 'WIO', 'NWC'), feature_group_count=u.shape[-1])
    return y + b.astype(u.dtype)


def _hybrid_layer(x, w_in, w_sb_proj, conv_w, conv_b, conv_ln_g, conv_ln_b,
                  w_cv_proj, w_out, ln_g, ln_b):
    bsz, seq, _ = x.shape
    h = x @ w_in
    q, k, v, z_sb, c_val, c_gate, z_cv, g_sb, g_cv = jnp.split(h, SPLIT_POINTS, axis=-1)

    to_heads = lambda t: t.reshape(bsz, seq, SB_HEADS, SB_HEAD_DIM)
    o_sb = _stick_breaking_attention(to_heads(q), to_heads(k), to_heads(v)).reshape(bsz, seq, SB_WIDTH)
    y_sb = (o_sb * jax.nn.silu(z_sb)) @ w_sb_proj

    u = c_val * jax.nn.sigmoid(c_gate)
    u = _causal_depthwise_conv(u, conv_w, conv_b)
    u = jax.nn.silu(_layer_norm(u, conv_ln_g, conv_ln_b))
    y_cv = (u * jax.nn.silu(z_cv)) @ w_cv_proj

    merged = jax.nn.sigmoid(g_sb) * y_sb + jax.nn.sigmoid(g_cv) * y_cv
    return _layer_norm(DEEPNORM_ALPHA * x + merged @ w_out, ln_g, ln_b)


def _fwd_setup_inputs(seed: int = 0) -> dict:
    key = jax.random.key(seed)
    ks = jax.random.split(key, 14)
    n = lambda kk, shape: jax.random.normal(kk, shape, dtype=jnp.float32)
    x = n(ks[0], (BATCH, SEQ, D_MODEL))
    ln_in_g = 1.0 + 0.02 * n(ks[1], (D_MODEL,))
    ln_in_b = 0.02 * n(ks[2], (D_MODEL,))
    w_in = n(ks[3], (DEPTH, D_MODEL, IN_WIDTH)) * D_MODEL ** -0.5
    w_sb_proj = n(ks[4], (DEPTH, SB_WIDTH, D_MODEL)) * (SB_WIDTH ** -0.5) * DEEPNORM_BETA
    conv_w = n(ks[5], (DEPTH, CV_KERNEL, CV_WIDTH)) * CV_KERNEL ** -0.5
    conv_b = 0.02 * n(ks[6], (DEPTH, CV_WIDTH))
    conv_ln_g = 1.0 + 0.02 * n(ks[7], (DEPTH, CV_WIDTH))
    conv_ln_b = 0.02 * n(ks[8], (DEPTH, CV_WIDTH))
    w_cv_proj = n(ks[9], (DEPTH, CV_WIDTH, D_MODEL)) * (CV_WIDTH ** -0.5) * DEEPNORM_BETA
    w_out = n(ks[10], (DEPTH, D_MODEL, D_MODEL)) * (D_MODEL ** -0.5) * DEEPNORM_BETA
    ln_post_g = 1.0 + 0.02 * n(ks[11], (DEPTH, D_MODEL))
    ln_post_b = 0.02 * n(ks[12], (DEPTH, D_MODEL))
    return {"x": x, "ln_in_g": ln_in_g, "ln_in_b": ln_in_b, "w_in": w_in,
            "w_sb_proj": w_sb_proj, "conv_w": conv_w, "conv_b": conv_b,
            "conv_ln_g": conv_ln_g, "conv_ln_b": conv_ln_b, "w_cv_proj": w_cv_proj,
            "w_out": w_out, "ln_post_g": ln_post_g, "ln_post_b": ln_post_b}


def _fwd_reference(x, ln_in_g, ln_in_b, w_in, w_sb_proj, conv_w, conv_b, conv_ln_g, conv_ln_b,
              w_cv_proj, w_out, ln_post_g, ln_post_b):
    h = _layer_norm(x, ln_in_g, ln_in_b)
    for l in range(DEPTH):
        h = _hybrid_layer(h, w_in[l], w_sb_proj[l], conv_w[l], conv_b[l], conv_ln_g[l],
                          conv_ln_b[l], w_cv_proj[l], w_out[l], ln_post_g[l], ln_post_b[l])
    return h


import jax as _jax
import jax.numpy as _jnp

TWIN_FORMAT = 'train_step'
FWD_PARAMS = ['x', 'ln_in_g', 'ln_in_b', 'w_in', 'w_sb_proj', 'conv_w', 'conv_b', 'conv_ln_g', 'conv_ln_b', 'w_cv_proj', 'w_out', 'ln_post_g', 'ln_post_b']
TWIN_WEIGHTS = ['ln_in_g', 'ln_in_b', 'w_in', 'w_sb_proj', 'conv_w', 'conv_b', 'conv_ln_g', 'conv_ln_b', 'w_cv_proj', 'w_out', 'ln_post_g', 'ln_post_b']
TWIN_DIFF_INPUT = 'x'
TWIN_INPUTS = ['x', 'ln_in_g', 'ln_in_b', 'w_in', 'w_sb_proj', 'conv_w', 'conv_b', 'conv_ln_g', 'conv_ln_b', 'w_cv_proj', 'w_out', 'ln_post_g', 'ln_post_b', 'loss_target', 'm_ln_in_g', 'm_ln_in_b', 'm_w_in', 'm_w_sb_proj', 'm_conv_w', 'm_conv_b', 'm_conv_ln_g', 'm_conv_ln_b', 'm_w_cv_proj', 'm_w_out', 'm_ln_post_g', 'm_ln_post_b', 'v_ln_in_g', 'v_ln_in_b', 'v_w_in', 'v_w_sb_proj', 'v_conv_w', 'v_conv_b', 'v_conv_ln_g', 'v_conv_ln_b', 'v_w_cv_proj', 'v_w_out', 'v_ln_post_g', 'v_ln_post_b']
TWIN_OUTPUTS = ['loss', 'grad_x', 'grad_ln_in_g', 'grad_ln_in_b', 'grad_w_in', 'grad_w_sb_proj', 'grad_conv_w', 'grad_conv_b', 'grad_conv_ln_g', 'grad_conv_ln_b', 'grad_w_cv_proj', 'grad_w_out', 'grad_ln_post_g', 'grad_ln_post_b', 'delta_ln_in_g', 'delta_ln_in_b', 'delta_w_in', 'delta_w_sb_proj', 'delta_conv_w', 'delta_conv_b', 'delta_conv_ln_g', 'delta_conv_ln_b', 'delta_w_cv_proj', 'delta_w_out', 'delta_ln_post_g', 'delta_ln_post_b', 'new_m_ln_in_g', 'new_m_ln_in_b', 'new_m_w_in', 'new_m_w_sb_proj', 'new_m_conv_w', 'new_m_conv_b', 'new_m_conv_ln_g', 'new_m_conv_ln_b', 'new_m_w_cv_proj', 'new_m_w_out', 'new_m_ln_post_g', 'new_m_ln_post_b', 'new_v_ln_in_g', 'new_v_ln_in_b', 'new_v_w_in', 'new_v_w_sb_proj', 'new_v_conv_w', 'new_v_conv_b', 'new_v_conv_ln_g', 'new_v_conv_ln_b', 'new_v_w_cv_proj', 'new_v_w_out', 'new_v_ln_post_g', 'new_v_ln_post_b']
TWIN_LEAF_KINDS = {'loss': 'loss', 'grad_x': 'grad_x', 'grad_ln_in_g': 'grad_w', 'grad_ln_in_b': 'grad_w', 'grad_w_in': 'grad_w', 'grad_w_sb_proj': 'grad_w', 'grad_conv_w': 'grad_w', 'grad_conv_b': 'grad_w', 'grad_conv_ln_g': 'grad_w', 'grad_conv_ln_b': 'grad_w', 'grad_w_cv_proj': 'grad_w', 'grad_w_out': 'grad_w', 'grad_ln_post_g': 'grad_w', 'grad_ln_post_b': 'grad_w', 'delta_ln_in_g': 'delta_w', 'delta_ln_in_b': 'delta_w', 'delta_w_in': 'delta_w', 'delta_w_sb_proj': 'delta_w', 'delta_conv_w': 'delta_w', 'delta_conv_b': 'delta_w', 'delta_conv_ln_g': 'delta_w', 'delta_conv_ln_b': 'delta_w', 'delta_w_cv_proj': 'delta_w', 'delta_w_out': 'delta_w', 'delta_ln_post_g': 'delta_w', 'delta_ln_post_b': 'delta_w', 'new_m_ln_in_g': 'new_m', 'new_m_ln_in_b': 'new_m', 'new_m_w_in': 'new_m', 'new_m_w_sb_proj': 'new_m', 'new_m_conv_w': 'new_m', 'new_m_conv_b': 'new_m', 'new_m_conv_ln_g': 'new_m', 'new_m_conv_ln_b': 'new_m', 'new_m_w_cv_proj': 'new_m', 'new_m_w_out': 'new_m', 'new_m_ln_post_g': 'new_m', 'new_m_ln_post_b': 'new_m', 'new_v_ln_in_g': 'new_v', 'new_v_ln_in_b': 'new_v', 'new_v_w_in': 'new_v', 'new_v_w_sb_proj': 'new_v', 'new_v_conv_w': 'new_v', 'new_v_conv_b': 'new_v', 'new_v_conv_ln_g': 'new_v', 'new_v_conv_ln_b': 'new_v', 'new_v_w_cv_proj': 'new_v', 'new_v_w_out': 'new_v', 'new_v_ln_post_g': 'new_v', 'new_v_ln_post_b': 'new_v'}


def _forward(args):
    return _fwd_reference(*[args[k] for k in FWD_PARAMS])


def _output_shape():
    out = _jax.eval_shape(lambda: _forward(_fwd_setup_inputs(0)))
    return out.shape, out.dtype

N_MICROBATCH = 1
ADAM_LR = 0.001
ADAM_B1 = 0.9
ADAM_B2 = 0.999
ADAM_EPS = 1e-08
ADAM_WD = 0.01
ADAM_STEP = 10
PER_EXAMPLE_BATCH_AXIS = {'x': 0, 'loss_target': 0}
SHARED_INPUTS = []
_WEIGHT_DTYPES = {'ln_in_g': _jnp.float32, 'ln_in_b': _jnp.float32, 'w_in': _jnp.float32, 'w_sb_proj': _jnp.float32, 'conv_w': _jnp.float32, 'conv_b': _jnp.float32, 'conv_ln_g': _jnp.float32, 'conv_ln_b': _jnp.float32, 'w_cv_proj': _jnp.float32, 'w_out': _jnp.float32, 'ln_post_g': _jnp.float32, 'ln_post_b': _jnp.float32}
MOMENT_SCALE = {'ln_in_g': 1.054111e+00, 'ln_in_b': 4.414572e-01, 'w_in': 7.918151e-03, 'w_sb_proj': 1.839426e-02, 'conv_w': 1.060121e-02, 'conv_b': 2.212949e-02, 'conv_ln_g': 1.260479e-02, 'conv_ln_b': 1.176393e-02, 'w_cv_proj': 1.731006e-02, 'w_out': 2.526697e-02, 'ln_post_g': 3.201177e+01, 'ln_post_b': 6.953930e-01}


def _to_microbatches(a, axis):
    t = _jnp.moveaxis(a, axis, 0)
    t = t.reshape((N_MICROBATCH, t.shape[0] // N_MICROBATCH) + t.shape[1:])
    return _jnp.moveaxis(t, 1, axis + 1)


def setup_inputs(seed: int = 0) -> dict:
    inp = _fwd_setup_inputs(seed)
    key = _jax.random.fold_in(_jax.random.key(seed), 7919)
    shape, _ = _output_shape()
    out = dict(inp)
    out["loss_target"] = _jax.random.normal(_jax.random.fold_in(key, 0), shape, _jnp.float32)
    for i, name in enumerate(TWIN_WEIGHTS):
        w = inp[name].astype(_jnp.float32)
        if MOMENT_SCALE is None:
            s = _jnp.sqrt(_jnp.mean(_jnp.square(w)) + 1e-30)
        else:
            s = MOMENT_SCALE[name]
        km, kv = _jax.random.split(_jax.random.fold_in(key, i + 1))
        out[name] = w
        out["m_" + name] = s * _jax.random.normal(km, w.shape, _jnp.float32)
        out["v_" + name] = (s * s) * _jax.random.uniform(kv, w.shape, _jnp.float32, 0.5, 1.5)
    if N_MICROBATCH > 1:
        for name, axis in PER_EXAMPLE_BATCH_AXIS.items():
            out[name] = _to_microbatches(out[name], axis)
    return {'x': out['x'], 'ln_in_g': out['ln_in_g'], 'ln_in_b': out['ln_in_b'], 'w_in': out['w_in'], 'w_sb_proj': out['w_sb_proj'], 'conv_w': out['conv_w'], 'conv_b': out['conv_b'], 'conv_ln_g': out['conv_ln_g'], 'conv_ln_b': out['conv_ln_b'], 'w_cv_proj': out['w_cv_proj'], 'w_out': out['w_out'], 'ln_post_g': out['ln_post_g'], 'ln_post_b': out['ln_post_b'], 'loss_target': out['loss_target'], 'm_ln_in_g': out['m_ln_in_g'], 'm_ln_in_b': out['m_ln_in_b'], 'm_w_in': out['m_w_in'], 'm_w_sb_proj': out['m_w_sb_proj'], 'm_conv_w': out['m_conv_w'], 'm_conv_b': out['m_conv_b'], 'm_conv_ln_g': out['m_conv_ln_g'], 'm_conv_ln_b': out['m_conv_ln_b'], 'm_w_cv_proj': out['m_w_cv_proj'], 'm_w_out': out['m_w_out'], 'm_ln_post_g': out['m_ln_post_g'], 'm_ln_post_b': out['m_ln_post_b'], 'v_ln_in_g': out['v_ln_in_g'], 'v_ln_in_b': out['v_ln_in_b'], 'v_w_in': out['v_w_in'], 'v_w_sb_proj': out['v_w_sb_proj'], 'v_conv_w': out['v_conv_w'], 'v_conv_b': out['v_conv_b'], 'v_conv_ln_g': out['v_conv_ln_g'], 'v_conv_ln_b': out['v_conv_ln_b'], 'v_w_cv_proj': out['v_w_cv_proj'], 'v_w_out': out['v_w_out'], 'v_ln_post_g': out['v_ln_post_g'], 'v_ln_post_b': out['v_ln_post_b']}


def _loss(weights, diff, rest, loss_target):
    with _jax.named_scope("forward"):
        args = {**rest, TWIN_DIFF_INPUT: diff, **{k: w.astype(_WEIGHT_DTYPES[k]) for k, w in weights.items()}}
        y = _forward(args)
    with _jax.named_scope("loss_head"):
        err = _jnp.square(y.astype(_jnp.float32) - loss_target)
        return 0.5 * _jnp.sum(_jnp.mean(err, axis=-1)) if err.ndim else 0.5 * err


def _adamw(w, g, m, v):
    m = ADAM_B1 * m + (1.0 - ADAM_B1) * g
    v = ADAM_B2 * v + (1.0 - ADAM_B2) * _jnp.square(g)
    m_hat = m / (1.0 - ADAM_B1 ** ADAM_STEP)
    v_hat = v / (1.0 - ADAM_B2 ** ADAM_STEP)
    delta = -ADAM_LR * (m_hat / (_jnp.sqrt(v_hat) + ADAM_EPS) + ADAM_WD * w)
    return delta, m, v


def reference(x, ln_in_g, ln_in_b, w_in, w_sb_proj, conv_w, conv_b, conv_ln_g, conv_ln_b, w_cv_proj, w_out, ln_post_g, ln_post_b, loss_target, m_ln_in_g, m_ln_in_b, m_w_in, m_w_sb_proj, m_conv_w, m_conv_b, m_conv_ln_g, m_conv_ln_b, m_w_cv_proj, m_w_out, m_ln_post_g, m_ln_post_b, v_ln_in_g, v_ln_in_b, v_w_in, v_w_sb_proj, v_conv_w, v_conv_b, v_conv_ln_g, v_conv_ln_b, v_w_cv_proj, v_w_out, v_ln_post_g, v_ln_post_b):
    given = dict(x=x, ln_in_g=ln_in_g, ln_in_b=ln_in_b, w_in=w_in, w_sb_proj=w_sb_proj, conv_w=conv_w, conv_b=conv_b, conv_ln_g=conv_ln_g, conv_ln_b=conv_ln_b, w_cv_proj=w_cv_proj, w_out=w_out, ln_post_g=ln_post_g, ln_post_b=ln_post_b, loss_target=loss_target, m_ln_in_g=m_ln_in_g, m_ln_in_b=m_ln_in_b, m_w_in=m_w_in, m_w_sb_proj=m_w_sb_proj, m_conv_w=m_conv_w, m_conv_b=m_conv_b, m_conv_ln_g=m_conv_ln_g, m_conv_ln_b=m_conv_ln_b, m_w_cv_proj=m_w_cv_proj, m_w_out=m_w_out, m_ln_post_g=m_ln_post_g, m_ln_post_b=m_ln_post_b, v_ln_in_g=v_ln_in_g, v_ln_in_b=v_ln_in_b, v_w_in=v_w_in, v_w_sb_proj=v_w_sb_proj, v_conv_w=v_conv_w, v_conv_b=v_conv_b, v_conv_ln_g=v_conv_ln_g, v_conv_ln_b=v_conv_ln_b, v_w_cv_proj=v_w_cv_proj, v_w_out=v_w_out, v_ln_post_g=v_ln_post_g, v_ln_post_b=v_ln_post_b)
    weights = {n: given[n] for n in TWIN_WEIGHTS}
    shared = {n: given[n] for n in SHARED_INPUTS}
    per_example = {n: given[n] for n in ['x']}
    grad_fn = _jax.value_and_grad(_loss, argnums=(0, 1))

    def one_microbatch(ex, loss_target):
        ex = dict(ex)
        diff = ex.pop(TWIN_DIFF_INPUT)
        return grad_fn(weights, diff, {**shared, **ex}, loss_target)

    if N_MICROBATCH == 1:
        loss, (grad_w, grad_x) = one_microbatch(per_example, given["loss_target"])
    else:
        def body(carry, xs):
            loss_sum, grad_sum = carry
            l_k, (gw_k, gx_k) = one_microbatch(xs[0], xs[1])
            with _jax.named_scope("update"):
                return (loss_sum + l_k, _jax.tree.map(_jnp.add, grad_sum, gw_k)), gx_k

        init = (_jnp.zeros((), _jnp.float32), _jax.tree.map(_jnp.zeros_like, weights))
        (loss, grad_w), grad_x = _jax.lax.scan(body, init, (per_example, given["loss_target"]))
    with _jax.named_scope("update"):
        delta_w, new_m, new_v = {}, {}, {}
        for n in TWIN_WEIGHTS:
            delta_w[n], new_m[n], new_v[n] = _adamw(weights[n], grad_w[n], given["m_" + n], given["v_" + n])
    return (loss, grad_x, *[grad_w[n] for n in TWIN_WEIGHTS], *[delta_w[n] for n in TWIN_WEIGHTS],
            *[new_m[n] for n in TWIN_WEIGHTS], *[new_v[n] for n in TWIN_WEIGHTS])
```

```python
import functools
import math

import jax
import jax.numpy as jnp
from jax import lax
from jax.experimental import pallas as pl
from jax.experimental.pallas import tpu as pltpu

F32 = jnp.float32
BF16 = jnp.bfloat16
MESH = pl.DeviceIdType.MESH

N_DEV = 8
SB_HEADS = 16
HEAD_DIM = 64
CV_KERNEL = 31
CV_HALO = 32
LN_EPS = 1e-5
DEEPNORM_ALPHA = 2.0 ** 0.25
ADAM_LR, ADAM_B1, ADAM_B2, ADAM_EPS, ADAM_WD, ADAM_STEP = 0.001, 0.9, 0.999, 1e-08, 0.01, 10

ATT_T = 128
ROW_TILE = 512
CONV_TILE = 256
VMEM_LIMIT = 56 * 1024 * 1024


def _params(*sem):
    return pltpu.CompilerParams(dimension_semantics=sem, vmem_limit_bytes=VMEM_LIMIT)


def _sigmoid(x):
    return 1.0 / (1.0 + jnp.exp(-x))


def _silu_and_grad(x):
    s = _sigmoid(x)
    return x * s, s * (1.0 + x * (1.0 - s))


def _ln_stats(x):
    mu = jnp.mean(x, axis=-1, keepdims=True)
    xc = x - mu
    var = jnp.mean(xc * xc, axis=-1, keepdims=True)
    rstd = lax.rsqrt(var + LN_EPS)
    return xc * rstd, rstd


def _ln_bwd(dxhat, xhat, rstd):
    m1 = jnp.mean(dxhat, axis=-1, keepdims=True)
    m2 = jnp.mean(dxhat * xhat, axis=-1, keepdims=True)
    return rstd * (dxhat - m1 - xhat * m2)


def _ln_in_fwd(x2d, g, b):
    m, d = x2d.shape

    def body(x_ref, g_ref, b_ref, h_ref, hb_ref):
        xhat, _ = _ln_stats(x_ref[...])
        y = xhat * g_ref[...] + b_ref[...]
        h_ref[...] = y
        hb_ref[...] = y.astype(BF16)

    row = pl.BlockSpec((ROW_TILE, d), lambda i: (i, 0))
    vec = pl.BlockSpec((1, d), lambda i: (0, 0))
    return pl.pallas_call(
        body, name="ln_in_fwd", grid=(m // ROW_TILE,),
        in_specs=[row, vec, vec], out_specs=[row, row],
        out_shape=[jax.ShapeDtypeStruct((m, d), F32), jax.ShapeDtypeStruct((m, d), BF16)],
        compiler_params=_params("parallel"),
    )(x2d, g, b)


def _ln_in_bwd(x2d, d_pre, dh_mm, g):
    m, d = x2d.shape

    def body(x_ref, dp_ref, dm_ref, g_ref, dx_ref, dg_ref, db_ref):
        @pl.when(pl.program_id(0) == 0)
        def _():
            dg_ref[...] = jnp.zeros_like(dg_ref)
            db_ref[...] = jnp.zeros_like(db_ref)

        xhat, rstd = _ln_stats(x_ref[...])
        dh = DEEPNORM_ALPHA * dp_ref[...] + dm_ref[...]
        dg_ref[...] += jnp.sum(dh * xhat, axis=0, keepdims=True)
        db_ref[...] += jnp.sum(dh, axis=0, keepdims=True)
        dx_ref[...] = _ln_bwd(dh * g_ref[...], xhat, rstd)

    row = pl.BlockSpec((ROW_TILE, d), lambda i: (i, 0))
    vec = pl.BlockSpec((1, d), lambda i: (0, 0))
    return pl.pallas_call(
        body, name="ln_in_bwd", grid=(m // ROW_TILE,),
        in_specs=[row, row, row, vec], out_specs=[row, vec, vec],
        out_shape=[jax.ShapeDtypeStruct((m, d), F32), jax.ShapeDtypeStruct((1, d), F32),
                   jax.ShapeDtypeStruct((1, d), F32)],
        compiler_params=_params("arbitrary"),
    )(x2d, d_pre, dh_mm, g)


def _merge_fwd(h, y_sb, y_cv):
    m, d = y_sb.shape

    def body(gs_ref, gc_ref, ys_ref, yc_ref, out_ref):
        out_ref[...] = (_sigmoid(gs_ref[...]) * ys_ref[...] + _sigmoid(gc_ref[...]) * yc_ref[...]).astype(BF16)

    row = pl.BlockSpec((ROW_TILE, d), lambda i: (i, 0))
    return pl.pallas_call(
        body, name="merge_fwd", grid=(m // ROW_TILE,),
        in_specs=[pl.BlockSpec((ROW_TILE, d), lambda i: (i, 7)), pl.BlockSpec((ROW_TILE, d), lambda i: (i, 8)), row, row],
        out_specs=row, out_shape=jax.ShapeDtypeStruct((m, d), BF16),
        compiler_params=_params("parallel"),
    )(h, h, y_sb, y_cv)


def _merge_bwd(h, y_sb, y_cv, dm):
    m, d = y_sb.shape

    def body(gs_ref, gc_ref, ys_ref, yc_ref, dm_ref, dys_ref, dyc_ref, dgate_ref):
        dmv = dm_ref[...]
        ss = _sigmoid(gs_ref[...])
        sc = _sigmoid(gc_ref[...])
        dys_ref[...] = (ss * dmv).astype(BF16)
        dyc_ref[...] = (sc * dmv).astype(BF16)
        dgate_ref[:, :d] = (dmv * ys_ref[...] * ss * (1.0 - ss)).astype(BF16)
        dgate_ref[:, d:] = (dmv * yc_ref[...] * sc * (1.0 - sc)).astype(BF16)

    row = pl.BlockSpec((ROW_TILE, d), lambda i: (i, 0))
    return pl.pallas_call(
        body, name="merge_bwd", grid=(m // ROW_TILE,),
        in_specs=[pl.BlockSpec((ROW_TILE, d), lambda i: (i, 7)), pl.BlockSpec((ROW_TILE, d), lambda i: (i, 8)), row, row, row],
        out_specs=[row, row, pl.BlockSpec((ROW_TILE, 2 * d), lambda i: (i, 0))],
        out_shape=[jax.ShapeDtypeStruct((m, d), BF16), jax.ShapeDtypeStruct((m, d), BF16),
                   jax.ShapeDtypeStruct((m, 2 * d), BF16)],
        compiler_params=_params("parallel"),
    )(h, h, y_sb, y_cv, dm)


def _post_ln_loss(h0, mo, target, g, b):
    m, d = h0.shape

    def body(h_ref, mo_ref, t_ref, g_ref, b_ref, loss_ref, dp_ref, dpb_ref, dg_ref, db_ref):
        @pl.when(pl.program_id(0) == 0)
        def _():
            loss_ref[...] = jnp.zeros_like(loss_ref)
            dg_ref[...] = jnp.zeros_like(dg_ref)
            db_ref[...] = jnp.zeros_like(db_ref)

        xhat, rstd = _ln_stats(DEEPNORM_ALPHA * h_ref[...] + mo_ref[...])
        err = xhat * g_ref[...] + b_ref[...] - t_ref[...]
        per_row = jnp.mean(err * err, axis=-1, keepdims=True)
        loss_ref[...] += 0.5 * jnp.sum(per_row, axis=0, keepdims=True)
        dy = err * (1.0 / d)
        dg_ref[...] += jnp.sum(dy * xhat, axis=0, keepdims=True)
        db_ref[...] += jnp.sum(dy, axis=0, keepdims=True)
        dp = _ln_bwd(dy * g_ref[...], xhat, rstd)
        dp_ref[...] = dp
        dpb_ref[...] = dp.astype(BF16)

    row = pl.BlockSpec((ROW_TILE, d), lambda i: (i, 0))
    vec = pl.BlockSpec((1, d), lambda i: (0, 0))
    one = pl.BlockSpec((1, 1), lambda i: (0, 0))
    return pl.pallas_call(
        body, name="post_ln_loss", grid=(m // ROW_TILE,),
        in_specs=[row, row, row, vec, vec], out_specs=[one, row, row, vec, vec],
        out_shape=[jax.ShapeDtypeStruct((1, 1), F32), jax.ShapeDtypeStruct((m, d), F32),
                   jax.ShapeDtypeStruct((m, d), BF16), jax.ShapeDtypeStruct((1, d), F32),
                   jax.ShapeDtypeStruct((1, d), F32)],
        compiler_params=_params("arbitrary"),
    )(h0, mo, target, g, b)


_NN = (((1,), (0,)), ((), ()))
_NT = (((1,), (1,)), ((), ()))
_TN = (((0,), (0,)), ((), ()))


def _mm(name, a, b, dims, grid, a_spec, b_spec, out_specs, out_shape, acc_shape, k_axis):
    n_k = 1 if k_axis is None else grid[k_axis]
    n_out = len(out_shape)

    def body(a_ref, b_ref, *rest):
        outs, acc_ref = rest[:n_out], rest[n_out]
        part = lax.dot_general(a_ref[...].astype(BF16), b_ref[...].astype(BF16), dims, preferred_element_type=F32)
        if n_k == 1:
            for o in outs:
                o[...] = part.astype(o.dtype)
            return
        k = pl.program_id(k_axis)

        @pl.when(k == 0)
        def _():
            acc_ref[...] = part

        @pl.when(k > 0)
        def _():
            acc_ref[...] += part

        @pl.when(k == n_k - 1)
        def _():
            for o in outs:
                o[...] = acc_ref[...].astype(o.dtype)

    sem = tuple("arbitrary" if ax == k_axis else "parallel" for ax in range(len(grid)))
    return pl.pallas_call(
        body, name=name, grid=grid, in_specs=[a_spec, b_spec], out_specs=out_specs, out_shape=out_shape,
        scratch_shapes=[pltpu.VMEM(acc_shape, F32)], compiler_params=_params(*sem),
    )(a, b)


def _mm_nn(name, a, b, out_dtype=F32):
    m, k = a.shape
    n = b.shape[1]
    return _mm(name, a, b, _NN, (m // ROW_TILE,), pl.BlockSpec((ROW_TILE, k), lambda i: (i, 0)),
               pl.BlockSpec((k, n), lambda i: (0, 0)), [pl.BlockSpec((ROW_TILE, n), lambda i: (i, 0))],
               [jax.ShapeDtypeStruct((m, n), out_dtype)], (8, 128), None)[0]


def _mm_nt(name, a, b, out_dtype=F32):
    m, k = a.shape
    n = b.shape[0]
    return _mm(name, a, b, _NT, (m // ROW_TILE,), pl.BlockSpec((ROW_TILE, k), lambda i: (i, 0)),
               pl.BlockSpec((n, k), lambda i: (0, 0)), [pl.BlockSpec((ROW_TILE, n), lambda i: (i, 0))],
               [jax.ShapeDtypeStruct((m, n), out_dtype)], (8, 128), None)[0]


def _mm_tn(name, a, b, out_dtype):
    m, k = a.shape
    n = b.shape[1]
    return _mm(name, a, b, _TN, (m // ROW_TILE,), pl.BlockSpec((ROW_TILE, k), lambda i: (i, 0)),
               pl.BlockSpec((ROW_TILE, n), lambda i: (i, 0)), [pl.BlockSpec((k, n), lambda i: (0, 0))],
               [jax.ShapeDtypeStruct((k, n), out_dtype)], (k, n), 0)[0]


def _mm_in_fwd(h0b, w_g):
    m, d = h0b.shape
    ns = w_g.shape[2]
    out = pl.BlockSpec((ROW_TILE, ns), lambda j, i: (i, j))
    return _mm("mm_in_fwd", h0b, w_g, _NN, (N_DEV, m // ROW_TILE), pl.BlockSpec((ROW_TILE, d), lambda j, i: (i, 0)),
               pl.BlockSpec((None, d, ns), lambda j, i: (j, 0, 0)), [out, out],
               [jax.ShapeDtypeStruct((m, N_DEV * ns), F32), jax.ShapeDtypeStruct((m, N_DEV * ns), BF16)],
               (8, 128), None)


def _mm_in_bwd_x(dhb, w_g):
    m = dhb.shape[0]
    _, d, ns = w_g.shape
    return _mm("mm_in_bwd_x", dhb, w_g, _NT, (m // ROW_TILE, N_DEV), pl.BlockSpec((ROW_TILE, ns), lambda i, j: (i, j)),
               pl.BlockSpec((None, d, ns), lambda i, j: (j, 0, 0)), [pl.BlockSpec((ROW_TILE, d), lambda i, j: (i, 0))],
               [jax.ShapeDtypeStruct((m, d), F32)], (ROW_TILE, d), 1)[0]


def _mm_in_bwd_w(h0b, dhb):
    m, d = h0b.shape
    ns = dhb.shape[1] // N_DEV
    return _mm("mm_in_bwd_w", h0b, dhb, _TN, (N_DEV, m // ROW_TILE), pl.BlockSpec((ROW_TILE, d), lambda j, k: (k, 0)),
               pl.BlockSpec((ROW_TILE, ns), lambda j, k: (k, j)), [pl.BlockSpec((None, d, ns), lambda j, k: (j, 0, 0))],
               [jax.ShapeDtypeStruct((N_DEV, d, ns), BF16)], (d, ns), 1)[0]


def _split_hi_lo(x):
    hi = x.astype(BF16)
    return hi, (x - hi.astype(F32)).astype(BF16)


def _dot2(x, u):
    hi, lo = _split_hi_lo(x)
    return (lax.dot_general(hi, u, _NN, preferred_element_type=F32)
            + lax.dot_general(lo, u, _NN, preferred_element_type=F32))


def _log_not_beta(l):
    e = jnp.exp(-jnp.abs(l))
    return jnp.minimum(-l, 0.0) - jnp.log(1.0 + e), e


def _attn_fwd(hb, h, nb, seq):
    m = nb * seq
    d = SB_HEADS * HEAD_DIM
    t = ATT_T
    nq = seq // t
    sec = d // 128
    scale = 1.0 / math.sqrt(HEAD_DIM)

    def body(q_ref, k_ref, v_ref, z_ref, o_ref, a_ref, c_ref):
        qi = pl.program_id(2)
        row = lax.broadcasted_iota(jnp.int32, (t, t), 0)
        col = lax.broadcasted_iota(jnp.int32, (t, t), 1)
        causal = col < row
        u_after = (row > col).astype(BF16)
        lane = lax.broadcasted_iota(jnp.int32, (t, 128), 1)
        carries = jnp.zeros((t, 128), F32)
        outs = []
        for hh in range(2):
            cols = slice(hh * HEAD_DIM, (hh + 1) * HEAD_DIM)
            q = (q_ref[:, cols].astype(F32) * scale).astype(BF16)

            def block(kb, carry, acc, carries, masked):
                ks = k_ref[pl.ds(pl.multiple_of(kb * t, t), t), cols]
                vs = v_ref[pl.ds(pl.multiple_of(kb * t, t), t), cols]
                l = lax.dot_general(q, ks, _NT, preferred_element_type=F32)
                lnb, _ = _log_not_beta(l)
                if masked:
                    lnb = jnp.where(causal, lnb, 0.0)
                later = carry + _dot2(lnb, u_after)
                a = jnp.exp(l + lnb + later)
                if masked:
                    a = jnp.where(causal, a, 0.0)
                acc = acc + lax.dot_general(a.astype(BF16), vs, _NN, preferred_element_type=F32)
                carries = carries + jnp.where(lane == hh * HEAD_DIM + kb, carry, 0.0)
                carry = later[:, 0:1] + lnb[:, 0:1]
                return carry, acc, carries

            carry, acc, carries = block(qi, jnp.zeros((t, 1), F32), jnp.zeros((t, HEAD_DIM), F32), carries, True)

            def step(i, state):
                return block(qi - 1 - i, *state, False)

            carry, acc, carries = lax.fori_loop(0, qi, step, (carry, acc, carries))
            outs.append(acc)
        o = jnp.concatenate(outs, axis=1)
        o_ref[...] = o
        silu, _ = _silu_and_grad(z_ref[...])
        a_ref[...] = (o * silu).astype(BF16)
        c_ref[...] = carries

    qspec = pl.BlockSpec((t, 128), lambda b, hp, qi: (b * nq + qi, hp))
    return pl.pallas_call(
        body, name="attn_fwd", grid=(nb, sec, nq),
        in_specs=[qspec,
                  pl.BlockSpec((seq, 128), lambda b, hp, qi: (b, sec + hp)),
                  pl.BlockSpec((seq, 128), lambda b, hp, qi: (b, 2 * sec + hp)),
                  pl.BlockSpec((t, 128), lambda b, hp, qi: (b * nq + qi, 3 * sec + hp))],
        out_specs=[qspec, qspec, pl.BlockSpec((None, t, 128), lambda b, hp, qi: (hp, b * nq + qi, 0))],
        out_shape=[jax.ShapeDtypeStruct((m, d), F32), jax.ShapeDtypeStruct((m, d), BF16),
                   jax.ShapeDtypeStruct((sec, m, 128), F32)],
        compiler_params=_params("parallel", "parallel", "parallel"),
    )(hb, hb, hb, h)


def _attn_bwd(hb, h, o, carries, da, nb, seq):
    m = nb * seq
    d = SB_HEADS * HEAD_DIM
    t = ATT_T
    nq = seq // t
    sec = d // 128
    scale = 1.0 / math.sqrt(HEAD_DIM)

    def body(q_ref, k_ref, v_ref, z_ref, o_ref, c_ref, da_ref, dq_ref, dk_ref, dv_ref, dz_ref, dk_acc, dv_acc):
        qi = pl.program_id(2)

        @pl.when(qi == 0)
        def _():
            dk_acc[...] = jnp.zeros_like(dk_acc)
            dv_acc[...] = jnp.zeros_like(dv_acc)

        row = lax.broadcasted_iota(jnp.int32, (t, t), 0)
        col = lax.broadcasted_iota(jnp.int32, (t, t), 1)
        causal = col < row
        u_after = (row > col).astype(BF16)
        u_before = (row < col).astype(BF16)
        lane = lax.broadcasted_iota(jnp.int32, (t, 128), 1)
        silu, dsilu = _silu_and_grad(z_ref[...])
        dav = da_ref[...]
        dz_ref[...] = (dav * o_ref[...] * dsilu).astype(BF16)
        do2 = (dav * silu).astype(BF16)
        cvals = c_ref[...]
        dqs = []
        for hh in range(2):
            cols = slice(hh * HEAD_DIM, (hh + 1) * HEAD_DIM)
            q = (q_ref[:, cols].astype(F32) * scale).astype(BF16)
            do = do2[:, cols]

            def block(kb, gcarry, dq, masked):
                rows = pl.ds(pl.multiple_of(kb * t, t), t)
                ks = k_ref[rows, cols]
                vs = v_ref[rows, cols]
                l = lax.dot_general(q, ks, _NT, preferred_element_type=F32)
                lnb, e = _log_not_beta(l)
                if masked:
                    lnb = jnp.where(causal, lnb, 0.0)
                carry = jnp.sum(jnp.where(lane == hh * HEAD_DIM + kb, cvals, 0.0), axis=1, keepdims=True)
                later = carry + _dot2(lnb, u_after)
                a = jnp.exp(l + lnb + later)
                if masked:
                    a = jnp.where(causal, a, 0.0)
                a_b = a.astype(BF16)
                dv_acc[rows, cols] += lax.dot_general(a_b, do, _TN, preferred_element_type=F32)
                g = a * lax.dot_general(do, vs, _NT, preferred_element_type=F32)
                c = gcarry + _dot2(g, u_before)
                r = 1.0 / (1.0 + e)
                beta = jnp.where(l >= 0.0, r, e * r)
                dl = g * (1.0 - beta) - c * beta
                if masked:
                    dl = jnp.where(causal, dl, 0.0)
                dl_b = dl.astype(BF16)
                dq = dq + lax.dot_general(dl_b, ks, _NN, preferred_element_type=F32)
                dk_acc[rows, cols] += lax.dot_general(dl_b, q, _TN, preferred_element_type=F32)
                gcarry = c[:, t - 1:t] + g[:, t - 1:t]
                return gcarry, dq

            def step(kb, state):
                return block(kb, *state, False)

            gcarry, dq = lax.fori_loop(0, qi, step, (jnp.zeros((t, 1), F32), jnp.zeros((t, HEAD_DIM), F32)))
            _, dq = block(qi, gcarry, dq, True)
            dqs.append(dq * scale)
        dq_ref[...] = jnp.concatenate(dqs, axis=1).astype(BF16)

        @pl.when(qi == nq - 1)
        def _():
            dk_ref[...] = dk_acc[...].astype(BF16)
            dv_ref[...] = dv_acc[...].astype(BF16)

    qspec = pl.BlockSpec((t, 128), lambda b, hp, qi: (b * nq + qi, hp))
    kvspec = pl.BlockSpec((seq, 128), lambda b, hp, qi: (b, hp))
    return pl.pallas_call(
        body, name="attn_bwd", grid=(nb, sec, nq),
        in_specs=[qspec,
                  pl.BlockSpec((seq, 128), lambda b, hp, qi: (b, sec + hp)),
                  pl.BlockSpec((seq, 128), lambda b, hp, qi: (b, 2 * sec + hp)),
                  pl.BlockSpec((t, 128), lambda b, hp, qi: (b * nq + qi, 3 * sec + hp)),
                  qspec,
                  pl.BlockSpec((None, t, 128), lambda b, hp, qi: (hp, b * nq + qi, 0)),
                  qspec],
        out_specs=[qspec, kvspec, kvspec, qspec],
        out_shape=[jax.ShapeDtypeStruct((m, d), BF16)] * 4,
        scratch_shapes=[pltpu.VMEM((seq, 128), F32), pltpu.VMEM((seq, 128), F32)],
        compiler_params=_params("parallel", "parallel", "arbitrary"),
    )(hb, hb, hb, h, o, carries, da)


def _conv_rows(pad_ref, w_ref, b_ref, n_rows):
    acc = jnp.broadcast_to(b_ref[...], (n_rows, b_ref.shape[1]))
    for k in range(CV_KERNEL):
        acc = acc + w_ref[k:k + 1, :] * pad_ref[pl.ds(CV_HALO - CV_KERNEL + 1 + k, n_rows), :]
    return acc


def _conv_fwd(h, conv_w, conv_b, ln_g, ln_b, nb, seq):
    m = nb * seq
    d = conv_b.shape[1]
    t = CONV_TILE
    tiles = seq // t
    hpt = t // CV_HALO

    def body(cv_ref, cg_ref, pv_ref, pg_ref, z_ref, w_ref, b_ref, g_ref, bb_ref, a_ref, pad_ref):
        first = pl.program_id(0) % tiles == 0
        pad_ref[0:CV_HALO, :] = jnp.where(first, 0.0, pv_ref[...] * _sigmoid(pg_ref[...]))
        pad_ref[CV_HALO:, :] = cv_ref[...] * _sigmoid(cg_ref[...])
        xhat, _ = _ln_stats(_conv_rows(pad_ref, w_ref, b_ref, t))
        s, _ = _silu_and_grad(xhat * g_ref[...] + bb_ref[...])
        sz, _ = _silu_and_grad(z_ref[...])
        a_ref[...] = (s * sz).astype(BF16)

    def main(c):
        return pl.BlockSpec((t, d), lambda i: (i, c))

    def prev(c):
        return pl.BlockSpec((CV_HALO, d), lambda i: (jnp.maximum(i * hpt - 1, 0), c))

    vec = pl.BlockSpec((1, d), lambda i: (0, 0))
    return pl.pallas_call(
        body, name="conv_fwd", grid=(m // t,),
        in_specs=[main(4), main(5), prev(4), prev(5), main(6),
                  pl.BlockSpec((CV_HALO, d), lambda i: (0, 0)), vec, vec, vec],
        out_specs=pl.BlockSpec((t, d), lambda i: (i, 0)),
        out_shape=jax.ShapeDtypeStruct((m, d), BF16),
        scratch_shapes=[pltpu.VMEM((CV_HALO + t, d), F32)],
        compiler_params=_params("parallel"),
    )(h, h, h, h, h, conv_w, conv_b, ln_g, ln_b)


def _conv_bwd(h, da, conv_w, conv_b, ln_g, ln_b, nb, seq):
    m = nb * seq
    d = conv_b.shape[1]
    t = CONV_TILE
    tiles = seq // t
    hpt = t // CV_HALO
    last_halo = m // CV_HALO - 1

    def body(cv_ref, cg_ref, pv_ref, pg_ref, nv_ref, ng_ref, z_ref, nz_ref, da_ref, nda_ref,
             w_ref, b_ref, g_ref, bb_ref, dh_ref, dw_ref, db_ref, dg_ref, dbb_ref, pad_ref, dc_ref):
        i = pl.program_id(0)

        @pl.when(i == 0)
        def _():
            dw_ref[...] = jnp.zeros_like(dw_ref)
            db_ref[...] = jnp.zeros_like(db_ref)
            dg_ref[...] = jnp.zeros_like(dg_ref)
            dbb_ref[...] = jnp.zeros_like(dbb_ref)

        first = i % tiles == 0
        last = i % tiles == tiles - 1
        cv = cv_ref[...]
        sg = _sigmoid(cg_ref[...])
        pad_ref[0:CV_HALO, :] = jnp.where(first, 0.0, pv_ref[...] * _sigmoid(pg_ref[...]))
        pad_ref[CV_HALO:CV_HALO + t, :] = cv * sg
        pad_ref[CV_HALO + t:, :] = nv_ref[...] * _sigmoid(ng_ref[...])

        def rows_bwd(conv_out, z, dav):
            xhat, rstd = _ln_stats(conv_out)
            s, ds = _silu_and_grad(xhat * g_ref[...] + bb_ref[...])
            sz, dsz = _silu_and_grad(z)
            dn = dav * sz * ds
            return _ln_bwd(dn * g_ref[...], xhat, rstd), dav * s * dsz, dn, xhat

        conv_all = _conv_rows(pad_ref, w_ref, b_ref, t + CV_HALO)
        dc, dz, dn, xhat = rows_bwd(conv_all[:t], z_ref[...], da_ref[...])
        dc_next, _, _, _ = rows_bwd(conv_all[t:], nz_ref[...], nda_ref[...])
        dc_ref[0:t, :] = dc
        dc_ref[t:, :] = jnp.where(last, 0.0, dc_next)
        dg_ref[...] += jnp.sum(dn * xhat, axis=0, keepdims=True)
        dbb_ref[...] += jnp.sum(dn, axis=0, keepdims=True)
        db_ref[...] += jnp.sum(dc, axis=0, keepdims=True)
        du = jnp.zeros((t, d), F32)
        for k in range(CV_KERNEL):
            dw_ref[k:k + 1, :] += jnp.sum(dc * pad_ref[pl.ds(CV_HALO - CV_KERNEL + 1 + k, t), :], axis=0, keepdims=True)
            du = du + w_ref[k:k + 1, :] * dc_ref[pl.ds(CV_KERNEL - 1 - k, t), :]
        dh_ref[:, 0:d] = (du * sg).astype(BF16)
        dh_ref[:, d:2 * d] = (du * cv * sg * (1.0 - sg)).astype(BF16)
        dh_ref[:, 2 * d:] = dz.astype(BF16)

    def main(c):
        return pl.BlockSpec((t, d), lambda i: (i, c))

    def prev(c):
        return pl.BlockSpec((CV_HALO, d), lambda i: (jnp.maximum(i * hpt - 1, 0), c))

    def nxt(c):
        return pl.BlockSpec((CV_HALO, d), lambda i: (jnp.minimum((i + 1) * hpt, last_halo), c))

    vec = pl.BlockSpec((1, d), lambda i: (0, 0))
    taps = pl.BlockSpec((CV_HALO, d), lambda i: (0, 0))
    return pl.pallas_call(
        body, name="conv_bwd", grid=(m // t,),
        in_specs=[main(4), main(5), prev(4), prev(5), nxt(4), nxt(5), main(6), nxt(6), main(0), nxt(0),
                  taps, vec, vec, vec],
        out_specs=[pl.BlockSpec((t, 3 * d), lambda i: (i, 0)), taps, vec, vec, vec],
        out_shape=[jax.ShapeDtypeStruct((m, 3 * d), BF16), jax.ShapeDtypeStruct((CV_HALO, d), F32),
                   jax.ShapeDtypeStruct((1, d), F32), jax.ShapeDtypeStruct((1, d), F32),
                   jax.ShapeDtypeStruct((1, d), F32)],
        scratch_shapes=[pltpu.VMEM((t + 2 * CV_HALO, d), F32), pltpu.VMEM((t + CV_HALO, d), F32)],
        compiler_params=_params("arbitrary"),
    )(h, h, h, h, h, h, h, h, da, da, conv_w, conv_b, ln_g, ln_b)


_HBM = pl.BlockSpec(memory_space=pltpu.HBM)


def _place():
    return lax.axis_index("x"), lax.axis_index("y"), lax.axis_index("c")


def _slot(p):
    return 4 * p[0] + 2 * p[1] + p[2]


def _all_gather(shards):
    n = len(shards)

    def body(*refs):
        ins, outs = refs[:n], refs[n:2 * n]
        send_sems, recv_sems, local_sems = refs[2 * n:]
        x, y, c = _place()
        me, sibling = (x, y, c), (x, y, 1 - c)
        chips = [(1 - x, y), (x, 1 - y), (1 - x, 1 - y)]

        def copy(a, k, block, to, src=None):
            dst = outs[a].at[_slot(block)]
            return pltpu.make_async_remote_copy(
                src_ref=dst if src is None else src, dst_ref=dst, send_sem=send_sems.at[7 * a + k],
                recv_sem=recv_sems.at[7 * a + k], device_id=to, device_id_type=MESH)

        mine = [pltpu.make_async_copy(ins[a], outs[a].at[_slot(me)], local_sems.at[a]) for a in range(n)]
        for cp in mine:
            cp.start()
        sent = []
        for a in range(n):
            sent.append(copy(a, 0, me, sibling, src=ins[a]))
            sent += [copy(a, 1 + j, me, (*chip, c), src=ins[a]) for j, chip in enumerate(chips)]
        for cp in sent:
            cp.start()
        for j, chip in enumerate(chips):
            for a in range(n):
                copy(a, 1 + j, (*chip, c), me).wait_recv()
                passed = copy(a, 4 + j, (*chip, c), sibling)
                passed.start()
                sent.append(passed)
        for a in range(n):
            copy(a, 0, sibling, me).wait_recv()
            for j, chip in enumerate(chips):
                copy(a, 4 + j, (*chip, 1 - c), me).wait_recv()
        for cp in sent:
            cp.wait_send()
        for cp in mine:
            cp.wait()

    return pl.pallas_call(
        body, name="all_gather_weights",
        in_specs=[_HBM] * n, out_specs=[_HBM] * n,
        out_shape=[jax.ShapeDtypeStruct((N_DEV,) + s.shape, s.dtype) for s in shards],
        scratch_shapes=[pltpu.SemaphoreType.DMA((7 * n,)), pltpu.SemaphoreType.DMA((7 * n,)),
                        pltpu.SemaphoreType.DMA((n,))],
    )(*shards)


def _exchange_partials(parts, whole):
    n_parts = len(parts)
    arrays = list(parts) + list(whole)
    n = len(arrays)

    def body(*refs):
        ins, outs = refs[:n], refs[n:2 * n]
        send_sems, recv_sems, local_sems = refs[2 * n:]
        x, y, c = _place()
        me = (x, y, c)

        def src_for(a, p):
            return ins[a].at[_slot(p)] if a < n_parts else ins[a]

        def copy(a, k, peer):
            return pltpu.make_async_remote_copy(
                src_ref=src_for(a, peer), dst_ref=outs[a].at[_slot(me)], send_sem=send_sems.at[7 * a + k],
                recv_sem=recv_sems.at[7 * a + k], device_id=peer, device_id_type=MESH)

        def landed(a, k, peer):
            return pltpu.make_async_remote_copy(
                src_ref=src_for(a, peer), dst_ref=outs[a].at[_slot(peer)], send_sem=send_sems.at[7 * a + k],
                recv_sem=recv_sems.at[7 * a + k], device_id=peer, device_id_type=MESH)

        peers = []
        for k in range(1, N_DEV):
            fx, fy, fc = (k >> 2) & 1, (k >> 1) & 1, k & 1
            peers.append((1 - x if fx else x, 1 - y if fy else y, 1 - c if fc else c))
        mine = [pltpu.make_async_copy(src_for(a, me), outs[a].at[_slot(me)], local_sems.at[a]) for a in range(n)]
        for cp in mine:
            cp.start()
        sent = [copy(a, k, peer) for a in range(n) for k, peer in enumerate(peers)]
        for cp in sent:
            cp.start()
        for a in range(n):
            for k, peer in enumerate(peers):
                landed(a, k, peer).wait_recv()
        for cp in sent:
            cp.wait_send()
        for cp in mine:
            cp.wait()

    out_shape = [jax.ShapeDtypeStruct(p.shape, p.dtype) for p in parts]
    out_shape += [jax.ShapeDtypeStruct((N_DEV,) + w.shape, w.dtype) for w in whole]
    return pl.pallas_call(
        body, name="exchange_grad_partials",
        in_specs=[_HBM] * n, out_specs=[_HBM] * n, out_shape=out_shape,
        scratch_shapes=[pltpu.SemaphoreType.DMA((7 * n,)), pltpu.SemaphoreType.DMA((7 * n,)),
                        pltpu.SemaphoreType.DMA((n,))],
    )(*arrays)


def _adamw(name, parts, w, mom, var, rows):
    r, c = w.shape

    def body(p_ref, w_ref, m_ref, v_ref, g_ref, d_ref, nm_ref, nv_ref):
        g = p_ref[0].astype(F32)
        for p in range(1, N_DEV):
            g = g + p_ref[p].astype(F32)
        m_new = ADAM_B1 * m_ref[...] + (1.0 - ADAM_B1) * g
        v_new = ADAM_B2 * v_ref[...] + (1.0 - ADAM_B2) * (g * g)
        m_hat = m_new / (1.0 - ADAM_B1 ** ADAM_STEP)
        v_hat = v_new / (1.0 - ADAM_B2 ** ADAM_STEP)
        g_ref[...] = g
        d_ref[...] = -ADAM_LR * (m_hat / (jnp.sqrt(v_hat) + ADAM_EPS) + ADAM_WD * w_ref[...])
        nm_ref[...] = m_new
        nv_ref[...] = v_new

    blk = pl.BlockSpec((rows, c), lambda i: (i, 0))
    return pl.pallas_call(
        body, name=name, grid=(r // rows,),
        in_specs=[pl.BlockSpec((N_DEV, rows, c), lambda i: (0, i, 0)), blk, blk, blk],
        out_specs=[blk] * 4, out_shape=[jax.ShapeDtypeStruct((r, c), F32)] * 4,
        compiler_params=_params("parallel"),
    )(parts, w, mom, var)


def kernel(x, ln_in_g, ln_in_b, w_in, w_sb_proj, conv_w, conv_b, conv_ln_g, conv_ln_b, w_cv_proj, w_out, ln_post_g, ln_post_b, loss_target, m_ln_in_g, m_ln_in_b, m_w_in, m_w_sb_proj, m_conv_w, m_conv_b, m_conv_ln_g, m_conv_ln_b, m_w_cv_proj, m_w_out, m_ln_post_g, m_ln_post_b, v_ln_in_g, v_ln_in_b, v_w_in, v_w_sb_proj, v_conv_w, v_conv_b, v_conv_ln_g, v_conv_ln_b, v_w_cv_proj, v_w_out, v_ln_post_g, v_ln_post_b):
    nb, seq, d = x.shape
    m = nb * seq
    x2d = x.reshape(m, d)
    target = loss_target.reshape(m, d)
    rs = d // N_DEV
    pad_taps = ((0, CV_HALO - CV_KERNEL), (0, 0))

    proj_shards = jnp.stack([w_sb_proj[0], w_cv_proj[0], w_out[0]]).astype(BF16)
    conv_w_shard = jnp.pad(conv_w[0], pad_taps)
    w_in_g, proj_g, conv_w_g = _all_gather([w_in[0].astype(BF16), proj_shards, conv_w_shard])
    w_sb_g = proj_g[:, 0].reshape(d, d)
    w_cv_g = proj_g[:, 1].reshape(d, d)
    w_out_g = proj_g[:, 2].reshape(d, d)
    conv_w_full = conv_w_g.transpose(1, 0, 2).reshape(CV_HALO, d)

    h0, h0b = _ln_in_fwd(x2d, ln_in_g.reshape(1, d), ln_in_b.reshape(1, d))
    h, hb = _mm_in_fwd(h0b, w_in_g)
    o, a_sb, carries = _attn_fwd(hb, h, nb, seq)
    a_cv = _conv_fwd(h, conv_w_full, conv_b, conv_ln_g, conv_ln_b, nb, seq)
    y_sb = _mm_nn("mm_sb_fwd", a_sb, w_sb_g)
    y_cv = _mm_nn("mm_cv_fwd", a_cv, w_cv_g)
    merged = _merge_fwd(h, y_sb, y_cv)
    mo = _mm_nn("mm_out_fwd", merged, w_out_g)
    loss_part, d_pre, d_pre_b, dg_post, db_post = _post_ln_loss(h0, mo, target, ln_post_g, ln_post_b)

    dw_out = _mm_tn("mm_out_bwd_w", merged, d_pre_b, BF16)
    d_merged = _mm_nt("mm_out_bwd_x", d_pre_b, w_out_g)
    dy_sb, dy_cv, d_gates = _merge_bwd(h, y_sb, y_cv, d_merged)
    dw_sb = _mm_tn("mm_sb_bwd_w", a_sb, dy_sb, BF16)
    dw_cv = _mm_tn("mm_cv_bwd_w", a_cv, dy_cv, BF16)
    da_sb = _mm_nt("mm_sb_bwd_x", dy_sb, w_sb_g)
    da_cv = _mm_nt("mm_cv_bwd_x", dy_cv, w_cv_g)
    dq, dk, dv, dz_sb = _attn_bwd(hb, h, o, carries, da_sb, nb, seq)
    d_conv3, dconv_w, dconv_b, dconv_ln_g, dconv_ln_b = _conv_bwd(
        h, da_cv, conv_w_full, conv_b, conv_ln_g, conv_ln_b, nb, seq)
    dhb = jnp.concatenate([dq, dk, dv, dz_sb, d_conv3, d_gates], axis=1)
    dw_in = _mm_in_bwd_w(h0b, dhb)
    dh_mm = _mm_in_bwd_x(dhb, w_in_g)
    dx, dg_in, db_in = _ln_in_bwd(x2d, d_pre, dh_mm, ln_in_g.reshape(1, d))

    dproj = jnp.stack([dw_sb.reshape(N_DEV, rs, d), dw_cv.reshape(N_DEV, rs, d), dw_out.reshape(N_DEV, rs, d)], axis=1)
    dconv_w_parts = dconv_w.reshape(CV_HALO, N_DEV, d // N_DEV).transpose(1, 0, 2)
    small = jnp.concatenate([dg_in, db_in, dconv_b, dconv_ln_g, dconv_ln_b, dg_post, db_post,
                             jnp.zeros((1, d), F32)], axis=0)
    r_in, r_proj, r_conv, r_small = _exchange_partials([dw_in, dproj, dconv_w_parts], [small])

    g_in, d_in, nm_in, nv_in = _adamw("adamw_w_in", r_in, w_in[0], m_w_in[0], v_w_in[0], 128)
    stack3 = lambda a, b, c: jnp.concatenate([a[0], b[0], c[0]], axis=0)
    g_pr, d_pr, nm_pr, nv_pr = _adamw(
        "adamw_proj", r_proj.reshape(N_DEV, 3 * rs, d), stack3(w_sb_proj, w_cv_proj, w_out),
        stack3(m_w_sb_proj, m_w_cv_proj, m_w_out), stack3(v_w_sb_proj, v_w_cv_proj, v_w_out), 3 * rs)
    padc = lambda a: jnp.pad(a[0], pad_taps)
    g_cw, d_cw, nm_cw, nv_cw = _adamw("adamw_conv_w", r_conv, padc(conv_w), padc(m_conv_w), padc(v_conv_w), CV_HALO)
    vecs = lambda *a: jnp.concatenate([t.reshape(1, d) for t in a] + [jnp.ones((1, d), F32)], axis=0)
    g_sm, d_sm, nm_sm, nv_sm = _adamw(
        "adamw_vectors", r_small,
        vecs(ln_in_g, ln_in_b, conv_b, conv_ln_g, conv_ln_b, ln_post_g, ln_post_b),
        vecs(m_ln_in_g, m_ln_in_b, m_conv_b, m_conv_ln_g, m_conv_ln_b, m_ln_post_g, m_ln_post_b),
        vecs(v_ln_in_g, v_ln_in_b, v_conv_b, v_conv_ln_g, v_conv_ln_b, v_ln_post_g, v_ln_post_b), 8)

    loss = lax.psum(loss_part[0, 0], ("x", "y", "c"))

    def leaves(big, pr, cw, sm):
        return (sm[0], sm[1], big[None], pr[None, 0:rs], cw[None, :CV_KERNEL], sm[2:3], sm[3:4], sm[4:5],
                pr[None, rs:2 * rs], pr[None, 2 * rs:], sm[5:6], sm[6:7])

    return (loss, dx.reshape(nb, seq, d), *leaves(g_in, g_pr, g_cw, g_sm), *leaves(d_in, d_pr, d_cw, d_sm),
            *leaves(nm_in, nm_pr, nm_cw, nm_sm), *leaves(nv_in, nv_pr, nv_cw, nv_sm))
```

```python
import functools
import math

import jax
import jax.numpy as jnp
from jax import lax
from jax.experimental import pallas as pl
from jax.experimental.pallas import tpu as pltpu

F32 = jnp.float32
BF16 = jnp.bfloat16
MESH = pl.DeviceIdType.MESH

N_DEV = 8
SB_HEADS = 16
HEAD_DIM = 64
CV_KERNEL = 31
CV_HALO = 32
LN_EPS = 1e-5
DEEPNORM_ALPHA = 2.0 ** 0.25
ADAM_LR, ADAM_B1, ADAM_B2, ADAM_EPS, ADAM_WD, ADAM_STEP = 0.001, 0.9, 0.999, 1e-08, 0.01, 10

ATT_T = 256
ROW_TILE = 512
CONV_TILE = 256
VMEM_LIMIT = 56 * 1024 * 1024


def _params(*sem):
    return pltpu.CompilerParams(dimension_semantics=sem, vmem_limit_bytes=VMEM_LIMIT)


def _sigmoid(x):
    return 1.0 / (1.0 + jnp.exp(-x))


def _silu_and_grad(x):
    s = _sigmoid(x)
    return x * s, s * (1.0 + x * (1.0 - s))


def _ln_stats(x):
    mu = jnp.mean(x, axis=-1, keepdims=True)
    xc = x - mu
    var = jnp.mean(xc * xc, axis=-1, keepdims=True)
    rstd = lax.rsqrt(var + LN_EPS)
    return xc * rstd, rstd


def _ln_bwd(dxhat, xhat, rstd):
    m1 = jnp.mean(dxhat, axis=-1, keepdims=True)
    m2 = jnp.mean(dxhat * xhat, axis=-1, keepdims=True)
    return rstd * (dxhat - m1 - xhat * m2)


def _ln_in_fwd(x2d, g, b):
    m, d = x2d.shape

    def body(x_ref, g_ref, b_ref, h_ref, hb_ref):
        xhat, _ = _ln_stats(x_ref[...])
        y = xhat * g_ref[...] + b_ref[...]
        h_ref[...] = y
        hb_ref[...] = y.astype(BF16)

    row = pl.BlockSpec((ROW_TILE, d), lambda i: (i, 0))
    vec = pl.BlockSpec((1, d), lambda i: (0, 0))
    return pl.pallas_call(
        body, name="ln_in_fwd", grid=(m // ROW_TILE,),
        in_specs=[row, vec, vec], out_specs=[row, row],
        out_shape=[jax.ShapeDtypeStruct((m, d), F32), jax.ShapeDtypeStruct((m, d), BF16)],
        compiler_params=_params("parallel"),
    )(x2d, g, b)


def _ln_in_bwd(x2d, d_pre, dh_mm, g):
    m, d = x2d.shape

    def body(x_ref, dp_ref, dm_ref, g_ref, dx_ref, dg_ref, db_ref):
        @pl.when(pl.program_id(0) == 0)
        def _():
            dg_ref[...] = jnp.zeros_like(dg_ref)
            db_ref[...] = jnp.zeros_like(db_ref)

        xhat, rstd = _ln_stats(x_ref[...])
        dh = DEEPNORM_ALPHA * dp_ref[...] + dm_ref[...]
        dg_ref[...] += jnp.sum(dh * xhat, axis=0, keepdims=True)
        db_ref[...] += jnp.sum(dh, axis=0, keepdims=True)
        dx_ref[...] = _ln_bwd(dh * g_ref[...], xhat, rstd)

    row = pl.BlockSpec((ROW_TILE, d), lambda i: (i, 0))
    vec = pl.BlockSpec((1, d), lambda i: (0, 0))
    return pl.pallas_call(
        body, name="ln_in_bwd", grid=(m // ROW_TILE,),
        in_specs=[row, row, row, vec], out_specs=[row, vec, vec],
        out_shape=[jax.ShapeDtypeStruct((m, d), F32), jax.ShapeDtypeStruct((1, d), F32),
                   jax.ShapeDtypeStruct((1, d), F32)],
        compiler_params=_params("arbitrary"),
    )(x2d, d_pre, dh_mm, g)


def _merge_fwd(h, y_sb, y_cv):
    m, d = y_sb.shape

    def body(gs_ref, gc_ref, ys_ref, yc_ref, out_ref):
        out_ref[...] = (_sigmoid(gs_ref[...]) * ys_ref[...] + _sigmoid(gc_ref[...]) * yc_ref[...]).astype(BF16)

    row = pl.BlockSpec((ROW_TILE, d), lambda i: (i, 0))
    return pl.pallas_call(
        body, name="merge_fwd", grid=(m // ROW_TILE,),
        in_specs=[pl.BlockSpec((ROW_TILE, d), lambda i: (i, 7)), pl.BlockSpec((ROW_TILE, d), lambda i: (i, 8)), row, row],
        out_specs=row, out_shape=jax.ShapeDtypeStruct((m, d), BF16),
        compiler_params=_params("parallel"),
    )(h, h, y_sb, y_cv)


def _merge_bwd(h, y_sb, y_cv, dm):
    m, d = y_sb.shape

    def body(gs_ref, gc_ref, ys_ref, yc_ref, dm_ref, dys_ref, dyc_ref, dgate_ref):
        dmv = dm_ref[...]
        ss = _sigmoid(gs_ref[...])
        sc = _sigmoid(gc_ref[...])
        dys_ref[...] = (ss * dmv).astype(BF16)
        dyc_ref[...] = (sc * dmv).astype(BF16)
        dgate_ref[:, :d] = (dmv * ys_ref[...] * ss * (1.0 - ss)).astype(BF16)
        dgate_ref[:, d:] = (dmv * yc_ref[...] * sc * (1.0 - sc)).astype(BF16)

    row = pl.BlockSpec((ROW_TILE, d), lambda i: (i, 0))
    return pl.pallas_call(
        body, name="merge_bwd", grid=(m // ROW_TILE,),
        in_specs=[pl.BlockSpec((ROW_TILE, d), lambda i: (i, 7)), pl.BlockSpec((ROW_TILE, d), lambda i: (i, 8)), row, row, row],
        out_specs=[row, row, pl.BlockSpec((ROW_TILE, 2 * d), lambda i: (i, 0))],
        out_shape=[jax.ShapeDtypeStruct((m, d), BF16), jax.ShapeDtypeStruct((m, d), BF16),
                   jax.ShapeDtypeStruct((m, 2 * d), BF16)],
        compiler_params=_params("parallel"),
    )(h, h, y_sb, y_cv, dm)


def _post_ln_loss(h0, mo, target, g, b):
    m, d = h0.shape

    def body(h_ref, mo_ref, t_ref, g_ref, b_ref, loss_ref, dp_ref, dpb_ref, dg_ref, db_ref):
        @pl.when(pl.program_id(0) == 0)
        def _():
            loss_ref[...] = jnp.zeros_like(loss_ref)
            dg_ref[...] = jnp.zeros_like(dg_ref)
            db_ref[...] = jnp.zeros_like(db_ref)

        xhat, rstd = _ln_stats(DEEPNORM_ALPHA * h_ref[...] + mo_ref[...])
        err = xhat * g_ref[...] + b_ref[...] - t_ref[...]
        per_row = jnp.mean(err * err, axis=-1, keepdims=True)
        loss_ref[...] += 0.5 * jnp.sum(per_row, axis=0, keepdims=True)
        dy = err * (1.0 / d)
        dg_ref[...] += jnp.sum(dy * xhat, axis=0, keepdims=True)
        db_ref[...] += jnp.sum(dy, axis=0, keepdims=True)
        dp = _ln_bwd(dy * g_ref[...], xhat, rstd)
        dp_ref[...] = dp
        dpb_ref[...] = dp.astype(BF16)

    row = pl.BlockSpec((ROW_TILE, d), lambda i: (i, 0))
    vec = pl.BlockSpec((1, d), lambda i: (0, 0))
    one = pl.BlockSpec((1, 1), lambda i: (0, 0))
    return pl.pallas_call(
        body, name="post_ln_loss", grid=(m // ROW_TILE,),
        in_specs=[row, row, row, vec, vec], out_specs=[one, row, row, vec, vec],
        out_shape=[jax.ShapeDtypeStruct((1, 1), F32), jax.ShapeDtypeStruct((m, d), F32),
                   jax.ShapeDtypeStruct((m, d), BF16), jax.ShapeDtypeStruct((1, d), F32),
                   jax.ShapeDtypeStruct((1, d), F32)],
        compiler_params=_params("arbitrary"),
    )(h0, mo, target, g, b)


_NN = (((1,), (0,)), ((), ()))
_NT = (((1,), (1,)), ((), ()))
_TN = (((0,), (0,)), ((), ()))


def _mm(name, a, b, dims, grid, a_spec, b_spec, out_specs, out_shape, acc_shape, k_axis):
    n_k = 1 if k_axis is None else grid[k_axis]
    n_out = len(out_shape)

    def body(a_ref, b_ref, *rest):
        outs, acc_ref = rest[:n_out], rest[n_out]
        part = lax.dot_general(a_ref[...].astype(BF16), b_ref[...].astype(BF16), dims, preferred_element_type=F32)
        if n_k == 1:
            for o in outs:
                o[...] = part.astype(o.dtype)
            return
        k = pl.program_id(k_axis)

        @pl.when(k == 0)
        def _():
            acc_ref[...] = part

        @pl.when(k > 0)
        def _():
            acc_ref[...] += part

        @pl.when(k == n_k - 1)
        def _():
            for o in outs:
                o[...] = acc_ref[...].astype(o.dtype)

    sem = tuple("arbitrary" if ax == k_axis else "parallel" for ax in range(len(grid)))
    return pl.pallas_call(
        body, name=name, grid=grid, in_specs=[a_spec, b_spec], out_specs=out_specs, out_shape=out_shape,
        scratch_shapes=[pltpu.VMEM(acc_shape, F32)], compiler_params=_params(*sem),
    )(a, b)


def _mm_nn(name, a, b, out_dtype=F32):
    m, k = a.shape
    n = b.shape[1]
    return _mm(name, a, b, _NN, (m // ROW_TILE,), pl.BlockSpec((ROW_TILE, k), lambda i: (i, 0)),
               pl.BlockSpec((k, n), lambda i: (0, 0)), [pl.BlockSpec((ROW_TILE, n), lambda i: (i, 0))],
               [jax.ShapeDtypeStruct((m, n), out_dtype)], (8, 128), None)[0]


def _mm_nt(name, a, b, out_dtype=F32):
    m, k = a.shape
    n = b.shape[0]
    return _mm(name, a, b, _NT, (m // ROW_TILE,), pl.BlockSpec((ROW_TILE, k), lambda i: (i, 0)),
               pl.BlockSpec((n, k), lambda i: (0, 0)), [pl.BlockSpec((ROW_TILE, n), lambda i: (i, 0))],
               [jax.ShapeDtypeStruct((m, n), out_dtype)], (8, 128), None)[0]


def _mm_tn(name, a, b, out_dtype):
    m, k = a.shape
    n = b.shape[1]
    return _mm(name, a, b, _TN, (m // ROW_TILE,), pl.BlockSpec((ROW_TILE, k), lambda i: (i, 0)),
               pl.BlockSpec((ROW_TILE, n), lambda i: (i, 0)), [pl.BlockSpec((k, n), lambda i: (0, 0))],
               [jax.ShapeDtypeStruct((k, n), out_dtype)], (k, n), 0)[0]


def _mm_in_fwd(h0b, w_g):
    m, d = h0b.shape
    ns = w_g.shape[2]
    out = pl.BlockSpec((ROW_TILE, ns), lambda j, i: (i, j))
    return _mm("mm_in_fwd", h0b, w_g, _NN, (N_DEV, m // ROW_TILE), pl.BlockSpec((ROW_TILE, d), lambda j, i: (i, 0)),
               pl.BlockSpec((None, d, ns), lambda j, i: (j, 0, 0)), [out, out],
               [jax.ShapeDtypeStruct((m, N_DEV * ns), F32), jax.ShapeDtypeStruct((m, N_DEV * ns), BF16)],
               (8, 128), None)


def _mm_in_bwd_x(dhb, w_g):
    m = dhb.shape[0]
    _, d, ns = w_g.shape
    return _mm("mm_in_bwd_x", dhb, w_g, _NT, (m // ROW_TILE, N_DEV), pl.BlockSpec((ROW_TILE, ns), lambda i, j: (i, j)),
               pl.BlockSpec((None, d, ns), lambda i, j: (j, 0, 0)), [pl.BlockSpec((ROW_TILE, d), lambda i, j: (i, 0))],
               [jax.ShapeDtypeStruct((m, d), F32)], (ROW_TILE, d), 1)[0]


def _mm_in_bwd_w(h0b, dhb):
    m, d = h0b.shape
    ns = dhb.shape[1] // N_DEV
    return _mm("mm_in_bwd_w", h0b, dhb, _TN, (N_DEV, m // ROW_TILE), pl.BlockSpec((ROW_TILE, d), lambda j, k: (k, 0)),
               pl.BlockSpec((ROW_TILE, ns), lambda j, k: (k, j)), [pl.BlockSpec((None, d, ns), lambda j, k: (j, 0, 0))],
               [jax.ShapeDtypeStruct((N_DEV, d, ns), BF16)], (d, ns), 1)[0]


def _split_hi_lo(x):
    hi = x.astype(BF16)
    return hi, (x - hi.astype(F32)).astype(BF16)


def _dot2(x, u):
    hi, lo = _split_hi_lo(x)
    return (lax.dot_general(hi, u, _NN, preferred_element_type=F32)
            + lax.dot_general(lo, u, _NN, preferred_element_type=F32))


def _log_not_beta(l):
    e = jnp.exp(-jnp.abs(l))
    return jnp.minimum(-l, 0.0) - jnp.log(1.0 + e), e


def _attn_fwd(hb, h, nb, seq):
    m = nb * seq
    d = SB_HEADS * HEAD_DIM
    t = ATT_T
    nq = seq // t
    sec = d // 128
    scale = 1.0 / math.sqrt(HEAD_DIM)

    def body(q_ref, k_ref, v_ref, z_ref, o_ref, a_ref, c_ref):
        qi = pl.program_id(2)
        row = lax.broadcasted_iota(jnp.int32, (t, t), 0)
        col = lax.broadcasted_iota(jnp.int32, (t, t), 1)
        causal = col < row
        u_after = (row > col).astype(BF16)
        lane = lax.broadcasted_iota(jnp.int32, (t, 128), 1)
        heads = [slice(hh * HEAD_DIM, (hh + 1) * HEAD_DIM) for hh in range(2)]
        qs = [(q_ref[:, cols].astype(F32) * scale).astype(BF16) for cols in heads]

        def block(kb, state, masked):
            carries = state[-1]
            rows = pl.ds(pl.multiple_of(kb * t, t), t)
            new = []
            for hh, cols in enumerate(heads):
                carry, acc = state[2 * hh], state[2 * hh + 1]
                l = lax.dot_general(qs[hh], k_ref[rows, cols], _NT, preferred_element_type=F32)
                lnb, _ = _log_not_beta(l)
                if masked:
                    lnb = jnp.where(causal, lnb, 0.0)
                later = carry + _dot2(lnb, u_after)
                a = jnp.exp(l + lnb + later)
                if masked:
                    a = jnp.where(causal, a, 0.0)
                acc = acc + lax.dot_general(a.astype(BF16), v_ref[rows, cols], _NN, preferred_element_type=F32)
                carries = carries + jnp.where(lane == hh * HEAD_DIM + kb, carry, 0.0)
                new += [later[:, 0:1] + lnb[:, 0:1], acc]
            return (*new, carries)

        zero = (jnp.zeros((t, 1), F32), jnp.zeros((t, HEAD_DIM), F32))
        state = block(qi, (*zero, *zero, jnp.zeros((t, 128), F32)), True)
        state = lax.fori_loop(0, qi, lambda i, s: block(qi - 1 - i, s, False), state)
        carries = state[-1]
        o = jnp.concatenate([state[1], state[3]], axis=1)
        o_ref[...] = o
        silu, _ = _silu_and_grad(z_ref[...])
        a_ref[...] = (o * silu).astype(BF16)
        c_ref[...] = carries

    qspec = pl.BlockSpec((t, 128), lambda b, hp, qi: (b * nq + qi, hp))
    return pl.pallas_call(
        body, name="attn_fwd", grid=(nb, sec, nq),
        in_specs=[qspec,
                  pl.BlockSpec((seq, 128), lambda b, hp, qi: (b, sec + hp)),
                  pl.BlockSpec((seq, 128), lambda b, hp, qi: (b, 2 * sec + hp)),
                  pl.BlockSpec((t, 128), lambda b, hp, qi: (b * nq + qi, 3 * sec + hp))],
        out_specs=[qspec, qspec, pl.BlockSpec((None, t, 128), lambda b, hp, qi: (hp, b * nq + qi, 0))],
        out_shape=[jax.ShapeDtypeStruct((m, d), F32), jax.ShapeDtypeStruct((m, d), BF16),
                   jax.ShapeDtypeStruct((sec, m, 128), F32)],
        compiler_params=_params("parallel", "parallel", "parallel"),
    )(hb, hb, hb, h)


def _attn_bwd(hb, h, o, carries, da, nb, seq):
    m = nb * seq
    d = SB_HEADS * HEAD_DIM
    t = ATT_T
    nq = seq // t
    sec = d // 128
    scale = 1.0 / math.sqrt(HEAD_DIM)

    def body(q_ref, k_ref, v_ref, z_ref, o_ref, c_ref, da_ref, dq_ref, dk_ref, dv_ref, dz_ref, dk_acc, dv_acc):
        qi = pl.program_id(2)

        @pl.when(qi == 0)
        def _():
            dk_acc[...] = jnp.zeros_like(dk_acc)
            dv_acc[...] = jnp.zeros_like(dv_acc)

        row = lax.broadcasted_iota(jnp.int32, (t, t), 0)
        col = lax.broadcasted_iota(jnp.int32, (t, t), 1)
        causal = col < row
        u_after = (row > col).astype(BF16)
        u_before = (row < col).astype(BF16)
        lane = lax.broadcasted_iota(jnp.int32, (t, 128), 1)
        silu, dsilu = _silu_and_grad(z_ref[...])
        dav = da_ref[...]
        dz_ref[...] = (dav * o_ref[...] * dsilu).astype(BF16)
        do2 = (dav * silu).astype(BF16)
        cvals = c_ref[...]
        heads = [slice(hh * HEAD_DIM, (hh + 1) * HEAD_DIM) for hh in range(2)]
        qs = [(q_ref[:, cols].astype(F32) * scale).astype(BF16) for cols in heads]
        dos = [do2[:, cols] for cols in heads]

        def block(kb, state, masked):
            rows = pl.ds(pl.multiple_of(kb * t, t), t)
            new = []
            for hh, cols in enumerate(heads):
                gcarry, dq = state[2 * hh], state[2 * hh + 1]
                ks = k_ref[rows, cols]
                l = lax.dot_general(qs[hh], ks, _NT, preferred_element_type=F32)
                lnb, e = _log_not_beta(l)
                if masked:
                    lnb = jnp.where(causal, lnb, 0.0)
                carry = jnp.sum(jnp.where(lane == hh * HEAD_DIM + kb, cvals, 0.0), axis=1, keepdims=True)
                later = carry + _dot2(lnb, u_after)
                a = jnp.exp(l + lnb + later)
                if masked:
                    a = jnp.where(causal, a, 0.0)
                dv_acc[rows, cols] += lax.dot_general(a.astype(BF16), dos[hh], _TN, preferred_element_type=F32)
                g = a * lax.dot_general(dos[hh], v_ref[rows, cols], _NT, preferred_element_type=F32)
                c = gcarry + _dot2(g, u_before)
                r = 1.0 / (1.0 + e)
                beta = jnp.where(l >= 0.0, r, e * r)
                dl = g * (1.0 - beta) - c * beta
                if masked:
                    dl = jnp.where(causal, dl, 0.0)
                dl_b = dl.astype(BF16)
                dq = dq + lax.dot_general(dl_b, ks, _NN, preferred_element_type=F32)
                dk_acc[rows, cols] += lax.dot_general(dl_b, qs[hh], _TN, preferred_element_type=F32)
                new += [c[:, t - 1:t] + g[:, t - 1:t], dq]
            return tuple(new)

        zero = (jnp.zeros((t, 1), F32), jnp.zeros((t, HEAD_DIM), F32))
        state = lax.fori_loop(0, qi, lambda kb, s: block(kb, s, False), (*zero, *zero))
        state = block(qi, state, True)
        dq_ref[...] = (jnp.concatenate([state[1], state[3]], axis=1) * scale).astype(BF16)

        @pl.when(qi == nq - 1)
        def _():
            dk_ref[...] = dk_acc[...].astype(BF16)
            dv_ref[...] = dv_acc[...].astype(BF16)

    qspec = pl.BlockSpec((t, 128), lambda b, hp, qi: (b * nq + qi, hp))
    kvspec = pl.BlockSpec((seq, 128), lambda b, hp, qi: (b, hp))
    return pl.pallas_call(
        body, name="attn_bwd", grid=(nb, sec, nq),
        in_specs=[qspec,
                  pl.BlockSpec((seq, 128), lambda b, hp, qi: (b, sec + hp)),
                  pl.BlockSpec((seq, 128), lambda b, hp, qi: (b, 2 * sec + hp)),
                  pl.BlockSpec((t, 128), lambda b, hp, qi: (b * nq + qi, 3 * sec + hp)),
                  qspec,
                  pl.BlockSpec((None, t, 128), lambda b, hp, qi: (hp, b * nq + qi, 0)),
                  qspec],
        out_specs=[qspec, kvspec, kvspec, qspec],
        out_shape=[jax.ShapeDtypeStruct((m, d), BF16)] * 4,
        scratch_shapes=[pltpu.VMEM((seq, 128), F32), pltpu.VMEM((seq, 128), F32)],
        compiler_params=_params("parallel", "parallel", "arbitrary"),
    )(hb, hb, hb, h, o, carries, da)


def _conv_rows(pad_ref, w_ref, b_ref, n_rows):
    acc = jnp.broadcast_to(b_ref[...], (n_rows, b_ref.shape[1]))
    for k in range(CV_KERNEL):
        acc = acc + w_ref[k:k + 1, :] * pad_ref[pl.ds(CV_HALO - CV_KERNEL + 1 + k, n_rows), :]
    return acc


def _conv_fwd(h, conv_w, conv_b, ln_g, ln_b, nb, seq):
    m = nb * seq
    d = conv_b.shape[1]
    t = CONV_TILE
    tiles = seq // t
    hpt = t // CV_HALO

    def body(cv_ref, cg_ref, pv_ref, pg_ref, z_ref, w_ref, b_ref, g_ref, bb_ref, a_ref, pad_ref):
        first = pl.program_id(0) % tiles == 0
        pad_ref[0:CV_HALO, :] = jnp.where(first, 0.0, pv_ref[...] * _sigmoid(pg_ref[...]))
        pad_ref[CV_HALO:, :] = cv_ref[...] * _sigmoid(cg_ref[...])
        xhat, _ = _ln_stats(_conv_rows(pad_ref, w_ref, b_ref, t))
        s, _ = _silu_and_grad(xhat * g_ref[...] + bb_ref[...])
        sz, _ = _silu_and_grad(z_ref[...])
        a_ref[...] = (s * sz).astype(BF16)

    def main(c):
        return pl.BlockSpec((t, d), lambda i: (i, c))

    def prev(c):
        return pl.BlockSpec((CV_HALO, d), lambda i: (jnp.maximum(i * hpt - 1, 0), c))

    vec = pl.BlockSpec((1, d), lambda i: (0, 0))
    return pl.pallas_call(
        body, name="conv_fwd", grid=(m // t,),
        in_specs=[main(4), main(5), prev(4), prev(5), main(6),
                  pl.BlockSpec((CV_HALO, d), lambda i: (0, 0)), vec, vec, vec],
        out_specs=pl.BlockSpec((t, d), lambda i: (i, 0)),
        out_shape=jax.ShapeDtypeStruct((m, d), BF16),
        scratch_shapes=[pltpu.VMEM((CV_HALO + t, d), F32)],
        compiler_params=_params("parallel"),
    )(h, h, h, h, h, conv_w, conv_b, ln_g, ln_b)


def _conv_bwd(h, da, conv_w, conv_b, ln_g, ln_b, nb, seq):
    m = nb * seq
    d = conv_b.shape[1]
    t = CONV_TILE
    tiles = seq // t
    hpt = t // CV_HALO
    last_halo = m // CV_HALO - 1

    def body(cv_ref, cg_ref, pv_ref, pg_ref, nv_ref, ng_ref, z_ref, nz_ref, da_ref, nda_ref,
             w_ref, b_ref, g_ref, bb_ref, dh_ref, dw_ref, db_ref, dg_ref, dbb_ref, pad_ref, dc_ref):
        i = pl.program_id(0)

        @pl.when(i == 0)
        def _():
            dw_ref[...] = jnp.zeros_like(dw_ref)
            db_ref[...] = jnp.zeros_like(db_ref)
            dg_ref[...] = jnp.zeros_like(dg_ref)
            dbb_ref[...] = jnp.zeros_like(dbb_ref)

        first = i % tiles == 0
        last = i % tiles == tiles - 1
        cv = cv_ref[...]
        sg = _sigmoid(cg_ref[...])
        pad_ref[0:CV_HALO, :] = jnp.where(first, 0.0, pv_ref[...] * _sigmoid(pg_ref[...]))
        pad_ref[CV_HALO:CV_HALO + t, :] = cv * sg
        pad_ref[CV_HALO + t:, :] = nv_ref[...] * _sigmoid(ng_ref[...])

        def rows_bwd(conv_out, z, dav):
            xhat, rstd = _ln_stats(conv_out)
            s, ds = _silu_and_grad(xhat * g_ref[...] + bb_ref[...])
            sz, dsz = _silu_and_grad(z)
            dn = dav * sz * ds
            return _ln_bwd(dn * g_ref[...], xhat, rstd), dav * s * dsz, dn, xhat

        conv_all = _conv_rows(pad_ref, w_ref, b_ref, t + CV_HALO)
        dc, dz, dn, xhat = rows_bwd(conv_all[:t], z_ref[...], da_ref[...])
        dc_next, _, _, _ = rows_bwd(conv_all[t:], nz_ref[...], nda_ref[...])
        dc_ref[0:t, :] = dc
        dc_ref[t:, :] = jnp.where(last, 0.0, dc_next)
        dg_ref[...] += jnp.sum(dn * xhat, axis=0, keepdims=True)
        dbb_ref[...] += jnp.sum(dn, axis=0, keepdims=True)
        db_ref[...] += jnp.sum(dc, axis=0, keepdims=True)
        du = jnp.zeros((t, d), F32)
        for k in range(CV_KERNEL):
            dw_ref[k:k + 1, :] += jnp.sum(dc * pad_ref[pl.ds(CV_HALO - CV_KERNEL + 1 + k, t), :], axis=0, keepdims=True)
            du = du + w_ref[k:k + 1, :] * dc_ref[pl.ds(CV_KERNEL - 1 - k, t), :]
        dh_ref[:, 0:d] = (du * sg).astype(BF16)
        dh_ref[:, d:2 * d] = (du * cv * sg * (1.0 - sg)).astype(BF16)
        dh_ref[:, 2 * d:] = dz.astype(BF16)

    def main(c):
        return pl.BlockSpec((t, d), lambda i: (i, c))

    def prev(c):
        return pl.BlockSpec((CV_HALO, d), lambda i: (jnp.maximum(i * hpt - 1, 0), c))

    def nxt(c):
        return pl.BlockSpec((CV_HALO, d), lambda i: (jnp.minimum((i + 1) * hpt, last_halo), c))

    vec = pl.BlockSpec((1, d), lambda i: (0, 0))
    taps = pl.BlockSpec((CV_HALO, d), lambda i: (0, 0))
    return pl.pallas_call(
        body, name="conv_bwd", grid=(m // t,),
        in_specs=[main(4), main(5), prev(4), prev(5), nxt(4), nxt(5), main(6), nxt(6), main(0), nxt(0),
                  taps, vec, vec, vec],
        out_specs=[pl.BlockSpec((t, 3 * d), lambda i: (i, 0)), taps, vec, vec, vec],
        out_shape=[jax.ShapeDtypeStruct((m, 3 * d), BF16), jax.ShapeDtypeStruct((CV_HALO, d), F32),
                   jax.ShapeDtypeStruct((1, d), F32), jax.ShapeDtypeStruct((1, d), F32),
                   jax.ShapeDtypeStruct((1, d), F32)],
        scratch_shapes=[pltpu.VMEM((t + 2 * CV_HALO, d), F32), pltpu.VMEM((t + CV_HALO, d), F32)],
        compiler_params=_params("arbitrary"),
    )(h, h, h, h, h, h, h, h, da, da, conv_w, conv_b, ln_g, ln_b)


_HBM = pl.BlockSpec(memory_space=pltpu.HBM)


def _place():
    return lax.axis_index("x"), lax.axis_index("y"), lax.axis_index("c")


def _slot(p):
    return 4 * p[0] + 2 * p[1] + p[2]


def _all_gather(shards):
    n = len(shards)

    def body(*refs):
        ins, outs = refs[:n], refs[n:2 * n]
        send_sems, recv_sems, local_sems = refs[2 * n:]
        x, y, c = _place()
        me, sibling = (x, y, c), (x, y, 1 - c)
        chips = [(1 - x, y), (x, 1 - y), (1 - x, 1 - y)]

        def copy(a, k, block, to, src=None):
            dst = outs[a].at[_slot(block)]
            return pltpu.make_async_remote_copy(
                src_ref=dst if src is None else src, dst_ref=dst, send_sem=send_sems.at[7 * a + k],
                recv_sem=recv_sems.at[7 * a + k], device_id=to, device_id_type=MESH)

        mine = [pltpu.make_async_copy(ins[a], outs[a].at[_slot(me)], local_sems.at[a]) for a in range(n)]
        for cp in mine:
            cp.start()
        sent = []
        for a in range(n):
            sent.append(copy(a, 0, me, sibling, src=ins[a]))
            sent += [copy(a, 1 + j, me, (*chip, c), src=ins[a]) for j, chip in enumerate(chips)]
        for cp in sent:
            cp.start()
        for j, chip in enumerate(chips):
            for a in range(n):
                copy(a, 1 + j, (*chip, c), me).wait_recv()
                passed = copy(a, 4 + j, (*chip, c), sibling)
                passed.start()
                sent.append(passed)
        for a in range(n):
            copy(a, 0, sibling, me).wait_recv()
            for j, chip in enumerate(chips):
                copy(a, 4 + j, (*chip, 1 - c), me).wait_recv()
        for cp in sent:
            cp.wait_send()
        for cp in mine:
            cp.wait()

    return pl.pallas_call(
        body, name="all_gather_weights",
        in_specs=[_HBM] * n, out_specs=[_HBM] * n,
        out_shape=[jax.ShapeDtypeStruct((N_DEV,) + s.shape, s.dtype) for s in shards],
        scratch_shapes=[pltpu.SemaphoreType.DMA((7 * n,)), pltpu.SemaphoreType.DMA((7 * n,)),
                        pltpu.SemaphoreType.DMA((n,))],
    )(*shards)


def _exchange_partials(parts, whole):
    n_parts = len(parts)
    arrays = list(parts) + list(whole)
    n = len(arrays)

    def body(*refs):
        ins, outs = refs[:n], refs[n:2 * n]
        send_sems, recv_sems, local_sems = refs[2 * n:]
        x, y, c = _place()
        me = (x, y, c)

        def src_for(a, p):
            return ins[a].at[_slot(p)] if a < n_parts else ins[a]

        def copy(a, k, peer):
            return pltpu.make_async_remote_copy(
                src_ref=src_for(a, peer), dst_ref=outs[a].at[_slot(me)], send_sem=send_sems.at[7 * a + k],
                recv_sem=recv_sems.at[7 * a + k], device_id=peer, device_id_type=MESH)

        def landed(a, k, peer):
            return pltpu.make_async_remote_copy(
                src_ref=src_for(a, peer), dst_ref=outs[a].at[_slot(peer)], send_sem=send_sems.at[7 * a + k],
                recv_sem=recv_sems.at[7 * a + k], device_id=peer, device_id_type=MESH)

        peers = []
        for k in range(1, N_DEV):
            fx, fy, fc = (k >> 2) & 1, (k >> 1) & 1, k & 1
            peers.append((1 - x if fx else x, 1 - y if fy else y, 1 - c if fc else c))
        mine = [pltpu.make_async_copy(src_for(a, me), outs[a].at[_slot(me)], local_sems.at[a]) for a in range(n)]
        for cp in mine:
            cp.start()
        sent = [copy(a, k, peer) for a in range(n) for k, peer in enumerate(peers)]
        for cp in sent:
            cp.start()
        for a in range(n):
            for k, peer in enumerate(peers):
                landed(a, k, peer).wait_recv()
        for cp in sent:
            cp.wait_send()
        for cp in mine:
            cp.wait()

    out_shape = [jax.ShapeDtypeStruct(p.shape, p.dtype) for p in parts]
    out_shape += [jax.ShapeDtypeStruct((N_DEV,) + w.shape, w.dtype) for w in whole]
    return pl.pallas_call(
        body, name="exchange_grad_partials",
        in_specs=[_HBM] * n, out_specs=[_HBM] * n, out_shape=out_shape,
        scratch_shapes=[pltpu.SemaphoreType.DMA((7 * n,)), pltpu.SemaphoreType.DMA((7 * n,)),
                        pltpu.SemaphoreType.DMA((n,))],
    )(*arrays)


def _adamw(name, parts, w, mom, var, rows):
    r, c = w.shape

    def body(p_ref, w_ref, m_ref, v_ref, g_ref, d_ref, nm_ref, nv_ref):
        g = p_ref[0].astype(F32)
        for p in range(1, N_DEV):
            g = g + p_ref[p].astype(F32)
        m_new = ADAM_B1 * m_ref[...] + (1.0 - ADAM_B1) * g
        v_new = ADAM_B2 * v_ref[...] + (1.0 - ADAM_B2) * (g * g)
        m_hat = m_new / (1.0 - ADAM_B1 ** ADAM_STEP)
        v_hat = v_new / (1.0 - ADAM_B2 ** ADAM_STEP)
        g_ref[...] = g
        d_ref[...] = -ADAM_LR * (m_hat / (jnp.sqrt(v_hat) + ADAM_EPS) + ADAM_WD * w_ref[...])
        nm_ref[...] = m_new
        nv_ref[...] = v_new

    blk = pl.BlockSpec((rows, c), lambda i: (i, 0))
    return pl.pallas_call(
        body, name=name, grid=(r // rows,),
        in_specs=[pl.BlockSpec((N_DEV, rows, c), lambda i: (0, i, 0)), blk, blk, blk],
        out_specs=[blk] * 4, out_shape=[jax.ShapeDtypeStruct((r, c), F32)] * 4,
        compiler_params=_params("parallel"),
    )(parts, w, mom, var)


def kernel(x, ln_in_g, ln_in_b, w_in, w_sb_proj, conv_w, conv_b, conv_ln_g, conv_ln_b, w_cv_proj, w_out, ln_post_g, ln_post_b, loss_target, m_ln_in_g, m_ln_in_b, m_w_in, m_w_sb_proj, m_conv_w, m_conv_b, m_conv_ln_g, m_conv_ln_b, m_w_cv_proj, m_w_out, m_ln_post_g, m_ln_post_b, v_ln_in_g, v_ln_in_b, v_w_in, v_w_sb_proj, v_conv_w, v_conv_b, v_conv_ln_g, v_conv_ln_b, v_w_cv_proj, v_w_out, v_ln_post_g, v_ln_post_b):
    nb, seq, d = x.shape
    m = nb * seq
    x2d = x.reshape(m, d)
    target = loss_target.reshape(m, d)
    rs = d // N_DEV
    pad_taps = ((0, CV_HALO - CV_KERNEL), (0, 0))

    proj_shards = jnp.stack([w_sb_proj[0], w_cv_proj[0], w_out[0]]).astype(BF16)
    conv_w_shard = jnp.pad(conv_w[0], pad_taps)
    w_in_g, proj_g, conv_w_g = _all_gather([w_in[0].astype(BF16), proj_shards, conv_w_shard])
    w_sb_g = proj_g[:, 0].reshape(d, d)
    w_cv_g = proj_g[:, 1].reshape(d, d)
    w_out_g = proj_g[:, 2].reshape(d, d)
    conv_w_full = conv_w_g.transpose(1, 0, 2).reshape(CV_HALO, d)

    h0, h0b = _ln_in_fwd(x2d, ln_in_g.reshape(1, d), ln_in_b.reshape(1, d))
    h, hb = _mm_in_fwd(h0b, w_in_g)
    o, a_sb, carries = _attn_fwd(hb, h, nb, seq)
    a_cv = _conv_fwd(h, conv_w_full, conv_b, conv_ln_g, conv_ln_b, nb, seq)
    y_sb = _mm_nn("mm_sb_fwd", a_sb, w_sb_g)
    y_cv = _mm_nn("mm_cv_fwd", a_cv, w_cv_g)
    merged = _merge_fwd(h, y_sb, y_cv)
    mo = _mm_nn("mm_out_fwd", merged, w_out_g)
    loss_part, d_pre, d_pre_b, dg_post, db_post = _post_ln_loss(h0, mo, target, ln_post_g, ln_post_b)

    dw_out = _mm_tn("mm_out_bwd_w", merged, d_pre_b, BF16)
    d_merged = _mm_nt("mm_out_bwd_x", d_pre_b, w_out_g)
    dy_sb, dy_cv, d_gates = _merge_bwd(h, y_sb, y_cv, d_merged)
    dw_sb = _mm_tn("mm_sb_bwd_w", a_sb, dy_sb, BF16)
    dw_cv = _mm_tn("mm_cv_bwd_w", a_cv, dy_cv, BF16)
    da_sb = _mm_nt("mm_sb_bwd_x", dy_sb, w_sb_g)
    da_cv = _mm_nt("mm_cv_bwd_x", dy_cv, w_cv_g)
    dq, dk, dv, dz_sb = _attn_bwd(hb, h, o, carries, da_sb, nb, seq)
    d_conv3, dconv_w, dconv_b, dconv_ln_g, dconv_ln_b = _conv_bwd(
        h, da_cv, conv_w_full, conv_b, conv_ln_g, conv_ln_b, nb, seq)
    dhb = jnp.concatenate([dq, dk, dv, dz_sb, d_conv3, d_gates], axis=1)
    dw_in = _mm_in_bwd_w(h0b, dhb)
    dh_mm = _mm_in_bwd_x(dhb, w_in_g)
    dx, dg_in, db_in = _ln_in_bwd(x2d, d_pre, dh_mm, ln_in_g.reshape(1, d))

    dproj = jnp.stack([dw_sb.reshape(N_DEV, rs, d), dw_cv.reshape(N_DEV, rs, d), dw_out.reshape(N_DEV, rs, d)], axis=1)
    dconv_w_parts = dconv_w.reshape(CV_HALO, N_DEV, d // N_DEV).transpose(1, 0, 2)
    small = jnp.concatenate([dg_in, db_in, dconv_b, dconv_ln_g, dconv_ln_b, dg_post, db_post,
                             jnp.zeros((1, d), F32)], axis=0)
    r_in, r_proj, r_conv, r_small = _exchange_partials([dw_in, dproj, dconv_w_parts], [small])

    g_in, d_in, nm_in, nv_in = _adamw("adamw_w_in", r_in, w_in[0], m_w_in[0], v_w_in[0], 128)
    stack3 = lambda a, b, c: jnp.concatenate([a[0], b[0], c[0]], axis=0)
    g_pr, d_pr, nm_pr, nv_pr = _adamw(
        "adamw_proj", r_proj.reshape(N_DEV, 3 * rs, d), stack3(w_sb_proj, w_cv_proj, w_out),
        stack3(m_w_sb_proj, m_w_cv_proj, m_w_out), stack3(v_w_sb_proj, v_w_cv_proj, v_w_out), 3 * rs)
    padc = lambda a: jnp.pad(a[0], pad_taps)
    g_cw, d_cw, nm_cw, nv_cw = _adamw("adamw_conv_w", r_conv, padc(conv_w), padc(m_conv_w), padc(v_conv_w), CV_HALO)
    vecs = lambda *a: jnp.concatenate([t.reshape(1, d) for t in a] + [jnp.ones((1, d), F32)], axis=0)
    g_sm, d_sm, nm_sm, nv_sm = _adamw(
        "adamw_vectors", r_small,
        vecs(ln_in_g, ln_in_b, conv_b, conv_ln_g, conv_ln_b, ln_post_g, ln_post_b),
        vecs(m_ln_in_g, m_ln_in_b, m_conv_b, m_conv_ln_g, m_conv_ln_b, m_ln_post_g, m_ln_post_b),
        vecs(v_ln_in_g, v_ln_in_b, v_conv_b, v_conv_ln_g, v_conv_ln_b, v_ln_post_g, v_ln_post_b), 8)

    loss = lax.psum(loss_part[0, 0], ("x", "y", "c"))

    def leaves(big, pr, cw, sm):
        return (sm[0], sm[1], big[None], pr[None, 0:rs], cw[None, :CV_KERNEL], sm[2:3], sm[3:4], sm[4:5],
                pr[None, rs:2 * rs], pr[None, 2 * rs:], sm[5:6], sm[6:7])

    return (loss, dx.reshape(nb, seq, d), *leaves(g_in, g_pr, g_cw, g_sm), *leaves(d_in, d_pr, d_cw, d_sm),
            *leaves(nm_in, nm_pr, nm_cw, nm_sm), *leaves(nv_in, nv_pr, nv_cw, nv_sm))
```

```python
import functools
import math

import jax
import jax.numpy as jnp
from jax import lax
from jax.experimental import pallas as pl
from jax.experimental.pallas import tpu as pltpu

F32 = jnp.float32
BF16 = jnp.bfloat16
MESH = pl.DeviceIdType.MESH

N_DEV = 8
SB_HEADS = 16
HEAD_DIM = 64
CV_KERNEL = 31
CV_HALO = 32
LN_EPS = 1e-5
DEEPNORM_ALPHA = 2.0 ** 0.25
ADAM_LR, ADAM_B1, ADAM_B2, ADAM_EPS, ADAM_WD, ADAM_STEP = 0.001, 0.9, 0.999, 1e-08, 0.01, 10

ATT_TK = 256
ATT_TQ = 1024
ROW_TILE = 512
CONV_TILE = 256
VMEM_LIMIT = 56 * 1024 * 1024


def _params(*sem):
    return pltpu.CompilerParams(dimension_semantics=sem, vmem_limit_bytes=VMEM_LIMIT)


def _sigmoid(x):
    return 1.0 / (1.0 + jnp.exp(-x))


def _silu_and_grad(x):
    s = _sigmoid(x)
    return x * s, s * (1.0 + x * (1.0 - s))


def _ln_stats(x):
    mu = jnp.mean(x, axis=-1, keepdims=True)
    xc = x - mu
    var = jnp.mean(xc * xc, axis=-1, keepdims=True)
    rstd = lax.rsqrt(var + LN_EPS)
    return xc * rstd, rstd


def _ln_bwd(dxhat, xhat, rstd):
    m1 = jnp.mean(dxhat, axis=-1, keepdims=True)
    m2 = jnp.mean(dxhat * xhat, axis=-1, keepdims=True)
    return rstd * (dxhat - m1 - xhat * m2)


def _ln_in_fwd(x2d, g, b):
    m, d = x2d.shape

    def body(x_ref, g_ref, b_ref, h_ref, hb_ref):
        xhat, _ = _ln_stats(x_ref[...])
        y = xhat * g_ref[...] + b_ref[...]
        h_ref[...] = y
        hb_ref[...] = y.astype(BF16)

    row = pl.BlockSpec((ROW_TILE, d), lambda i: (i, 0))
    vec = pl.BlockSpec((1, d), lambda i: (0, 0))
    return pl.pallas_call(
        body, name="ln_in_fwd", grid=(m // ROW_TILE,),
        in_specs=[row, vec, vec], out_specs=[row, row],
        out_shape=[jax.ShapeDtypeStruct((m, d), F32), jax.ShapeDtypeStruct((m, d), BF16)],
        compiler_params=_params("parallel"),
    )(x2d, g, b)


def _ln_in_bwd(x2d, d_pre, dh_mm, g):
    m, d = x2d.shape

    def body(x_ref, dp_ref, dm_ref, g_ref, dx_ref, dg_ref, db_ref):
        @pl.when(pl.program_id(0) == 0)
        def _():
            dg_ref[...] = jnp.zeros_like(dg_ref)
            db_ref[...] = jnp.zeros_like(db_ref)

        xhat, rstd = _ln_stats(x_ref[...])
        dh = DEEPNORM_ALPHA * dp_ref[...] + dm_ref[...]
        dg_ref[...] += jnp.sum(dh * xhat, axis=0, keepdims=True)
        db_ref[...] += jnp.sum(dh, axis=0, keepdims=True)
        dx_ref[...] = _ln_bwd(dh * g_ref[...], xhat, rstd)

    row = pl.BlockSpec((ROW_TILE, d), lambda i: (i, 0))
    vec = pl.BlockSpec((1, d), lambda i: (0, 0))
    return pl.pallas_call(
        body, name="ln_in_bwd", grid=(m // ROW_TILE,),
        in_specs=[row, row, row, vec], out_specs=[row, vec, vec],
        out_shape=[jax.ShapeDtypeStruct((m, d), F32), jax.ShapeDtypeStruct((1, d), F32),
                   jax.ShapeDtypeStruct((1, d), F32)],
        compiler_params=_params("arbitrary"),
    )(x2d, d_pre, dh_mm, g)


def _merge_fwd(h, y_sb, y_cv):
    m, d = y_sb.shape

    def body(gs_ref, gc_ref, ys_ref, yc_ref, out_ref):
        out_ref[...] = (_sigmoid(gs_ref[...]) * ys_ref[...] + _sigmoid(gc_ref[...]) * yc_ref[...]).astype(BF16)

    row = pl.BlockSpec((ROW_TILE, d), lambda i: (i, 0))
    return pl.pallas_call(
        body, name="merge_fwd", grid=(m // ROW_TILE,),
        in_specs=[pl.BlockSpec((ROW_TILE, d), lambda i: (i, 7)), pl.BlockSpec((ROW_TILE, d), lambda i: (i, 8)), row, row],
        out_specs=row, out_shape=jax.ShapeDtypeStruct((m, d), BF16),
        compiler_params=_params("parallel"),
    )(h, h, y_sb, y_cv)


def _merge_bwd(h, y_sb, y_cv, dm):
    m, d = y_sb.shape

    def body(gs_ref, gc_ref, ys_ref, yc_ref, dm_ref, dys_ref, dyc_ref, dgate_ref):
        dmv = dm_ref[...]
        ss = _sigmoid(gs_ref[...])
        sc = _sigmoid(gc_ref[...])
        dys_ref[...] = (ss * dmv).astype(BF16)
        dyc_ref[...] = (sc * dmv).astype(BF16)
        dgate_ref[:, :d] = (dmv * ys_ref[...] * ss * (1.0 - ss)).astype(BF16)
        dgate_ref[:, d:] = (dmv * yc_ref[...] * sc * (1.0 - sc)).astype(BF16)

    row = pl.BlockSpec((ROW_TILE, d), lambda i: (i, 0))
    return pl.pallas_call(
        body, name="merge_bwd", grid=(m // ROW_TILE,),
        in_specs=[pl.BlockSpec((ROW_TILE, d), lambda i: (i, 7)), pl.BlockSpec((ROW_TILE, d), lambda i: (i, 8)), row, row, row],
        out_specs=[row, row, pl.BlockSpec((ROW_TILE, 2 * d), lambda i: (i, 0))],
        out_shape=[jax.ShapeDtypeStruct((m, d), BF16), jax.ShapeDtypeStruct((m, d), BF16),
                   jax.ShapeDtypeStruct((m, 2 * d), BF16)],
        compiler_params=_params("parallel"),
    )(h, h, y_sb, y_cv, dm)


def _post_ln_loss(h0, mo, target, g, b):
    m, d = h0.shape

    def body(h_ref, mo_ref, t_ref, g_ref, b_ref, loss_ref, dp_ref, dpb_ref, dg_ref, db_ref):
        @pl.when(pl.program_id(0) == 0)
        def _():
            loss_ref[...] = jnp.zeros_like(loss_ref)
            dg_ref[...] = jnp.zeros_like(dg_ref)
            db_ref[...] = jnp.zeros_like(db_ref)

        xhat, rstd = _ln_stats(DEEPNORM_ALPHA * h_ref[...] + mo_ref[...])
        err = xhat * g_ref[...] + b_ref[...] - t_ref[...]
        per_row = jnp.mean(err * err, axis=-1, keepdims=True)
        loss_ref[...] += 0.5 * jnp.sum(per_row, axis=0, keepdims=True)
        dy = err * (1.0 / d)
        dg_ref[...] += jnp.sum(dy * xhat, axis=0, keepdims=True)
        db_ref[...] += jnp.sum(dy, axis=0, keepdims=True)
        dp = _ln_bwd(dy * g_ref[...], xhat, rstd)
        dp_ref[...] = dp
        dpb_ref[...] = dp.astype(BF16)

    row = pl.BlockSpec((ROW_TILE, d), lambda i: (i, 0))
    vec = pl.BlockSpec((1, d), lambda i: (0, 0))
    one = pl.BlockSpec((1, 1), lambda i: (0, 0))
    return pl.pallas_call(
        body, name="post_ln_loss", grid=(m // ROW_TILE,),
        in_specs=[row, row, row, vec, vec], out_specs=[one, row, row, vec, vec],
        out_shape=[jax.ShapeDtypeStruct((1, 1), F32), jax.ShapeDtypeStruct((m, d), F32),
                   jax.ShapeDtypeStruct((m, d), BF16), jax.ShapeDtypeStruct((1, d), F32),
                   jax.ShapeDtypeStruct((1, d), F32)],
        compiler_params=_params("arbitrary"),
    )(h0, mo, target, g, b)


_NN = (((1,), (0,)), ((), ()))
_NT = (((1,), (1,)), ((), ()))
_TN = (((0,), (0,)), ((), ()))


def _mm(name, a, b, dims, grid, a_spec, b_spec, out_specs, out_shape, acc_shape, k_axis):
    n_k = 1 if k_axis is None else grid[k_axis]
    n_out = len(out_shape)

    def body(a_ref, b_ref, *rest):
        outs, acc_ref = rest[:n_out], rest[n_out]
        part = lax.dot_general(a_ref[...].astype(BF16), b_ref[...].astype(BF16), dims, preferred_element_type=F32)
        if n_k == 1:
            for o in outs:
                o[...] = part.astype(o.dtype)
            return
        k = pl.program_id(k_axis)

        @pl.when(k == 0)
        def _():
            acc_ref[...] = part

        @pl.when(k > 0)
        def _():
            acc_ref[...] += part

        @pl.when(k == n_k - 1)
        def _():
            for o in outs:
                o[...] = acc_ref[...].astype(o.dtype)

    sem = tuple("arbitrary" if ax == k_axis else "parallel" for ax in range(len(grid)))
    return pl.pallas_call(
        body, name=name, grid=grid, in_specs=[a_spec, b_spec], out_specs=out_specs, out_shape=out_shape,
        scratch_shapes=[pltpu.VMEM(acc_shape, F32)], compiler_params=_params(*sem),
    )(a, b)


def _mm_nn(name, a, b, out_dtype=F32):
    m, k = a.shape
    n = b.shape[1]
    return _mm(name, a, b, _NN, (m // ROW_TILE,), pl.BlockSpec((ROW_TILE, k), lambda i: (i, 0)),
               pl.BlockSpec((k, n), lambda i: (0, 0)), [pl.BlockSpec((ROW_TILE, n), lambda i: (i, 0))],
               [jax.ShapeDtypeStruct((m, n), out_dtype)], (8, 128), None)[0]


def _mm_nt(name, a, b, out_dtype=F32):
    m, k = a.shape
    n = b.shape[0]
    return _mm(name, a, b, _NT, (m // ROW_TILE,), pl.BlockSpec((ROW_TILE, k), lambda i: (i, 0)),
               pl.BlockSpec((n, k), lambda i: (0, 0)), [pl.BlockSpec((ROW_TILE, n), lambda i: (i, 0))],
               [jax.ShapeDtypeStruct((m, n), out_dtype)], (8, 128), None)[0]


def _mm_tn(name, a, b, out_dtype):
    m, k = a.shape
    n = b.shape[1]
    return _mm(name, a, b, _TN, (m // ROW_TILE,), pl.BlockSpec((ROW_TILE, k), lambda i: (i, 0)),
               pl.BlockSpec((ROW_TILE, n), lambda i: (i, 0)), [pl.BlockSpec((k, n), lambda i: (0, 0))],
               [jax.ShapeDtypeStruct((k, n), out_dtype)], (k, n), 0)[0]


def _mm_in_fwd(h0b, w_g):
    m, d = h0b.shape
    ns = w_g.shape[2]
    out = pl.BlockSpec((ROW_TILE, ns), lambda j, i: (i, j))
    return _mm("mm_in_fwd", h0b, w_g, _NN, (N_DEV, m // ROW_TILE), pl.BlockSpec((ROW_TILE, d), lambda j, i: (i, 0)),
               pl.BlockSpec((None, d, ns), lambda j, i: (j, 0, 0)), [out, out],
               [jax.ShapeDtypeStruct((m, N_DEV * ns), F32), jax.ShapeDtypeStruct((m, N_DEV * ns), BF16)],
               (8, 128), None)


def _mm_in_bwd_x(dhb, w_g):
    m = dhb.shape[0]
    _, d, ns = w_g.shape
    return _mm("mm_in_bwd_x", dhb, w_g, _NT, (m // ROW_TILE, N_DEV), pl.BlockSpec((ROW_TILE, ns), lambda i, j: (i, j)),
               pl.BlockSpec((None, d, ns), lambda i, j: (j, 0, 0)), [pl.BlockSpec((ROW_TILE, d), lambda i, j: (i, 0))],
               [jax.ShapeDtypeStruct((m, d), F32)], (ROW_TILE, d), 1)[0]


def _mm_in_bwd_w(h0b, dhb):
    m, d = h0b.shape
    ns = dhb.shape[1] // N_DEV
    return _mm("mm_in_bwd_w", h0b, dhb, _TN, (N_DEV, m // ROW_TILE), pl.BlockSpec((ROW_TILE, d), lambda j, k: (k, 0)),
               pl.BlockSpec((ROW_TILE, ns), lambda j, k: (k, j)), [pl.BlockSpec((None, d, ns), lambda j, k: (j, 0, 0))],
               [jax.ShapeDtypeStruct((N_DEV, d, ns), BF16)], (d, ns), 1)[0]


def _split_hi_lo(x):
    hi = x.astype(BF16)
    return hi, (x - hi.astype(F32)).astype(BF16)


def _dot2(x, u):
    hi, lo = _split_hi_lo(x)
    return (lax.dot_general(hi, u, _NN, preferred_element_type=F32)
            + lax.dot_general(lo, u, _NN, preferred_element_type=F32))


def _log_not_beta(l):
    e = jnp.exp(-jnp.abs(l))
    return jnp.minimum(-l, 0.0) - jnp.log(1.0 + e), e


def _attn_fwd(hb, h, nb, seq):
    m = nb * seq
    d = SB_HEADS * HEAD_DIM
    tq, tk = ATT_TQ, ATT_TK
    n_sub = tq // tk
    nq = seq // tq
    sec = d // 128
    scale = 1.0 / math.sqrt(HEAD_DIM)
    heads = [slice(hh * HEAD_DIM, (hh + 1) * HEAD_DIM) for hh in range(2)]

    def body(q_ref, k_ref, v_ref, z_ref, o_ref, a_ref, c_ref, carry_ref, acc_ref):
        qi = pl.program_id(2)
        u_after = (lax.broadcasted_iota(jnp.int32, (tk, tk), 0)
                   > lax.broadcasted_iota(jnp.int32, (tk, tk), 1)).astype(BF16)
        qs = [(q_ref[:, cols].astype(F32) * scale).astype(BF16) for cols in heads]
        carry_ref[...] = jnp.zeros_like(carry_ref)
        acc_ref[...] = jnp.zeros_like(acc_ref)
        c_ref[...] = jnp.zeros_like(c_ref)

        def block(kb, r0, masked):
            n = tq - r0
            krows = pl.ds(pl.multiple_of(kb * tk, tk), tk)
            lane = lax.broadcasted_iota(jnp.int32, (n, 128), 1)
            if masked:
                causal = lax.broadcasted_iota(jnp.int32, (n, tk), 1) < lax.broadcasted_iota(jnp.int32, (n, tk), 0)
            for hh, cols in enumerate(heads):
                carry = carry_ref[hh, r0:, :]
                l = lax.dot_general(qs[hh][r0:], k_ref[krows, cols], _NT, preferred_element_type=F32)
                lnb, _ = _log_not_beta(l)
                if masked:
                    lnb = jnp.where(causal, lnb, 0.0)
                later = carry + _dot2(lnb, u_after)
                a = jnp.exp(l + lnb + later)
                if masked:
                    a = jnp.where(causal, a, 0.0)
                acc_ref[r0:, cols] += lax.dot_general(a.astype(BF16), v_ref[krows, cols], _NN, preferred_element_type=F32)
                c_ref[r0:, :] += jnp.where(lane == hh * HEAD_DIM + kb, carry, 0.0)
                carry_ref[hh, r0:, :] = later[:, 0:1] + lnb[:, 0:1]

        for j in reversed(range(n_sub)):
            block(n_sub * qi + j, j * tk, True)

        def step(i, _):
            block(n_sub * qi - 1 - i, 0, False)
            return 0

        lax.fori_loop(0, n_sub * qi, step, 0)
        o = acc_ref[...]
        o_ref[...] = o
        silu, _ = _silu_and_grad(z_ref[...])
        a_ref[...] = (o * silu).astype(BF16)

    qspec = pl.BlockSpec((tq, 128), lambda b, hp, qi: (b * nq + qi, hp))
    return pl.pallas_call(
        body, name="attn_fwd", grid=(nb, sec, nq),
        in_specs=[qspec,
                  pl.BlockSpec((seq, 128), lambda b, hp, qi: (b, sec + hp)),
                  pl.BlockSpec((seq, 128), lambda b, hp, qi: (b, 2 * sec + hp)),
                  pl.BlockSpec((tq, 128), lambda b, hp, qi: (b * nq + qi, 3 * sec + hp))],
        out_specs=[qspec, qspec, pl.BlockSpec((None, tq, 128), lambda b, hp, qi: (hp, b * nq + qi, 0))],
        out_shape=[jax.ShapeDtypeStruct((m, d), F32), jax.ShapeDtypeStruct((m, d), BF16),
                   jax.ShapeDtypeStruct((sec, m, 128), F32)],
        scratch_shapes=[pltpu.VMEM((2, tq, 1), F32), pltpu.VMEM((tq, 128), F32)],
        compiler_params=_params("parallel", "parallel", "parallel"),
    )(hb, hb, hb, h)


def _attn_bwd(hb, h, o, carries, da, nb, seq):
    m = nb * seq
    d = SB_HEADS * HEAD_DIM
    tq, tk = ATT_TQ, ATT_TK
    n_sub = tq // tk
    nq = seq // tq
    sec = d // 128
    scale = 1.0 / math.sqrt(HEAD_DIM)
    heads = [slice(hh * HEAD_DIM, (hh + 1) * HEAD_DIM) for hh in range(2)]

    def body(q_ref, k_ref, v_ref, z_ref, o_ref, c_ref, da_ref, dq_ref, dk_ref, dv_ref, dz_ref,
             dk_acc, dv_acc, dq_acc, gcarry_ref):
        qi = pl.program_id(2)

        @pl.when(qi == 0)
        def _():
            dk_acc[...] = jnp.zeros_like(dk_acc)
            dv_acc[...] = jnp.zeros_like(dv_acc)

        row = lax.broadcasted_iota(jnp.int32, (tk, tk), 0)
        col = lax.broadcasted_iota(jnp.int32, (tk, tk), 1)
        u_after = (row > col).astype(BF16)
        u_before = (row < col).astype(BF16)
        silu, dsilu = _silu_and_grad(z_ref[...])
        dav = da_ref[...]
        dz_ref[...] = (dav * o_ref[...] * dsilu).astype(BF16)
        do2 = (dav * silu).astype(BF16)
        qs = [(q_ref[:, cols].astype(F32) * scale).astype(BF16) for cols in heads]
        dos = [do2[:, cols] for cols in heads]
        dq_acc[...] = jnp.zeros_like(dq_acc)
        gcarry_ref[...] = jnp.zeros_like(gcarry_ref)

        def block(kb, r0, masked):
            n = tq - r0
            krows = pl.ds(pl.multiple_of(kb * tk, tk), tk)
            lane = lax.broadcasted_iota(jnp.int32, (n, 128), 1)
            if masked:
                causal = lax.broadcasted_iota(jnp.int32, (n, tk), 1) < lax.broadcasted_iota(jnp.int32, (n, tk), 0)
            for hh, cols in enumerate(heads):
                q, do = qs[hh][r0:], dos[hh][r0:]
                ks = k_ref[krows, cols]
                l = lax.dot_general(q, ks, _NT, preferred_element_type=F32)
                lnb, e = _log_not_beta(l)
                if masked:
                    lnb = jnp.where(causal, lnb, 0.0)
                carry = jnp.sum(jnp.where(lane == hh * HEAD_DIM + kb, c_ref[r0:, :], 0.0), axis=1, keepdims=True)
                later = carry + _dot2(lnb, u_after)
                a = jnp.exp(l + lnb + later)
                if masked:
                    a = jnp.where(causal, a, 0.0)
                dv_acc[krows, cols] += lax.dot_general(a.astype(BF16), do, _TN, preferred_element_type=F32)
                g = a * lax.dot_general(do, v_ref[krows, cols], _NT, preferred_element_type=F32)
                c = gcarry_ref[hh, r0:, :] + _dot2(g, u_before)
                r = 1.0 / (1.0 + e)
                beta = jnp.where(l >= 0.0, r, e * r)
                dl = g * (1.0 - beta) - c * beta
                if masked:
                    dl = jnp.where(causal, dl, 0.0)
                dl_b = dl.astype(BF16)
                dq_acc[r0:, cols] += lax.dot_general(dl_b, ks, _NN, preferred_element_type=F32)
                dk_acc[krows, cols] += lax.dot_general(dl_b, q, _TN, preferred_element_type=F32)
                gcarry_ref[hh, r0:, :] = c[:, tk - 1:tk] + g[:, tk - 1:tk]

        def step(kb, _):
            block(kb, 0, False)
            return 0

        lax.fori_loop(0, n_sub * qi, step, 0)
        for j in range(n_sub):
            block(n_sub * qi + j, j * tk, True)
        dq_ref[...] = (dq_acc[...] * scale).astype(BF16)

        @pl.when(qi == nq - 1)
        def _():
            dk_ref[...] = dk_acc[...].astype(BF16)
            dv_ref[...] = dv_acc[...].astype(BF16)

    qspec = pl.BlockSpec((tq, 128), lambda b, hp, qi: (b * nq + qi, hp))
    kvspec = pl.BlockSpec((seq, 128), lambda b, hp, qi: (b, hp))
    return pl.pallas_call(
        body, name="attn_bwd", grid=(nb, sec, nq),
        in_specs=[qspec,
                  pl.BlockSpec((seq, 128), lambda b, hp, qi: (b, sec + hp)),
                  pl.BlockSpec((seq, 128), lambda b, hp, qi: (b, 2 * sec + hp)),
                  pl.BlockSpec((tq, 128), lambda b, hp, qi: (b * nq + qi, 3 * sec + hp)),
                  qspec,
                  pl.BlockSpec((None, tq, 128), lambda b, hp, qi: (hp, b * nq + qi, 0)),
                  qspec],
        out_specs=[qspec, kvspec, kvspec, qspec],
        out_shape=[jax.ShapeDtypeStruct((m, d), BF16)] * 4,
        scratch_shapes=[pltpu.VMEM((seq, 128), F32), pltpu.VMEM((seq, 128), F32), pltpu.VMEM((tq, 128), F32),
                        pltpu.VMEM((2, tq, 1), F32)],
        compiler_params=_params("parallel", "parallel", "arbitrary"),
    )(hb, hb, hb, h, o, carries, da)


def _conv_rows(pad_ref, w_ref, b_ref, n_rows):
    acc = jnp.broadcast_to(b_ref[...], (n_rows, b_ref.shape[1]))
    for k in range(CV_KERNEL):
        acc = acc + w_ref[k:k + 1, :] * pad_ref[pl.ds(CV_HALO - CV_KERNEL + 1 + k, n_rows), :]
    return acc


def _conv_fwd(h, conv_w, conv_b, ln_g, ln_b, nb, seq):
    m = nb * seq
    d = conv_b.shape[1]
    t = CONV_TILE
    tiles = seq // t
    hpt = t // CV_HALO

    def body(cv_ref, cg_ref, pv_ref, pg_ref, z_ref, w_ref, b_ref, g_ref, bb_ref, a_ref, pad_ref):
        first = pl.program_id(0) % tiles == 0
        pad_ref[0:CV_HALO, :] = jnp.where(first, 0.0, pv_ref[...] * _sigmoid(pg_ref[...]))
        pad_ref[CV_HALO:, :] = cv_ref[...] * _sigmoid(cg_ref[...])
        xhat, _ = _ln_stats(_conv_rows(pad_ref, w_ref, b_ref, t))
        s, _ = _silu_and_grad(xhat * g_ref[...] + bb_ref[...])
        sz, _ = _silu_and_grad(z_ref[...])
        a_ref[...] = (s * sz).astype(BF16)

    def main(c):
        return pl.BlockSpec((t, d), lambda i: (i, c))

    def prev(c):
        return pl.BlockSpec((CV_HALO, d), lambda i: (jnp.maximum(i * hpt - 1, 0), c))

    vec = pl.BlockSpec((1, d), lambda i: (0, 0))
    return pl.pallas_call(
        body, name="conv_fwd", grid=(m // t,),
        in_specs=[main(4), main(5), prev(4), prev(5), main(6),
                  pl.BlockSpec((CV_HALO, d), lambda i: (0, 0)), vec, vec, vec],
        out_specs=pl.BlockSpec((t, d), lambda i: (i, 0)),
        out_shape=jax.ShapeDtypeStruct((m, d), BF16),
        scratch_shapes=[pltpu.VMEM((CV_HALO + t, d), F32)],
        compiler_params=_params("parallel"),
    )(h, h, h, h, h, conv_w, conv_b, ln_g, ln_b)


def _conv_bwd(h, da, conv_w, conv_b, ln_g, ln_b, nb, seq):
    m = nb * seq
    d = conv_b.shape[1]
    t = CONV_TILE
    tiles = seq // t
    hpt = t // CV_HALO
    last_halo = m // CV_HALO - 1

    def body(cv_ref, cg_ref, pv_ref, pg_ref, nv_ref, ng_ref, z_ref, nz_ref, da_ref, nda_ref,
             w_ref, b_ref, g_ref, bb_ref, dh_ref, dw_ref, db_ref, dg_ref, dbb_ref, pad_ref, dc_ref):
        i = pl.program_id(0)

        @pl.when(i == 0)
        def _():
            dw_ref[...] = jnp.zeros_like(dw_ref)
            db_ref[...] = jnp.zeros_like(db_ref)
            dg_ref[...] = jnp.zeros_like(dg_ref)
            dbb_ref[...] = jnp.zeros_like(dbb_ref)

        first = i % tiles == 0
        last = i % tiles == tiles - 1
        cv = cv_ref[...]
        sg = _sigmoid(cg_ref[...])
        pad_ref[0:CV_HALO, :] = jnp.where(first, 0.0, pv_ref[...] * _sigmoid(pg_ref[...]))
        pad_ref[CV_HALO:CV_HALO + t, :] = cv * sg
        pad_ref[CV_HALO + t:, :] = nv_ref[...] * _sigmoid(ng_ref[...])

        def rows_bwd(conv_out, z, dav):
            xhat, rstd = _ln_stats(conv_out)
            s, ds = _silu_and_grad(xhat * g_ref[...] + bb_ref[...])
            sz, dsz = _silu_and_grad(z)
            dn = dav * sz * ds
            return _ln_bwd(dn * g_ref[...], xhat, rstd), dav * s * dsz, dn, xhat

        conv_all = _conv_rows(pad_ref, w_ref, b_ref, t + CV_HALO)
        dc, dz, dn, xhat = rows_bwd(conv_all[:t], z_ref[...], da_ref[...])
        dc_next, _, _, _ = rows_bwd(conv_all[t:], nz_ref[...], nda_ref[...])
        dc_ref[0:t, :] = dc
        dc_ref[t:, :] = jnp.where(last, 0.0, dc_next)
        dg_ref[...] += jnp.sum(dn * xhat, axis=0, keepdims=True)
        dbb_ref[...] += jnp.sum(dn, axis=0, keepdims=True)
        db_ref[...] += jnp.sum(dc, axis=0, keepdims=True)
        du = jnp.zeros((t, d), F32)
        for k in range(CV_KERNEL):
            dw_ref[k:k + 1, :] += jnp.sum(dc * pad_ref[pl.ds(CV_HALO - CV_KERNEL + 1 + k, t), :], axis=0, keepdims=True)
            du = du + w_ref[k:k + 1, :] * dc_ref[pl.ds(CV_KERNEL - 1 - k, t), :]
        dh_ref[:, 0:d] = (du * sg).astype(BF16)
        dh_ref[:, d:2 * d] = (du * cv * sg * (1.0 - sg)).astype(BF16)
        dh_ref[:, 2 * d:] = dz.astype(BF16)

    def main(c):
        return pl.BlockSpec((t, d), lambda i: (i, c))

    def prev(c):
        return pl.BlockSpec((CV_HALO, d), lambda i: (jnp.maximum(i * hpt - 1, 0), c))

    def nxt(c):
        return pl.BlockSpec((CV_HALO, d), lambda i: (jnp.minimum((i + 1) * hpt, last_halo), c))

    vec = pl.BlockSpec((1, d), lambda i: (0, 0))
    taps = pl.BlockSpec((CV_HALO, d), lambda i: (0, 0))
    return pl.pallas_call(
        body, name="conv_bwd", grid=(m // t,),
        in_specs=[main(4), main(5), prev(4), prev(5), nxt(4), nxt(5), main(6), nxt(6), main(0), nxt(0),
                  taps, vec, vec, vec],
        out_specs=[pl.BlockSpec((t, 3 * d), lambda i: (i, 0)), taps, vec, vec, vec],
        out_shape=[jax.ShapeDtypeStruct((m, 3 * d), BF16), jax.ShapeDtypeStruct((CV_HALO, d), F32),
                   jax.ShapeDtypeStruct((1, d), F32), jax.ShapeDtypeStruct((1, d), F32),
                   jax.ShapeDtypeStruct((1, d), F32)],
        scratch_shapes=[pltpu.VMEM((t + 2 * CV_HALO, d), F32), pltpu.VMEM((t + CV_HALO, d), F32)],
        compiler_params=_params("arbitrary"),
    )(h, h, h, h, h, h, h, h, da, da, conv_w, conv_b, ln_g, ln_b)


_HBM = pl.BlockSpec(memory_space=pltpu.HBM)


def _place():
    return lax.axis_index("x"), lax.axis_index("y"), lax.axis_index("c")


def _slot(p):
    return 4 * p[0] + 2 * p[1] + p[2]


def _all_gather(shards):
    n = len(shards)

    def body(*refs):
        ins, outs = refs[:n], refs[n:2 * n]
        send_sems, recv_sems, local_sems = refs[2 * n:]
        x, y, c = _place()
        me, sibling = (x, y, c), (x, y, 1 - c)
        chips = [(1 - x, y), (x, 1 - y), (1 - x, 1 - y)]

        def copy(a, k, block, to, src=None):
            dst = outs[a].at[_slot(block)]
            return pltpu.make_async_remote_copy(
                src_ref=dst if src is None else src, dst_ref=dst, send_sem=send_sems.at[7 * a + k],
                recv_sem=recv_sems.at[7 * a + k], device_id=to, device_id_type=MESH)

        mine = [pltpu.make_async_copy(ins[a], outs[a].at[_slot(me)], local_sems.at[a]) for a in range(n)]
        for cp in mine:
            cp.start()
        sent = []
        for a in range(n):
            sent.append(copy(a, 0, me, sibling, src=ins[a]))
            sent += [copy(a, 1 + j, me, (*chip, c), src=ins[a]) for j, chip in enumerate(chips)]
        for cp in sent:
            cp.start()
        for j, chip in enumerate(chips):
            for a in range(n):
                copy(a, 1 + j, (*chip, c), me).wait_recv()
                passed = copy(a, 4 + j, (*chip, c), sibling)
                passed.start()
                sent.append(passed)
        for a in range(n):
            copy(a, 0, sibling, me).wait_recv()
            for j, chip in enumerate(chips):
                copy(a, 4 + j, (*chip, 1 - c), me).wait_recv()
        for cp in sent:
            cp.wait_send()
        for cp in mine:
            cp.wait()

    return pl.pallas_call(
        body, name="all_gather_weights",
        in_specs=[_HBM] * n, out_specs=[_HBM] * n,
        out_shape=[jax.ShapeDtypeStruct((N_DEV,) + s.shape, s.dtype) for s in shards],
        scratch_shapes=[pltpu.SemaphoreType.DMA((7 * n,)), pltpu.SemaphoreType.DMA((7 * n,)),
                        pltpu.SemaphoreType.DMA((n,))],
    )(*shards)


def _exchange_partials(parts, whole):
    n_parts = len(parts)
    arrays = list(parts) + list(whole)
    n = len(arrays)

    def body(*refs):
        ins, outs = refs[:n], refs[n:2 * n]
        send_sems, recv_sems, local_sems = refs[2 * n:]
        x, y, c = _place()
        me = (x, y, c)

        def src_for(a, p):
            return ins[a].at[_slot(p)] if a < n_parts else ins[a]

        def copy(a, k, peer):
            return pltpu.make_async_remote_copy(
                src_ref=src_for(a, peer), dst_ref=outs[a].at[_slot(me)], send_sem=send_sems.at[7 * a + k],
                recv_sem=recv_sems.at[7 * a + k], device_id=peer, device_id_type=MESH)

        def landed(a, k, peer):
            return pltpu.make_async_remote_copy(
                src_ref=src_for(a, peer), dst_ref=outs[a].at[_slot(peer)], send_sem=send_sems.at[7 * a + k],
                recv_sem=recv_sems.at[7 * a + k], device_id=peer, device_id_type=MESH)

        peers = []
        for k in range(1, N_DEV):
            fx, fy, fc = (k >> 2) & 1, (k >> 1) & 1, k & 1
            peers.append((1 - x if fx else x, 1 - y if fy else y, 1 - c if fc else c))
        mine = [pltpu.make_async_copy(src_for(a, me), outs[a].at[_slot(me)], local_sems.at[a]) for a in range(n)]
        for cp in mine:
            cp.start()
        sent = [copy(a, k, peer) for a in range(n) for k, peer in enumerate(peers)]
        for cp in sent:
            cp.start()
        for a in range(n):
            for k, peer in enumerate(peers):
                landed(a, k, peer).wait_recv()
        for cp in sent:
            cp.wait_send()
        for cp in mine:
            cp.wait()

    out_shape = [jax.ShapeDtypeStruct(p.shape, p.dtype) for p in parts]
    out_shape += [jax.ShapeDtypeStruct((N_DEV,) + w.shape, w.dtype) for w in whole]
    return pl.pallas_call(
        body, name="exchange_grad_partials",
        in_specs=[_HBM] * n, out_specs=[_HBM] * n, out_shape=out_shape,
        scratch_shapes=[pltpu.SemaphoreType.DMA((7 * n,)), pltpu.SemaphoreType.DMA((7 * n,)),
                        pltpu.SemaphoreType.DMA((n,))],
    )(*arrays)


def _adamw(name, parts, w, mom, var, rows):
    r, c = w.shape

    def body(p_ref, w_ref, m_ref, v_ref, g_ref, d_ref, nm_ref, nv_ref):
        g = p_ref[0].astype(F32)
        for p in range(1, N_DEV):
            g = g + p_ref[p].astype(F32)
        m_new = ADAM_B1 * m_ref[...] + (1.0 - ADAM_B1) * g
        v_new = ADAM_B2 * v_ref[...] + (1.0 - ADAM_B2) * (g * g)
        m_hat = m_new / (1.0 - ADAM_B1 ** ADAM_STEP)
        v_hat = v_new / (1.0 - ADAM_B2 ** ADAM_STEP)
        g_ref[...] = g
        d_ref[...] = -ADAM_LR * (m_hat / (jnp.sqrt(v_hat) + ADAM_EPS) + ADAM_WD * w_ref[...])
        nm_ref[...] = m_new
        nv_ref[...] = v_new

    blk = pl.BlockSpec((rows, c), lambda i: (i, 0))
    return pl.pallas_call(
        body, name=name, grid=(r // rows,),
        in_specs=[pl.BlockSpec((N_DEV, rows, c), lambda i: (0, i, 0)), blk, blk, blk],
        out_specs=[blk] * 4, out_shape=[jax.ShapeDtypeStruct((r, c), F32)] * 4,
        compiler_params=_params("parallel"),
    )(parts, w, mom, var)


def kernel(x, ln_in_g, ln_in_b, w_in, w_sb_proj, conv_w, conv_b, conv_ln_g, conv_ln_b, w_cv_proj, w_out, ln_post_g, ln_post_b, loss_target, m_ln_in_g, m_ln_in_b, m_w_in, m_w_sb_proj, m_conv_w, m_conv_b, m_conv_ln_g, m_conv_ln_b, m_w_cv_proj, m_w_out, m_ln_post_g, m_ln_post_b, v_ln_in_g, v_ln_in_b, v_w_in, v_w_sb_proj, v_conv_w, v_conv_b, v_conv_ln_g, v_conv_ln_b, v_w_cv_proj, v_w_out, v_ln_post_g, v_ln_post_b):
    nb, seq, d = x.shape
    m = nb * seq
    x2d = x.reshape(m, d)
    target = loss_target.reshape(m, d)
    rs = d // N_DEV
    pad_taps = ((0, CV_HALO - CV_KERNEL), (0, 0))

    proj_shards = jnp.stack([w_sb_proj[0], w_cv_proj[0], w_out[0]]).astype(BF16)
    conv_w_shard = jnp.pad(conv_w[0], pad_taps)
    w_in_g, proj_g, conv_w_g = _all_gather([w_in[0].astype(BF16), proj_shards, conv_w_shard])
    w_sb_g = proj_g[:, 0].reshape(d, d)
    w_cv_g = proj_g[:, 1].reshape(d, d)
    w_out_g = proj_g[:, 2].reshape(d, d)
    conv_w_full = conv_w_g.transpose(1, 0, 2).reshape(CV_HALO, d)

    h0, h0b = _ln_in_fwd(x2d, ln_in_g.reshape(1, d), ln_in_b.reshape(1, d))
    h, hb = _mm_in_fwd(h0b, w_in_g)
    o, a_sb, carries = _attn_fwd(hb, h, nb, seq)
    a_cv = _conv_fwd(h, conv_w_full, conv_b, conv_ln_g, conv_ln_b, nb, seq)
    y_sb = _mm_nn("mm_sb_fwd", a_sb, w_sb_g)
    y_cv = _mm_nn("mm_cv_fwd", a_cv, w_cv_g)
    merged = _merge_fwd(h, y_sb, y_cv)
    mo = _mm_nn("mm_out_fwd", merged, w_out_g)
    loss_part, d_pre, d_pre_b, dg_post, db_post = _post_ln_loss(h0, mo, target, ln_post_g, ln_post_b)

    dw_out = _mm_tn("mm_out_bwd_w", merged, d_pre_b, BF16)
    d_merged = _mm_nt("mm_out_bwd_x", d_pre_b, w_out_g)
    dy_sb, dy_cv, d_gates = _merge_bwd(h, y_sb, y_cv, d_merged)
    dw_sb = _mm_tn("mm_sb_bwd_w", a_sb, dy_sb, BF16)
    dw_cv = _mm_tn("mm_cv_bwd_w", a_cv, dy_cv, BF16)
    da_sb = _mm_nt("mm_sb_bwd_x", dy_sb, w_sb_g)
    da_cv = _mm_nt("mm_cv_bwd_x", dy_cv, w_cv_g)
    dq, dk, dv, dz_sb = _attn_bwd(hb, h, o, carries, da_sb, nb, seq)
    d_conv3, dconv_w, dconv_b, dconv_ln_g, dconv_ln_b = _conv_bwd(
        h, da_cv, conv_w_full, conv_b, conv_ln_g, conv_ln_b, nb, seq)
    dhb = jnp.concatenate([dq, dk, dv, dz_sb, d_conv3, d_gates], axis=1)
    dw_in = _mm_in_bwd_w(h0b, dhb)
    dh_mm = _mm_in_bwd_x(dhb, w_in_g)
    dx, dg_in, db_in = _ln_in_bwd(x2d, d_pre, dh_mm, ln_in_g.reshape(1, d))

    dproj = jnp.stack([dw_sb.reshape(N_DEV, rs, d), dw_cv.reshape(N_DEV, rs, d), dw_out.reshape(N_DEV, rs, d)], axis=1)
    dconv_w_parts = dconv_w.reshape(CV_HALO, N_DEV, d // N_DEV).transpose(1, 0, 2)
    small = jnp.concatenate([dg_in, db_in, dconv_b, dconv_ln_g, dconv_ln_b, dg_post, db_post,
                             jnp.zeros((1, d), F32)], axis=0)
    r_in, r_proj, r_conv, r_small = _exchange_partials([dw_in, dproj, dconv_w_parts], [small])

    g_in, d_in, nm_in, nv_in = _adamw("adamw_w_in", r_in, w_in[0], m_w_in[0], v_w_in[0], 128)
    stack3 = lambda a, b, c: jnp.concatenate([a[0], b[0], c[0]], axis=0)
    g_pr, d_pr, nm_pr, nv_pr = _adamw(
        "adamw_proj", r_proj.reshape(N_DEV, 3 * rs, d), stack3(w_sb_proj, w_cv_proj, w_out),
        stack3(m_w_sb_proj, m_w_cv_proj, m_w_out), stack3(v_w_sb_proj, v_w_cv_proj, v_w_out), 3 * rs)
    padc = lambda a: jnp.pad(a[0], pad_taps)
    g_cw, d_cw, nm_cw, nv_cw = _adamw("adamw_conv_w", r_conv, padc(conv_w), padc(m_conv_w), padc(v_conv_w), CV_HALO)
    vecs = lambda *a: jnp.concatenate([t.reshape(1, d) for t in a] + [jnp.ones((1, d), F32)], axis=0)
    g_sm, d_sm, nm_sm, nv_sm = _adamw(
        "adamw_vectors", r_small,
        vecs(ln_in_g, ln_in_b, conv_b, conv_ln_g, conv_ln_b, ln_post_g, ln_post_b),
        vecs(m_ln_in_g, m_ln_in_b, m_conv_b, m_conv_ln_g, m_conv_ln_b, m_ln_post_g, m_ln_post_b),
        vecs(v_ln_in_g, v_ln_in_b, v_conv_b, v_conv_ln_g, v_conv_ln_b, v_ln_post_g, v_ln_post_b), 8)

    loss = lax.psum(loss_part[0, 0], ("x", "y", "c"))

    def leaves(big, pr, cw, sm):
        return (sm[0], sm[1], big[None], pr[None, 0:rs], cw[None, :CV_KERNEL], sm[2:3], sm[3:4], sm[4:5],
                pr[None, rs:2 * rs], pr[None, 2 * rs:], sm[5:6], sm[6:7])

    return (loss, dx.reshape(nb, seq, d), *leaves(g_in, g_pr, g_cw, g_sm), *leaves(d_in, d_pr, d_cw, d_sm),
            *leaves(nm_in, nm_pr, nm_cw, nm_sm), *leaves(nv_in, nv_pr, nv_cw, nv_sm))
```

```python
import functools
import math

import jax
import jax.numpy as jnp
from jax import lax
from jax.experimental import pallas as pl
from jax.experimental.pallas import tpu as pltpu

F32 = jnp.float32
BF16 = jnp.bfloat16
MESH = pl.DeviceIdType.MESH

N_DEV = 8
SB_HEADS = 16
HEAD_DIM = 64
CV_KERNEL = 31
CV_HALO = 32
LN_EPS = 1e-5
DEEPNORM_ALPHA = 2.0 ** 0.25
ADAM_LR, ADAM_B1, ADAM_B2, ADAM_EPS, ADAM_WD, ADAM_STEP = 0.001, 0.9, 0.999, 1e-08, 0.01, 10

ATT_TK = 256
ATT_TQ = 1024
ROW_TILE = 512
CONV_TILE = 256
VMEM_LIMIT = 56 * 1024 * 1024


def _params(*sem):
    return pltpu.CompilerParams(dimension_semantics=sem, vmem_limit_bytes=VMEM_LIMIT)


def _sigmoid(x):
    return 1.0 / (1.0 + jnp.exp(-x))


def _silu_and_grad(x):
    s = _sigmoid(x)
    return x * s, s * (1.0 + x * (1.0 - s))


def _ln_stats(x):
    mu = jnp.mean(x, axis=-1, keepdims=True)
    xc = x - mu
    var = jnp.mean(xc * xc, axis=-1, keepdims=True)
    rstd = lax.rsqrt(var + LN_EPS)
    return xc * rstd, rstd


def _ln_bwd(dxhat, xhat, rstd):
    m1 = jnp.mean(dxhat, axis=-1, keepdims=True)
    m2 = jnp.mean(dxhat * xhat, axis=-1, keepdims=True)
    return rstd * (dxhat - m1 - xhat * m2)


def _ln_in_fwd(x2d, g, b):
    m, d = x2d.shape

    def body(x_ref, g_ref, b_ref, h_ref, hb_ref):
        xhat, _ = _ln_stats(x_ref[...])
        y = xhat * g_ref[...] + b_ref[...]
        h_ref[...] = y
        hb_ref[...] = y.astype(BF16)

    row = pl.BlockSpec((ROW_TILE, d), lambda i: (i, 0))
    vec = pl.BlockSpec((1, d), lambda i: (0, 0))
    return pl.pallas_call(
        body, name="ln_in_fwd", grid=(m // ROW_TILE,),
        in_specs=[row, vec, vec], out_specs=[row, row],
        out_shape=[jax.ShapeDtypeStruct((m, d), F32), jax.ShapeDtypeStruct((m, d), BF16)],
        compiler_params=_params("parallel"),
    )(x2d, g, b)


def _ln_in_bwd(x2d, d_pre, dh_mm, g):
    m, d = x2d.shape

    def body(x_ref, dp_ref, dm_ref, g_ref, dx_ref, dg_ref, db_ref):
        @pl.when(pl.program_id(0) == 0)
        def _():
            dg_ref[...] = jnp.zeros_like(dg_ref)
            db_ref[...] = jnp.zeros_like(db_ref)

        xhat, rstd = _ln_stats(x_ref[...])
        dh = DEEPNORM_ALPHA * dp_ref[...] + dm_ref[...]
        dg_ref[...] += jnp.sum(dh * xhat, axis=0, keepdims=True)
        db_ref[...] += jnp.sum(dh, axis=0, keepdims=True)
        dx_ref[...] = _ln_bwd(dh * g_ref[...], xhat, rstd)

    row = pl.BlockSpec((ROW_TILE, d), lambda i: (i, 0))
    vec = pl.BlockSpec((1, d), lambda i: (0, 0))
    return pl.pallas_call(
        body, name="ln_in_bwd", grid=(m // ROW_TILE,),
        in_specs=[row, row, row, vec], out_specs=[row, vec, vec],
        out_shape=[jax.ShapeDtypeStruct((m, d), F32), jax.ShapeDtypeStruct((1, d), F32),
                   jax.ShapeDtypeStruct((1, d), F32)],
        compiler_params=_params("arbitrary"),
    )(x2d, d_pre, dh_mm, g)


def _merge_fwd(h, y_sb, y_cv):
    m, d = y_sb.shape

    def body(gs_ref, gc_ref, ys_ref, yc_ref, out_ref):
        out_ref[...] = (_sigmoid(gs_ref[...]) * ys_ref[...] + _sigmoid(gc_ref[...]) * yc_ref[...]).astype(BF16)

    row = pl.BlockSpec((ROW_TILE, d), lambda i: (i, 0))
    return pl.pallas_call(
        body, name="merge_fwd", grid=(m // ROW_TILE,),
        in_specs=[pl.BlockSpec((ROW_TILE, d), lambda i: (i, 7)), pl.BlockSpec((ROW_TILE, d), lambda i: (i, 8)), row, row],
        out_specs=row, out_shape=jax.ShapeDtypeStruct((m, d), BF16),
        compiler_params=_params("parallel"),
    )(h, h, y_sb, y_cv)


def _merge_bwd(h, y_sb, y_cv, dm):
    m, d = y_sb.shape

    def body(gs_ref, gc_ref, ys_ref, yc_ref, dm_ref, dys_ref, dyc_ref, dgate_ref):
        dmv = dm_ref[...]
        ss = _sigmoid(gs_ref[...])
        sc = _sigmoid(gc_ref[...])
        dys_ref[...] = (ss * dmv).astype(BF16)
        dyc_ref[...] = (sc * dmv).astype(BF16)
        dgate_ref[:, :d] = (dmv * ys_ref[...] * ss * (1.0 - ss)).astype(BF16)
        dgate_ref[:, d:] = (dmv * yc_ref[...] * sc * (1.0 - sc)).astype(BF16)

    row = pl.BlockSpec((ROW_TILE, d), lambda i: (i, 0))
    return pl.pallas_call(
        body, name="merge_bwd", grid=(m // ROW_TILE,),
        in_specs=[pl.BlockSpec((ROW_TILE, d), lambda i: (i, 7)), pl.BlockSpec((ROW_TILE, d), lambda i: (i, 8)), row, row, row],
        out_specs=[row, row, pl.BlockSpec((ROW_TILE, 2 * d), lambda i: (i, 0))],
        out_shape=[jax.ShapeDtypeStruct((m, d), BF16), jax.ShapeDtypeStruct((m, d), BF16),
                   jax.ShapeDtypeStruct((m, 2 * d), BF16)],
        compiler_params=_params("parallel"),
    )(h, h, y_sb, y_cv, dm)


def _post_ln_loss(h0, mo, target, g, b):
    m, d = h0.shape

    def body(h_ref, mo_ref, t_ref, g_ref, b_ref, loss_ref, dp_ref, dpb_ref, dg_ref, db_ref):
        @pl.when(pl.program_id(0) == 0)
        def _():
            loss_ref[...] = jnp.zeros_like(loss_ref)
            dg_ref[...] = jnp.zeros_like(dg_ref)
            db_ref[...] = jnp.zeros_like(db_ref)

        xhat, rstd = _ln_stats(DEEPNORM_ALPHA * h_ref[...] + mo_ref[...])
        err = xhat * g_ref[...] + b_ref[...] - t_ref[...]
        per_row = jnp.mean(err * err, axis=-1, keepdims=True)
        loss_ref[...] += 0.5 * jnp.sum(per_row, axis=0, keepdims=True)
        dy = err * (1.0 / d)
        dg_ref[...] += jnp.sum(dy * xhat, axis=0, keepdims=True)
        db_ref[...] += jnp.sum(dy, axis=0, keepdims=True)
        dp = _ln_bwd(dy * g_ref[...], xhat, rstd)
        dp_ref[...] = dp
        dpb_ref[...] = dp.astype(BF16)

    row = pl.BlockSpec((ROW_TILE, d), lambda i: (i, 0))
    vec = pl.BlockSpec((1, d), lambda i: (0, 0))
    one = pl.BlockSpec((1, 1), lambda i: (0, 0))
    return pl.pallas_call(
        body, name="post_ln_loss", grid=(m // ROW_TILE,),
        in_specs=[row, row, row, vec, vec], out_specs=[one, row, row, vec, vec],
        out_shape=[jax.ShapeDtypeStruct((1, 1), F32), jax.ShapeDtypeStruct((m, d), F32),
                   jax.ShapeDtypeStruct((m, d), BF16), jax.ShapeDtypeStruct((1, d), F32),
                   jax.ShapeDtypeStruct((1, d), F32)],
        compiler_params=_params("arbitrary"),
    )(h0, mo, target, g, b)


_NN = (((1,), (0,)), ((), ()))
_NT = (((1,), (1,)), ((), ()))
_TN = (((0,), (0,)), ((), ()))


def _after_spec():
    return pl.BlockSpec((8, 128), lambda *_: (0, 0))


def _mm(name, a, b, dims, grid, a_spec, b_spec, out_specs, out_shape, acc_shape, k_axis, after=None):
    n_k = 1 if k_axis is None else grid[k_axis]
    n_out = len(out_shape)
    extra = [] if after is None else [after]

    def body(a_ref, b_ref, *rest):
        rest = rest[len(extra):]
        outs, acc_ref = rest[:n_out], rest[n_out]
        part = lax.dot_general(a_ref[...].astype(BF16), b_ref[...].astype(BF16), dims, preferred_element_type=F32)
        if n_k == 1:
            for o in outs:
                o[...] = part.astype(o.dtype)
            return
        k = pl.program_id(k_axis)

        @pl.when(k == 0)
        def _():
            acc_ref[...] = part

        @pl.when(k > 0)
        def _():
            acc_ref[...] += part

        @pl.when(k == n_k - 1)
        def _():
            for o in outs:
                o[...] = acc_ref[...].astype(o.dtype)

    sem = tuple("arbitrary" if ax == k_axis else "parallel" for ax in range(len(grid)))
    return pl.pallas_call(
        body, name=name, grid=grid, in_specs=[a_spec, b_spec] + [_after_spec()] * len(extra),
        out_specs=out_specs, out_shape=out_shape,
        scratch_shapes=[pltpu.VMEM(acc_shape, F32)], compiler_params=_params(*sem),
    )(a, b, *extra)


def _mm_nn(name, a, b, out_dtype=F32):
    m, k = a.shape
    n = b.shape[1]
    return _mm(name, a, b, _NN, (m // ROW_TILE,), pl.BlockSpec((ROW_TILE, k), lambda i: (i, 0)),
               pl.BlockSpec((k, n), lambda i: (0, 0)), [pl.BlockSpec((ROW_TILE, n), lambda i: (i, 0))],
               [jax.ShapeDtypeStruct((m, n), out_dtype)], (8, 128), None)[0]


def _mm_nt(name, a, b, out_dtype=F32):
    m, k = a.shape
    n = b.shape[0]
    return _mm(name, a, b, _NT, (m // ROW_TILE,), pl.BlockSpec((ROW_TILE, k), lambda i: (i, 0)),
               pl.BlockSpec((n, k), lambda i: (0, 0)), [pl.BlockSpec((ROW_TILE, n), lambda i: (i, 0))],
               [jax.ShapeDtypeStruct((m, n), out_dtype)], (8, 128), None)[0]


def _mm_tn(name, a, b, out_dtype):
    m, k = a.shape
    n = b.shape[1]
    return _mm(name, a, b, _TN, (m // ROW_TILE,), pl.BlockSpec((ROW_TILE, k), lambda i: (i, 0)),
               pl.BlockSpec((ROW_TILE, n), lambda i: (i, 0)), [pl.BlockSpec((k, n), lambda i: (0, 0))],
               [jax.ShapeDtypeStruct((k, n), out_dtype)], (k, n), 0)[0]


def _mm_in_fwd(h0b, w_g):
    m, d = h0b.shape
    ns = w_g.shape[2]
    out = pl.BlockSpec((ROW_TILE, ns), lambda j, i: (i, j))
    return _mm("mm_in_fwd", h0b, w_g, _NN, (N_DEV, m // ROW_TILE), pl.BlockSpec((ROW_TILE, d), lambda j, i: (i, 0)),
               pl.BlockSpec((None, d, ns), lambda j, i: (j, 0, 0)), [out, out],
               [jax.ShapeDtypeStruct((m, N_DEV * ns), F32), jax.ShapeDtypeStruct((m, N_DEV * ns), BF16)],
               (8, 128), None)


def _mm_in_bwd_x(dhb, w_g, after):
    m = dhb.shape[0]
    _, d, ns = w_g.shape
    return _mm("mm_in_bwd_x", dhb, w_g, _NT, (m // ROW_TILE, N_DEV), pl.BlockSpec((ROW_TILE, ns), lambda i, j: (i, j)),
               pl.BlockSpec((None, d, ns), lambda i, j: (j, 0, 0)), [pl.BlockSpec((ROW_TILE, d), lambda i, j: (i, 0))],
               [jax.ShapeDtypeStruct((m, d), F32)], (ROW_TILE, d), 1, after=after)[0]


def _mm_in_bwd_w(h0b, dhb):
    m, d = h0b.shape
    ns = dhb.shape[1] // N_DEV
    return _mm("mm_in_bwd_w", h0b, dhb, _TN, (N_DEV, m // ROW_TILE), pl.BlockSpec((ROW_TILE, d), lambda j, k: (k, 0)),
               pl.BlockSpec((ROW_TILE, ns), lambda j, k: (k, j)), [pl.BlockSpec((None, d, ns), lambda j, k: (j, 0, 0))],
               [jax.ShapeDtypeStruct((N_DEV, d, ns), BF16)], (d, ns), 1)[0]


def _split_hi_lo(x):
    hi = x.astype(BF16)
    return hi, (x - hi.astype(F32)).astype(BF16)


def _dot2(x, u):
    hi, lo = _split_hi_lo(x)
    return (lax.dot_general(hi, u, _NN, preferred_element_type=F32)
            + lax.dot_general(lo, u, _NN, preferred_element_type=F32))


def _log_not_beta(l):
    e = jnp.exp(-jnp.abs(l))
    return jnp.minimum(-l, 0.0) - jnp.log(1.0 + e), e


def _attn_fwd(hb, h, nb, seq):
    m = nb * seq
    d = SB_HEADS * HEAD_DIM
    tq, tk = ATT_TQ, ATT_TK
    n_sub = tq // tk
    nq = seq // tq
    sec = d // 128
    scale = 1.0 / math.sqrt(HEAD_DIM)
    heads = [slice(hh * HEAD_DIM, (hh + 1) * HEAD_DIM) for hh in range(2)]

    def body(q_ref, k_ref, v_ref, z_ref, o_ref, a_ref, c_ref, carry_ref, acc_ref):
        qi = pl.program_id(2)
        u_after = (lax.broadcasted_iota(jnp.int32, (tk, tk), 0)
                   > lax.broadcasted_iota(jnp.int32, (tk, tk), 1)).astype(BF16)
        qs = [(q_ref[:, cols].astype(F32) * scale).astype(BF16) for cols in heads]
        carry_ref[...] = jnp.zeros_like(carry_ref)
        acc_ref[...] = jnp.zeros_like(acc_ref)
        c_ref[...] = jnp.zeros_like(c_ref)

        def block(kb, r0, masked):
            n = tq - r0
            krows = pl.ds(pl.multiple_of(kb * tk, tk), tk)
            lane = lax.broadcasted_iota(jnp.int32, (n, 128), 1)
            if masked:
                causal = lax.broadcasted_iota(jnp.int32, (n, tk), 1) < lax.broadcasted_iota(jnp.int32, (n, tk), 0)
            for hh, cols in enumerate(heads):
                carry = carry_ref[hh, r0:, :]
                l = lax.dot_general(qs[hh][r0:], k_ref[krows, cols], _NT, preferred_element_type=F32)
                lnb, _ = _log_not_beta(l)
                if masked:
                    lnb = jnp.where(causal, lnb, 0.0)
                later = carry + _dot2(lnb, u_after)
                a = jnp.exp(l + lnb + later)
                if masked:
                    a = jnp.where(causal, a, 0.0)
                acc_ref[r0:, cols] += lax.dot_general(a.astype(BF16), v_ref[krows, cols], _NN, preferred_element_type=F32)
                c_ref[r0:, :] += jnp.where(lane == hh * HEAD_DIM + kb, carry, 0.0)
                carry_ref[hh, r0:, :] = later[:, 0:1] + lnb[:, 0:1]

        for j in reversed(range(n_sub)):
            block(n_sub * qi + j, j * tk, True)

        def step(i, _):
            block(n_sub * qi - 1 - i, 0, False)
            return 0

        lax.fori_loop(0, n_sub * qi, step, 0)
        o = acc_ref[...]
        o_ref[...] = o
        silu, _ = _silu_and_grad(z_ref[...])
        a_ref[...] = (o * silu).astype(BF16)

    qspec = pl.BlockSpec((tq, 128), lambda b, hp, qi: (b * nq + qi, hp))
    return pl.pallas_call(
        body, name="attn_fwd", grid=(nb, sec, nq),
        in_specs=[qspec,
                  pl.BlockSpec((seq, 128), lambda b, hp, qi: (b, sec + hp)),
                  pl.BlockSpec((seq, 128), lambda b, hp, qi: (b, 2 * sec + hp)),
                  pl.BlockSpec((tq, 128), lambda b, hp, qi: (b * nq + qi, 3 * sec + hp))],
        out_specs=[qspec, qspec, pl.BlockSpec((None, tq, 128), lambda b, hp, qi: (hp, b * nq + qi, 0))],
        out_shape=[jax.ShapeDtypeStruct((m, d), F32), jax.ShapeDtypeStruct((m, d), BF16),
                   jax.ShapeDtypeStruct((sec, m, 128), F32)],
        scratch_shapes=[pltpu.VMEM((2, tq, 1), F32), pltpu.VMEM((tq, 128), F32)],
        compiler_params=_params("parallel", "parallel", "parallel"),
    )(hb, hb, hb, h)


def _attn_bwd(hb, h, o, carries, da, nb, seq, after):
    m = nb * seq
    d = SB_HEADS * HEAD_DIM
    tq, tk = ATT_TQ, ATT_TK
    n_sub = tq // tk
    nq = seq // tq
    sec = d // 128
    scale = 1.0 / math.sqrt(HEAD_DIM)
    heads = [slice(hh * HEAD_DIM, (hh + 1) * HEAD_DIM) for hh in range(2)]

    def body(q_ref, k_ref, v_ref, z_ref, o_ref, c_ref, da_ref, after_ref, dq_ref, dk_ref, dv_ref, dz_ref,
             dk_acc, dv_acc, dq_acc, gcarry_ref):
        qi = pl.program_id(2)

        @pl.when(qi == 0)
        def _():
            dk_acc[...] = jnp.zeros_like(dk_acc)
            dv_acc[...] = jnp.zeros_like(dv_acc)

        row = lax.broadcasted_iota(jnp.int32, (tk, tk), 0)
        col = lax.broadcasted_iota(jnp.int32, (tk, tk), 1)
        u_after = (row > col).astype(BF16)
        u_before = (row < col).astype(BF16)
        silu, dsilu = _silu_and_grad(z_ref[...])
        dav = da_ref[...]
        dz_ref[...] = (dav * o_ref[...] * dsilu).astype(BF16)
        do2 = (dav * silu).astype(BF16)
        qs = [(q_ref[:, cols].astype(F32) * scale).astype(BF16) for cols in heads]
        dos = [do2[:, cols] for cols in heads]
        dq_acc[...] = jnp.zeros_like(dq_acc)
        gcarry_ref[...] = jnp.zeros_like(gcarry_ref)

        def block(kb, r0, masked):
            n = tq - r0
            krows = pl.ds(pl.multiple_of(kb * tk, tk), tk)
            lane = lax.broadcasted_iota(jnp.int32, (n, 128), 1)
            if masked:
                causal = lax.broadcasted_iota(jnp.int32, (n, tk), 1) < lax.broadcasted_iota(jnp.int32, (n, tk), 0)
            for hh, cols in enumerate(heads):
                q, do = qs[hh][r0:], dos[hh][r0:]
                ks = k_ref[krows, cols]
                l = lax.dot_general(q, ks, _NT, preferred_element_type=F32)
                lnb, e = _log_not_beta(l)
                if masked:
                    lnb = jnp.where(causal, lnb, 0.0)
                carry = jnp.sum(jnp.where(lane == hh * HEAD_DIM + kb, c_ref[r0:, :], 0.0), axis=1, keepdims=True)
                later = carry + _dot2(lnb, u_after)
                a = jnp.exp(l + lnb + later)
                if masked:
                    a = jnp.where(causal, a, 0.0)
                dv_acc[krows, cols] += lax.dot_general(a.astype(BF16), do, _TN, preferred_element_type=F32)
                g = a * lax.dot_general(do, v_ref[krows, cols], _NT, preferred_element_type=F32)
                c = gcarry_ref[hh, r0:, :] + _dot2(g, u_before)
                r = 1.0 / (1.0 + e)
                beta = jnp.where(l >= 0.0, r, e * r)
                dl = g * (1.0 - beta) - c * beta
                if masked:
                    dl = jnp.where(causal, dl, 0.0)
                dl_b = dl.astype(BF16)
                dq_acc[r0:, cols] += lax.dot_general(dl_b, ks, _NN, preferred_element_type=F32)
                dk_acc[krows, cols] += lax.dot_general(dl_b, q, _TN, preferred_element_type=F32)
                gcarry_ref[hh, r0:, :] = c[:, tk - 1:tk] + g[:, tk - 1:tk]

        def step(kb, _):
            block(kb, 0, False)
            return 0

        lax.fori_loop(0, n_sub * qi, step, 0)
        for j in range(n_sub):
            block(n_sub * qi + j, j * tk, True)
        dq_ref[...] = (dq_acc[...] * scale).astype(BF16)

        @pl.when(qi == nq - 1)
        def _():
            dk_ref[...] = dk_acc[...].astype(BF16)
            dv_ref[...] = dv_acc[...].astype(BF16)

    qspec = pl.BlockSpec((tq, 128), lambda b, hp, qi: (b * nq + qi, hp))
    kvspec = pl.BlockSpec((seq, 128), lambda b, hp, qi: (b, hp))
    return pl.pallas_call(
        body, name="attn_bwd", grid=(nb, sec, nq),
        in_specs=[qspec,
                  pl.BlockSpec((seq, 128), lambda b, hp, qi: (b, sec + hp)),
                  pl.BlockSpec((seq, 128), lambda b, hp, qi: (b, 2 * sec + hp)),
                  pl.BlockSpec((tq, 128), lambda b, hp, qi: (b * nq + qi, 3 * sec + hp)),
                  qspec,
                  pl.BlockSpec((None, tq, 128), lambda b, hp, qi: (hp, b * nq + qi, 0)),
                  qspec, _after_spec()],
        out_specs=[qspec, kvspec, kvspec, qspec],
        out_shape=[jax.ShapeDtypeStruct((m, d), BF16)] * 4,
        scratch_shapes=[pltpu.VMEM((seq, 128), F32), pltpu.VMEM((seq, 128), F32), pltpu.VMEM((tq, 128), F32),
                        pltpu.VMEM((2, tq, 1), F32)],
        compiler_params=_params("parallel", "parallel", "arbitrary"),
    )(hb, hb, hb, h, o, carries, da, after)


SUBLANES = 8


def _by_residue(taps):
    groups = []
    for res in range(SUBLANES):
        mine = [(off - res, k) for off, k in taps if off % SUBLANES == res]
        if mine:
            groups.append((res, max(a for a, _ in mine), mine))
    return groups


def _staged(src_ref, stage_ref, res, n):
    buf = stage_ref.at[res % 2]
    if res == 0:
        return src_ref
    buf[0:n, :] = src_ref[pl.ds(res, n), :]
    return buf


def _tap_sum(src_ref, stage_ref, w_ref, n_rows, taps, acc):
    for res, reach, mine in _by_residue(taps):
        shifted = _staged(src_ref, stage_ref, res, n_rows + reach)
        for a, k in mine:
            acc = acc + w_ref[k:k + 1, :] * shifted[a:a + n_rows, :]
    return acc


_CONV_TAPS = [(CV_HALO - CV_KERNEL + 1 + k, k) for k in range(CV_KERNEL)]
_CONV_TAPS_T = [(CV_KERNEL - 1 - k, k) for k in range(CV_KERNEL)]


def _conv_rows(pad_ref, stage_ref, w_ref, b_ref, n_rows):
    bias = jnp.broadcast_to(b_ref[...], (n_rows, b_ref.shape[1]))
    return _tap_sum(pad_ref, stage_ref, w_ref, n_rows, _CONV_TAPS, bias)


def _conv_fwd(h, conv_w, conv_b, ln_g, ln_b, nb, seq):
    m = nb * seq
    d = conv_b.shape[1]
    t = CONV_TILE
    tiles = seq // t
    hpt = t // CV_HALO

    def body(cv_ref, cg_ref, pv_ref, pg_ref, z_ref, w_ref, b_ref, g_ref, bb_ref, a_ref, pad_ref, stage_ref):
        first = pl.program_id(0) % tiles == 0
        pad_ref[0:CV_HALO, :] = jnp.where(first, 0.0, pv_ref[...] * _sigmoid(pg_ref[...]))
        pad_ref[CV_HALO:, :] = cv_ref[...] * _sigmoid(cg_ref[...])
        xhat, _ = _ln_stats(_conv_rows(pad_ref, stage_ref, w_ref, b_ref, t))
        s, _ = _silu_and_grad(xhat * g_ref[...] + bb_ref[...])
        sz, _ = _silu_and_grad(z_ref[...])
        a_ref[...] = (s * sz).astype(BF16)

    def main(c):
        return pl.BlockSpec((t, d), lambda i: (i, c))

    def prev(c):
        return pl.BlockSpec((CV_HALO, d), lambda i: (jnp.maximum(i * hpt - 1, 0), c))

    vec = pl.BlockSpec((1, d), lambda i: (0, 0))
    return pl.pallas_call(
        body, name="conv_fwd", grid=(m // t,),
        in_specs=[main(4), main(5), prev(4), prev(5), main(6),
                  pl.BlockSpec((CV_HALO, d), lambda i: (0, 0)), vec, vec, vec],
        out_specs=pl.BlockSpec((t, d), lambda i: (i, 0)),
        out_shape=jax.ShapeDtypeStruct((m, d), BF16),
        scratch_shapes=[pltpu.VMEM((CV_HALO + t, d), F32), pltpu.VMEM((2, CV_HALO + t, d), F32)],
        compiler_params=_params("parallel"),
    )(h, h, h, h, h, conv_w, conv_b, ln_g, ln_b)


def _conv_bwd(h, da, conv_w, conv_b, ln_g, ln_b, nb, seq):
    m = nb * seq
    d = conv_b.shape[1]
    t = CONV_TILE
    tiles = seq // t
    hpt = t // CV_HALO
    last_halo = m // CV_HALO - 1

    def body(cv_ref, cg_ref, pv_ref, pg_ref, nv_ref, ng_ref, z_ref, nz_ref, da_ref, nda_ref,
             w_ref, b_ref, g_ref, bb_ref, dh_ref, dw_ref, db_ref, dg_ref, dbb_ref, pad_ref, dc_ref, stage_ref):
        i = pl.program_id(0)

        @pl.when(i == 0)
        def _():
            dw_ref[...] = jnp.zeros_like(dw_ref)
            db_ref[...] = jnp.zeros_like(db_ref)
            dg_ref[...] = jnp.zeros_like(dg_ref)
            dbb_ref[...] = jnp.zeros_like(dbb_ref)

        first = i % tiles == 0
        last = i % tiles == tiles - 1
        cv = cv_ref[...]
        sg = _sigmoid(cg_ref[...])
        pad_ref[0:CV_HALO, :] = jnp.where(first, 0.0, pv_ref[...] * _sigmoid(pg_ref[...]))
        pad_ref[CV_HALO:CV_HALO + t, :] = cv * sg
        pad_ref[CV_HALO + t:, :] = nv_ref[...] * _sigmoid(ng_ref[...])

        def rows_bwd(conv_out, z, dav):
            xhat, rstd = _ln_stats(conv_out)
            s, ds = _silu_and_grad(xhat * g_ref[...] + bb_ref[...])
            sz, dsz = _silu_and_grad(z)
            dn = dav * sz * ds
            return _ln_bwd(dn * g_ref[...], xhat, rstd), dav * s * dsz, dn, xhat

        conv_all = _conv_rows(pad_ref, stage_ref, w_ref, b_ref, t + CV_HALO)
        dc, dz, dn, xhat = rows_bwd(conv_all[:t], z_ref[...], da_ref[...])
        dc_next, _, _, _ = rows_bwd(conv_all[t:], nz_ref[...], nda_ref[...])
        dc_ref[0:t, :] = dc
        dc_ref[t:, :] = jnp.where(last, 0.0, dc_next)
        dg_ref[...] += jnp.sum(dn * xhat, axis=0, keepdims=True)
        dbb_ref[...] += jnp.sum(dn, axis=0, keepdims=True)
        db_ref[...] += jnp.sum(dc, axis=0, keepdims=True)
        for res, reach, mine in _by_residue(_CONV_TAPS):
            shifted = _staged(pad_ref, stage_ref, res, t + reach)
            for a, k in mine:
                dw_ref[k:k + 1, :] += jnp.sum(dc * shifted[a:a + t, :], axis=0, keepdims=True)
        du = _tap_sum(dc_ref, stage_ref, w_ref, t, _CONV_TAPS_T, jnp.zeros((t, d), F32))
        dh_ref[:, 0:d] = (du * sg).astype(BF16)
        dh_ref[:, d:2 * d] = (du * cv * sg * (1.0 - sg)).astype(BF16)
        dh_ref[:, 2 * d:] = dz.astype(BF16)

    def main(c):
        return pl.BlockSpec((t, d), lambda i: (i, c))

    def prev(c):
        return pl.BlockSpec((CV_HALO, d), lambda i: (jnp.maximum(i * hpt - 1, 0), c))

    def nxt(c):
        return pl.BlockSpec((CV_HALO, d), lambda i: (jnp.minimum((i + 1) * hpt, last_halo), c))

    vec = pl.BlockSpec((1, d), lambda i: (0, 0))
    taps = pl.BlockSpec((CV_HALO, d), lambda i: (0, 0))
    return pl.pallas_call(
        body, name="conv_bwd", grid=(m // t,),
        in_specs=[main(4), main(5), prev(4), prev(5), nxt(4), nxt(5), main(6), nxt(6), main(0), nxt(0),
                  taps, vec, vec, vec],
        out_specs=[pl.BlockSpec((t, 3 * d), lambda i: (i, 0)), taps, vec, vec, vec],
        out_shape=[jax.ShapeDtypeStruct((m, 3 * d), BF16), jax.ShapeDtypeStruct((CV_HALO, d), F32),
                   jax.ShapeDtypeStruct((1, d), F32), jax.ShapeDtypeStruct((1, d), F32),
                   jax.ShapeDtypeStruct((1, d), F32)],
        scratch_shapes=[pltpu.VMEM((t + 2 * CV_HALO, d), F32), pltpu.VMEM((t + CV_HALO, d), F32),
                        pltpu.VMEM((2, t + 2 * CV_HALO, d), F32)],
        compiler_params=_params("arbitrary"),
    )(h, h, h, h, h, h, h, h, da, da, conv_w, conv_b, ln_g, ln_b)


_HBM = pl.BlockSpec(memory_space=pltpu.HBM)


def _place():
    return lax.axis_index("x"), lax.axis_index("y"), lax.axis_index("c")


def _slot(p):
    return 4 * p[0] + 2 * p[1] + p[2]


def _all_gather(shards, after):
    n = len(shards)

    def body(*refs):
        ins, outs = refs[:n], refs[n + 1:2 * n + 1]
        send_sems, recv_sems, local_sems = refs[2 * n + 1:]
        x, y, c = _place()
        me, sibling = (x, y, c), (x, y, 1 - c)
        chips = [(1 - x, y), (x, 1 - y), (1 - x, 1 - y)]

        def copy(a, k, block, to, src=None):
            dst = outs[a].at[_slot(block)]
            return pltpu.make_async_remote_copy(
                src_ref=dst if src is None else src, dst_ref=dst, send_sem=send_sems.at[7 * a + k],
                recv_sem=recv_sems.at[7 * a + k], device_id=to, device_id_type=MESH)

        mine = [pltpu.make_async_copy(ins[a], outs[a].at[_slot(me)], local_sems.at[a]) for a in range(n)]
        for cp in mine:
            cp.start()
        sent = []
        for a in range(n):
            sent.append(copy(a, 0, me, sibling, src=ins[a]))
            sent += [copy(a, 1 + j, me, (*chip, c), src=ins[a]) for j, chip in enumerate(chips)]
        for cp in sent:
            cp.start()
        for j, chip in enumerate(chips):
            for a in range(n):
                copy(a, 1 + j, (*chip, c), me).wait_recv()
                passed = copy(a, 4 + j, (*chip, c), sibling)
                passed.start()
                sent.append(passed)
        for a in range(n):
            copy(a, 0, sibling, me).wait_recv()
            for j, chip in enumerate(chips):
                copy(a, 4 + j, (*chip, 1 - c), me).wait_recv()
        for cp in sent:
            cp.wait_send()
        for cp in mine:
            cp.wait()

    return pl.pallas_call(
        body, name="all_gather_weights",
        in_specs=[_HBM] * n + [_ANY], out_specs=[_HBM] * n,
        out_shape=[jax.ShapeDtypeStruct((N_DEV,) + s.shape, s.dtype) for s in shards],
        scratch_shapes=[pltpu.SemaphoreType.DMA((7 * n,)), pltpu.SemaphoreType.DMA((7 * n,)),
                        pltpu.SemaphoreType.DMA((n,))],
    )(*shards, after)


def _exchange_partials(parts, whole):
    n_parts = len(parts)
    arrays = list(parts) + list(whole)
    n = len(arrays)

    def body(*refs):
        ins, outs = refs[:n], refs[n:2 * n]
        send_sems, recv_sems, local_sems = refs[2 * n:]
        x, y, c = _place()
        me = (x, y, c)

        def src_for(a, p):
            return ins[a].at[_slot(p)] if a < n_parts else ins[a]

        def copy(a, k, peer):
            return pltpu.make_async_remote_copy(
                src_ref=src_for(a, peer), dst_ref=outs[a].at[_slot(me)], send_sem=send_sems.at[7 * a + k],
                recv_sem=recv_sems.at[7 * a + k], device_id=peer, device_id_type=MESH)

        def landed(a, k, peer):
            return pltpu.make_async_remote_copy(
                src_ref=src_for(a, peer), dst_ref=outs[a].at[_slot(peer)], send_sem=send_sems.at[7 * a + k],
                recv_sem=recv_sems.at[7 * a + k], device_id=peer, device_id_type=MESH)

        peers = []
        for k in range(1, N_DEV):
            fx, fy, fc = (k >> 2) & 1, (k >> 1) & 1, k & 1
            peers.append((1 - x if fx else x, 1 - y if fy else y, 1 - c if fc else c))
        mine = [pltpu.make_async_copy(src_for(a, me), outs[a].at[_slot(me)], local_sems.at[a]) for a in range(n)]
        for cp in mine:
            cp.start()
        sent = [copy(a, k, peer) for a in range(n) for k, peer in enumerate(peers)]
        for cp in sent:
            cp.start()
        for a in range(n):
            for k, peer in enumerate(peers):
                landed(a, k, peer).wait_recv()
        for cp in sent:
            cp.wait_send()
        for cp in mine:
            cp.wait()

    out_shape = [jax.ShapeDtypeStruct(p.shape, p.dtype) for p in parts]
    out_shape += [jax.ShapeDtypeStruct((N_DEV,) + w.shape, w.dtype) for w in whole]
    return pl.pallas_call(
        body, name="exchange_grad_partials",
        in_specs=[_HBM] * n, out_specs=[_HBM] * n, out_shape=out_shape,
        scratch_shapes=[pltpu.SemaphoreType.DMA((7 * n,)), pltpu.SemaphoreType.DMA((7 * n,)),
                        pltpu.SemaphoreType.DMA((n,))],
    )(*arrays)


_SEM = pl.BlockSpec(memory_space=pltpu.SEMAPHORE)
_ANY = pl.BlockSpec(memory_space=pl.ANY)
_SIDE_EFFECT = pltpu.SideEffectType.DATAFLOW_SIDE_EFFECTING


def _peers_of(x, y, c):
    peers = []
    for k in range(1, N_DEV):
        fx, fy, fc = (k >> 2) & 1, (k >> 1) & 1, k & 1
        peers.append((1 - x if fx else x, 1 - y if fy else y, 1 - c if fc else c))
    return peers


def _send_start(name, parts, whole):
    n_parts = len(parts)
    arrays = list(parts) + list(whole)
    n = len(arrays)
    lands = [lax.empty(p.shape, p.dtype) for p in parts] + [lax.empty((N_DEV,) + w.shape, w.dtype) for w in whole]

    def body(*refs):
        srcs, land = refs[:n], refs[n:2 * n]
        send_sems, recv_sems = refs[2 * n], refs[2 * n + 1]
        token = refs[-1]
        x, y, c = _place()
        for k, peer in enumerate(_peers_of(x, y, c)):
            for a in range(n):
                pltpu.make_async_remote_copy(
                    src_ref=srcs[a].at[_slot(peer)] if a < n_parts else srcs[a], dst_ref=land[a].at[_slot((x, y, c))],
                    send_sem=send_sems.at[7 * a + k], recv_sem=recv_sems.at[7 * a + k],
                    device_id=peer, device_id_type=MESH).start()
        token[...] = jnp.zeros_like(token)

    operands = [pltpu.with_memory_space_constraint(a, pltpu.HBM) for a in arrays + lands]
    out = pl.pallas_call(
        body, name=name,
        out_shape=(pltpu.SemaphoreType.DMA((7 * n,)), pltpu.SemaphoreType.DMA((7 * n,)),
                   *[pltpu.HBM(a.shape, a.dtype) for a in arrays + lands], jax.ShapeDtypeStruct((8, 128), F32)),
        in_specs=[_HBM] * (2 * n),
        out_specs=(_SEM, _SEM, *[_HBM] * (2 * n), pl.BlockSpec(memory_space=pltpu.VMEM)),
        input_output_aliases={i: 2 + i for i in range(2 * n)},
        compiler_params=pltpu.CompilerParams(has_side_effects=_SIDE_EFFECT),
    )(*operands)
    return (out[0], out[1], list(out[2:2 + n]), list(out[2 + n:2 + 2 * n]), n_parts), out[-1]


def _send_wait(name, started, after):
    send_sems, recv_sems, srcs, lands, n_parts = started
    n = len(srcs)

    def body(*refs):
        src_refs, land_refs = refs[:n], refs[n:2 * n]
        send_sems, recv_sems = refs[2 * n], refs[2 * n + 1]
        x, y, c = _place()
        for k, peer in enumerate(_peers_of(x, y, c)):
            for a in range(n):
                copy = pltpu.make_async_remote_copy(
                    src_ref=src_refs[a].at[_slot(peer)] if a < n_parts else src_refs[a],
                    dst_ref=land_refs[a].at[_slot(peer)], send_sem=send_sems.at[7 * a + k],
                    recv_sem=recv_sems.at[7 * a + k], device_id=peer, device_id_type=MESH)
                copy.wait_send()
                copy.wait_recv()

    out = pl.pallas_call(
        body, name=name,
        out_shape=[pltpu.HBM(a.shape, a.dtype) for a in srcs + lands],
        in_specs=[_HBM] * (2 * n) + [_SEM, _SEM, _ANY], out_specs=[_HBM] * (2 * n),
        input_output_aliases={i: i for i in range(2 * n)},
        compiler_params=pltpu.CompilerParams(has_side_effects=_SIDE_EFFECT),
    )(*srcs, *lands, send_sems, recv_sems, after)
    return list(out[:n]), list(out[n:])


def _with_own(landed, own, me):
    return lax.dynamic_update_index_in_dim(landed, own, me, 0)


def _adamw(name, parts, w, mom, var, rows):
    r, c = w.shape

    def body(p_ref, w_ref, m_ref, v_ref, g_ref, d_ref, nm_ref, nv_ref):
        g = p_ref[0].astype(F32)
        for p in range(1, N_DEV):
            g = g + p_ref[p].astype(F32)
        m_new = ADAM_B1 * m_ref[...] + (1.0 - ADAM_B1) * g
        v_new = ADAM_B2 * v_ref[...] + (1.0 - ADAM_B2) * (g * g)
        m_hat = m_new / (1.0 - ADAM_B1 ** ADAM_STEP)
        v_hat = v_new / (1.0 - ADAM_B2 ** ADAM_STEP)
        g_ref[...] = g
        d_ref[...] = -ADAM_LR * (m_hat / (jnp.sqrt(v_hat) + ADAM_EPS) + ADAM_WD * w_ref[...])
        nm_ref[...] = m_new
        nv_ref[...] = v_new

    blk = pl.BlockSpec((rows, c), lambda i: (i, 0))
    return pl.pallas_call(
        body, name=name, grid=(r // rows,),
        in_specs=[pl.BlockSpec((N_DEV, rows, c), lambda i: (0, i, 0)), blk, blk, blk],
        out_specs=[blk] * 4, out_shape=[jax.ShapeDtypeStruct((r, c), F32)] * 4,
        compiler_params=_params("parallel"),
    )(parts, w, mom, var)


def kernel(x, ln_in_g, ln_in_b, w_in, w_sb_proj, conv_w, conv_b, conv_ln_g, conv_ln_b, w_cv_proj, w_out, ln_post_g, ln_post_b, loss_target, m_ln_in_g, m_ln_in_b, m_w_in, m_w_sb_proj, m_conv_w, m_conv_b, m_conv_ln_g, m_conv_ln_b, m_w_cv_proj, m_w_out, m_ln_post_g, m_ln_post_b, v_ln_in_g, v_ln_in_b, v_w_in, v_w_sb_proj, v_conv_w, v_conv_b, v_conv_ln_g, v_conv_ln_b, v_w_cv_proj, v_w_out, v_ln_post_g, v_ln_post_b):
    nb, seq, d = x.shape
    m = nb * seq
    x2d = x.reshape(m, d)
    target = loss_target.reshape(m, d)
    rs = d // N_DEV
    pad_taps = ((0, CV_HALO - CV_KERNEL), (0, 0))

    proj_shards = jnp.stack([w_sb_proj[0], w_cv_proj[0], w_out[0]]).astype(BF16)
    conv_w_shard = jnp.pad(conv_w[0], pad_taps)
    me = _slot(_place())
    own = lambda parts: lax.dynamic_index_in_dim(parts, me, 0, keepdims=False)
    gather_small, started = _send_start("gather_proj_start", [], [proj_shards, conv_w_shard])
    (w_in_g,) = _all_gather([w_in[0].astype(BF16)], started)

    h0, h0b = _ln_in_fwd(x2d, ln_in_g.reshape(1, d), ln_in_b.reshape(1, d))
    h, hb = _mm_in_fwd(h0b, w_in_g)
    o, a_sb, carries = _attn_fwd(hb, h, nb, seq)
    (proj_mine, conv_w_mine), (proj_g, conv_w_g) = _send_wait("gather_proj_wait", gather_small, a_sb)
    proj_g = _with_own(proj_g, proj_mine, me)
    conv_w_g = _with_own(conv_w_g, conv_w_mine, me)
    w_sb_g = proj_g[:, 0].reshape(d, d)
    w_cv_g = proj_g[:, 1].reshape(d, d)
    w_out_g = proj_g[:, 2].reshape(d, d)
    conv_w_full = conv_w_g.transpose(1, 0, 2).reshape(CV_HALO, d)
    a_cv = _conv_fwd(h, conv_w_full, conv_b, conv_ln_g, conv_ln_b, nb, seq)
    y_sb = _mm_nn("mm_sb_fwd", a_sb, w_sb_g)
    y_cv = _mm_nn("mm_cv_fwd", a_cv, w_cv_g)
    merged = _merge_fwd(h, y_sb, y_cv)
    mo = _mm_nn("mm_out_fwd", merged, w_out_g)
    loss_part, d_pre, d_pre_b, dg_post, db_post = _post_ln_loss(h0, mo, target, ln_post_g, ln_post_b)

    dw_out = _mm_tn("mm_out_bwd_w", merged, d_pre_b, BF16)
    d_merged = _mm_nt("mm_out_bwd_x", d_pre_b, w_out_g)
    dy_sb, dy_cv, d_gates = _merge_bwd(h, y_sb, y_cv, d_merged)
    dw_sb = _mm_tn("mm_sb_bwd_w", a_sb, dy_sb, BF16)
    dw_cv = _mm_tn("mm_cv_bwd_w", a_cv, dy_cv, BF16)
    da_sb = _mm_nt("mm_sb_bwd_x", dy_sb, w_sb_g)
    da_cv = _mm_nt("mm_cv_bwd_x", dy_cv, w_cv_g)
    dproj = jnp.stack([dw_sb.reshape(N_DEV, rs, d), dw_cv.reshape(N_DEV, rs, d), dw_out.reshape(N_DEV, rs, d)], axis=1)
    grads_proj, started = _send_start("grads_proj_start", [dproj], [])
    dq, dk, dv, dz_sb = _attn_bwd(hb, h, o, carries, da_sb, nb, seq, started)
    d_conv3, dconv_w, dconv_b, dconv_ln_g, dconv_ln_b = _conv_bwd(
        h, da_cv, conv_w_full, conv_b, conv_ln_g, conv_ln_b, nb, seq)
    (dproj,), (r_proj,) = _send_wait("grads_proj_wait", grads_proj, d_conv3)
    r_proj = _with_own(r_proj, own(dproj), me)
    dhb = jnp.concatenate([dq, dk, dv, dz_sb, d_conv3, d_gates], axis=1)
    dw_in = _mm_in_bwd_w(h0b, dhb)
    dconv_w_parts = dconv_w.reshape(CV_HALO, N_DEV, d // N_DEV).transpose(1, 0, 2)
    grads_in, started = _send_start("grads_in_start", [dw_in, dconv_w_parts], [])
    dh_mm = _mm_in_bwd_x(dhb, w_in_g, started)
    dx, dg_in, db_in = _ln_in_bwd(x2d, d_pre, dh_mm, ln_in_g.reshape(1, d))
    (dw_in, dconv_w_parts), (r_in, r_conv) = _send_wait("grads_in_wait", grads_in, dx)
    r_in = _with_own(r_in, own(dw_in), me)
    r_conv = _with_own(r_conv, own(dconv_w_parts), me)

    small = jnp.concatenate([dg_in, db_in, dconv_b, dconv_ln_g, dconv_ln_b, dg_post, db_post,
                             jnp.zeros((1, d), F32)], axis=0)
    (r_small,) = _exchange_partials([], [small])

    g_in, d_in, nm_in, nv_in = _adamw("adamw_w_in", r_in, w_in[0], m_w_in[0], v_w_in[0], 128)
    stack3 = lambda a, b, c: jnp.concatenate([a[0], b[0], c[0]], axis=0)
    g_pr, d_pr, nm_pr, nv_pr = _adamw(
        "adamw_proj", r_proj.reshape(N_DEV, 3 * rs, d), stack3(w_sb_proj, w_cv_proj, w_out),
        stack3(m_w_sb_proj, m_w_cv_proj, m_w_out), stack3(v_w_sb_proj, v_w_cv_proj, v_w_out), 3 * rs)
    padc = lambda a: jnp.pad(a[0], pad_taps)
    g_cw, d_cw, nm_cw, nv_cw = _adamw("adamw_conv_w", r_conv, padc(conv_w), padc(m_conv_w), padc(v_conv_w), CV_HALO)
    vecs = lambda *a: jnp.concatenate([t.reshape(1, d) for t in a] + [jnp.ones((1, d), F32)], axis=0)
    g_sm, d_sm, nm_sm, nv_sm = _adamw(
        "adamw_vectors", r_small,
        vecs(ln_in_g, ln_in_b, conv_b, conv_ln_g, conv_ln_b, ln_post_g, ln_post_b),
        vecs(m_ln_in_g, m_ln_in_b, m_conv_b, m_conv_ln_g, m_conv_ln_b, m_ln_post_g, m_ln_post_b),
        vecs(v_ln_in_g, v_ln_in_b, v_conv_b, v_conv_ln_g, v_conv_ln_b, v_ln_post_g, v_ln_post_b), 8)

    loss = lax.psum(loss_part[0, 0], ("x", "y", "c"))

    def leaves(big, pr, cw, sm):
        return (sm[0], sm[1], big[None], pr[None, 0:rs], cw[None, :CV_KERNEL], sm[2:3], sm[3:4], sm[4:5],
                pr[None, rs:2 * rs], pr[None, 2 * rs:], sm[5:6], sm[6:7])

    return (loss, dx.reshape(nb, seq, d), *leaves(g_in, g_pr, g_cw, g_sm), *leaves(d_in, d_pr, d_cw, d_sm),
            *leaves(nm_in, nm_pr, nm_cw, nm_sm), *leaves(nv_in, nv_pr, nv_cw, nv_sm))
```

```python
import functools
import math

import jax
import jax.numpy as jnp
from jax import lax
from jax.experimental import pallas as pl
from jax.experimental.pallas import tpu as pltpu

F32 = jnp.float32
BF16 = jnp.bfloat16
MESH = pl.DeviceIdType.MESH

N_DEV = 8
SB_HEADS = 16
HEAD_DIM = 64
CV_KERNEL = 31
CV_HALO = 32
LN_EPS = 1e-5
DEEPNORM_ALPHA = 2.0 ** 0.25
ADAM_LR, ADAM_B1, ADAM_B2, ADAM_EPS, ADAM_WD, ADAM_STEP = 0.001, 0.9, 0.999, 1e-08, 0.01, 10

ATT_TK = 256
ATT_TQ = 1024
ROW_TILE = 512
IN_TILE = 1024
CONV_TILE = 256
VMEM_LIMIT = 56 * 1024 * 1024


def _params(*sem):
    return pltpu.CompilerParams(dimension_semantics=sem, vmem_limit_bytes=VMEM_LIMIT)


def _sigmoid(x):
    return 1.0 / (1.0 + jnp.exp(-x))


def _silu_and_grad(x):
    s = _sigmoid(x)
    return x * s, s * (1.0 + x * (1.0 - s))


def _ln_stats(x):
    mu = jnp.mean(x, axis=-1, keepdims=True)
    xc = x - mu
    var = jnp.mean(xc * xc, axis=-1, keepdims=True)
    rstd = lax.rsqrt(var + LN_EPS)
    return xc * rstd, rstd


def _ln_bwd(dxhat, xhat, rstd):
    m1 = jnp.mean(dxhat, axis=-1, keepdims=True)
    m2 = jnp.mean(dxhat * xhat, axis=-1, keepdims=True)
    return rstd * (dxhat - m1 - xhat * m2)


def _ln_in_fwd(x2d, g, b):
    m, d = x2d.shape

    def body(x_ref, g_ref, b_ref, h_ref, hb_ref):
        xhat, _ = _ln_stats(x_ref[...])
        y = xhat * g_ref[...] + b_ref[...]
        h_ref[...] = y
        hb_ref[...] = y.astype(BF16)

    row = pl.BlockSpec((ROW_TILE, d), lambda i: (i, 0))
    vec = pl.BlockSpec((1, d), lambda i: (0, 0))
    return pl.pallas_call(
        body, name="ln_in_fwd", grid=(m // ROW_TILE,),
        in_specs=[row, vec, vec], out_specs=[row, row],
        out_shape=[jax.ShapeDtypeStruct((m, d), F32), jax.ShapeDtypeStruct((m, d), BF16)],
        compiler_params=_params("parallel"),
    )(x2d, g, b)


def _ln_in_bwd(x2d, d_pre, dh_mm, g):
    m, d = x2d.shape

    def body(x_ref, dp_ref, dm_ref, g_ref, dx_ref, dg_ref, db_ref):
        @pl.when(pl.program_id(0) == 0)
        def _():
            dg_ref[...] = jnp.zeros_like(dg_ref)
            db_ref[...] = jnp.zeros_like(db_ref)

        xhat, rstd = _ln_stats(x_ref[...])
        dh = DEEPNORM_ALPHA * dp_ref[...] + dm_ref[...]
        dg_ref[...] += jnp.sum(dh * xhat, axis=0, keepdims=True)
        db_ref[...] += jnp.sum(dh, axis=0, keepdims=True)
        dx_ref[...] = _ln_bwd(dh * g_ref[...], xhat, rstd)

    row = pl.BlockSpec((ROW_TILE, d), lambda i: (i, 0))
    vec = pl.BlockSpec((1, d), lambda i: (0, 0))
    return pl.pallas_call(
        body, name="ln_in_bwd", grid=(m // ROW_TILE,),
        in_specs=[row, row, row, vec], out_specs=[row, vec, vec],
        out_shape=[jax.ShapeDtypeStruct((m, d), F32), jax.ShapeDtypeStruct((1, d), F32),
                   jax.ShapeDtypeStruct((1, d), F32)],
        compiler_params=_params("arbitrary"),
    )(x2d, d_pre, dh_mm, g)


def _merge_fwd(h, y_sb, y_cv):
    m, d = y_sb.shape

    def body(gs_ref, gc_ref, ys_ref, yc_ref, out_ref):
        out_ref[...] = (_sigmoid(gs_ref[...]) * ys_ref[...] + _sigmoid(gc_ref[...]) * yc_ref[...]).astype(BF16)

    row = pl.BlockSpec((ROW_TILE, d), lambda i: (i, 0))
    return pl.pallas_call(
        body, name="merge_fwd", grid=(m // ROW_TILE,),
        in_specs=[pl.BlockSpec((ROW_TILE, d), lambda i: (i, 7)), pl.BlockSpec((ROW_TILE, d), lambda i: (i, 8)), row, row],
        out_specs=row, out_shape=jax.ShapeDtypeStruct((m, d), BF16),
        compiler_params=_params("parallel"),
    )(h, h, y_sb, y_cv)


def _merge_bwd(h, y_sb, y_cv, dm):
    m, d = y_sb.shape

    def body(gs_ref, gc_ref, ys_ref, yc_ref, dm_ref, dys_ref, dyc_ref, dgate_ref):
        dmv = dm_ref[...]
        ss = _sigmoid(gs_ref[...])
        sc = _sigmoid(gc_ref[...])
        dys_ref[...] = (ss * dmv).astype(BF16)
        dyc_ref[...] = (sc * dmv).astype(BF16)
        dgate_ref[:, :d] = (dmv * ys_ref[...] * ss * (1.0 - ss)).astype(BF16)
        dgate_ref[:, d:] = (dmv * yc_ref[...] * sc * (1.0 - sc)).astype(BF16)

    row = pl.BlockSpec((ROW_TILE, d), lambda i: (i, 0))
    return pl.pallas_call(
        body, name="merge_bwd", grid=(m // ROW_TILE,),
        in_specs=[pl.BlockSpec((ROW_TILE, d), lambda i: (i, 7)), pl.BlockSpec((ROW_TILE, d), lambda i: (i, 8)), row, row, row],
        out_specs=[row, row, pl.BlockSpec((ROW_TILE, 2 * d), lambda i: (i, 0))],
        out_shape=[jax.ShapeDtypeStruct((m, d), BF16), jax.ShapeDtypeStruct((m, d), BF16),
                   jax.ShapeDtypeStruct((m, 2 * d), BF16)],
        compiler_params=_params("parallel"),
    )(h, h, y_sb, y_cv, dm)


def _post_ln_loss(h0, mo, target, g, b):
    m, d = h0.shape

    def body(h_ref, mo_ref, t_ref, g_ref, b_ref, loss_ref, dp_ref, dpb_ref, dg_ref, db_ref):
        @pl.when(pl.program_id(0) == 0)
        def _():
            loss_ref[...] = jnp.zeros_like(loss_ref)
            dg_ref[...] = jnp.zeros_like(dg_ref)
            db_ref[...] = jnp.zeros_like(db_ref)

        xhat, rstd = _ln_stats(DEEPNORM_ALPHA * h_ref[...] + mo_ref[...])
        err = xhat * g_ref[...] + b_ref[...] - t_ref[...]
        per_row = jnp.mean(err * err, axis=-1, keepdims=True)
        loss_ref[...] += 0.5 * jnp.sum(per_row, axis=0, keepdims=True)
        dy = err * (1.0 / d)
        dg_ref[...] += jnp.sum(dy * xhat, axis=0, keepdims=True)
        db_ref[...] += jnp.sum(dy, axis=0, keepdims=True)
        dp = _ln_bwd(dy * g_ref[...], xhat, rstd)
        dp_ref[...] = dp
        dpb_ref[...] = dp.astype(BF16)

    row = pl.BlockSpec((ROW_TILE, d), lambda i: (i, 0))
    vec = pl.BlockSpec((1, d), lambda i: (0, 0))
    one = pl.BlockSpec((1, 1), lambda i: (0, 0))
    return pl.pallas_call(
        body, name="post_ln_loss", grid=(m // ROW_TILE,),
        in_specs=[row, row, row, vec, vec], out_specs=[one, row, row, vec, vec],
        out_shape=[jax.ShapeDtypeStruct((1, 1), F32), jax.ShapeDtypeStruct((m, d), F32),
                   jax.ShapeDtypeStruct((m, d), BF16), jax.ShapeDtypeStruct((1, d), F32),
                   jax.ShapeDtypeStruct((1, d), F32)],
        compiler_params=_params("arbitrary"),
    )(h0, mo, target, g, b)


_NN = (((1,), (0,)), ((), ()))
_NT = (((1,), (1,)), ((), ()))
_TN = (((0,), (0,)), ((), ()))


def _after_spec():
    return pl.BlockSpec((8, 128), lambda *_: (0, 0))


def _mm(name, a, b, dims, grid, a_spec, b_spec, out_specs, out_shape, acc_shape, k_axis, after=None):
    n_k = 1 if k_axis is None else grid[k_axis]
    n_out = len(out_shape)
    extra = [] if after is None else [after]

    def body(a_ref, b_ref, *rest):
        rest = rest[len(extra):]
        outs, acc_ref = rest[:n_out], rest[n_out]
        part = lax.dot_general(a_ref[...].astype(BF16), b_ref[...].astype(BF16), dims, preferred_element_type=F32)
        if n_k == 1:
            for o in outs:
                o[...] = part.astype(o.dtype)
            return
        k = pl.program_id(k_axis)

        @pl.when(k == 0)
        def _():
            acc_ref[...] = part

        @pl.when(k > 0)
        def _():
            acc_ref[...] += part

        @pl.when(k == n_k - 1)
        def _():
            for o in outs:
                o[...] = acc_ref[...].astype(o.dtype)

    sem = tuple("arbitrary" if ax == k_axis else "parallel" for ax in range(len(grid)))
    return pl.pallas_call(
        body, name=name, grid=grid, in_specs=[a_spec, b_spec] + [_after_spec()] * len(extra),
        out_specs=out_specs, out_shape=out_shape,
        scratch_shapes=[pltpu.VMEM(acc_shape, F32)], compiler_params=_params(*sem),
    )(a, b, *extra)


def _mm_nn(name, a, b, out_dtype=F32):
    m, k = a.shape
    n = b.shape[1]
    return _mm(name, a, b, _NN, (m // ROW_TILE,), pl.BlockSpec((ROW_TILE, k), lambda i: (i, 0)),
               pl.BlockSpec((k, n), lambda i: (0, 0)), [pl.BlockSpec((ROW_TILE, n), lambda i: (i, 0))],
               [jax.ShapeDtypeStruct((m, n), out_dtype)], (8, 128), None)[0]


def _mm_nt(name, a, b, out_dtype=F32):
    m, k = a.shape
    n = b.shape[0]
    return _mm(name, a, b, _NT, (m // ROW_TILE,), pl.BlockSpec((ROW_TILE, k), lambda i: (i, 0)),
               pl.BlockSpec((n, k), lambda i: (0, 0)), [pl.BlockSpec((ROW_TILE, n), lambda i: (i, 0))],
               [jax.ShapeDtypeStruct((m, n), out_dtype)], (8, 128), None)[0]


def _mm_tn(name, a, b, out_dtype):
    m, k = a.shape
    n = b.shape[1]
    return _mm(name, a, b, _TN, (m // ROW_TILE,), pl.BlockSpec((ROW_TILE, k), lambda i: (i, 0)),
               pl.BlockSpec((ROW_TILE, n), lambda i: (i, 0)), [pl.BlockSpec((k, n), lambda i: (0, 0))],
               [jax.ShapeDtypeStruct((k, n), out_dtype)], (k, n), 0)[0]


def _mm_in_fwd(h0b, w_g):
    m, d = h0b.shape
    ns = w_g.shape[2]
    out = pl.BlockSpec((IN_TILE, ns), lambda j, i: (i, j))
    return _mm("mm_in_fwd", h0b, w_g, _NN, (N_DEV, m // IN_TILE), pl.BlockSpec((IN_TILE, d), lambda j, i: (i, 0)),
               pl.BlockSpec((None, d, ns), lambda j, i: (j, 0, 0)), [out, out],
               [jax.ShapeDtypeStruct((m, N_DEV * ns), F32), jax.ShapeDtypeStruct((m, N_DEV * ns), BF16)],
               (8, 128), None)


def _mm_in_bwd_x(dhb, w_g, after):
    m = dhb.shape[0]
    _, d, ns = w_g.shape
    return _mm("mm_in_bwd_x", dhb, w_g, _NT, (m // IN_TILE, N_DEV), pl.BlockSpec((IN_TILE, ns), lambda i, j: (i, j)),
               pl.BlockSpec((None, d, ns), lambda i, j: (j, 0, 0)), [pl.BlockSpec((IN_TILE, d), lambda i, j: (i, 0))],
               [jax.ShapeDtypeStruct((m, d), F32)], (IN_TILE, d), 1, after=after)[0]


def _mm_in_bwd_w(h0b, dhb):
    m, d = h0b.shape
    ns = dhb.shape[1] // N_DEV
    return _mm("mm_in_bwd_w", h0b, dhb, _TN, (N_DEV, m // IN_TILE), pl.BlockSpec((IN_TILE, d), lambda j, k: (k, 0)),
               pl.BlockSpec((IN_TILE, ns), lambda j, k: (k, j)), [pl.BlockSpec((None, d, ns), lambda j, k: (j, 0, 0))],
               [jax.ShapeDtypeStruct((N_DEV, d, ns), BF16)], (d, ns), 1)[0]


LOG2_E = 1.4426950408889634


def _twice(u):
    return jnp.concatenate([u, u], axis=0)


def _dot2(x, u2):
    hi = x.astype(BF16)
    lo = (x - hi.astype(F32)).astype(BF16)
    return lax.dot_general(jnp.concatenate([hi, lo], axis=1), u2, _NN, preferred_element_type=F32)


def _softplus(l):
    e = jnp.exp2(jnp.abs(l) * (-LOG2_E))
    e1 = 1.0 + e
    return jnp.maximum(l, 0.0) + jnp.log(e1), e, e1


def _attn_fwd(hb, h, nb, seq):
    m = nb * seq
    d = SB_HEADS * HEAD_DIM
    tq, tk = ATT_TQ, ATT_TK
    n_sub = tq // tk
    nq = seq // tq
    sec = d // 128
    scale = 1.0 / math.sqrt(HEAD_DIM)
    heads = [slice(hh * HEAD_DIM, (hh + 1) * HEAD_DIM) for hh in range(2)]

    def body(q_ref, k_ref, v_ref, z_ref, o_ref, a_ref, c_ref, carry_ref, acc_ref):
        qi = pl.program_id(2)
        u_after = _twice((lax.broadcasted_iota(jnp.int32, (tk, tk), 0)
                          > lax.broadcasted_iota(jnp.int32, (tk, tk), 1)).astype(BF16))
        qs = [(q_ref[:, cols].astype(F32) * scale).astype(BF16) for cols in heads]
        carry_ref[...] = jnp.zeros_like(carry_ref)
        acc_ref[...] = jnp.zeros_like(acc_ref)
        c_ref[...] = jnp.zeros_like(c_ref)

        def block(kb, r0, masked):
            n = tq - r0
            qrows = slice(r0, tq)
            krows = pl.ds(pl.multiple_of(kb * tk, tk), tk)
            lane = lax.broadcasted_iota(jnp.int32, (n, 128), 1)
            if masked:
                causal = lax.broadcasted_iota(jnp.int32, (n, tk), 1) < lax.broadcasted_iota(jnp.int32, (n, tk), 0)
            for hh, cols in enumerate(heads):
                carry = carry_ref[hh, qrows, :]
                l = lax.dot_general(qs[hh][qrows], k_ref[krows, cols], _NT, preferred_element_type=F32)
                sp, _, _ = _softplus(l)
                if masked:
                    sp = jnp.where(causal, sp, 0.0)
                after = carry + _dot2(sp, u_after)
                a = jnp.exp(l - (sp + after))
                if masked:
                    a = jnp.where(causal, a, 0.0)
                acc_ref[qrows, cols] += lax.dot_general(a.astype(BF16), v_ref[krows, cols], _NN,
                                                        preferred_element_type=F32)
                c_ref[qrows, :] += jnp.where(lane == hh * HEAD_DIM + kb, carry, 0.0)
                carry_ref[hh, qrows, :] = after[:, 0:1] + sp[:, 0:1]

        for j in reversed(range(n_sub)):
            block(n_sub * qi + j, j * tk, True)

        def step(i, _):
            block(n_sub * qi - 1 - i, 0, False)
            return 0

        lax.fori_loop(0, n_sub * qi, step, 0)
        o = acc_ref[...]
        o_ref[...] = o
        silu, _ = _silu_and_grad(z_ref[...])
        a_ref[...] = (o * silu).astype(BF16)

    qspec = pl.BlockSpec((tq, 128), lambda b, hp, qi: (b * nq + qi, hp))
    return pl.pallas_call(
        body, name="attn_fwd", grid=(nb, sec, nq),
        in_specs=[qspec,
                  pl.BlockSpec((seq, 128), lambda b, hp, qi: (b, sec + hp)),
                  pl.BlockSpec((seq, 128), lambda b, hp, qi: (b, 2 * sec + hp)),
                  pl.BlockSpec((tq, 128), lambda b, hp, qi: (b * nq + qi, 3 * sec + hp))],
        out_specs=[qspec, qspec, pl.BlockSpec((None, tq, 128), lambda b, hp, qi: (hp, b * nq + qi, 0))],
        out_shape=[jax.ShapeDtypeStruct((m, d), F32), jax.ShapeDtypeStruct((m, d), BF16),
                   jax.ShapeDtypeStruct((sec, m, 128), F32)],
        scratch_shapes=[pltpu.VMEM((2, tq, 1), F32), pltpu.VMEM((tq, 128), F32)],
        compiler_params=_params("parallel", "parallel", "parallel"),
    )(hb, hb, hb, h)


def _attn_bwd(hb, h, o, carries, da, nb, seq, after):
    m = nb * seq
    d = SB_HEADS * HEAD_DIM
    tq, tk = ATT_TQ, ATT_TK
    n_sub = tq // tk
    nq = seq // tq
    sec = d // 128
    scale = 1.0 / math.sqrt(HEAD_DIM)
    heads = [slice(hh * HEAD_DIM, (hh + 1) * HEAD_DIM) for hh in range(2)]

    def body(q_ref, k_ref, v_ref, z_ref, o_ref, c_ref, da_ref, after_ref, dq_ref, dk_ref, dv_ref, dz_ref,
             dk_acc, dv_acc, dq_acc, gcarry_ref):
        qi = pl.program_id(2)

        @pl.when(qi == 0)
        def _():
            dk_acc[...] = jnp.zeros_like(dk_acc)
            dv_acc[...] = jnp.zeros_like(dv_acc)

        row = lax.broadcasted_iota(jnp.int32, (tk, tk), 0)
        col = lax.broadcasted_iota(jnp.int32, (tk, tk), 1)
        u_after = _twice((row > col).astype(BF16))
        u_before = _twice((row < col).astype(BF16))
        silu, dsilu = _silu_and_grad(z_ref[...])
        dav = da_ref[...]
        dz_ref[...] = (dav * o_ref[...] * dsilu).astype(BF16)
        do2 = (dav * silu).astype(BF16)
        qs = [(q_ref[:, cols].astype(F32) * scale).astype(BF16) for cols in heads]
        dos = [do2[:, cols] for cols in heads]
        dq_acc[...] = jnp.zeros_like(dq_acc)
        gcarry_ref[...] = jnp.zeros_like(gcarry_ref)

        def block(kb, r0, masked):
            n = tq - r0
            qrows = slice(r0, tq)
            krows = pl.ds(pl.multiple_of(kb * tk, tk), tk)
            lane = lax.broadcasted_iota(jnp.int32, (n, 128), 1)
            if masked:
                causal = lax.broadcasted_iota(jnp.int32, (n, tk), 1) < lax.broadcasted_iota(jnp.int32, (n, tk), 0)
            for hh, cols in enumerate(heads):
                q, do = qs[hh][qrows], dos[hh][qrows]
                ks = k_ref[krows, cols]
                l = lax.dot_general(q, ks, _NT, preferred_element_type=F32)
                sp, e, e1 = _softplus(l)
                if masked:
                    sp = jnp.where(causal, sp, 0.0)
                carry = jnp.sum(jnp.where(lane == hh * HEAD_DIM + kb, c_ref[qrows, :], 0.0), axis=1, keepdims=True)
                after = carry + _dot2(sp, u_after)
                a = jnp.exp(l - (sp + after))
                if masked:
                    a = jnp.where(causal, a, 0.0)
                dv_acc[krows, cols] += lax.dot_general(a.astype(BF16), do, _TN, preferred_element_type=F32)
                g = a * lax.dot_general(do, v_ref[krows, cols], _NT, preferred_element_type=F32)
                c = gcarry_ref[hh, qrows, :] + _dot2(g, u_before)
                r = pl.reciprocal(e1, approx=True)
                beta = jnp.where(l >= 0.0, r, e * r)
                dl = g - (g + c) * beta
                if masked:
                    dl = jnp.where(causal, dl, 0.0)
                dl_b = dl.astype(BF16)
                dq_acc[qrows, cols] += lax.dot_general(dl_b, ks, _NN, preferred_element_type=F32)
                dk_acc[krows, cols] += lax.dot_general(dl_b, q, _TN, preferred_element_type=F32)
                gcarry_ref[hh, qrows, :] = c[:, tk - 1:tk] + g[:, tk - 1:tk]

        def step(kb, _):
            block(kb, 0, False)
            return 0

        lax.fori_loop(0, n_sub * qi, step, 0)
        for j in range(n_sub):
            block(n_sub * qi + j, j * tk, True)
        dq_ref[...] = (dq_acc[...] * scale).astype(BF16)

        @pl.when(qi == nq - 1)
        def _():
            dk_ref[...] = dk_acc[...].astype(BF16)
            dv_ref[...] = dv_acc[...].astype(BF16)

    qspec = pl.BlockSpec((tq, 128), lambda b, hp, qi: (b * nq + qi, hp))
    kvspec = pl.BlockSpec((seq, 128), lambda b, hp, qi: (b, hp))
    return pl.pallas_call(
        body, name="attn_bwd", grid=(nb, sec, nq),
        in_specs=[qspec,
                  pl.BlockSpec((seq, 128), lambda b, hp, qi: (b, sec + hp)),
                  pl.BlockSpec((seq, 128), lambda b, hp, qi: (b, 2 * sec + hp)),
                  pl.BlockSpec((tq, 128), lambda b, hp, qi: (b * nq + qi, 3 * sec + hp)),
                  qspec,
                  pl.BlockSpec((None, tq, 128), lambda b, hp, qi: (hp, b * nq + qi, 0)),
                  qspec, _after_spec()],
        out_specs=[qspec, kvspec, kvspec, qspec],
        out_shape=[jax.ShapeDtypeStruct((m, d), BF16)] * 4,
        scratch_shapes=[pltpu.VMEM((seq, 128), F32), pltpu.VMEM((seq, 128), F32), pltpu.VMEM((tq, 128), F32),
                        pltpu.VMEM((2, tq, 1), F32)],
        compiler_params=_params("parallel", "parallel", "arbitrary"),
    )(hb, hb, hb, h, o, carries, da, after)


SUBLANES = 8


def _by_residue(taps):
    groups = []
    for res in range(SUBLANES):
        mine = [(off - res, k) for off, k in taps if off % SUBLANES == res]
        if mine:
            groups.append((res, max(a for a, _ in mine), mine))
    return groups


def _staged(src_ref, stage_ref, res, n):
    buf = stage_ref.at[res % 2]
    if res == 0:
        return src_ref
    buf[0:n, :] = src_ref[pl.ds(res, n), :]
    return buf


def _tap_sum(src_ref, stage_ref, w_ref, n_rows, taps, acc):
    for res, reach, mine in _by_residue(taps):
        shifted = _staged(src_ref, stage_ref, res, n_rows + reach)
        for a, k in mine:
            acc = acc + w_ref[k:k + 1, :] * shifted[a:a + n_rows, :]
    return acc


_CONV_TAPS = [(CV_HALO - CV_KERNEL + 1 + k, k) for k in range(CV_KERNEL)]
_CONV_TAPS_T = [(CV_KERNEL - 1 - k, k) for k in range(CV_KERNEL)]


def _conv_rows(pad_ref, stage_ref, w_ref, b_ref, n_rows):
    bias = jnp.broadcast_to(b_ref[...], (n_rows, b_ref.shape[1]))
    return _tap_sum(pad_ref, stage_ref, w_ref, n_rows, _CONV_TAPS, bias)


def _conv_fwd(h, conv_w, conv_b, ln_g, ln_b, nb, seq):
    m = nb * seq
    d = conv_b.shape[1]
    t = CONV_TILE
    tiles = seq // t
    hpt = t // CV_HALO

    def body(cv_ref, cg_ref, pv_ref, pg_ref, z_ref, w_ref, b_ref, g_ref, bb_ref, a_ref, pad_ref, stage_ref):
        first = pl.program_id(0) % tiles == 0
        pad_ref[0:CV_HALO, :] = jnp.where(first, 0.0, pv_ref[...] * _sigmoid(pg_ref[...]))
        pad_ref[CV_HALO:, :] = cv_ref[...] * _sigmoid(cg_ref[...])
        xhat, _ = _ln_stats(_conv_rows(pad_ref, stage_ref, w_ref, b_ref, t))
        s, _ = _silu_and_grad(xhat * g_ref[...] + bb_ref[...])
        sz, _ = _silu_and_grad(z_ref[...])
        a_ref[...] = (s * sz).astype(BF16)

    def main(c):
        return pl.BlockSpec((t, d), lambda i: (i, c))

    def prev(c):
        return pl.BlockSpec((CV_HALO, d), lambda i: (jnp.maximum(i * hpt - 1, 0), c))

    vec = pl.BlockSpec((1, d), lambda i: (0, 0))
    return pl.pallas_call(
        body, name="conv_fwd", grid=(m // t,),
        in_specs=[main(4), main(5), prev(4), prev(5), main(6),
                  pl.BlockSpec((CV_HALO, d), lambda i: (0, 0)), vec, vec, vec],
        out_specs=pl.BlockSpec((t, d), lambda i: (i, 0)),
        out_shape=jax.ShapeDtypeStruct((m, d), BF16),
        scratch_shapes=[pltpu.VMEM((CV_HALO + t, d), F32), pltpu.VMEM((2, CV_HALO + t, d), F32)],
        compiler_params=_params("parallel"),
    )(h, h, h, h, h, conv_w, conv_b, ln_g, ln_b)


def _conv_bwd(h, da, conv_w, conv_b, ln_g, ln_b, nb, seq):
    m = nb * seq
    d = conv_b.shape[1]
    t = CONV_TILE
    tiles = seq // t
    hpt = t // CV_HALO
    last_halo = m // CV_HALO - 1

    def body(cv_ref, cg_ref, pv_ref, pg_ref, nv_ref, ng_ref, z_ref, nz_ref, da_ref, nda_ref,
             w_ref, b_ref, g_ref, bb_ref, dh_ref, dw_ref, db_ref, dg_ref, dbb_ref, pad_ref, dc_ref, stage_ref):
        i = pl.program_id(0)

        @pl.when(i == 0)
        def _():
            dw_ref[...] = jnp.zeros_like(dw_ref)
            db_ref[...] = jnp.zeros_like(db_ref)
            dg_ref[...] = jnp.zeros_like(dg_ref)
            dbb_ref[...] = jnp.zeros_like(dbb_ref)

        first = i % tiles == 0
        last = i % tiles == tiles - 1
        cv = cv_ref[...]
        sg = _sigmoid(cg_ref[...])
        pad_ref[0:CV_HALO, :] = jnp.where(first, 0.0, pv_ref[...] * _sigmoid(pg_ref[...]))
        pad_ref[CV_HALO:CV_HALO + t, :] = cv * sg
        pad_ref[CV_HALO + t:, :] = nv_ref[...] * _sigmoid(ng_ref[...])

        def rows_bwd(conv_out, z, dav):
            xhat, rstd = _ln_stats(conv_out)
            s, ds = _silu_and_grad(xhat * g_ref[...] + bb_ref[...])
            sz, dsz = _silu_and_grad(z)
            dn = dav * sz * ds
            return _ln_bwd(dn * g_ref[...], xhat, rstd), dav * s * dsz, dn, xhat

        conv_all = _conv_rows(pad_ref, stage_ref, w_ref, b_ref, t + CV_HALO)
        dc, dz, dn, xhat = rows_bwd(conv_all[:t], z_ref[...], da_ref[...])
        dc_next, _, _, _ = rows_bwd(conv_all[t:], nz_ref[...], nda_ref[...])
        dc_ref[0:t, :] = dc
        dc_ref[t:, :] = jnp.where(last, 0.0, dc_next)
        dg_ref[...] += jnp.sum(dn * xhat, axis=0, keepdims=True)
        dbb_ref[...] += jnp.sum(dn, axis=0, keepdims=True)
        db_ref[...] += jnp.sum(dc, axis=0, keepdims=True)
        for res, reach, mine in _by_residue(_CONV_TAPS):
            shifted = _staged(pad_ref, stage_ref, res, t + reach)
            for a, k in mine:
                dw_ref[k:k + 1, :] += jnp.sum(dc * shifted[a:a + t, :], axis=0, keepdims=True)
        du = _tap_sum(dc_ref, stage_ref, w_ref, t, _CONV_TAPS_T, jnp.zeros((t, d), F32))
        dh_ref[:, 0:d] = (du * sg).astype(BF16)
        dh_ref[:, d:2 * d] = (du * cv * sg * (1.0 - sg)).astype(BF16)
        dh_ref[:, 2 * d:] = dz.astype(BF16)

    def main(c):
        return pl.BlockSpec((t, d), lambda i: (i, c))

    def prev(c):
        return pl.BlockSpec((CV_HALO, d), lambda i: (jnp.maximum(i * hpt - 1, 0), c))

    def nxt(c):
        return pl.BlockSpec((CV_HALO, d), lambda i: (jnp.minimum((i + 1) * hpt, last_halo), c))

    vec = pl.BlockSpec((1, d), lambda i: (0, 0))
    taps = pl.BlockSpec((CV_HALO, d), lambda i: (0, 0))
    return pl.pallas_call(
        body, name="conv_bwd", grid=(m // t,),
        in_specs=[main(4), main(5), prev(4), prev(5), nxt(4), nxt(5), main(6), nxt(6), main(0), nxt(0),
                  taps, vec, vec, vec],
        out_specs=[pl.BlockSpec((t, 3 * d), lambda i: (i, 0)), taps, vec, vec, vec],
        out_shape=[jax.ShapeDtypeStruct((m, 3 * d), BF16), jax.ShapeDtypeStruct((CV_HALO, d), F32),
                   jax.ShapeDtypeStruct((1, d), F32), jax.ShapeDtypeStruct((1, d), F32),
                   jax.ShapeDtypeStruct((1, d), F32)],
        scratch_shapes=[pltpu.VMEM((t + 2 * CV_HALO, d), F32), pltpu.VMEM((t + CV_HALO, d), F32),
                        pltpu.VMEM((2, t + 2 * CV_HALO, d), F32)],
        compiler_params=_params("arbitrary"),
    )(h, h, h, h, h, h, h, h, da, da, conv_w, conv_b, ln_g, ln_b)


_HBM = pl.BlockSpec(memory_space=pltpu.HBM)


def _place():
    return lax.axis_index("x"), lax.axis_index("y"), lax.axis_index("c")


def _slot(p):
    return 4 * p[0] + 2 * p[1] + p[2]


def _all_gather(shards, after):
    n = len(shards)

    def body(*refs):
        ins, outs = refs[:n], refs[n + 1:2 * n + 1]
        send_sems, recv_sems, local_sems = refs[2 * n + 1:]
        x, y, c = _place()
        me, sibling = (x, y, c), (x, y, 1 - c)
        chips = [(1 - x, y), (x, 1 - y), (1 - x, 1 - y)]

        def copy(a, k, block, to, src=None):
            dst = outs[a].at[_slot(block)]
            return pltpu.make_async_remote_copy(
                src_ref=dst if src is None else src, dst_ref=dst, send_sem=send_sems.at[7 * a + k],
                recv_sem=recv_sems.at[7 * a + k], device_id=to, device_id_type=MESH)

        mine = [pltpu.make_async_copy(ins[a], outs[a].at[_slot(me)], local_sems.at[a]) for a in range(n)]
        for cp in mine:
            cp.start()
        sent = []
        for a in range(n):
            sent.append(copy(a, 0, me, sibling, src=ins[a]))
            sent += [copy(a, 1 + j, me, (*chip, c), src=ins[a]) for j, chip in enumerate(chips)]
        for cp in sent:
            cp.start()
        for j, chip in enumerate(chips):
            for a in range(n):
                copy(a, 1 + j, (*chip, c), me).wait_recv()
                passed = copy(a, 4 + j, (*chip, c), sibling)
                passed.start()
                sent.append(passed)
        for a in range(n):
            copy(a, 0, sibling, me).wait_recv()
            for j, chip in enumerate(chips):
                copy(a, 4 + j, (*chip, 1 - c), me).wait_recv()
        for cp in sent:
            cp.wait_send()
        for cp in mine:
            cp.wait()

    return pl.pallas_call(
        body, name="all_gather_weights",
        in_specs=[_HBM] * n + [_ANY], out_specs=[_HBM] * n,
        out_shape=[jax.ShapeDtypeStruct((N_DEV,) + s.shape, s.dtype) for s in shards],
        scratch_shapes=[pltpu.SemaphoreType.DMA((7 * n,)), pltpu.SemaphoreType.DMA((7 * n,)),
                        pltpu.SemaphoreType.DMA((n,))],
    )(*shards, after)


def _exchange_partials(parts, whole):
    n_parts = len(parts)
    arrays = list(parts) + list(whole)
    n = len(arrays)

    def body(*refs):
        ins, outs = refs[:n], refs[n:2 * n]
        send_sems, recv_sems, local_sems = refs[2 * n:]
        x, y, c = _place()
        me = (x, y, c)

        def src_for(a, p):
            return ins[a].at[_slot(p)] if a < n_parts else ins[a]

        def copy(a, k, peer):
            return pltpu.make_async_remote_copy(
                src_ref=src_for(a, peer), dst_ref=outs[a].at[_slot(me)], send_sem=send_sems.at[7 * a + k],
                recv_sem=recv_sems.at[7 * a + k], device_id=peer, device_id_type=MESH)

        def landed(a, k, peer):
            return pltpu.make_async_remote_copy(
                src_ref=src_for(a, peer), dst_ref=outs[a].at[_slot(peer)], send_sem=send_sems.at[7 * a + k],
                recv_sem=recv_sems.at[7 * a + k], device_id=peer, device_id_type=MESH)

        peers = []
        for k in range(1, N_DEV):
            fx, fy, fc = (k >> 2) & 1, (k >> 1) & 1, k & 1
            peers.append((1 - x if fx else x, 1 - y if fy else y, 1 - c if fc else c))
        mine = [pltpu.make_async_copy(src_for(a, me), outs[a].at[_slot(me)], local_sems.at[a]) for a in range(n)]
        for cp in mine:
            cp.start()
        sent = [copy(a, k, peer) for a in range(n) for k, peer in enumerate(peers)]
        for cp in sent:
            cp.start()
        for a in range(n):
            for k, peer in enumerate(peers):
                landed(a, k, peer).wait_recv()
        for cp in sent:
            cp.wait_send()
        for cp in mine:
            cp.wait()

    out_shape = [jax.ShapeDtypeStruct(p.shape, p.dtype) for p in parts]
    out_shape += [jax.ShapeDtypeStruct((N_DEV,) + w.shape, w.dtype) for w in whole]
    return pl.pallas_call(
        body, name="exchange_grad_partials",
        in_specs=[_HBM] * n, out_specs=[_HBM] * n, out_shape=out_shape,
        scratch_shapes=[pltpu.SemaphoreType.DMA((7 * n,)), pltpu.SemaphoreType.DMA((7 * n,)),
                        pltpu.SemaphoreType.DMA((n,))],
    )(*arrays)


_SEM = pl.BlockSpec(memory_space=pltpu.SEMAPHORE)
_ANY = pl.BlockSpec(memory_space=pl.ANY)
_SIDE_EFFECT = pltpu.SideEffectType.DATAFLOW_SIDE_EFFECTING


def _peers_of(x, y, c):
    peers = []
    for k in range(1, N_DEV):
        fx, fy, fc = (k >> 2) & 1, (k >> 1) & 1, k & 1
        peers.append((1 - x if fx else x, 1 - y if fy else y, 1 - c if fc else c))
    return peers


def _send_start(name, parts, whole):
    n_parts = len(parts)
    arrays = list(parts) + list(whole)
    n = len(arrays)
    lands = [lax.empty(p.shape, p.dtype) for p in parts] + [lax.empty((N_DEV,) + w.shape, w.dtype) for w in whole]

    def body(*refs):
        srcs, land = refs[:n], refs[n:2 * n]
        send_sems, recv_sems = refs[2 * n], refs[2 * n + 1]
        token = refs[-1]
        x, y, c = _place()
        for k, peer in enumerate(_peers_of(x, y, c)):
            for a in range(n):
                pltpu.make_async_remote_copy(
                    src_ref=srcs[a].at[_slot(peer)] if a < n_parts else srcs[a], dst_ref=land[a].at[_slot((x, y, c))],
                    send_sem=send_sems.at[7 * a + k], recv_sem=recv_sems.at[7 * a + k],
                    device_id=peer, device_id_type=MESH).start()
        token[...] = jnp.zeros_like(token)

    operands = [pltpu.with_memory_space_constraint(a, pltpu.HBM) for a in arrays + lands]
    out = pl.pallas_call(
        body, name=name,
        out_shape=(pltpu.SemaphoreType.DMA((7 * n,)), pltpu.SemaphoreType.DMA((7 * n,)),
                   *[pltpu.HBM(a.shape, a.dtype) for a in arrays + lands], jax.ShapeDtypeStruct((8, 128), F32)),
        in_specs=[_HBM] * (2 * n),
        out_specs=(_SEM, _SEM, *[_HBM] * (2 * n), pl.BlockSpec(memory_space=pltpu.VMEM)),
        input_output_aliases={i: 2 + i for i in range(2 * n)},
        compiler_params=pltpu.CompilerParams(has_side_effects=_SIDE_EFFECT),
    )(*operands)
    return (out[0], out[1], list(out[2:2 + n]), list(out[2 + n:2 + 2 * n]), n_parts), out[-1]


def _send_wait(name, started, after):
    send_sems, recv_sems, srcs, lands, n_parts = started
    n = len(srcs)

    def body(*refs):
        src_refs, land_refs = refs[:n], refs[n:2 * n]
        send_sems, recv_sems = refs[2 * n], refs[2 * n + 1]
        x, y, c = _place()
        for k, peer in enumerate(_peers_of(x, y, c)):
            for a in range(n):
                copy = pltpu.make_async_remote_copy(
                    src_ref=src_refs[a].at[_slot(peer)] if a < n_parts else src_refs[a],
                    dst_ref=land_refs[a].at[_slot(peer)], send_sem=send_sems.at[7 * a + k],
                    recv_sem=recv_sems.at[7 * a + k], device_id=peer, device_id_type=MESH)
                copy.wait_send()
                copy.wait_recv()

    out = pl.pallas_call(
        body, name=name,
        out_shape=[pltpu.HBM(a.shape, a.dtype) for a in srcs + lands],
        in_specs=[_HBM] * (2 * n) + [_SEM, _SEM, _ANY], out_specs=[_HBM] * (2 * n),
        input_output_aliases={i: i for i in range(2 * n)},
        compiler_params=pltpu.CompilerParams(has_side_effects=_SIDE_EFFECT),
    )(*srcs, *lands, send_sems, recv_sems, after)
    return list(out[:n]), list(out[n:])


def _with_own(landed, own, me):
    return lax.dynamic_update_index_in_dim(landed, own, me, 0)


def _adamw(name, parts, w, mom, var, rows):
    r, c = w.shape

    def body(p_ref, w_ref, m_ref, v_ref, g_ref, d_ref, nm_ref, nv_ref):
        g = p_ref[0].astype(F32)
        for p in range(1, N_DEV):
            g = g + p_ref[p].astype(F32)
        m_new = ADAM_B1 * m_ref[...] + (1.0 - ADAM_B1) * g
        v_new = ADAM_B2 * v_ref[...] + (1.0 - ADAM_B2) * (g * g)
        m_hat = m_new / (1.0 - ADAM_B1 ** ADAM_STEP)
        v_hat = v_new / (1.0 - ADAM_B2 ** ADAM_STEP)
        g_ref[...] = g
        d_ref[...] = -ADAM_LR * (m_hat / (jnp.sqrt(v_hat) + ADAM_EPS) + ADAM_WD * w_ref[...])
        nm_ref[...] = m_new
        nv_ref[...] = v_new

    blk = pl.BlockSpec((rows, c), lambda i: (i, 0))
    return pl.pallas_call(
        body, name=name, grid=(r // rows,),
        in_specs=[pl.BlockSpec((N_DEV, rows, c), lambda i: (0, i, 0)), blk, blk, blk],
        out_specs=[blk] * 4, out_shape=[jax.ShapeDtypeStruct((r, c), F32)] * 4,
        compiler_params=_params("parallel"),
    )(parts, w, mom, var)


def kernel(x, ln_in_g, ln_in_b, w_in, w_sb_proj, conv_w, conv_b, conv_ln_g, conv_ln_b, w_cv_proj, w_out, ln_post_g, ln_post_b, loss_target, m_ln_in_g, m_ln_in_b, m_w_in, m_w_sb_proj, m_conv_w, m_conv_b, m_conv_ln_g, m_conv_ln_b, m_w_cv_proj, m_w_out, m_ln_post_g, m_ln_post_b, v_ln_in_g, v_ln_in_b, v_w_in, v_w_sb_proj, v_conv_w, v_conv_b, v_conv_ln_g, v_conv_ln_b, v_w_cv_proj, v_w_out, v_ln_post_g, v_ln_post_b):
    nb, seq, d = x.shape
    m = nb * seq
    x2d = x.reshape(m, d)
    target = loss_target.reshape(m, d)
    rs = d // N_DEV
    pad_taps = ((0, CV_HALO - CV_KERNEL), (0, 0))

    proj_shards = jnp.stack([w_sb_proj[0], w_cv_proj[0], w_out[0]]).astype(BF16)
    conv_w_shard = jnp.pad(conv_w[0], pad_taps)
    me = _slot(_place())
    own = lambda parts: lax.dynamic_index_in_dim(parts, me, 0, keepdims=False)
    gather_small, started = _send_start("gather_proj_start", [], [proj_shards, conv_w_shard])
    (w_in_g,) = _all_gather([w_in[0].astype(BF16)], started)

    h0, h0b = _ln_in_fwd(x2d, ln_in_g.reshape(1, d), ln_in_b.reshape(1, d))
    h, hb = _mm_in_fwd(h0b, w_in_g)
    o, a_sb, carries = _attn_fwd(hb, h, nb, seq)
    (proj_mine, conv_w_mine), (proj_g, conv_w_g) = _send_wait("gather_proj_wait", gather_small, a_sb)
    proj_g = _with_own(proj_g, proj_mine, me)
    conv_w_g = _with_own(conv_w_g, conv_w_mine, me)
    w_sb_g = proj_g[:, 0].reshape(d, d)
    w_cv_g = proj_g[:, 1].reshape(d, d)
    w_out_g = proj_g[:, 2].reshape(d, d)
    conv_w_full = conv_w_g.transpose(1, 0, 2).reshape(CV_HALO, d)
    a_cv = _conv_fwd(h, conv_w_full, conv_b, conv_ln_g, conv_ln_b, nb, seq)
    y_sb = _mm_nn("mm_sb_fwd", a_sb, w_sb_g)
    y_cv = _mm_nn("mm_cv_fwd", a_cv, w_cv_g)
    merged = _merge_fwd(h, y_sb, y_cv)
    mo = _mm_nn("mm_out_fwd", merged, w_out_g)
    loss_part, d_pre, d_pre_b, dg_post, db_post = _post_ln_loss(h0, mo, target, ln_post_g, ln_post_b)

    dw_out = _mm_tn("mm_out_bwd_w", merged, d_pre_b, BF16)
    d_merged = _mm_nt("mm_out_bwd_x", d_pre_b, w_out_g)
    dy_sb, dy_cv, d_gates = _merge_bwd(h, y_sb, y_cv, d_merged)
    dw_sb = _mm_tn("mm_sb_bwd_w", a_sb, dy_sb, BF16)
    dw_cv = _mm_tn("mm_cv_bwd_w", a_cv, dy_cv, BF16)
    da_sb = _mm_nt("mm_sb_bwd_x", dy_sb, w_sb_g)
    da_cv = _mm_nt("mm_cv_bwd_x", dy_cv, w_cv_g)
    dproj = jnp.stack([dw_sb.reshape(N_DEV, rs, d), dw_cv.reshape(N_DEV, rs, d), dw_out.reshape(N_DEV, rs, d)], axis=1)
    grads_proj, started = _send_start("grads_proj_start", [dproj], [])
    dq, dk, dv, dz_sb = _attn_bwd(hb, h, o, carries, da_sb, nb, seq, started)
    d_conv3, dconv_w, dconv_b, dconv_ln_g, dconv_ln_b = _conv_bwd(
        h, da_cv, conv_w_full, conv_b, conv_ln_g, conv_ln_b, nb, seq)
    (dproj,), (r_proj,) = _send_wait("grads_proj_wait", grads_proj, d_conv3)
    r_proj = _with_own(r_proj, own(dproj), me)
    dhb = jnp.concatenate([dq, dk, dv, dz_sb, d_conv3, d_gates], axis=1)
    dw_in = _mm_in_bwd_w(h0b, dhb)
    dconv_w_parts = dconv_w.reshape(CV_HALO, N_DEV, d // N_DEV).transpose(1, 0, 2)
    grads_in, started = _send_start("grads_in_start", [dw_in, dconv_w_parts], [])
    dh_mm = _mm_in_bwd_x(dhb, w_in_g, started)
    dx, dg_in, db_in = _ln_in_bwd(x2d, d_pre, dh_mm, ln_in_g.reshape(1, d))
    (dw_in, dconv_w_parts), (r_in, r_conv) = _send_wait("grads_in_wait", grads_in, dx)
    r_in = _with_own(r_in, own(dw_in), me)
    r_conv = _with_own(r_conv, own(dconv_w_parts), me)

    small = jnp.concatenate([dg_in, db_in, dconv_b, dconv_ln_g, dconv_ln_b, dg_post, db_post,
                             jnp.zeros((1, d), F32)], axis=0)
    (r_small,) = _exchange_partials([], [small])

    g_in, d_in, nm_in, nv_in = _adamw("adamw_w_in", r_in, w_in[0], m_w_in[0], v_w_in[0], 128)
    stack3 = lambda a, b, c: jnp.concatenate([a[0], b[0], c[0]], axis=0)
    g_pr, d_pr, nm_pr, nv_pr = _adamw(
        "adamw_proj", r_proj.reshape(N_DEV, 3 * rs, d), stack3(w_sb_proj, w_cv_proj, w_out),
        stack3(m_w_sb_proj, m_w_cv_proj, m_w_out), stack3(v_w_sb_proj, v_w_cv_proj, v_w_out), 3 * rs)
    padc = lambda a: jnp.pad(a[0], pad_taps)
    g_cw, d_cw, nm_cw, nv_cw = _adamw("adamw_conv_w", r_conv, padc(conv_w), padc(m_conv_w), padc(v_conv_w), CV_HALO)
    vecs = lambda *a: jnp.concatenate([t.reshape(1, d) for t in a] + [jnp.ones((1, d), F32)], axis=0)
    g_sm, d_sm, nm_sm, nv_sm = _adamw(
        "adamw_vectors", r_small,
        vecs(ln_in_g, ln_in_b, conv_b, conv_ln_g, conv_ln_b, ln_post_g, ln_post_b),
        vecs(m_ln_in_g, m_ln_in_b, m_conv_b, m_conv_ln_g, m_conv_ln_b, m_ln_post_g, m_ln_post_b),
        vecs(v_ln_in_g, v_ln_in_b, v_conv_b, v_conv_ln_g, v_conv_ln_b, v_ln_post_g, v_ln_post_b), 8)

    loss = lax.psum(loss_part[0, 0], ("x", "y", "c"))

    def leaves(big, pr, cw, sm):
        return (sm[0], sm[1], big[None], pr[None, 0:rs], cw[None, :CV_KERNEL], sm[2:3], sm[3:4], sm[4:5],
                pr[None, rs:2 * rs], pr[None, 2 * rs:], sm[5:6], sm[6:7])

    return (loss, dx.reshape(nb, seq, d), *leaves(g_in, g_pr, g_cw, g_sm), *leaves(d_in, d_pr, d_cw, d_sm),
            *leaves(nm_in, nm_pr, nm_cw, nm_sm), *leaves(nv_in, nv_pr, nv_cw, nv_sm))
```

```python
import functools
import math

import jax
import jax.numpy as jnp
from jax import lax
from jax.experimental import pallas as pl
from jax.experimental.pallas import tpu as pltpu

F32 = jnp.float32
BF16 = jnp.bfloat16
MESH = pl.DeviceIdType.MESH

N_DEV = 8
SB_HEADS = 16
HEAD_DIM = 64
CV_KERNEL = 31
CV_HALO = 32
LN_EPS = 1e-5
DEEPNORM_ALPHA = 2.0 ** 0.25
ADAM_LR, ADAM_B1, ADAM_B2, ADAM_EPS, ADAM_WD, ADAM_STEP = 0.001, 0.9, 0.999, 1e-08, 0.01, 10

ATT_TK = 256
ATT_TQ = 1024
ROW_TILE = 512
IN_TILE = 1024
CONV_TILE = 256
VMEM_LIMIT = 56 * 1024 * 1024


def _params(*sem):
    return pltpu.CompilerParams(dimension_semantics=sem, vmem_limit_bytes=VMEM_LIMIT)


def _sigmoid(x):
    return 1.0 / (1.0 + jnp.exp(-x))


def _silu_and_grad(x):
    s = _sigmoid(x)
    return x * s, s * (1.0 + x * (1.0 - s))


def _ln_stats(x):
    mu = jnp.mean(x, axis=-1, keepdims=True)
    xc = x - mu
    var = jnp.mean(xc * xc, axis=-1, keepdims=True)
    rstd = lax.rsqrt(var + LN_EPS)
    return xc * rstd, rstd


def _ln_bwd(dxhat, xhat, rstd):
    m1 = jnp.mean(dxhat, axis=-1, keepdims=True)
    m2 = jnp.mean(dxhat * xhat, axis=-1, keepdims=True)
    return rstd * (dxhat - m1 - xhat * m2)


def _ln_in_fwd(x2d, g, b):
    m, d = x2d.shape

    def body(x_ref, g_ref, b_ref, h_ref, hb_ref):
        xhat, _ = _ln_stats(x_ref[...])
        y = xhat * g_ref[...] + b_ref[...]
        h_ref[...] = y
        hb_ref[...] = y.astype(BF16)

    row = pl.BlockSpec((ROW_TILE, d), lambda i: (i, 0))
    vec = pl.BlockSpec((1, d), lambda i: (0, 0))
    return pl.pallas_call(
        body, name="ln_in_fwd", grid=(m // ROW_TILE,),
        in_specs=[row, vec, vec], out_specs=[row, row],
        out_shape=[jax.ShapeDtypeStruct((m, d), F32), jax.ShapeDtypeStruct((m, d), BF16)],
        compiler_params=_params("parallel"),
    )(x2d, g, b)


def _ln_in_bwd(x2d, d_pre, dh_mm, g):
    m, d = x2d.shape

    def body(x_ref, dp_ref, dm_ref, g_ref, dx_ref, dg_ref, db_ref):
        @pl.when(pl.program_id(0) == 0)
        def _():
            dg_ref[...] = jnp.zeros_like(dg_ref)
            db_ref[...] = jnp.zeros_like(db_ref)

        xhat, rstd = _ln_stats(x_ref[...])
        dh = DEEPNORM_ALPHA * dp_ref[...] + dm_ref[...]
        dg_ref[...] += jnp.sum(dh * xhat, axis=0, keepdims=True)
        db_ref[...] += jnp.sum(dh, axis=0, keepdims=True)
        dx_ref[...] = _ln_bwd(dh * g_ref[...], xhat, rstd)

    row = pl.BlockSpec((ROW_TILE, d), lambda i: (i, 0))
    vec = pl.BlockSpec((1, d), lambda i: (0, 0))
    return pl.pallas_call(
        body, name="ln_in_bwd", grid=(m // ROW_TILE,),
        in_specs=[row, row, row, vec], out_specs=[row, vec, vec],
        out_shape=[jax.ShapeDtypeStruct((m, d), F32), jax.ShapeDtypeStruct((1, d), F32),
                   jax.ShapeDtypeStruct((1, d), F32)],
        compiler_params=_params("arbitrary"),
    )(x2d, d_pre, dh_mm, g)


def _merge_fwd(h, y_sb, y_cv):
    m, d = y_sb.shape

    def body(gs_ref, gc_ref, ys_ref, yc_ref, out_ref):
        out_ref[...] = (_sigmoid(gs_ref[...]) * ys_ref[...] + _sigmoid(gc_ref[...]) * yc_ref[...]).astype(BF16)

    row = pl.BlockSpec((ROW_TILE, d), lambda i: (i, 0))
    return pl.pallas_call(
        body, name="merge_fwd", grid=(m // ROW_TILE,),
        in_specs=[pl.BlockSpec((ROW_TILE, d), lambda i: (i, 7)), pl.BlockSpec((ROW_TILE, d), lambda i: (i, 8)), row, row],
        out_specs=row, out_shape=jax.ShapeDtypeStruct((m, d), BF16),
        compiler_params=_params("parallel"),
    )(h, h, y_sb, y_cv)


def _merge_bwd(h, y_sb, y_cv, dm):
    m, d = y_sb.shape

    def body(gs_ref, gc_ref, ys_ref, yc_ref, dm_ref, dys_ref, dyc_ref, dgate_ref):
        dmv = dm_ref[...]
        ss = _sigmoid(gs_ref[...])
        sc = _sigmoid(gc_ref[...])
        dys_ref[...] = (ss * dmv).astype(BF16)
        dyc_ref[...] = (sc * dmv).astype(BF16)
        dgate_ref[:, :d] = (dmv * ys_ref[...] * ss * (1.0 - ss)).astype(BF16)
        dgate_ref[:, d:] = (dmv * yc_ref[...] * sc * (1.0 - sc)).astype(BF16)

    row = pl.BlockSpec((ROW_TILE, d), lambda i: (i, 0))
    return pl.pallas_call(
        body, name="merge_bwd", grid=(m // ROW_TILE,),
        in_specs=[pl.BlockSpec((ROW_TILE, d), lambda i: (i, 7)), pl.BlockSpec((ROW_TILE, d), lambda i: (i, 8)), row, row, row],
        out_specs=[row, row, pl.BlockSpec((ROW_TILE, 2 * d), lambda i: (i, 0))],
        out_shape=[jax.ShapeDtypeStruct((m, d), BF16), jax.ShapeDtypeStruct((m, d), BF16),
                   jax.ShapeDtypeStruct((m, 2 * d), BF16)],
        compiler_params=_params("parallel"),
    )(h, h, y_sb, y_cv, dm)


def _post_ln_loss(h0, mo, target, g, b):
    m, d = h0.shape

    def body(h_ref, mo_ref, t_ref, g_ref, b_ref, loss_ref, dp_ref, dpb_ref, dg_ref, db_ref):
        @pl.when(pl.program_id(0) == 0)
        def _():
            loss_ref[...] = jnp.zeros_like(loss_ref)
            dg_ref[...] = jnp.zeros_like(dg_ref)
            db_ref[...] = jnp.zeros_like(db_ref)

        xhat, rstd = _ln_stats(DEEPNORM_ALPHA * h_ref[...] + mo_ref[...])
        err = xhat * g_ref[...] + b_ref[...] - t_ref[...]
        per_row = jnp.mean(err * err, axis=-1, keepdims=True)
        loss_ref[...] += 0.5 * jnp.sum(per_row, axis=0, keepdims=True)
        dy = err * (1.0 / d)
        dg_ref[...] += jnp.sum(dy * xhat, axis=0, keepdims=True)
        db_ref[...] += jnp.sum(dy, axis=0, keepdims=True)
        dp = _ln_bwd(dy * g_ref[...], xhat, rstd)
        dp_ref[...] = dp
        dpb_ref[...] = dp.astype(BF16)

    row = pl.BlockSpec((ROW_TILE, d), lambda i: (i, 0))
    vec = pl.BlockSpec((1, d), lambda i: (0, 0))
    one = pl.BlockSpec((1, 1), lambda i: (0, 0))
    return pl.pallas_call(
        body, name="post_ln_loss", grid=(m // ROW_TILE,),
        in_specs=[row, row, row, vec, vec], out_specs=[one, row, row, vec, vec],
        out_shape=[jax.ShapeDtypeStruct((1, 1), F32), jax.ShapeDtypeStruct((m, d), F32),
                   jax.ShapeDtypeStruct((m, d), BF16), jax.ShapeDtypeStruct((1, d), F32),
                   jax.ShapeDtypeStruct((1, d), F32)],
        compiler_params=_params("arbitrary"),
    )(h0, mo, target, g, b)


_NN = (((1,), (0,)), ((), ()))
_NT = (((1,), (1,)), ((), ()))
_TN = (((0,), (0,)), ((), ()))


def _after_spec():
    return pl.BlockSpec((8, 128), lambda *_: (0, 0))


def _mm(name, a, b, dims, grid, a_spec, b_spec, out_specs, out_shape, acc_shape, k_axis, after=None):
    n_k = 1 if k_axis is None else grid[k_axis]
    n_out = len(out_shape)
    extra = [] if after is None else [after]

    def body(a_ref, b_ref, *rest):
        rest = rest[len(extra):]
        outs, acc_ref = rest[:n_out], rest[n_out]
        part = lax.dot_general(a_ref[...].astype(BF16), b_ref[...].astype(BF16), dims, preferred_element_type=F32)
        if n_k == 1:
            for o in outs:
                o[...] = part.astype(o.dtype)
            return
        k = pl.program_id(k_axis)

        @pl.when(k == 0)
        def _():
            acc_ref[...] = part

        @pl.when(k > 0)
        def _():
            acc_ref[...] += part

        @pl.when(k == n_k - 1)
        def _():
            for o in outs:
                o[...] = acc_ref[...].astype(o.dtype)

    sem = tuple("arbitrary" if ax == k_axis else "parallel" for ax in range(len(grid)))
    return pl.pallas_call(
        body, name=name, grid=grid, in_specs=[a_spec, b_spec] + [_after_spec()] * len(extra),
        out_specs=out_specs, out_shape=out_shape,
        scratch_shapes=[pltpu.VMEM(acc_shape, F32)], compiler_params=_params(*sem),
    )(a, b, *extra)


def _mm_nn(name, a, b, out_dtype=F32):
    m, k = a.shape
    n = b.shape[1]
    return _mm(name, a, b, _NN, (m // ROW_TILE,), pl.BlockSpec((ROW_TILE, k), lambda i: (i, 0)),
               pl.BlockSpec((k, n), lambda i: (0, 0)), [pl.BlockSpec((ROW_TILE, n), lambda i: (i, 0))],
               [jax.ShapeDtypeStruct((m, n), out_dtype)], (8, 128), None)[0]


def _mm_nt(name, a, b, out_dtype=F32):
    m, k = a.shape
    n = b.shape[0]
    return _mm(name, a, b, _NT, (m // ROW_TILE,), pl.BlockSpec((ROW_TILE, k), lambda i: (i, 0)),
               pl.BlockSpec((n, k), lambda i: (0, 0)), [pl.BlockSpec((ROW_TILE, n), lambda i: (i, 0))],
               [jax.ShapeDtypeStruct((m, n), out_dtype)], (8, 128), None)[0]


def _mm_tn(name, a, b, out_dtype):
    m, k = a.shape
    n = b.shape[1]
    return _mm(name, a, b, _TN, (m // ROW_TILE,), pl.BlockSpec((ROW_TILE, k), lambda i: (i, 0)),
               pl.BlockSpec((ROW_TILE, n), lambda i: (i, 0)), [pl.BlockSpec((k, n), lambda i: (0, 0))],
               [jax.ShapeDtypeStruct((k, n), out_dtype)], (k, n), 0)[0]


def _mm_in_bwd_x(dhb, w_g, after):
    m = dhb.shape[0]
    _, d, ns = w_g.shape
    return _mm("mm_in_bwd_x", dhb, w_g, _NT, (m // IN_TILE, N_DEV), pl.BlockSpec((IN_TILE, ns), lambda i, j: (i, j)),
               pl.BlockSpec((None, d, ns), lambda i, j: (j, 0, 0)), [pl.BlockSpec((IN_TILE, d), lambda i, j: (i, 0))],
               [jax.ShapeDtypeStruct((m, d), F32)], (IN_TILE, d), 1, after=after)[0]


def _mm_in_bwd_w(h0b, dhb):
    m, d = h0b.shape
    ns = dhb.shape[1] // N_DEV
    return _mm("mm_in_bwd_w", h0b, dhb, _TN, (N_DEV, m // IN_TILE), pl.BlockSpec((IN_TILE, d), lambda j, k: (k, 0)),
               pl.BlockSpec((IN_TILE, ns), lambda j, k: (k, j)), [pl.BlockSpec((None, d, ns), lambda j, k: (j, 0, 0))],
               [jax.ShapeDtypeStruct((N_DEV, d, ns), BF16)], (d, ns), 1)[0]


LOG2_E = 1.4426950408889634


def _twice(u):
    return jnp.concatenate([u, u], axis=0)


def _dot2(x, u2):
    hi = x.astype(BF16)
    lo = (x - hi.astype(F32)).astype(BF16)
    return lax.dot_general(jnp.concatenate([hi, lo], axis=1), u2, _NN, preferred_element_type=F32)


def _softplus(l):
    e = jnp.exp2(jnp.abs(l) * (-LOG2_E))
    e1 = 1.0 + e
    return jnp.maximum(l, 0.0) + jnp.log(e1), e, e1


def _attn_fwd(hb, h, nb, seq):
    m = nb * seq
    d = SB_HEADS * HEAD_DIM
    tq, tk = ATT_TQ, ATT_TK
    n_sub = tq // tk
    nq = seq // tq
    sec = d // 128
    scale = 1.0 / math.sqrt(HEAD_DIM)
    heads = [slice(hh * HEAD_DIM, (hh + 1) * HEAD_DIM) for hh in range(2)]

    def body(q_ref, k_ref, v_ref, z_ref, o_ref, a_ref, c_ref, carry_ref, acc_ref):
        qi = pl.program_id(2)
        u_after = _twice((lax.broadcasted_iota(jnp.int32, (tk, tk), 0)
                          > lax.broadcasted_iota(jnp.int32, (tk, tk), 1)).astype(BF16))
        qs = [(q_ref[:, cols].astype(F32) * scale).astype(BF16) for cols in heads]
        carry_ref[...] = jnp.zeros_like(carry_ref)
        acc_ref[...] = jnp.zeros_like(acc_ref)
        c_ref[...] = jnp.zeros_like(c_ref)

        def block(kb, r0, masked):
            n = tq - r0
            qrows = slice(r0, tq)
            krows = pl.ds(pl.multiple_of(kb * tk, tk), tk)
            lane = lax.broadcasted_iota(jnp.int32, (n, 128), 1)
            if masked:
                causal = lax.broadcasted_iota(jnp.int32, (n, tk), 1) < lax.broadcasted_iota(jnp.int32, (n, tk), 0)
            for hh, cols in enumerate(heads):
                carry = carry_ref[hh, qrows, :]
                l = lax.dot_general(qs[hh][qrows], k_ref[krows, cols], _NT, preferred_element_type=F32)
                sp, _, _ = _softplus(l)
                if masked:
                    sp = jnp.where(causal, sp, 0.0)
                after = carry + _dot2(sp, u_after)
                a = jnp.exp(l - (sp + after))
                if masked:
                    a = jnp.where(causal, a, 0.0)
                acc_ref[qrows, cols] += lax.dot_general(a.astype(BF16), v_ref[krows, cols], _NN,
                                                        preferred_element_type=F32)
                c_ref[qrows, :] += jnp.where(lane == hh * HEAD_DIM + kb, carry, 0.0)
                carry_ref[hh, qrows, :] = after[:, 0:1] + sp[:, 0:1]

        for j in reversed(range(n_sub)):
            block(n_sub * qi + j, j * tk, True)

        def step(i, _):
            block(n_sub * qi - 1 - i, 0, False)
            return 0

        lax.fori_loop(0, n_sub * qi, step, 0)
        o = acc_ref[...]
        o_ref[...] = o
        silu, _ = _silu_and_grad(z_ref[...])
        a_ref[...] = (o * silu).astype(BF16)

    qspec = pl.BlockSpec((tq, 128), lambda b, hp, qi: (b * nq + qi, hp))
    return pl.pallas_call(
        body, name="attn_fwd", grid=(nb, sec, nq),
        in_specs=[qspec,
                  pl.BlockSpec((seq, 128), lambda b, hp, qi: (b, sec + hp)),
                  pl.BlockSpec((seq, 128), lambda b, hp, qi: (b, 2 * sec + hp)),
                  pl.BlockSpec((tq, 128), lambda b, hp, qi: (b * nq + qi, 3 * sec + hp))],
        out_specs=[qspec, qspec, pl.BlockSpec((None, tq, 128), lambda b, hp, qi: (hp, b * nq + qi, 0))],
        out_shape=[jax.ShapeDtypeStruct((m, d), F32), jax.ShapeDtypeStruct((m, d), BF16),
                   jax.ShapeDtypeStruct((sec, m, 128), F32)],
        scratch_shapes=[pltpu.VMEM((2, tq, 1), F32), pltpu.VMEM((tq, 128), F32)],
        compiler_params=_params("parallel", "parallel", "parallel"),
    )(hb, hb, hb, h)


def _attn_bwd(hb, h, o, carries, da, nb, seq, after):
    m = nb * seq
    d = SB_HEADS * HEAD_DIM
    tq, tk = ATT_TQ, ATT_TK
    n_sub = tq // tk
    nq = seq // tq
    sec = d // 128
    scale = 1.0 / math.sqrt(HEAD_DIM)
    heads = [slice(hh * HEAD_DIM, (hh + 1) * HEAD_DIM) for hh in range(2)]

    def body(q_ref, k_ref, v_ref, z_ref, o_ref, c_ref, da_ref, after_ref, dq_ref, dk_ref, dv_ref, dz_ref,
             dk_acc, dv_acc, dq_acc, gcarry_ref):
        qi = pl.program_id(2)

        @pl.when(qi == 0)
        def _():
            dk_acc[...] = jnp.zeros_like(dk_acc)
            dv_acc[...] = jnp.zeros_like(dv_acc)

        row = lax.broadcasted_iota(jnp.int32, (tk, tk), 0)
        col = lax.broadcasted_iota(jnp.int32, (tk, tk), 1)
        u_after = _twice((row > col).astype(BF16))
        u_before = _twice((row < col).astype(BF16))
        silu, dsilu = _silu_and_grad(z_ref[...])
        dav = da_ref[...]
        dz_ref[...] = (dav * o_ref[...] * dsilu).astype(BF16)
        do2 = (dav * silu).astype(BF16)
        qs = [(q_ref[:, cols].astype(F32) * scale).astype(BF16) for cols in heads]
        dos = [do2[:, cols] for cols in heads]
        dq_acc[...] = jnp.zeros_like(dq_acc)
        gcarry_ref[...] = jnp.zeros_like(gcarry_ref)

        def block(kb, r0, masked):
            n = tq - r0
            qrows = slice(r0, tq)
            krows = pl.ds(pl.multiple_of(kb * tk, tk), tk)
            lane = lax.broadcasted_iota(jnp.int32, (n, 128), 1)
            if masked:
                causal = lax.broadcasted_iota(jnp.int32, (n, tk), 1) < lax.broadcasted_iota(jnp.int32, (n, tk), 0)
            for hh, cols in enumerate(heads):
                q, do = qs[hh][qrows], dos[hh][qrows]
                ks = k_ref[krows, cols]
                l = lax.dot_general(q, ks, _NT, preferred_element_type=F32)
                sp, e, e1 = _softplus(l)
                if masked:
                    sp = jnp.where(causal, sp, 0.0)
                carry = jnp.sum(jnp.where(lane == hh * HEAD_DIM + kb, c_ref[qrows, :], 0.0), axis=1, keepdims=True)
                after = carry + _dot2(sp, u_after)
                a = jnp.exp(l - (sp + after))
                if masked:
                    a = jnp.where(causal, a, 0.0)
                dv_acc[krows, cols] += lax.dot_general(a.astype(BF16), do, _TN, preferred_element_type=F32)
                g = a * lax.dot_general(do, v_ref[krows, cols], _NT, preferred_element_type=F32)
                c = gcarry_ref[hh, qrows, :] + _dot2(g, u_before)
                r = pl.reciprocal(e1, approx=True)
                beta = jnp.where(l >= 0.0, r, e * r)
                dl = g - (g + c) * beta
                if masked:
                    dl = jnp.where(causal, dl, 0.0)
                dl_b = dl.astype(BF16)
                dq_acc[qrows, cols] += lax.dot_general(dl_b, ks, _NN, preferred_element_type=F32)
                dk_acc[krows, cols] += lax.dot_general(dl_b, q, _TN, preferred_element_type=F32)
                gcarry_ref[hh, qrows, :] = c[:, tk - 1:tk] + g[:, tk - 1:tk]

        def step(kb, _):
            block(kb, 0, False)
            return 0

        lax.fori_loop(0, n_sub * qi, step, 0)
        for j in range(n_sub):
            block(n_sub * qi + j, j * tk, True)
        dq_ref[...] = (dq_acc[...] * scale).astype(BF16)

        @pl.when(qi == nq - 1)
        def _():
            dk_ref[...] = dk_acc[...].astype(BF16)
            dv_ref[...] = dv_acc[...].astype(BF16)

    qspec = pl.BlockSpec((tq, 128), lambda b, hp, qi: (b * nq + qi, hp))
    kvspec = pl.BlockSpec((seq, 128), lambda b, hp, qi: (b, hp))
    return pl.pallas_call(
        body, name="attn_bwd", grid=(nb, sec, nq),
        in_specs=[qspec,
                  pl.BlockSpec((seq, 128), lambda b, hp, qi: (b, sec + hp)),
                  pl.BlockSpec((seq, 128), lambda b, hp, qi: (b, 2 * sec + hp)),
                  pl.BlockSpec((tq, 128), lambda b, hp, qi: (b * nq + qi, 3 * sec + hp)),
                  qspec,
                  pl.BlockSpec((None, tq, 128), lambda b, hp, qi: (hp, b * nq + qi, 0)),
                  qspec, _after_spec()],
        out_specs=[qspec, kvspec, kvspec, qspec],
        out_shape=[jax.ShapeDtypeStruct((m, d), BF16)] * 4,
        scratch_shapes=[pltpu.VMEM((seq, 128), F32), pltpu.VMEM((seq, 128), F32), pltpu.VMEM((tq, 128), F32),
                        pltpu.VMEM((2, tq, 1), F32)],
        compiler_params=_params("parallel", "parallel", "arbitrary"),
    )(hb, hb, hb, h, o, carries, da, after)


SUBLANES = 8


def _by_residue(taps):
    groups = []
    for res in range(SUBLANES):
        mine = [(off - res, k) for off, k in taps if off % SUBLANES == res]
        if mine:
            groups.append((res, max(a for a, _ in mine), mine))
    return groups


def _staged(src_ref, stage_ref, res, n):
    buf = stage_ref.at[res % 2]
    if res == 0:
        return src_ref
    buf[0:n, :] = src_ref[pl.ds(res, n), :]
    return buf


def _tap_sum(src_ref, stage_ref, w_ref, n_rows, taps, acc):
    for res, reach, mine in _by_residue(taps):
        shifted = _staged(src_ref, stage_ref, res, n_rows + reach)
        for a, k in mine:
            acc = acc + w_ref[k:k + 1, :] * shifted[a:a + n_rows, :]
    return acc


_CONV_TAPS = [(CV_HALO - CV_KERNEL + 1 + k, k) for k in range(CV_KERNEL)]
_CONV_TAPS_T = [(CV_KERNEL - 1 - k, k) for k in range(CV_KERNEL)]


def _conv_rows(pad_ref, stage_ref, w_ref, b_ref, n_rows):
    bias = jnp.broadcast_to(b_ref[...], (n_rows, b_ref.shape[1]))
    return _tap_sum(pad_ref, stage_ref, w_ref, n_rows, _CONV_TAPS, bias)


def _conv_fwd(h, conv_w, conv_b, ln_g, ln_b, nb, seq):
    m = nb * seq
    d = conv_b.shape[1]
    t = CONV_TILE
    tiles = seq // t
    hpt = t // CV_HALO

    def body(cv_ref, cg_ref, pv_ref, pg_ref, z_ref, w_ref, b_ref, g_ref, bb_ref, a_ref, pad_ref, stage_ref):
        first = pl.program_id(0) % tiles == 0
        pad_ref[0:CV_HALO, :] = jnp.where(first, 0.0, pv_ref[...] * _sigmoid(pg_ref[...]))
        pad_ref[CV_HALO:, :] = cv_ref[...] * _sigmoid(cg_ref[...])
        xhat, _ = _ln_stats(_conv_rows(pad_ref, stage_ref, w_ref, b_ref, t))
        s, _ = _silu_and_grad(xhat * g_ref[...] + bb_ref[...])
        sz, _ = _silu_and_grad(z_ref[...])
        a_ref[...] = (s * sz).astype(BF16)

    def main(c):
        return pl.BlockSpec((t, d), lambda i: (i, c))

    def prev(c):
        return pl.BlockSpec((CV_HALO, d), lambda i: (jnp.maximum(i * hpt - 1, 0), c))

    vec = pl.BlockSpec((1, d), lambda i: (0, 0))
    return pl.pallas_call(
        body, name="conv_fwd", grid=(m // t,),
        in_specs=[main(4), main(5), prev(4), prev(5), main(6),
                  pl.BlockSpec((CV_HALO, d), lambda i: (0, 0)), vec, vec, vec],
        out_specs=pl.BlockSpec((t, d), lambda i: (i, 0)),
        out_shape=jax.ShapeDtypeStruct((m, d), BF16),
        scratch_shapes=[pltpu.VMEM((CV_HALO + t, d), F32), pltpu.VMEM((2, CV_HALO + t, d), F32)],
        compiler_params=_params("parallel"),
    )(h, h, h, h, h, conv_w, conv_b, ln_g, ln_b)


def _conv_bwd(h, da, conv_w, conv_b, ln_g, ln_b, nb, seq):
    m = nb * seq
    d = conv_b.shape[1]
    t = CONV_TILE
    tiles = seq // t
    hpt = t // CV_HALO
    last_halo = m // CV_HALO - 1

    def body(cv_ref, cg_ref, pv_ref, pg_ref, nv_ref, ng_ref, z_ref, nz_ref, da_ref, nda_ref,
             w_ref, b_ref, g_ref, bb_ref, dh_ref, dw_ref, db_ref, dg_ref, dbb_ref, pad_ref, dc_ref, stage_ref):
        i = pl.program_id(0)

        @pl.when(i == 0)
        def _():
            dw_ref[...] = jnp.zeros_like(dw_ref)
            db_ref[...] = jnp.zeros_like(db_ref)
            dg_ref[...] = jnp.zeros_like(dg_ref)
            dbb_ref[...] = jnp.zeros_like(dbb_ref)

        first = i % tiles == 0
        last = i % tiles == tiles - 1
        cv = cv_ref[...]
        sg = _sigmoid(cg_ref[...])
        pad_ref[0:CV_HALO, :] = jnp.where(first, 0.0, pv_ref[...] * _sigmoid(pg_ref[...]))
        pad_ref[CV_HALO:CV_HALO + t, :] = cv * sg
        pad_ref[CV_HALO + t:, :] = nv_ref[...] * _sigmoid(ng_ref[...])

        def rows_bwd(conv_out, z, dav):
            xhat, rstd = _ln_stats(conv_out)
            s, ds = _silu_and_grad(xhat * g_ref[...] + bb_ref[...])
            sz, dsz = _silu_and_grad(z)
            dn = dav * sz * ds
            return _ln_bwd(dn * g_ref[...], xhat, rstd), dav * s * dsz, dn, xhat

        conv_all = _conv_rows(pad_ref, stage_ref, w_ref, b_ref, t + CV_HALO)
        dc, dz, dn, xhat = rows_bwd(conv_all[:t], z_ref[...], da_ref[...])
        dc_next, _, _, _ = rows_bwd(conv_all[t:], nz_ref[...], nda_ref[...])
        dc_ref[0:t, :] = dc
        dc_ref[t:, :] = jnp.where(last, 0.0, dc_next)
        dg_ref[...] += jnp.sum(dn * xhat, axis=0, keepdims=True)
        dbb_ref[...] += jnp.sum(dn, axis=0, keepdims=True)
        db_ref[...] += jnp.sum(dc, axis=0, keepdims=True)
        for res, reach, mine in _by_residue(_CONV_TAPS):
            shifted = _staged(pad_ref, stage_ref, res, t + reach)
            for a, k in mine:
                dw_ref[k:k + 1, :] += jnp.sum(dc * shifted[a:a + t, :], axis=0, keepdims=True)
        du = _tap_sum(dc_ref, stage_ref, w_ref, t, _CONV_TAPS_T, jnp.zeros((t, d), F32))
        dh_ref[:, 0:d] = (du * sg).astype(BF16)
        dh_ref[:, d:2 * d] = (du * cv * sg * (1.0 - sg)).astype(BF16)
        dh_ref[:, 2 * d:] = dz.astype(BF16)

    def main(c):
        return pl.BlockSpec((t, d), lambda i: (i, c))

    def prev(c):
        return pl.BlockSpec((CV_HALO, d), lambda i: (jnp.maximum(i * hpt - 1, 0), c))

    def nxt(c):
        return pl.BlockSpec((CV_HALO, d), lambda i: (jnp.minimum((i + 1) * hpt, last_halo), c))

    vec = pl.BlockSpec((1, d), lambda i: (0, 0))
    taps = pl.BlockSpec((CV_HALO, d), lambda i: (0, 0))
    return pl.pallas_call(
        body, name="conv_bwd", grid=(m // t,),
        in_specs=[main(4), main(5), prev(4), prev(5), nxt(4), nxt(5), main(6), nxt(6), main(0), nxt(0),
                  taps, vec, vec, vec],
        out_specs=[pl.BlockSpec((t, 3 * d), lambda i: (i, 0)), taps, vec, vec, vec],
        out_shape=[jax.ShapeDtypeStruct((m, 3 * d), BF16), jax.ShapeDtypeStruct((CV_HALO, d), F32),
                   jax.ShapeDtypeStruct((1, d), F32), jax.ShapeDtypeStruct((1, d), F32),
                   jax.ShapeDtypeStruct((1, d), F32)],
        scratch_shapes=[pltpu.VMEM((t + 2 * CV_HALO, d), F32), pltpu.VMEM((t + CV_HALO, d), F32),
                        pltpu.VMEM((2, t + 2 * CV_HALO, d), F32)],
        compiler_params=_params("arbitrary"),
    )(h, h, h, h, h, h, h, h, da, da, conv_w, conv_b, ln_g, ln_b)


_HBM = pl.BlockSpec(memory_space=pltpu.HBM)


def _place():
    return lax.axis_index("x"), lax.axis_index("y"), lax.axis_index("c")


def _slot(p):
    return 4 * p[0] + 2 * p[1] + p[2]


def _gather_mm_in_fwd(h0b, w_shard, after):
    m, d = h0b.shape
    ns = w_shard.shape[1]
    n_i = m // IN_TILE
    n_steps = N_DEV

    def places():
        x, y, c = _place()
        chips = [(1 - x, y), (x, 1 - y), (1 - x, 1 - y)]
        return (x, y, c), (x, y, 1 - c), chips

    me, sibling, chips = places()
    order = [me, sibling] + [(*chip, me[2]) for chip in chips] + [(*chip, sibling[2]) for chip in chips]
    order = jnp.stack([_slot(p) for p in order]).astype(jnp.int32)

    def body(order_ref, h_ref, w_ref, after_ref, o32_ref, o16_ref, wg_ref, wbuf, send_sems, recv_sems, own_sem,
             load_sems):
        s, i = pl.program_id(0), pl.program_id(1)
        me, sibling, chips = places()
        c = me[2]

        def remote(k, block, to, src=None):
            dst = wg_ref.at[_slot(block)]
            return pltpu.make_async_remote_copy(
                src_ref=dst if src is None else src, dst_ref=dst, send_sem=send_sems.at[k],
                recv_sem=recv_sems.at[k], device_id=to, device_id_type=MESH)

        def load(step):
            return pltpu.make_async_copy(wg_ref.at[order_ref[step]], wbuf.at[step % 2], load_sems.at[step % 2])

        own = pltpu.make_async_copy(w_ref, wg_ref.at[_slot(me)], own_sem)

        @pl.when((s == 0) & (i == 0))
        def _():
            own.start()
            remote(0, me, sibling, src=w_ref).start()
            for j, chip in enumerate(chips):
                remote(1 + j, me, (*chip, c), src=w_ref).start()
            first = pltpu.make_async_copy(w_ref, wbuf.at[0], load_sems.at[0])
            first.start()
            first.wait()

        for nxt in range(1, n_steps):
            @pl.when((s == nxt - 1) & (i == n_i - 1))
            def _():
                if nxt == 1:
                    remote(0, sibling, me).wait_recv()
                elif nxt <= 4:
                    j = nxt - 2
                    remote(1 + j, (*chips[j], c), me).wait_recv()
                    remote(4 + j, (*chips[j], c), sibling).start()
                else:
                    j = nxt - 5
                    remote(4 + j, (*chips[j], 1 - c), me).wait_recv()
                load(nxt).start()

        for step in range(1, n_steps):
            @pl.when((s == step) & (i == 0))
            def _():
                load(step).wait()

        part = lax.dot_general(h_ref[...], wbuf[s % 2], _NN, preferred_element_type=F32)
        o32_ref[...] = part
        o16_ref[...] = part.astype(BF16)

        @pl.when((s == n_steps - 1) & (i == n_i - 1))
        def _():
            remote(0, me, sibling, src=w_ref).wait_send()
            for j, chip in enumerate(chips):
                remote(1 + j, me, (*chip, c), src=w_ref).wait_send()
                remote(4 + j, (*chip, c), sibling).wait_send()
            own.wait()

    out = pl.BlockSpec((IN_TILE, ns), lambda s, i, order: (i, order[s]))
    grid_spec = pltpu.PrefetchScalarGridSpec(
        num_scalar_prefetch=1, grid=(n_steps, n_i),
        in_specs=[pl.BlockSpec((IN_TILE, d), lambda s, i, order: (i, 0)), _HBM, _ANY],
        out_specs=[out, out, _HBM],
        scratch_shapes=[pltpu.VMEM((2, d, ns), BF16), pltpu.SemaphoreType.DMA((7,)), pltpu.SemaphoreType.DMA((7,)),
                        pltpu.SemaphoreType.DMA, pltpu.SemaphoreType.DMA((2,))])
    return pl.pallas_call(
        body, name="gather_mm_in_fwd", grid_spec=grid_spec,
        out_shape=[jax.ShapeDtypeStruct((m, N_DEV * ns), F32), jax.ShapeDtypeStruct((m, N_DEV * ns), BF16),
                   jax.ShapeDtypeStruct((N_DEV, d, ns), BF16)],
        compiler_params=_params("arbitrary", "arbitrary"),
    )(order, h0b, w_shard, after)


def _exchange_partials(parts, whole):
    n_parts = len(parts)
    arrays = list(parts) + list(whole)
    n = len(arrays)

    def body(*refs):
        ins, outs = refs[:n], refs[n:2 * n]
        send_sems, recv_sems, local_sems = refs[2 * n:]
        x, y, c = _place()
        me = (x, y, c)

        def src_for(a, p):
            return ins[a].at[_slot(p)] if a < n_parts else ins[a]

        def copy(a, k, peer):
            return pltpu.make_async_remote_copy(
                src_ref=src_for(a, peer), dst_ref=outs[a].at[_slot(me)], send_sem=send_sems.at[7 * a + k],
                recv_sem=recv_sems.at[7 * a + k], device_id=peer, device_id_type=MESH)

        def landed(a, k, peer):
            return pltpu.make_async_remote_copy(
                src_ref=src_for(a, peer), dst_ref=outs[a].at[_slot(peer)], send_sem=send_sems.at[7 * a + k],
                recv_sem=recv_sems.at[7 * a + k], device_id=peer, device_id_type=MESH)

        peers = []
        for k in range(1, N_DEV):
            fx, fy, fc = (k >> 2) & 1, (k >> 1) & 1, k & 1
            peers.append((1 - x if fx else x, 1 - y if fy else y, 1 - c if fc else c))
        mine = [pltpu.make_async_copy(src_for(a, me), outs[a].at[_slot(me)], local_sems.at[a]) for a in range(n)]
        for cp in mine:
            cp.start()
        sent = [copy(a, k, peer) for a in range(n) for k, peer in enumerate(peers)]
        for cp in sent:
            cp.start()
        for a in range(n):
            for k, peer in enumerate(peers):
                landed(a, k, peer).wait_recv()
        for cp in sent:
            cp.wait_send()
        for cp in mine:
            cp.wait()

    out_shape = [jax.ShapeDtypeStruct(p.shape, p.dtype) for p in parts]
    out_shape += [jax.ShapeDtypeStruct((N_DEV,) + w.shape, w.dtype) for w in whole]
    return pl.pallas_call(
        body, name="exchange_grad_partials",
        in_specs=[_HBM] * n, out_specs=[_HBM] * n, out_shape=out_shape,
        scratch_shapes=[pltpu.SemaphoreType.DMA((7 * n,)), pltpu.SemaphoreType.DMA((7 * n,)),
                        pltpu.SemaphoreType.DMA((n,))],
    )(*arrays)


_SEM = pl.BlockSpec(memory_space=pltpu.SEMAPHORE)
_ANY = pl.BlockSpec(memory_space=pl.ANY)
_SIDE_EFFECT = pltpu.SideEffectType.DATAFLOW_SIDE_EFFECTING


def _peers_of(x, y, c):
    peers = []
    for k in range(1, N_DEV):
        fx, fy, fc = (k >> 2) & 1, (k >> 1) & 1, k & 1
        peers.append((1 - x if fx else x, 1 - y if fy else y, 1 - c if fc else c))
    return peers


def _send_start(name, parts, whole):
    n_parts = len(parts)
    arrays = list(parts) + list(whole)
    n = len(arrays)
    lands = [lax.empty(p.shape, p.dtype) for p in parts] + [lax.empty((N_DEV,) + w.shape, w.dtype) for w in whole]

    def body(*refs):
        srcs, land = refs[:n], refs[n:2 * n]
        send_sems, recv_sems = refs[2 * n], refs[2 * n + 1]
        token = refs[-1]
        x, y, c = _place()
        for k, peer in enumerate(_peers_of(x, y, c)):
            for a in range(n):
                pltpu.make_async_remote_copy(
                    src_ref=srcs[a].at[_slot(peer)] if a < n_parts else srcs[a], dst_ref=land[a].at[_slot((x, y, c))],
                    send_sem=send_sems.at[7 * a + k], recv_sem=recv_sems.at[7 * a + k],
                    device_id=peer, device_id_type=MESH).start()
        token[...] = jnp.zeros_like(token)

    operands = [pltpu.with_memory_space_constraint(a, pltpu.HBM) for a in arrays + lands]
    out = pl.pallas_call(
        body, name=name,
        out_shape=(pltpu.SemaphoreType.DMA((7 * n,)), pltpu.SemaphoreType.DMA((7 * n,)),
                   *[pltpu.HBM(a.shape, a.dtype) for a in arrays + lands], jax.ShapeDtypeStruct((8, 128), F32)),
        in_specs=[_HBM] * (2 * n),
        out_specs=(_SEM, _SEM, *[_HBM] * (2 * n), pl.BlockSpec(memory_space=pltpu.VMEM)),
        input_output_aliases={i: 2 + i for i in range(2 * n)},
        compiler_params=pltpu.CompilerParams(has_side_effects=_SIDE_EFFECT),
    )(*operands)
    return (out[0], out[1], list(out[2:2 + n]), list(out[2 + n:2 + 2 * n]), n_parts), out[-1]


def _send_wait(name, started, after):
    send_sems, recv_sems, srcs, lands, n_parts = started
    n = len(srcs)

    def body(*refs):
        src_refs, land_refs = refs[:n], refs[n:2 * n]
        send_sems, recv_sems = refs[2 * n], refs[2 * n + 1]
        x, y, c = _place()
        for k, peer in enumerate(_peers_of(x, y, c)):
            for a in range(n):
                copy = pltpu.make_async_remote_copy(
                    src_ref=src_refs[a].at[_slot(peer)] if a < n_parts else src_refs[a],
                    dst_ref=land_refs[a].at[_slot(peer)], send_sem=send_sems.at[7 * a + k],
                    recv_sem=recv_sems.at[7 * a + k], device_id=peer, device_id_type=MESH)
                copy.wait_send()
                copy.wait_recv()

    out = pl.pallas_call(
        body, name=name,
        out_shape=[pltpu.HBM(a.shape, a.dtype) for a in srcs + lands],
        in_specs=[_HBM] * (2 * n) + [_SEM, _SEM, _ANY], out_specs=[_HBM] * (2 * n),
        input_output_aliases={i: i for i in range(2 * n)},
        compiler_params=pltpu.CompilerParams(has_side_effects=_SIDE_EFFECT),
    )(*srcs, *lands, send_sems, recv_sems, after)
    return list(out[:n]), list(out[n:])


def _with_own(landed, own, me):
    return lax.dynamic_update_index_in_dim(landed, own, me, 0)


def _adamw(name, parts, w, mom, var, rows):
    r, c = w.shape

    def body(p_ref, w_ref, m_ref, v_ref, g_ref, d_ref, nm_ref, nv_ref):
        g = p_ref[0].astype(F32)
        for p in range(1, N_DEV):
            g = g + p_ref[p].astype(F32)
        m_new = ADAM_B1 * m_ref[...] + (1.0 - ADAM_B1) * g
        v_new = ADAM_B2 * v_ref[...] + (1.0 - ADAM_B2) * (g * g)
        m_hat = m_new / (1.0 - ADAM_B1 ** ADAM_STEP)
        v_hat = v_new / (1.0 - ADAM_B2 ** ADAM_STEP)
        g_ref[...] = g
        d_ref[...] = -ADAM_LR * (m_hat / (jnp.sqrt(v_hat) + ADAM_EPS) + ADAM_WD * w_ref[...])
        nm_ref[...] = m_new
        nv_ref[...] = v_new

    blk = pl.BlockSpec((rows, c), lambda i: (i, 0))
    return pl.pallas_call(
        body, name=name, grid=(r // rows,),
        in_specs=[pl.BlockSpec((N_DEV, rows, c), lambda i: (0, i, 0)), blk, blk, blk],
        out_specs=[blk] * 4, out_shape=[jax.ShapeDtypeStruct((r, c), F32)] * 4,
        compiler_params=_params("parallel"),
    )(parts, w, mom, var)


def kernel(x, ln_in_g, ln_in_b, w_in, w_sb_proj, conv_w, conv_b, conv_ln_g, conv_ln_b, w_cv_proj, w_out, ln_post_g, ln_post_b, loss_target, m_ln_in_g, m_ln_in_b, m_w_in, m_w_sb_proj, m_conv_w, m_conv_b, m_conv_ln_g, m_conv_ln_b, m_w_cv_proj, m_w_out, m_ln_post_g, m_ln_post_b, v_ln_in_g, v_ln_in_b, v_w_in, v_w_sb_proj, v_conv_w, v_conv_b, v_conv_ln_g, v_conv_ln_b, v_w_cv_proj, v_w_out, v_ln_post_g, v_ln_post_b):
    nb, seq, d = x.shape
    m = nb * seq
    x2d = x.reshape(m, d)
    target = loss_target.reshape(m, d)
    rs = d // N_DEV
    pad_taps = ((0, CV_HALO - CV_KERNEL), (0, 0))

    proj_shards = jnp.stack([w_sb_proj[0], w_cv_proj[0], w_out[0]]).astype(BF16)
    conv_w_shard = jnp.pad(conv_w[0], pad_taps)
    me = _slot(_place())
    own = lambda parts: lax.dynamic_index_in_dim(parts, me, 0, keepdims=False)
    gather_small, started = _send_start("gather_proj_start", [], [proj_shards, conv_w_shard])

    h0, h0b = _ln_in_fwd(x2d, ln_in_g.reshape(1, d), ln_in_b.reshape(1, d))
    h, hb, w_in_g = _gather_mm_in_fwd(h0b, w_in[0].astype(BF16), started)
    o, a_sb, carries = _attn_fwd(hb, h, nb, seq)
    (proj_mine, conv_w_mine), (proj_g, conv_w_g) = _send_wait("gather_proj_wait", gather_small, a_sb)
    proj_g = _with_own(proj_g, proj_mine, me)
    conv_w_g = _with_own(conv_w_g, conv_w_mine, me)
    w_sb_g = proj_g[:, 0].reshape(d, d)
    w_cv_g = proj_g[:, 1].reshape(d, d)
    w_out_g = proj_g[:, 2].reshape(d, d)
    conv_w_full = conv_w_g.transpose(1, 0, 2).reshape(CV_HALO, d)
    a_cv = _conv_fwd(h, conv_w_full, conv_b, conv_ln_g, conv_ln_b, nb, seq)
    y_sb = _mm_nn("mm_sb_fwd", a_sb, w_sb_g)
    y_cv = _mm_nn("mm_cv_fwd", a_cv, w_cv_g)
    merged = _merge_fwd(h, y_sb, y_cv)
    mo = _mm_nn("mm_out_fwd", merged, w_out_g)
    loss_part, d_pre, d_pre_b, dg_post, db_post = _post_ln_loss(h0, mo, target, ln_post_g, ln_post_b)

    dw_out = _mm_tn("mm_out_bwd_w", merged, d_pre_b, BF16)
    d_merged = _mm_nt("mm_out_bwd_x", d_pre_b, w_out_g)
    dy_sb, dy_cv, d_gates = _merge_bwd(h, y_sb, y_cv, d_merged)
    dw_sb = _mm_tn("mm_sb_bwd_w", a_sb, dy_sb, BF16)
    dw_cv = _mm_tn("mm_cv_bwd_w", a_cv, dy_cv, BF16)
    da_sb = _mm_nt("mm_sb_bwd_x", dy_sb, w_sb_g)
    da_cv = _mm_nt("mm_cv_bwd_x", dy_cv, w_cv_g)
    dproj = jnp.stack([dw_sb.reshape(N_DEV, rs, d), dw_cv.reshape(N_DEV, rs, d), dw_out.reshape(N_DEV, rs, d)], axis=1)
    grads_proj, started = _send_start("grads_proj_start", [dproj], [])
    dq, dk, dv, dz_sb = _attn_bwd(hb, h, o, carries, da_sb, nb, seq, started)
    d_conv3, dconv_w, dconv_b, dconv_ln_g, dconv_ln_b = _conv_bwd(
        h, da_cv, conv_w_full, conv_b, conv_ln_g, conv_ln_b, nb, seq)
    (dproj,), (r_proj,) = _send_wait("grads_proj_wait", grads_proj, d_conv3)
    r_proj = _with_own(r_proj, own(dproj), me)
    dhb = jnp.concatenate([dq, dk, dv, dz_sb, d_conv3, d_gates], axis=1)
    dw_in = _mm_in_bwd_w(h0b, dhb)
    dconv_w_parts = dconv_w.reshape(CV_HALO, N_DEV, d // N_DEV).transpose(1, 0, 2)
    grads_in, started = _send_start("grads_in_start", [dw_in, dconv_w_parts], [])
    dh_mm = _mm_in_bwd_x(dhb, w_in_g, started)
    dx, dg_in, db_in = _ln_in_bwd(x2d, d_pre, dh_mm, ln_in_g.reshape(1, d))
    (dw_in, dconv_w_parts), (r_in, r_conv) = _send_wait("grads_in_wait", grads_in, dx)
    r_in = _with_own(r_in, own(dw_in), me)
    r_conv = _with_own(r_conv, own(dconv_w_parts), me)

    small = jnp.concatenate([dg_in, db_in, dconv_b, dconv_ln_g, dconv_ln_b, dg_post, db_post,
                             jnp.zeros((1, d), F32)], axis=0)
    (r_small,) = _exchange_partials([], [small])

    g_in, d_in, nm_in, nv_in = _adamw("adamw_w_in", r_in, w_in[0], m_w_in[0], v_w_in[0], 128)
    stack3 = lambda a, b, c: jnp.concatenate([a[0], b[0], c[0]], axis=0)
    g_pr, d_pr, nm_pr, nv_pr = _adamw(
        "adamw_proj", r_proj.reshape(N_DEV, 3 * rs, d), stack3(w_sb_proj, w_cv_proj, w_out),
        stack3(m_w_sb_proj, m_w_cv_proj, m_w_out), stack3(v_w_sb_proj, v_w_cv_proj, v_w_out), 3 * rs)
    padc = lambda a: jnp.pad(a[0], pad_taps)
    g_cw, d_cw, nm_cw, nv_cw = _adamw("adamw_conv_w", r_conv, padc(conv_w), padc(m_conv_w), padc(v_conv_w), CV_HALO)
    vecs = lambda *a: jnp.concatenate([t.reshape(1, d) for t in a] + [jnp.ones((1, d), F32)], axis=0)
    g_sm, d_sm, nm_sm, nv_sm = _adamw(
        "adamw_vectors", r_small,
        vecs(ln_in_g, ln_in_b, conv_b, conv_ln_g, conv_ln_b, ln_post_g, ln_post_b),
        vecs(m_ln_in_g, m_ln_in_b, m_conv_b, m_conv_ln_g, m_conv_ln_b, m_ln_post_g, m_ln_post_b),
        vecs(v_ln_in_g, v_ln_in_b, v_conv_b, v_conv_ln_g, v_conv_ln_b, v_ln_post_g, v_ln_post_b), 8)

    loss = lax.psum(loss_part[0, 0], ("x", "y", "c"))

    def leaves(big, pr, cw, sm):
        return (sm[0], sm[1], big[None], pr[None, 0:rs], cw[None, :CV_KERNEL], sm[2:3], sm[3:4], sm[4:5],
                pr[None, rs:2 * rs], pr[None, 2 * rs:], sm[5:6], sm[6:7])

    return (loss, dx.reshape(nb, seq, d), *leaves(g_in, g_pr, g_cw, g_sm), *leaves(d_in, d_pr, d_cw, d_sm),
            *leaves(nm_in, nm_pr, nm_cw, nm_sm), *leaves(nv_in, nv_pr, nv_cw, nv_sm))
```

```python
import functools
import math

import jax
import jax.numpy as jnp
from jax import lax
from jax.experimental import pallas as pl
from jax.experimental.pallas import tpu as pltpu

F32 = jnp.float32
BF16 = jnp.bfloat16
MESH = pl.DeviceIdType.MESH

N_DEV = 8
SB_HEADS = 16
HEAD_DIM = 64
CV_KERNEL = 31
CV_HALO = 32
LN_EPS = 1e-5
DEEPNORM_ALPHA = 2.0 ** 0.25
ADAM_LR, ADAM_B1, ADAM_B2, ADAM_EPS, ADAM_WD, ADAM_STEP = 0.001, 0.9, 0.999, 1e-08, 0.01, 10

ATT_TK = 256
ATT_TQ = 1024
ROW_TILE = 512
IN_TILE = 1024
CONV_TILE = 256
VMEM_LIMIT = 56 * 1024 * 1024


def _params(*sem):
    return pltpu.CompilerParams(dimension_semantics=sem, vmem_limit_bytes=VMEM_LIMIT)


def _sigmoid(x):
    return 1.0 / (1.0 + jnp.exp(-x))


def _silu_and_grad(x):
    s = _sigmoid(x)
    return x * s, s * (1.0 + x * (1.0 - s))


def _ln_stats(x):
    mu = jnp.mean(x, axis=-1, keepdims=True)
    xc = x - mu
    var = jnp.mean(xc * xc, axis=-1, keepdims=True)
    rstd = lax.rsqrt(var + LN_EPS)
    return xc * rstd, rstd


def _ln_bwd(dxhat, xhat, rstd):
    m1 = jnp.mean(dxhat, axis=-1, keepdims=True)
    m2 = jnp.mean(dxhat * xhat, axis=-1, keepdims=True)
    return rstd * (dxhat - m1 - xhat * m2)


def _ln_in_fwd(x2d, g, b):
    m, d = x2d.shape

    def body(x_ref, g_ref, b_ref, h_ref, hb_ref):
        xhat, _ = _ln_stats(x_ref[...])
        y = xhat * g_ref[...] + b_ref[...]
        h_ref[...] = y
        hb_ref[...] = y.astype(BF16)

    row = pl.BlockSpec((ROW_TILE, d), lambda i: (i, 0))
    vec = pl.BlockSpec((1, d), lambda i: (0, 0))
    return pl.pallas_call(
        body, name="ln_in_fwd", grid=(m // ROW_TILE,),
        in_specs=[row, vec, vec], out_specs=[row, row],
        out_shape=[jax.ShapeDtypeStruct((m, d), F32), jax.ShapeDtypeStruct((m, d), BF16)],
        compiler_params=_params("parallel"),
    )(x2d, g, b)


def _ln_in_bwd(x2d, d_pre, dh_mm, g):
    m, d = x2d.shape

    def body(x_ref, dp_ref, dm_ref, g_ref, dx_ref, dg_ref, db_ref):
        @pl.when(pl.program_id(0) == 0)
        def _():
            dg_ref[...] = jnp.zeros_like(dg_ref)
            db_ref[...] = jnp.zeros_like(db_ref)

        xhat, rstd = _ln_stats(x_ref[...])
        dh = DEEPNORM_ALPHA * dp_ref[...] + dm_ref[...]
        dg_ref[...] += jnp.sum(dh * xhat, axis=0, keepdims=True)
        db_ref[...] += jnp.sum(dh, axis=0, keepdims=True)
        dx_ref[...] = _ln_bwd(dh * g_ref[...], xhat, rstd)

    row = pl.BlockSpec((ROW_TILE, d), lambda i: (i, 0))
    vec = pl.BlockSpec((1, d), lambda i: (0, 0))
    return pl.pallas_call(
        body, name="ln_in_bwd", grid=(m // ROW_TILE,),
        in_specs=[row, row, row, vec], out_specs=[row, vec, vec],
        out_shape=[jax.ShapeDtypeStruct((m, d), F32), jax.ShapeDtypeStruct((1, d), F32),
                   jax.ShapeDtypeStruct((1, d), F32)],
        compiler_params=_params("arbitrary"),
    )(x2d, d_pre, dh_mm, g)


def _merge_fwd(h, y_sb, y_cv):
    m, d = y_sb.shape

    def body(gs_ref, gc_ref, ys_ref, yc_ref, out_ref):
        out_ref[...] = (_sigmoid(gs_ref[...]) * ys_ref[...] + _sigmoid(gc_ref[...]) * yc_ref[...]).astype(BF16)

    row = pl.BlockSpec((ROW_TILE, d), lambda i: (i, 0))
    return pl.pallas_call(
        body, name="merge_fwd", grid=(m // ROW_TILE,),
        in_specs=[pl.BlockSpec((ROW_TILE, d), lambda i: (i, 7)), pl.BlockSpec((ROW_TILE, d), lambda i: (i, 8)), row, row],
        out_specs=row, out_shape=jax.ShapeDtypeStruct((m, d), BF16),
        compiler_params=_params("parallel"),
    )(h, h, y_sb, y_cv)


def _merge_bwd(h, y_sb, y_cv, dm):
    m, d = y_sb.shape

    def body(gs_ref, gc_ref, ys_ref, yc_ref, dm_ref, dys_ref, dyc_ref, dgate_ref):
        dmv = dm_ref[...]
        ss = _sigmoid(gs_ref[...])
        sc = _sigmoid(gc_ref[...])
        dys_ref[...] = (ss * dmv).astype(BF16)
        dyc_ref[...] = (sc * dmv).astype(BF16)
        dgate_ref[:, :d] = (dmv * ys_ref[...] * ss * (1.0 - ss)).astype(BF16)
        dgate_ref[:, d:] = (dmv * yc_ref[...] * sc * (1.0 - sc)).astype(BF16)

    row = pl.BlockSpec((ROW_TILE, d), lambda i: (i, 0))
    return pl.pallas_call(
        body, name="merge_bwd", grid=(m // ROW_TILE,),
        in_specs=[pl.BlockSpec((ROW_TILE, d), lambda i: (i, 7)), pl.BlockSpec((ROW_TILE, d), lambda i: (i, 8)), row, row, row],
        out_specs=[row, row, pl.BlockSpec((ROW_TILE, 2 * d), lambda i: (i, 0))],
        out_shape=[jax.ShapeDtypeStruct((m, d), BF16), jax.ShapeDtypeStruct((m, d), BF16),
                   jax.ShapeDtypeStruct((m, 2 * d), BF16)],
        compiler_params=_params("parallel"),
    )(h, h, y_sb, y_cv, dm)


def _post_ln_loss(h0, mo, target, g, b):
    m, d = h0.shape

    def body(h_ref, mo_ref, t_ref, g_ref, b_ref, loss_ref, dp_ref, dpb_ref, dg_ref, db_ref):
        @pl.when(pl.program_id(0) == 0)
        def _():
            loss_ref[...] = jnp.zeros_like(loss_ref)
            dg_ref[...] = jnp.zeros_like(dg_ref)
            db_ref[...] = jnp.zeros_like(db_ref)

        xhat, rstd = _ln_stats(DEEPNORM_ALPHA * h_ref[...] + mo_ref[...])
        err = xhat * g_ref[...] + b_ref[...] - t_ref[...]
        per_row = jnp.mean(err * err, axis=-1, keepdims=True)
        loss_ref[...] += 0.5 * jnp.sum(per_row, axis=0, keepdims=True)
        dy = err * (1.0 / d)
        dg_ref[...] += jnp.sum(dy * xhat, axis=0, keepdims=True)
        db_ref[...] += jnp.sum(dy, axis=0, keepdims=True)
        dp = _ln_bwd(dy * g_ref[...], xhat, rstd)
        dp_ref[...] = dp
        dpb_ref[...] = dp.astype(BF16)

    row = pl.BlockSpec((ROW_TILE, d), lambda i: (i, 0))
    vec = pl.BlockSpec((1, d), lambda i: (0, 0))
    one = pl.BlockSpec((1, 1), lambda i: (0, 0))
    return pl.pallas_call(
        body, name="post_ln_loss", grid=(m // ROW_TILE,),
        in_specs=[row, row, row, vec, vec], out_specs=[one, row, row, vec, vec],
        out_shape=[jax.ShapeDtypeStruct((1, 1), F32), jax.ShapeDtypeStruct((m, d), F32),
                   jax.ShapeDtypeStruct((m, d), BF16), jax.ShapeDtypeStruct((1, d), F32),
                   jax.ShapeDtypeStruct((1, d), F32)],
        compiler_params=_params("arbitrary"),
    )(h0, mo, target, g, b)


_NN = (((1,), (0,)), ((), ()))
_NT = (((1,), (1,)), ((), ()))
_TN = (((0,), (0,)), ((), ()))


def _after_spec():
    return pl.BlockSpec((8, 128), lambda *_: (0, 0))


def _mm(name, a, b, dims, grid, a_spec, b_spec, out_specs, out_shape, acc_shape, k_axis, after=None):
    n_k = 1 if k_axis is None else grid[k_axis]
    n_out = len(out_shape)
    extra = [] if after is None else [after]

    def body(a_ref, b_ref, *rest):
        rest = rest[len(extra):]
        outs, acc_ref = rest[:n_out], rest[n_out]
        part = lax.dot_general(a_ref[...].astype(BF16), b_ref[...].astype(BF16), dims, preferred_element_type=F32)
        if n_k == 1:
            for o in outs:
                o[...] = part.astype(o.dtype)
            return
        k = pl.program_id(k_axis)

        @pl.when(k == 0)
        def _():
            acc_ref[...] = part

        @pl.when(k > 0)
        def _():
            acc_ref[...] += part

        @pl.when(k == n_k - 1)
        def _():
            for o in outs:
                o[...] = acc_ref[...].astype(o.dtype)

    sem = tuple("arbitrary" if ax == k_axis else "parallel" for ax in range(len(grid)))
    return pl.pallas_call(
        body, name=name, grid=grid, in_specs=[a_spec, b_spec] + [_after_spec()] * len(extra),
        out_specs=out_specs, out_shape=out_shape,
        scratch_shapes=[pltpu.VMEM(acc_shape, F32)], compiler_params=_params(*sem),
    )(a, b, *extra)


def _mm_nn(name, a, b, out_dtype=F32):
    m, k = a.shape
    n = b.shape[1]
    return _mm(name, a, b, _NN, (m // ROW_TILE,), pl.BlockSpec((ROW_TILE, k), lambda i: (i, 0)),
               pl.BlockSpec((k, n), lambda i: (0, 0)), [pl.BlockSpec((ROW_TILE, n), lambda i: (i, 0))],
               [jax.ShapeDtypeStruct((m, n), out_dtype)], (8, 128), None)[0]


def _mm_nt(name, a, b, out_dtype=F32):
    m, k = a.shape
    n = b.shape[0]
    return _mm(name, a, b, _NT, (m // ROW_TILE,), pl.BlockSpec((ROW_TILE, k), lambda i: (i, 0)),
               pl.BlockSpec((n, k), lambda i: (0, 0)), [pl.BlockSpec((ROW_TILE, n), lambda i: (i, 0))],
               [jax.ShapeDtypeStruct((m, n), out_dtype)], (8, 128), None)[0]


def _mm_tn(name, a, b, out_dtype):
    m, k = a.shape
    n = b.shape[1]
    return _mm(name, a, b, _TN, (m // ROW_TILE,), pl.BlockSpec((ROW_TILE, k), lambda i: (i, 0)),
               pl.BlockSpec((ROW_TILE, n), lambda i: (i, 0)), [pl.BlockSpec((k, n), lambda i: (0, 0))],
               [jax.ShapeDtypeStruct((k, n), out_dtype)], (k, n), 0)[0]


def _mm_in_bwd_x(dhb, w_g, after):
    m = dhb.shape[0]
    _, d, ns = w_g.shape
    return _mm("mm_in_bwd_x", dhb, w_g, _NT, (m // IN_TILE, N_DEV), pl.BlockSpec((IN_TILE, ns), lambda i, j: (i, j)),
               pl.BlockSpec((None, d, ns), lambda i, j: (j, 0, 0)), [pl.BlockSpec((IN_TILE, d), lambda i, j: (i, 0))],
               [jax.ShapeDtypeStruct((m, d), F32)], (IN_TILE, d), 1, after=after)[0]


def _mm_in_bwd_w(h0b, dhb, half, after):
    m, d = h0b.shape
    ns = dhb.shape[1] // N_DEV
    dh = d // 2
    return _mm(f"mm_in_bwd_w{half}", h0b, dhb, _TN, (N_DEV, m // IN_TILE),
               pl.BlockSpec((IN_TILE, dh), lambda j, k: (k, half)),
               pl.BlockSpec((IN_TILE, ns), lambda j, k: (k, j)), [pl.BlockSpec((None, dh, ns), lambda j, k: (j, 0, 0))],
               [jax.ShapeDtypeStruct((N_DEV, dh, ns), BF16)], (dh, ns), 1, after=after)[0]


LOG2_E = 1.4426950408889634


def _twice(u):
    return jnp.concatenate([u, u], axis=0)


def _dot2(x, u2):
    hi = x.astype(BF16)
    lo = (x - hi.astype(F32)).astype(BF16)
    return lax.dot_general(jnp.concatenate([hi, lo], axis=1), u2, _NN, preferred_element_type=F32)


def _softplus(l):
    e = jnp.exp2(jnp.abs(l) * (-LOG2_E))
    e1 = 1.0 + e
    return jnp.maximum(l, 0.0) + jnp.log(e1), e, e1


def _attn_fwd(hb, h, nb, seq, after):
    m = nb * seq
    d = SB_HEADS * HEAD_DIM
    tq, tk = ATT_TQ, ATT_TK
    n_sub = tq // tk
    nq = seq // tq
    sec = d // 128
    scale = 1.0 / math.sqrt(HEAD_DIM)
    heads = [slice(hh * HEAD_DIM, (hh + 1) * HEAD_DIM) for hh in range(2)]

    def body(q_ref, k_ref, v_ref, z_ref, after_ref, o_ref, a_ref, c_ref, carry_ref, acc_ref):
        qi = pl.program_id(2)
        u_after = _twice((lax.broadcasted_iota(jnp.int32, (tk, tk), 0)
                          > lax.broadcasted_iota(jnp.int32, (tk, tk), 1)).astype(BF16))
        qs = [(q_ref[:, cols].astype(F32) * scale).astype(BF16) for cols in heads]
        carry_ref[...] = jnp.zeros_like(carry_ref)
        acc_ref[...] = jnp.zeros_like(acc_ref)
        c_ref[...] = jnp.zeros_like(c_ref)

        def block(kb, r0, masked):
            n = tq - r0
            qrows = slice(r0, tq)
            krows = pl.ds(pl.multiple_of(kb * tk, tk), tk)
            lane = lax.broadcasted_iota(jnp.int32, (n, 128), 1)
            if masked:
                causal = lax.broadcasted_iota(jnp.int32, (n, tk), 1) < lax.broadcasted_iota(jnp.int32, (n, tk), 0)
            for hh, cols in enumerate(heads):
                carry = carry_ref[hh, qrows, :]
                l = lax.dot_general(qs[hh][qrows], k_ref[krows, cols], _NT, preferred_element_type=F32)
                sp, _, _ = _softplus(l)
                if masked:
                    sp = jnp.where(causal, sp, 0.0)
                after = carry + _dot2(sp, u_after)
                a = jnp.exp(l - (sp + after))
                if masked:
                    a = jnp.where(causal, a, 0.0)
                acc_ref[qrows, cols] += lax.dot_general(a.astype(BF16), v_ref[krows, cols], _NN,
                                                        preferred_element_type=F32)
                c_ref[qrows, :] += jnp.where(lane == hh * HEAD_DIM + kb, carry, 0.0)
                carry_ref[hh, qrows, :] = after[:, 0:1] + sp[:, 0:1]

        for j in reversed(range(n_sub)):
            block(n_sub * qi + j, j * tk, True)

        def step(i, _):
            block(n_sub * qi - 1 - i, 0, False)
            return 0

        lax.fori_loop(0, n_sub * qi, step, 0)
        o = acc_ref[...]
        o_ref[...] = o
        silu, _ = _silu_and_grad(z_ref[...])
        a_ref[...] = (o * silu).astype(BF16)

    qspec = pl.BlockSpec((tq, 128), lambda b, hp, qi: (b * nq + qi, hp))
    return pl.pallas_call(
        body, name="attn_fwd", grid=(nb, sec, nq),
        in_specs=[qspec,
                  pl.BlockSpec((seq, 128), lambda b, hp, qi: (b, sec + hp)),
                  pl.BlockSpec((seq, 128), lambda b, hp, qi: (b, 2 * sec + hp)),
                  pl.BlockSpec((tq, 128), lambda b, hp, qi: (b * nq + qi, 3 * sec + hp)), _after_spec()],
        out_specs=[qspec, qspec, pl.BlockSpec((None, tq, 128), lambda b, hp, qi: (hp, b * nq + qi, 0))],
        out_shape=[jax.ShapeDtypeStruct((m, d), F32), jax.ShapeDtypeStruct((m, d), BF16),
                   jax.ShapeDtypeStruct((sec, m, 128), F32)],
        scratch_shapes=[pltpu.VMEM((2, tq, 1), F32), pltpu.VMEM((tq, 128), F32)],
        compiler_params=_params("parallel", "parallel", "parallel"),
    )(hb, hb, hb, h, after)


def _attn_bwd(hb, h, o, carries, da, nb, seq, after):
    m = nb * seq
    d = SB_HEADS * HEAD_DIM
    tq, tk = ATT_TQ, ATT_TK
    n_sub = tq // tk
    nq = seq // tq
    sec = d // 128
    scale = 1.0 / math.sqrt(HEAD_DIM)
    heads = [slice(hh * HEAD_DIM, (hh + 1) * HEAD_DIM) for hh in range(2)]

    def body(q_ref, k_ref, v_ref, z_ref, o_ref, c_ref, da_ref, after_ref, dq_ref, dk_ref, dv_ref, dz_ref,
             dk_acc, dv_acc, dq_acc, gcarry_ref):
        qi = pl.program_id(2)

        @pl.when(qi == 0)
        def _():
            dk_acc[...] = jnp.zeros_like(dk_acc)
            dv_acc[...] = jnp.zeros_like(dv_acc)

        row = lax.broadcasted_iota(jnp.int32, (tk, tk), 0)
        col = lax.broadcasted_iota(jnp.int32, (tk, tk), 1)
        u_after = _twice((row > col).astype(BF16))
        u_before = _twice((row < col).astype(BF16))
        silu, dsilu = _silu_and_grad(z_ref[...])
        dav = da_ref[...]
        dz_ref[...] = (dav * o_ref[...] * dsilu).astype(BF16)
        do2 = (dav * silu).astype(BF16)
        qs = [(q_ref[:, cols].astype(F32) * scale).astype(BF16) for cols in heads]
        dos = [do2[:, cols] for cols in heads]
        dq_acc[...] = jnp.zeros_like(dq_acc)
        gcarry_ref[...] = jnp.zeros_like(gcarry_ref)

        def block(kb, r0, masked):
            n = tq - r0
            qrows = slice(r0, tq)
            krows = pl.ds(pl.multiple_of(kb * tk, tk), tk)
            lane = lax.broadcasted_iota(jnp.int32, (n, 128), 1)
            if masked:
                causal = lax.broadcasted_iota(jnp.int32, (n, tk), 1) < lax.broadcasted_iota(jnp.int32, (n, tk), 0)
            for hh, cols in enumerate(heads):
                q, do = qs[hh][qrows], dos[hh][qrows]
                ks = k_ref[krows, cols]
                l = lax.dot_general(q, ks, _NT, preferred_element_type=F32)
                sp, e, e1 = _softplus(l)
                if masked:
                    sp = jnp.where(causal, sp, 0.0)
                carry = jnp.sum(jnp.where(lane == hh * HEAD_DIM + kb, c_ref[qrows, :], 0.0), axis=1, keepdims=True)
                after = carry + _dot2(sp, u_after)
                a = jnp.exp(l - (sp + after))
                if masked:
                    a = jnp.where(causal, a, 0.0)
                dv_acc[krows, cols] += lax.dot_general(a.astype(BF16), do, _TN, preferred_element_type=F32)
                g = a * lax.dot_general(do, v_ref[krows, cols], _NT, preferred_element_type=F32)
                c = gcarry_ref[hh, qrows, :] + _dot2(g, u_before)
                r = pl.reciprocal(e1, approx=True)
                beta = jnp.where(l >= 0.0, r, e * r)
                dl = g - (g + c) * beta
                if masked:
                    dl = jnp.where(causal, dl, 0.0)
                dl_b = dl.astype(BF16)
                dq_acc[qrows, cols] += lax.dot_general(dl_b, ks, _NN, preferred_element_type=F32)
                dk_acc[krows, cols] += lax.dot_general(dl_b, q, _TN, preferred_element_type=F32)
                gcarry_ref[hh, qrows, :] = c[:, tk - 1:tk] + g[:, tk - 1:tk]

        def step(kb, _):
            block(kb, 0, False)
            return 0

        lax.fori_loop(0, n_sub * qi, step, 0)
        for j in range(n_sub):
            block(n_sub * qi + j, j * tk, True)
        dq_ref[...] = (dq_acc[...] * scale).astype(BF16)

        @pl.when(qi == nq - 1)
        def _():
            dk_ref[...] = dk_acc[...].astype(BF16)
            dv_ref[...] = dv_acc[...].astype(BF16)

    qspec = pl.BlockSpec((tq, 128), lambda b, hp, qi: (b * nq + qi, hp))
    kvspec = pl.BlockSpec((seq, 128), lambda b, hp, qi: (b, hp))
    return pl.pallas_call(
        body, name="attn_bwd", grid=(nb, sec, nq),
        in_specs=[qspec,
                  pl.BlockSpec((seq, 128), lambda b, hp, qi: (b, sec + hp)),
                  pl.BlockSpec((seq, 128), lambda b, hp, qi: (b, 2 * sec + hp)),
                  pl.BlockSpec((tq, 128), lambda b, hp, qi: (b * nq + qi, 3 * sec + hp)),
                  qspec,
                  pl.BlockSpec((None, tq, 128), lambda b, hp, qi: (hp, b * nq + qi, 0)),
                  qspec, _after_spec()],
        out_specs=[qspec, kvspec, kvspec, qspec],
        out_shape=[jax.ShapeDtypeStruct((m, d), BF16)] * 4,
        scratch_shapes=[pltpu.VMEM((seq, 128), F32), pltpu.VMEM((seq, 128), F32), pltpu.VMEM((tq, 128), F32),
                        pltpu.VMEM((2, tq, 1), F32)],
        compiler_params=_params("parallel", "parallel", "arbitrary"),
    )(hb, hb, hb, h, o, carries, da, after)


SUBLANES = 8


def _by_residue(taps):
    groups = []
    for res in range(SUBLANES):
        mine = [(off - res, k) for off, k in taps if off % SUBLANES == res]
        if mine:
            groups.append((res, max(a for a, _ in mine), mine))
    return groups


def _staged(src_ref, stage_ref, res, n):
    buf = stage_ref.at[res % 2]
    if res == 0:
        return src_ref
    buf[0:n, :] = src_ref[pl.ds(res, n), :]
    return buf


def _tap_sum(src_ref, stage_ref, w_ref, n_rows, taps, acc):
    for res, reach, mine in _by_residue(taps):
        shifted = _staged(src_ref, stage_ref, res, n_rows + reach)
        for a, k in mine:
            acc = acc + w_ref[k:k + 1, :] * shifted[a:a + n_rows, :]
    return acc


_CONV_TAPS = [(CV_HALO - CV_KERNEL + 1 + k, k) for k in range(CV_KERNEL)]
_CONV_TAPS_T = [(CV_KERNEL - 1 - k, k) for k in range(CV_KERNEL)]


def _conv_rows(pad_ref, stage_ref, w_ref, b_ref, n_rows):
    bias = jnp.broadcast_to(b_ref[...], (n_rows, b_ref.shape[1]))
    return _tap_sum(pad_ref, stage_ref, w_ref, n_rows, _CONV_TAPS, bias)


def _conv_fwd(h, conv_w, conv_b, ln_g, ln_b, nb, seq):
    m = nb * seq
    d = conv_b.shape[1]
    t = CONV_TILE
    tiles = seq // t
    hpt = t // CV_HALO

    def body(cv_ref, cg_ref, pv_ref, pg_ref, z_ref, w_ref, b_ref, g_ref, bb_ref, a_ref, pad_ref, stage_ref):
        first = pl.program_id(0) % tiles == 0
        pad_ref[0:CV_HALO, :] = jnp.where(first, 0.0, pv_ref[...] * _sigmoid(pg_ref[...]))
        pad_ref[CV_HALO:, :] = cv_ref[...] * _sigmoid(cg_ref[...])
        xhat, _ = _ln_stats(_conv_rows(pad_ref, stage_ref, w_ref, b_ref, t))
        s, _ = _silu_and_grad(xhat * g_ref[...] + bb_ref[...])
        sz, _ = _silu_and_grad(z_ref[...])
        a_ref[...] = (s * sz).astype(BF16)

    def main(c):
        return pl.BlockSpec((t, d), lambda i: (i, c))

    def prev(c):
        return pl.BlockSpec((CV_HALO, d), lambda i: (jnp.maximum(i * hpt - 1, 0), c))

    vec = pl.BlockSpec((1, d), lambda i: (0, 0))
    return pl.pallas_call(
        body, name="conv_fwd", grid=(m // t,),
        in_specs=[main(4), main(5), prev(4), prev(5), main(6),
                  pl.BlockSpec((CV_HALO, d), lambda i: (0, 0)), vec, vec, vec],
        out_specs=pl.BlockSpec((t, d), lambda i: (i, 0)),
        out_shape=jax.ShapeDtypeStruct((m, d), BF16),
        scratch_shapes=[pltpu.VMEM((CV_HALO + t, d), F32), pltpu.VMEM((2, CV_HALO + t, d), F32)],
        compiler_params=_params("parallel"),
    )(h, h, h, h, h, conv_w, conv_b, ln_g, ln_b)


def _conv_bwd(h, da, conv_w, conv_b, ln_g, ln_b, nb, seq):
    m = nb * seq
    d = conv_b.shape[1]
    t = CONV_TILE
    tiles = seq // t
    hpt = t // CV_HALO
    last_halo = m // CV_HALO - 1

    def body(cv_ref, cg_ref, pv_ref, pg_ref, nv_ref, ng_ref, z_ref, nz_ref, da_ref, nda_ref,
             w_ref, b_ref, g_ref, bb_ref, dh_ref, dw_ref, db_ref, dg_ref, dbb_ref, pad_ref, dc_ref, stage_ref):
        i = pl.program_id(0)

        @pl.when(i == 0)
        def _():
            dw_ref[...] = jnp.zeros_like(dw_ref)
            db_ref[...] = jnp.zeros_like(db_ref)
            dg_ref[...] = jnp.zeros_like(dg_ref)
            dbb_ref[...] = jnp.zeros_like(dbb_ref)

        first = i % tiles == 0
        last = i % tiles == tiles - 1
        cv = cv_ref[...]
        sg = _sigmoid(cg_ref[...])
        pad_ref[0:CV_HALO, :] = jnp.where(first, 0.0, pv_ref[...] * _sigmoid(pg_ref[...]))
        pad_ref[CV_HALO:CV_HALO + t, :] = cv * sg
        pad_ref[CV_HALO + t:, :] = nv_ref[...] * _sigmoid(ng_ref[...])

        def rows_bwd(conv_out, z, dav):
            xhat, rstd = _ln_stats(conv_out)
            s, ds = _silu_and_grad(xhat * g_ref[...] + bb_ref[...])
            sz, dsz = _silu_and_grad(z)
            dn = dav * sz * ds
            return _ln_bwd(dn * g_ref[...], xhat, rstd), dav * s * dsz, dn, xhat

        conv_all = _conv_rows(pad_ref, stage_ref, w_ref, b_ref, t + CV_HALO)
        dc, dz, dn, xhat = rows_bwd(conv_all[:t], z_ref[...], da_ref[...])
        dc_next, _, _, _ = rows_bwd(conv_all[t:], nz_ref[...], nda_ref[...])
        dc_ref[0:t, :] = dc
        dc_ref[t:, :] = jnp.where(last, 0.0, dc_next)
        dg_ref[...] += jnp.sum(dn * xhat, axis=0, keepdims=True)
        dbb_ref[...] += jnp.sum(dn, axis=0, keepdims=True)
        db_ref[...] += jnp.sum(dc, axis=0, keepdims=True)
        for res, reach, mine in _by_residue(_CONV_TAPS):
            shifted = _staged(pad_ref, stage_ref, res, t + reach)
            for a, k in mine:
                dw_ref[k:k + 1, :] += jnp.sum(dc * shifted[a:a + t, :], axis=0, keepdims=True)
        du = _tap_sum(dc_ref, stage_ref, w_ref, t, _CONV_TAPS_T, jnp.zeros((t, d), F32))
        dh_ref[:, 0:d] = (du * sg).astype(BF16)
        dh_ref[:, d:2 * d] = (du * cv * sg * (1.0 - sg)).astype(BF16)
        dh_ref[:, 2 * d:] = dz.astype(BF16)

    def main(c):
        return pl.BlockSpec((t, d), lambda i: (i, c))

    def prev(c):
        return pl.BlockSpec((CV_HALO, d), lambda i: (jnp.maximum(i * hpt - 1, 0), c))

    def nxt(c):
        return pl.BlockSpec((CV_HALO, d), lambda i: (jnp.minimum((i + 1) * hpt, last_halo), c))

    vec = pl.BlockSpec((1, d), lambda i: (0, 0))
    taps = pl.BlockSpec((CV_HALO, d), lambda i: (0, 0))
    return pl.pallas_call(
        body, name="conv_bwd", grid=(m // t,),
        in_specs=[main(4), main(5), prev(4), prev(5), nxt(4), nxt(5), main(6), nxt(6), main(0), nxt(0),
                  taps, vec, vec, vec],
        out_specs=[pl.BlockSpec((t, 3 * d), lambda i: (i, 0)), taps, vec, vec, vec],
        out_shape=[jax.ShapeDtypeStruct((m, 3 * d), BF16), jax.ShapeDtypeStruct((CV_HALO, d), F32),
                   jax.ShapeDtypeStruct((1, d), F32), jax.ShapeDtypeStruct((1, d), F32),
                   jax.ShapeDtypeStruct((1, d), F32)],
        scratch_shapes=[pltpu.VMEM((t + 2 * CV_HALO, d), F32), pltpu.VMEM((t + CV_HALO, d), F32),
                        pltpu.VMEM((2, t + 2 * CV_HALO, d), F32)],
        compiler_params=_params("arbitrary"),
    )(h, h, h, h, h, h, h, h, da, da, conv_w, conv_b, ln_g, ln_b)


_HBM = pl.BlockSpec(memory_space=pltpu.HBM)


def _place():
    return lax.axis_index("x"), lax.axis_index("y"), lax.axis_index("c")


def _slot(p):
    return 4 * p[0] + 2 * p[1] + p[2]


def _gather_mm_in_fwd(h0b, w_shard, after):
    m, d = h0b.shape
    ns = w_shard.shape[1]
    n_i = m // IN_TILE
    n_steps = N_DEV

    def places():
        x, y, c = _place()
        chips = [(1 - x, y), (x, 1 - y), (1 - x, 1 - y)]
        return (x, y, c), (x, y, 1 - c), chips

    me, sibling, chips = places()
    order = [me, sibling] + [(*chip, me[2]) for chip in chips] + [(*chip, sibling[2]) for chip in chips]
    order = jnp.stack([_slot(p) for p in order]).astype(jnp.int32)

    def body(order_ref, h_ref, w_ref, after_ref, o32_ref, o16_ref, wg_ref, wbuf, send_sems, recv_sems, own_sem,
             load_sems):
        s, i = pl.program_id(0), pl.program_id(1)
        me, sibling, chips = places()
        c = me[2]

        def remote(k, block, to, src=None):
            dst = wg_ref.at[_slot(block)]
            return pltpu.make_async_remote_copy(
                src_ref=dst if src is None else src, dst_ref=dst, send_sem=send_sems.at[k],
                recv_sem=recv_sems.at[k], device_id=to, device_id_type=MESH)

        def load(step):
            return pltpu.make_async_copy(wg_ref.at[order_ref[step]], wbuf.at[step % 2], load_sems.at[step % 2])

        own = pltpu.make_async_copy(w_ref, wg_ref.at[_slot(me)], own_sem)

        @pl.when((s == 0) & (i == 0))
        def _():
            own.start()
            remote(0, me, sibling, src=w_ref).start()
            for j, chip in enumerate(chips):
                remote(1 + j, me, (*chip, c), src=w_ref).start()
            first = pltpu.make_async_copy(w_ref, wbuf.at[0], load_sems.at[0])
            first.start()
            first.wait()

        for nxt in range(1, n_steps):
            @pl.when((s == nxt - 1) & (i == n_i - 1))
            def _():
                if nxt == 1:
                    remote(0, sibling, me).wait_recv()
                elif nxt <= 4:
                    j = nxt - 2
                    remote(1 + j, (*chips[j], c), me).wait_recv()
                    remote(4 + j, (*chips[j], c), sibling).start()
                else:
                    j = nxt - 5
                    remote(4 + j, (*chips[j], 1 - c), me).wait_recv()
                load(nxt).start()

        for step in range(1, n_steps):
            @pl.when((s == step) & (i == 0))
            def _():
                load(step).wait()

        part = lax.dot_general(h_ref[...], wbuf[s % 2], _NN, preferred_element_type=F32)
        o32_ref[...] = part
        o16_ref[...] = part.astype(BF16)

        @pl.when((s == n_steps - 1) & (i == n_i - 1))
        def _():
            remote(0, me, sibling, src=w_ref).wait_send()
            for j, chip in enumerate(chips):
                remote(1 + j, me, (*chip, c), src=w_ref).wait_send()
                remote(4 + j, (*chip, c), sibling).wait_send()
            own.wait()

    out = pl.BlockSpec((IN_TILE, ns), lambda s, i, order: (i, order[s]))
    grid_spec = pltpu.PrefetchScalarGridSpec(
        num_scalar_prefetch=1, grid=(n_steps, n_i),
        in_specs=[pl.BlockSpec((IN_TILE, d), lambda s, i, order: (i, 0)), _HBM, _ANY],
        out_specs=[out, out, _HBM],
        scratch_shapes=[pltpu.VMEM((2, d, ns), BF16), pltpu.SemaphoreType.DMA((7,)), pltpu.SemaphoreType.DMA((7,)),
                        pltpu.SemaphoreType.DMA, pltpu.SemaphoreType.DMA((2,))])
    return pl.pallas_call(
        body, name="gather_mm_in_fwd", grid_spec=grid_spec,
        out_shape=[jax.ShapeDtypeStruct((m, N_DEV * ns), F32), jax.ShapeDtypeStruct((m, N_DEV * ns), BF16),
                   jax.ShapeDtypeStruct((N_DEV, d, ns), BF16)],
        compiler_params=_params("arbitrary", "arbitrary"),
    )(order, h0b, w_shard, after)


def _exchange_partials(parts, whole):
    n_parts = len(parts)
    arrays = list(parts) + list(whole)
    n = len(arrays)

    def body(*refs):
        ins, outs = refs[:n], refs[n:2 * n]
        send_sems, recv_sems, local_sems = refs[2 * n:]
        x, y, c = _place()
        me = (x, y, c)

        def src_for(a, p):
            return ins[a].at[_slot(p)] if a < n_parts else ins[a]

        def copy(a, k, peer):
            return pltpu.make_async_remote_copy(
                src_ref=src_for(a, peer), dst_ref=outs[a].at[_slot(me)], send_sem=send_sems.at[7 * a + k],
                recv_sem=recv_sems.at[7 * a + k], device_id=peer, device_id_type=MESH)

        def landed(a, k, peer):
            return pltpu.make_async_remote_copy(
                src_ref=src_for(a, peer), dst_ref=outs[a].at[_slot(peer)], send_sem=send_sems.at[7 * a + k],
                recv_sem=recv_sems.at[7 * a + k], device_id=peer, device_id_type=MESH)

        peers = []
        for k in range(1, N_DEV):
            fx, fy, fc = (k >> 2) & 1, (k >> 1) & 1, k & 1
            peers.append((1 - x if fx else x, 1 - y if fy else y, 1 - c if fc else c))
        mine = [pltpu.make_async_copy(src_for(a, me), outs[a].at[_slot(me)], local_sems.at[a]) for a in range(n)]
        for cp in mine:
            cp.start()
        sent = [copy(a, k, peer) for a in range(n) for k, peer in enumerate(peers)]
        for cp in sent:
            cp.start()
        for a in range(n):
            for k, peer in enumerate(peers):
                landed(a, k, peer).wait_recv()
        for cp in sent:
            cp.wait_send()
        for cp in mine:
            cp.wait()

    out_shape = [jax.ShapeDtypeStruct(p.shape, p.dtype) for p in parts]
    out_shape += [jax.ShapeDtypeStruct((N_DEV,) + w.shape, w.dtype) for w in whole]
    return pl.pallas_call(
        body, name="exchange_grad_partials",
        in_specs=[_HBM] * n, out_specs=[_HBM] * n, out_shape=out_shape,
        scratch_shapes=[pltpu.SemaphoreType.DMA((7 * n,)), pltpu.SemaphoreType.DMA((7 * n,)),
                        pltpu.SemaphoreType.DMA((n,))],
    )(*arrays)


_SEM = pl.BlockSpec(memory_space=pltpu.SEMAPHORE)
_ANY = pl.BlockSpec(memory_space=pl.ANY)
_SIDE_EFFECT = pltpu.SideEffectType.DATAFLOW_SIDE_EFFECTING


def _peers_of(x, y, c):
    peers = []
    for k in range(1, N_DEV):
        fx, fy, fc = (k >> 2) & 1, (k >> 1) & 1, k & 1
        peers.append((1 - x if fx else x, 1 - y if fy else y, 1 - c if fc else c))
    return peers


def _send_start(name, parts, whole):
    n_parts = len(parts)
    arrays = list(parts) + list(whole)
    n = len(arrays)
    lands = [lax.empty(p.shape, p.dtype) for p in parts] + [lax.empty((N_DEV,) + w.shape, w.dtype) for w in whole]

    def body(*refs):
        srcs, land = refs[:n], refs[n:2 * n]
        send_sems, recv_sems = refs[2 * n], refs[2 * n + 1]
        token = refs[-1]
        x, y, c = _place()
        for k, peer in enumerate(_peers_of(x, y, c)):
            for a in range(n):
                pltpu.make_async_remote_copy(
                    src_ref=srcs[a].at[_slot(peer)] if a < n_parts else srcs[a], dst_ref=land[a].at[_slot((x, y, c))],
                    send_sem=send_sems.at[7 * a + k], recv_sem=recv_sems.at[7 * a + k],
                    device_id=peer, device_id_type=MESH).start()
        token[...] = jnp.zeros_like(token)

    operands = [pltpu.with_memory_space_constraint(a, pltpu.HBM) for a in arrays + lands]
    out = pl.pallas_call(
        body, name=name,
        out_shape=(pltpu.SemaphoreType.DMA((7 * n,)), pltpu.SemaphoreType.DMA((7 * n,)),
                   *[pltpu.HBM(a.shape, a.dtype) for a in arrays + lands], jax.ShapeDtypeStruct((8, 128), F32)),
        in_specs=[_HBM] * (2 * n),
        out_specs=(_SEM, _SEM, *[_HBM] * (2 * n), pl.BlockSpec(memory_space=pltpu.VMEM)),
        input_output_aliases={i: 2 + i for i in range(2 * n)},
        compiler_params=pltpu.CompilerParams(has_side_effects=_SIDE_EFFECT),
    )(*operands)
    return (out[0], out[1], list(out[2:2 + n]), list(out[2 + n:2 + 2 * n]), n_parts), out[-1]


def _send_wait(name, started, after):
    send_sems, recv_sems, srcs, lands, n_parts = started
    n = len(srcs)

    def body(*refs):
        src_refs, land_refs = refs[:n], refs[n:2 * n]
        send_sems, recv_sems = refs[2 * n], refs[2 * n + 1]
        x, y, c = _place()
        for k, peer in enumerate(_peers_of(x, y, c)):
            for a in range(n):
                copy = pltpu.make_async_remote_copy(
                    src_ref=src_refs[a].at[_slot(peer)] if a < n_parts else src_refs[a],
                    dst_ref=land_refs[a].at[_slot(peer)], send_sem=send_sems.at[7 * a + k],
                    recv_sem=recv_sems.at[7 * a + k], device_id=peer, device_id_type=MESH)
                copy.wait_send()
                copy.wait_recv()

    out = pl.pallas_call(
        body, name=name,
        out_shape=[pltpu.HBM(a.shape, a.dtype) for a in srcs + lands],
        in_specs=[_HBM] * (2 * n) + [_SEM, _SEM, _ANY], out_specs=[_HBM] * (2 * n),
        input_output_aliases={i: i for i in range(2 * n)},
        compiler_params=pltpu.CompilerParams(has_side_effects=_SIDE_EFFECT),
    )(*srcs, *lands, send_sems, recv_sems, after)
    return list(out[:n]), list(out[n:])


def _with_own(landed, own, me):
    return lax.dynamic_update_index_in_dim(landed, own, me, 0)


def _adamw(name, slabs, w, mom, var, rows):
    r, c = w.shape
    n_slabs = len(slabs)
    per_slab = r // n_slabs // rows

    def body(*refs):
        p_refs = refs[:n_slabs]
        w_ref, m_ref, v_ref, g_ref, d_ref, nm_ref, nv_ref = refs[n_slabs:]
        slab = pl.program_id(0) // per_slab
        g = None
        for p in range(N_DEV):
            part = p_refs[0][p]
            for k in range(1, n_slabs):
                part = jnp.where(slab == k, p_refs[k][p], part)
            g = part.astype(F32) if g is None else g + part.astype(F32)
        m_new = ADAM_B1 * m_ref[...] + (1.0 - ADAM_B1) * g
        v_new = ADAM_B2 * v_ref[...] + (1.0 - ADAM_B2) * (g * g)
        m_hat = m_new / (1.0 - ADAM_B1 ** ADAM_STEP)
        v_hat = v_new / (1.0 - ADAM_B2 ** ADAM_STEP)
        g_ref[...] = g
        d_ref[...] = -ADAM_LR * (m_hat / (jnp.sqrt(v_hat) + ADAM_EPS) + ADAM_WD * w_ref[...])
        nm_ref[...] = m_new
        nv_ref[...] = v_new

    def slab_spec(k):
        return pl.BlockSpec((N_DEV, rows, c), lambda i: (0, jnp.clip(i - k * per_slab, 0, per_slab - 1), 0))

    blk = pl.BlockSpec((rows, c), lambda i: (i, 0))
    return pl.pallas_call(
        body, name=name, grid=(r // rows,),
        in_specs=[slab_spec(k) for k in range(n_slabs)] + [blk, blk, blk],
        out_specs=[blk] * 4, out_shape=[jax.ShapeDtypeStruct((r, c), F32)] * 4,
        compiler_params=_params("parallel"),
    )(*slabs, w, mom, var)


def kernel(x, ln_in_g, ln_in_b, w_in, w_sb_proj, conv_w, conv_b, conv_ln_g, conv_ln_b, w_cv_proj, w_out, ln_post_g, ln_post_b, loss_target, m_ln_in_g, m_ln_in_b, m_w_in, m_w_sb_proj, m_conv_w, m_conv_b, m_conv_ln_g, m_conv_ln_b, m_w_cv_proj, m_w_out, m_ln_post_g, m_ln_post_b, v_ln_in_g, v_ln_in_b, v_w_in, v_w_sb_proj, v_conv_w, v_conv_b, v_conv_ln_g, v_conv_ln_b, v_w_cv_proj, v_w_out, v_ln_post_g, v_ln_post_b):
    nb, seq, d = x.shape
    m = nb * seq
    x2d = x.reshape(m, d)
    target = loss_target.reshape(m, d)
    rs = d // N_DEV
    pad_taps = ((0, CV_HALO - CV_KERNEL), (0, 0))

    proj_shards = jnp.stack([w_sb_proj[0], w_cv_proj[0], w_out[0]]).astype(BF16)
    conv_w_shard = jnp.pad(conv_w[0], pad_taps)
    me = _slot(_place())
    own = lambda parts: lax.dynamic_index_in_dim(parts, me, 0, keepdims=False)

    h0, h0b = _ln_in_fwd(x2d, ln_in_g.reshape(1, d), ln_in_b.reshape(1, d))
    h, hb, w_in_g = _gather_mm_in_fwd(h0b, w_in[0].astype(BF16), h0)
    proj_shards, conv_w_shard, hb = lax.optimization_barrier((proj_shards, conv_w_shard, hb))
    gather_small, started = _send_start("gather_proj_start", [], [proj_shards, conv_w_shard])
    o, a_sb, carries = _attn_fwd(hb, h, nb, seq, started)
    (proj_mine, conv_w_mine), (proj_g, conv_w_g) = _send_wait("gather_proj_wait", gather_small, a_sb)
    proj_g = _with_own(proj_g, proj_mine, me)
    conv_w_g = _with_own(conv_w_g, conv_w_mine, me)
    w_sb_g = proj_g[:, 0].reshape(d, d)
    w_cv_g = proj_g[:, 1].reshape(d, d)
    w_out_g = proj_g[:, 2].reshape(d, d)
    conv_w_full = conv_w_g.transpose(1, 0, 2).reshape(CV_HALO, d)
    a_cv = _conv_fwd(h, conv_w_full, conv_b, conv_ln_g, conv_ln_b, nb, seq)
    y_sb = _mm_nn("mm_sb_fwd", a_sb, w_sb_g)
    y_cv = _mm_nn("mm_cv_fwd", a_cv, w_cv_g)
    merged = _merge_fwd(h, y_sb, y_cv)
    mo = _mm_nn("mm_out_fwd", merged, w_out_g)
    loss_part, d_pre, d_pre_b, dg_post, db_post = _post_ln_loss(h0, mo, target, ln_post_g, ln_post_b)

    dw_out = _mm_tn("mm_out_bwd_w", merged, d_pre_b, BF16)
    d_merged = _mm_nt("mm_out_bwd_x", d_pre_b, w_out_g)
    dy_sb, dy_cv, d_gates = _merge_bwd(h, y_sb, y_cv, d_merged)
    dw_sb = _mm_tn("mm_sb_bwd_w", a_sb, dy_sb, BF16)
    dw_cv = _mm_tn("mm_cv_bwd_w", a_cv, dy_cv, BF16)
    da_sb = _mm_nt("mm_sb_bwd_x", dy_sb, w_sb_g)
    da_cv = _mm_nt("mm_cv_bwd_x", dy_cv, w_cv_g)
    dproj = jnp.stack([dw_sb.reshape(N_DEV, rs, d), dw_cv.reshape(N_DEV, rs, d), dw_out.reshape(N_DEV, rs, d)], axis=1)
    grads_proj, started = _send_start("grads_proj_start", [dproj], [])
    dq, dk, dv, dz_sb = _attn_bwd(hb, h, o, carries, da_sb, nb, seq, started)
    d_conv3, dconv_w, dconv_b, dconv_ln_g, dconv_ln_b = _conv_bwd(
        h, da_cv, conv_w_full, conv_b, conv_ln_g, conv_ln_b, nb, seq)
    (dproj,), (r_proj,) = _send_wait("grads_proj_wait", grads_proj, d_conv3)
    r_proj = _with_own(r_proj, own(dproj), me)
    dhb = jnp.concatenate([dq, dk, dv, dz_sb, d_conv3, d_gates], axis=1)
    dconv_w_parts = dconv_w.reshape(CV_HALO, N_DEV, d // N_DEV).transpose(1, 0, 2)
    dw_in_a = _mm_in_bwd_w(h0b, dhb, 0, None)
    grads_in_a, started = _send_start("grads_in_a_start", [dw_in_a, dconv_w_parts], [])
    dw_in_b = _mm_in_bwd_w(h0b, dhb, 1, started)
    grads_in_b, started = _send_start("grads_in_b_start", [dw_in_b], [])
    dh_mm = _mm_in_bwd_x(dhb, w_in_g, started)
    dx, dg_in, db_in = _ln_in_bwd(x2d, d_pre, dh_mm, ln_in_g.reshape(1, d))
    (dw_in_a, dconv_w_parts), (r_in_a, r_conv) = _send_wait("grads_in_a_wait", grads_in_a, dx)
    (dw_in_b,), (r_in_b,) = _send_wait("grads_in_b_wait", grads_in_b, dx)
    r_in = [_with_own(r_in_a, own(dw_in_a), me), _with_own(r_in_b, own(dw_in_b), me)]
    r_conv = _with_own(r_conv, own(dconv_w_parts), me)

    small = jnp.concatenate([dg_in, db_in, dconv_b, dconv_ln_g, dconv_ln_b, dg_post, db_post,
                             jnp.zeros((1, d), F32)], axis=0)
    (r_small,) = _exchange_partials([], [small])

    g_in, d_in, nm_in, nv_in = _adamw("adamw_w_in", r_in, w_in[0], m_w_in[0], v_w_in[0], 128)
    stack3 = lambda a, b, c: jnp.concatenate([a[0], b[0], c[0]], axis=0)
    g_pr, d_pr, nm_pr, nv_pr = _adamw(
        "adamw_proj", [r_proj.reshape(N_DEV, 3 * rs, d)], stack3(w_sb_proj, w_cv_proj, w_out),
        stack3(m_w_sb_proj, m_w_cv_proj, m_w_out), stack3(v_w_sb_proj, v_w_cv_proj, v_w_out), 3 * rs)
    padc = lambda a: jnp.pad(a[0], pad_taps)
    g_cw, d_cw, nm_cw, nv_cw = _adamw("adamw_conv_w", [r_conv], padc(conv_w), padc(m_conv_w), padc(v_conv_w), CV_HALO)
    vecs = lambda *a: jnp.concatenate([t.reshape(1, d) for t in a] + [jnp.ones((1, d), F32)], axis=0)
    g_sm, d_sm, nm_sm, nv_sm = _adamw(
        "adamw_vectors", [r_small],
        vecs(ln_in_g, ln_in_b, conv_b, conv_ln_g, conv_ln_b, ln_post_g, ln_post_b),
        vecs(m_ln_in_g, m_ln_in_b, m_conv_b, m_conv_ln_g, m_conv_ln_b, m_ln_post_g, m_ln_post_b),
        vecs(v_ln_in_g, v_ln_in_b, v_conv_b, v_conv_ln_g, v_conv_ln_b, v_ln_post_g, v_ln_post_b), 8)

    loss = lax.psum(loss_part[0, 0], ("x", "y", "c"))

    def leaves(big, pr, cw, sm):
        return (sm[0], sm[1], big[None], pr[None, 0:rs], cw[None, :CV_KERNEL], sm[2:3], sm[3:4], sm[4:5],
                pr[None, rs:2 * rs], pr[None, 2 * rs:], sm[5:6], sm[6:7])

    return (loss, dx.reshape(nb, seq, d), *leaves(g_in, g_pr, g_cw, g_sm), *leaves(d_in, d_pr, d_cw, d_sm),
            *leaves(nm_in, nm_pr, nm_cw, nm_sm), *leaves(nv_in, nv_pr, nv_cw, nv_sm))
```

```python
import functools
import math

import jax
import jax.numpy as jnp
from jax import lax
from jax.experimental import pallas as pl
from jax.experimental.pallas import tpu as pltpu

F32 = jnp.float32
BF16 = jnp.bfloat16
MESH = pl.DeviceIdType.MESH

N_DEV = 8
SB_HEADS = 16
HEAD_DIM = 64
CV_KERNEL = 31
CV_HALO = 32
LN_EPS = 1e-5
DEEPNORM_ALPHA = 2.0 ** 0.25
ADAM_LR, ADAM_B1, ADAM_B2, ADAM_EPS, ADAM_WD, ADAM_STEP = 0.001, 0.9, 0.999, 1e-08, 0.01, 10

ATT_TK = 256
ATT_TQ = 1024
ROW_TILE = 512
IN_TILE = 1024
CONV_TILE = 256
VMEM_LIMIT = 56 * 1024 * 1024


def _params(*sem):
    return pltpu.CompilerParams(dimension_semantics=sem, vmem_limit_bytes=VMEM_LIMIT)


def _sigmoid(x):
    return 1.0 / (1.0 + jnp.exp(-x))


def _silu_and_grad(x):
    s = _sigmoid(x)
    return x * s, s * (1.0 + x * (1.0 - s))


def _ln_stats(x):
    mu = jnp.mean(x, axis=-1, keepdims=True)
    xc = x - mu
    var = jnp.mean(xc * xc, axis=-1, keepdims=True)
    rstd = lax.rsqrt(var + LN_EPS)
    return xc * rstd, rstd


def _ln_bwd(dxhat, xhat, rstd):
    m1 = jnp.mean(dxhat, axis=-1, keepdims=True)
    m2 = jnp.mean(dxhat * xhat, axis=-1, keepdims=True)
    return rstd * (dxhat - m1 - xhat * m2)


def _ln_in_fwd(x2d, g, b):
    m, d = x2d.shape

    def body(x_ref, g_ref, b_ref, h_ref, hb_ref):
        xhat, _ = _ln_stats(x_ref[...])
        y = xhat * g_ref[...] + b_ref[...]
        h_ref[...] = y
        hb_ref[...] = y.astype(BF16)

    row = pl.BlockSpec((ROW_TILE, d), lambda i: (i, 0))
    vec = pl.BlockSpec((1, d), lambda i: (0, 0))
    return pl.pallas_call(
        body, name="ln_in_fwd", grid=(m // ROW_TILE,),
        in_specs=[row, vec, vec], out_specs=[row, row],
        out_shape=[jax.ShapeDtypeStruct((m, d), F32), jax.ShapeDtypeStruct((m, d), BF16)],
        compiler_params=_params("parallel"),
    )(x2d, g, b)


def _ln_in_bwd(x2d, d_pre, dh_mm, g):
    m, d = x2d.shape

    def body(x_ref, dp_ref, dm_ref, g_ref, dx_ref, dg_ref, db_ref):
        @pl.when(pl.program_id(0) == 0)
        def _():
            dg_ref[...] = jnp.zeros_like(dg_ref)
            db_ref[...] = jnp.zeros_like(db_ref)

        xhat, rstd = _ln_stats(x_ref[...])
        dh = DEEPNORM_ALPHA * dp_ref[...] + dm_ref[...]
        dg_ref[...] += jnp.sum(dh * xhat, axis=0, keepdims=True)
        db_ref[...] += jnp.sum(dh, axis=0, keepdims=True)
        dx_ref[...] = _ln_bwd(dh * g_ref[...], xhat, rstd)

    row = pl.BlockSpec((ROW_TILE, d), lambda i: (i, 0))
    vec = pl.BlockSpec((1, d), lambda i: (0, 0))
    return pl.pallas_call(
        body, name="ln_in_bwd", grid=(m // ROW_TILE,),
        in_specs=[row, row, row, vec], out_specs=[row, vec, vec],
        out_shape=[jax.ShapeDtypeStruct((m, d), F32), jax.ShapeDtypeStruct((1, d), F32),
                   jax.ShapeDtypeStruct((1, d), F32)],
        compiler_params=_params("arbitrary"),
    )(x2d, d_pre, dh_mm, g)


def _merge_fwd(h, y_sb, y_cv):
    m, d = y_sb.shape

    def body(gs_ref, gc_ref, ys_ref, yc_ref, out_ref):
        out_ref[...] = (_sigmoid(gs_ref[...]) * ys_ref[...] + _sigmoid(gc_ref[...]) * yc_ref[...]).astype(BF16)

    row = pl.BlockSpec((ROW_TILE, d), lambda i: (i, 0))
    return pl.pallas_call(
        body, name="merge_fwd", grid=(m // ROW_TILE,),
        in_specs=[pl.BlockSpec((ROW_TILE, d), lambda i: (i, 7)), pl.BlockSpec((ROW_TILE, d), lambda i: (i, 8)), row, row],
        out_specs=row, out_shape=jax.ShapeDtypeStruct((m, d), BF16),
        compiler_params=_params("parallel"),
    )(h, h, y_sb, y_cv)


def _merge_bwd(h, y_sb, y_cv, dm):
    m, d = y_sb.shape

    def body(gs_ref, gc_ref, ys_ref, yc_ref, dm_ref, dys_ref, dyc_ref, dgate_ref):
        dmv = dm_ref[...]
        ss = _sigmoid(gs_ref[...])
        sc = _sigmoid(gc_ref[...])
        dys_ref[...] = (ss * dmv).astype(BF16)
        dyc_ref[...] = (sc * dmv).astype(BF16)
        dgate_ref[:, :d] = (dmv * ys_ref[...] * ss * (1.0 - ss)).astype(BF16)
        dgate_ref[:, d:] = (dmv * yc_ref[...] * sc * (1.0 - sc)).astype(BF16)

    row = pl.BlockSpec((ROW_TILE, d), lambda i: (i, 0))
    return pl.pallas_call(
        body, name="merge_bwd", grid=(m // ROW_TILE,),
        in_specs=[pl.BlockSpec((ROW_TILE, d), lambda i: (i, 7)), pl.BlockSpec((ROW_TILE, d), lambda i: (i, 8)), row, row, row],
        out_specs=[row, row, pl.BlockSpec((ROW_TILE, 2 * d), lambda i: (i, 0))],
        out_shape=[jax.ShapeDtypeStruct((m, d), BF16), jax.ShapeDtypeStruct((m, d), BF16),
                   jax.ShapeDtypeStruct((m, 2 * d), BF16)],
        compiler_params=_params("parallel"),
    )(h, h, y_sb, y_cv, dm)


def _post_ln_loss(h0, mo, target, g, b):
    m, d = h0.shape

    def body(h_ref, mo_ref, t_ref, g_ref, b_ref, loss_ref, dp_ref, dpb_ref, dg_ref, db_ref):
        @pl.when(pl.program_id(0) == 0)
        def _():
            loss_ref[...] = jnp.zeros_like(loss_ref)
            dg_ref[...] = jnp.zeros_like(dg_ref)
            db_ref[...] = jnp.zeros_like(db_ref)

        xhat, rstd = _ln_stats(DEEPNORM_ALPHA * h_ref[...] + mo_ref[...])
        err = xhat * g_ref[...] + b_ref[...] - t_ref[...]
        per_row = jnp.mean(err * err, axis=-1, keepdims=True)
        loss_ref[...] += 0.5 * jnp.sum(per_row, axis=0, keepdims=True)
        dy = err * (1.0 / d)
        dg_ref[...] += jnp.sum(dy * xhat, axis=0, keepdims=True)
        db_ref[...] += jnp.sum(dy, axis=0, keepdims=True)
        dp = _ln_bwd(dy * g_ref[...], xhat, rstd)
        dp_ref[...] = dp
        dpb_ref[...] = dp.astype(BF16)

    row = pl.BlockSpec((ROW_TILE, d), lambda i: (i, 0))
    vec = pl.BlockSpec((1, d), lambda i: (0, 0))
    one = pl.BlockSpec((1, 1), lambda i: (0, 0))
    return pl.pallas_call(
        body, name="post_ln_loss", grid=(m // ROW_TILE,),
        in_specs=[row, row, row, vec, vec], out_specs=[one, row, row, vec, vec],
        out_shape=[jax.ShapeDtypeStruct((1, 1), F32), jax.ShapeDtypeStruct((m, d), F32),
                   jax.ShapeDtypeStruct((m, d), BF16), jax.ShapeDtypeStruct((1, d), F32),
                   jax.ShapeDtypeStruct((1, d), F32)],
        compiler_params=_params("arbitrary"),
    )(h0, mo, target, g, b)


_NN = (((1,), (0,)), ((), ()))
_NT = (((1,), (1,)), ((), ()))
_TN = (((0,), (0,)), ((), ()))


def _mm(name, a, b, dims, grid, a_spec, b_spec, out_specs, out_shape, acc_shape, k_axis, send=()):
    n_k = 1 if k_axis is None else grid[k_axis]
    n_out = len(out_shape)
    n_send = len(send)

    def body(a_ref, b_ref, *rest):
        src_refs, rest = rest[:n_send], rest[n_send:]
        outs, land_refs, acc_ref = rest[:n_out], rest[n_out:n_out + n_send], rest[n_out + n_send]
        if n_send:
            send_sems, recv_sems = rest[n_out + n_send + 1:]
            ids = [pl.program_id(ax) for ax in range(len(grid))]
            first = functools.reduce(jnp.logical_and, [i == 0 for i in ids])
            last = functools.reduce(jnp.logical_and, [i == n - 1 for i, n in zip(ids, grid)])
            x, y, c = _place()

            def copy(t, k, peer, lands_at):
                return pltpu.make_async_remote_copy(
                    src_ref=src_refs[t].at[_slot(peer)], dst_ref=land_refs[t].at[_slot(lands_at)],
                    send_sem=send_sems.at[7 * t + k], recv_sem=recv_sems.at[7 * t + k],
                    device_id=peer, device_id_type=MESH)

            @pl.when(first)
            def _():
                for k, peer in enumerate(_peers_of(x, y, c)):
                    for t in range(n_send):
                        copy(t, k, peer, (x, y, c)).start()

        part = lax.dot_general(a_ref[...].astype(BF16), b_ref[...].astype(BF16), dims, preferred_element_type=F32)
        if n_k == 1:
            for o in outs:
                o[...] = part.astype(o.dtype)
        else:
            k = pl.program_id(k_axis)

            @pl.when(k == 0)
            def _():
                acc_ref[...] = part

            @pl.when(k > 0)
            def _():
                acc_ref[...] += part

            @pl.when(k == n_k - 1)
            def _():
                for o in outs:
                    o[...] = acc_ref[...].astype(o.dtype)

        if n_send:
            @pl.when(last)
            def _():
                for k, peer in enumerate(_peers_of(x, y, c)):
                    for t in range(n_send):
                        arrived = copy(t, k, peer, peer)
                        arrived.wait_send()
                        arrived.wait_recv()

    sem = tuple("arbitrary" for _ in grid) if n_send else tuple(
        "arbitrary" if ax == k_axis else "parallel" for ax in range(len(grid)))
    scratch = [pltpu.VMEM(acc_shape, F32)]
    if n_send:
        scratch += [pltpu.SemaphoreType.DMA((7 * n_send,)), pltpu.SemaphoreType.DMA((7 * n_send,))]
    return pl.pallas_call(
        body, name=name, grid=grid, in_specs=[a_spec, b_spec] + [_HBM] * n_send,
        out_specs=list(out_specs) + [_HBM] * n_send,
        out_shape=list(out_shape) + [jax.ShapeDtypeStruct(s.shape, s.dtype) for s in send],
        scratch_shapes=scratch, compiler_params=_params(*sem),
    )(a, b, *send)


def _mm_nn(name, a, b, out_dtype=F32):
    m, k = a.shape
    n = b.shape[1]
    return _mm(name, a, b, _NN, (m // ROW_TILE,), pl.BlockSpec((ROW_TILE, k), lambda i: (i, 0)),
               pl.BlockSpec((k, n), lambda i: (0, 0)), [pl.BlockSpec((ROW_TILE, n), lambda i: (i, 0))],
               [jax.ShapeDtypeStruct((m, n), out_dtype)], (8, 128), None)[0]


def _mm_nt(name, a, b, out_dtype=F32):
    m, k = a.shape
    n = b.shape[0]
    return _mm(name, a, b, _NT, (m // ROW_TILE,), pl.BlockSpec((ROW_TILE, k), lambda i: (i, 0)),
               pl.BlockSpec((n, k), lambda i: (0, 0)), [pl.BlockSpec((ROW_TILE, n), lambda i: (i, 0))],
               [jax.ShapeDtypeStruct((m, n), out_dtype)], (8, 128), None)[0]


def _mm_tn(name, a, b, out_dtype):
    m, k = a.shape
    n = b.shape[1]
    return _mm(name, a, b, _TN, (m // ROW_TILE,), pl.BlockSpec((ROW_TILE, k), lambda i: (i, 0)),
               pl.BlockSpec((ROW_TILE, n), lambda i: (i, 0)), [pl.BlockSpec((k, n), lambda i: (0, 0))],
               [jax.ShapeDtypeStruct((k, n), out_dtype)], (k, n), 0)[0]


def _mm_in_bwd_x(dhb, w_g, send):
    m = dhb.shape[0]
    _, d, ns = w_g.shape
    return _mm("mm_in_bwd_x", dhb, w_g, _NT, (m // IN_TILE, N_DEV), pl.BlockSpec((IN_TILE, ns), lambda i, j: (i, j)),
               pl.BlockSpec((None, d, ns), lambda i, j: (j, 0, 0)), [pl.BlockSpec((IN_TILE, d), lambda i, j: (i, 0))],
               [jax.ShapeDtypeStruct((m, d), F32)], (IN_TILE, d), 1, send=send)


def _mm_in_bwd_w(h0b, dhb, half, send):
    m, d = h0b.shape
    ns = dhb.shape[1] // N_DEV
    dh = d // 2
    return _mm(f"mm_in_bwd_w{half}", h0b, dhb, _TN, (N_DEV, m // IN_TILE),
               pl.BlockSpec((IN_TILE, dh), lambda j, k: (k, half)),
               pl.BlockSpec((IN_TILE, ns), lambda j, k: (k, j)), [pl.BlockSpec((None, dh, ns), lambda j, k: (j, 0, 0))],
               [jax.ShapeDtypeStruct((N_DEV, dh, ns), BF16)], (dh, ns), 1, send=send)


LOG2_E = 1.4426950408889634
DEAD_AFTER = 110.0


def _twice(u):
    return jnp.concatenate([u, u], axis=0)


def _dot2(x, u2):
    hi = x.astype(BF16)
    lo = (x - hi.astype(F32)).astype(BF16)
    return lax.dot_general(jnp.concatenate([hi, lo], axis=1), u2, _NN, preferred_element_type=F32)


def _softplus(l):
    e = jnp.exp2(jnp.abs(l) * (-LOG2_E))
    e1 = 1.0 + e
    return jnp.maximum(l, 0.0) + jnp.log(e1), e, e1


def _attn_fwd(hb, h, nb, seq):
    m = nb * seq
    d = SB_HEADS * HEAD_DIM
    tq, tk = ATT_TQ, ATT_TK
    n_sub = tq // tk
    nq = seq // tq
    sec = d // 128
    scale = 1.0 / math.sqrt(HEAD_DIM)
    heads = [slice(hh * HEAD_DIM, (hh + 1) * HEAD_DIM) for hh in range(2)]

    def body(q_ref, k_ref, v_ref, z_ref, o_ref, a_ref, c_ref, carry_ref, acc_ref):
        qi = pl.program_id(2)
        u_after = _twice((lax.broadcasted_iota(jnp.int32, (tk, tk), 0)
                          > lax.broadcasted_iota(jnp.int32, (tk, tk), 1)).astype(BF16))
        qs = [(q_ref[:, cols].astype(F32) * scale).astype(BF16) for cols in heads]
        carry_ref[...] = jnp.zeros_like(carry_ref)
        acc_ref[...] = jnp.zeros_like(acc_ref)
        c_ref[...] = jnp.zeros_like(c_ref)

        def block(kb, r0, masked):
            n = tq - r0
            qrows = slice(r0, tq)
            krows = pl.ds(pl.multiple_of(kb * tk, tk), tk)
            lane = lax.broadcasted_iota(jnp.int32, (n, 128), 1)
            if masked:
                causal = lax.broadcasted_iota(jnp.int32, (n, tk), 1) < lax.broadcasted_iota(jnp.int32, (n, tk), 0)
            for hh, cols in enumerate(heads):
                carry = carry_ref[hh, qrows, :]
                l = lax.dot_general(qs[hh][qrows], k_ref[krows, cols], _NT, preferred_element_type=F32)
                sp, _, _ = _softplus(l)
                if masked:
                    sp = jnp.where(causal, sp, 0.0)
                after = carry + _dot2(sp, u_after)
                a = jnp.exp(l - (sp + after))
                if masked:
                    a = jnp.where(causal, a, 0.0)
                acc_ref[qrows, cols] += lax.dot_general(a.astype(BF16), v_ref[krows, cols], _NN,
                                                        preferred_element_type=F32)
                c_ref[qrows, :] += jnp.where(lane == hh * HEAD_DIM + kb, carry, 0.0)
                carry_ref[hh, qrows, :] = after[:, 0:1] + sp[:, 0:1]

        for j in reversed(range(n_sub)):
            block(n_sub * qi + j, j * tk, True)

        def step(i, _):
            kb = n_sub * qi - 1 - i
            alive = jnp.min(carry_ref[...]) <= DEAD_AFTER

            @pl.when(alive)
            def _():
                block(kb, 0, False)

            @pl.when(jnp.logical_not(alive))
            def _():
                lane = lax.broadcasted_iota(jnp.int32, (tq, 128), 1)
                for hh in range(2):
                    c_ref[...] += jnp.where(lane == hh * HEAD_DIM + kb, carry_ref[hh], 0.0)

            return 0

        lax.fori_loop(0, n_sub * qi, step, 0)
        o = acc_ref[...]
        o_ref[...] = o
        silu, _ = _silu_and_grad(z_ref[...])
        a_ref[...] = (o * silu).astype(BF16)

    qspec = pl.BlockSpec((tq, 128), lambda b, hp, qi: (b * nq + qi, hp))
    return pl.pallas_call(
        body, name="attn_fwd", grid=(nb, sec, nq),
        in_specs=[qspec,
                  pl.BlockSpec((seq, 128), lambda b, hp, qi: (b, sec + hp)),
                  pl.BlockSpec((seq, 128), lambda b, hp, qi: (b, 2 * sec + hp)),
                  pl.BlockSpec((tq, 128), lambda b, hp, qi: (b * nq + qi, 3 * sec + hp))],
        out_specs=[qspec, qspec, pl.BlockSpec((None, tq, 128), lambda b, hp, qi: (hp, b * nq + qi, 0))],
        out_shape=[jax.ShapeDtypeStruct((m, d), F32), jax.ShapeDtypeStruct((m, d), BF16),
                   jax.ShapeDtypeStruct((sec, m, 128), F32)],
        scratch_shapes=[pltpu.VMEM((2, tq, 1), F32), pltpu.VMEM((tq, 128), F32)],
        compiler_params=_params("parallel", "parallel", "parallel"),
    )(hb, hb, hb, h)


def _attn_bwd(hb, h, o, carries, da, nb, seq):
    m = nb * seq
    d = SB_HEADS * HEAD_DIM
    tq, tk = ATT_TQ, ATT_TK
    n_sub = tq // tk
    nq = seq // tq
    sec = d // 128
    scale = 1.0 / math.sqrt(HEAD_DIM)
    heads = [slice(hh * HEAD_DIM, (hh + 1) * HEAD_DIM) for hh in range(2)]

    def body(q_ref, k_ref, v_ref, z_ref, o_ref, c_ref, da_ref, dq_ref, dk_ref, dv_ref, dz_ref,
             dk_acc, dv_acc, dq_acc, gcarry_ref):
        qi = pl.program_id(2)

        @pl.when(qi == 0)
        def _():
            dk_acc[...] = jnp.zeros_like(dk_acc)
            dv_acc[...] = jnp.zeros_like(dv_acc)

        row = lax.broadcasted_iota(jnp.int32, (tk, tk), 0)
        col = lax.broadcasted_iota(jnp.int32, (tk, tk), 1)
        u_after = _twice((row > col).astype(BF16))
        u_before = _twice((row < col).astype(BF16))
        silu, dsilu = _silu_and_grad(z_ref[...])
        dav = da_ref[...]
        dz_ref[...] = (dav * o_ref[...] * dsilu).astype(BF16)
        do2 = (dav * silu).astype(BF16)
        qs = [(q_ref[:, cols].astype(F32) * scale).astype(BF16) for cols in heads]
        dos = [do2[:, cols] for cols in heads]
        dq_acc[...] = jnp.zeros_like(dq_acc)
        gcarry_ref[...] = jnp.zeros_like(gcarry_ref)

        def block(kb, r0, masked):
            n = tq - r0
            qrows = slice(r0, tq)
            krows = pl.ds(pl.multiple_of(kb * tk, tk), tk)
            lane = lax.broadcasted_iota(jnp.int32, (n, 128), 1)
            if masked:
                causal = lax.broadcasted_iota(jnp.int32, (n, tk), 1) < lax.broadcasted_iota(jnp.int32, (n, tk), 0)
            for hh, cols in enumerate(heads):
                q, do = qs[hh][qrows], dos[hh][qrows]
                ks = k_ref[krows, cols]
                l = lax.dot_general(q, ks, _NT, preferred_element_type=F32)
                sp, e, e1 = _softplus(l)
                if masked:
                    sp = jnp.where(causal, sp, 0.0)
                carry = jnp.sum(jnp.where(lane == hh * HEAD_DIM + kb, c_ref[qrows, :], 0.0), axis=1, keepdims=True)
                after = carry + _dot2(sp, u_after)
                a = jnp.exp(l - (sp + after))
                if masked:
                    a = jnp.where(causal, a, 0.0)
                dv_acc[krows, cols] += lax.dot_general(a.astype(BF16), do, _TN, preferred_element_type=F32)
                g = a * lax.dot_general(do, v_ref[krows, cols], _NT, preferred_element_type=F32)
                c = gcarry_ref[hh, qrows, :] + _dot2(g, u_before)
                r = pl.reciprocal(e1, approx=True)
                beta = jnp.where(l >= 0.0, r, e * r)
                dl = g - (g + c) * beta
                if masked:
                    dl = jnp.where(causal, dl, 0.0)
                dl_b = dl.astype(BF16)
                dq_acc[qrows, cols] += lax.dot_general(dl_b, ks, _NN, preferred_element_type=F32)
                dk_acc[krows, cols] += lax.dot_general(dl_b, q, _TN, preferred_element_type=F32)
                gcarry_ref[hh, qrows, :] = c[:, tk - 1:tk] + g[:, tk - 1:tk]

        def step(kb, _):
            lane = lax.broadcasted_iota(jnp.int32, (tq, 128), 1)
            mine = (lane == kb) | (lane == HEAD_DIM + kb)
            alive = jnp.min(jnp.where(mine, c_ref[...], jnp.inf)) <= DEAD_AFTER

            @pl.when(alive)
            def _():
                block(kb, 0, False)

            return 0

        lax.fori_loop(0, n_sub * qi, step, 0)
        for j in range(n_sub):
            block(n_sub * qi + j, j * tk, True)
        dq_ref[...] = (dq_acc[...] * scale).astype(BF16)

        @pl.when(qi == nq - 1)
        def _():
            dk_ref[...] = dk_acc[...].astype(BF16)
            dv_ref[...] = dv_acc[...].astype(BF16)

    qspec = pl.BlockSpec((tq, 128), lambda b, hp, qi: (b * nq + qi, hp))
    kvspec = pl.BlockSpec((seq, 128), lambda b, hp, qi: (b, hp))
    return pl.pallas_call(
        body, name="attn_bwd", grid=(nb, sec, nq),
        in_specs=[qspec,
                  pl.BlockSpec((seq, 128), lambda b, hp, qi: (b, sec + hp)),
                  pl.BlockSpec((seq, 128), lambda b, hp, qi: (b, 2 * sec + hp)),
                  pl.BlockSpec((tq, 128), lambda b, hp, qi: (b * nq + qi, 3 * sec + hp)),
                  qspec,
                  pl.BlockSpec((None, tq, 128), lambda b, hp, qi: (hp, b * nq + qi, 0)),
                  qspec],
        out_specs=[qspec, kvspec, kvspec, qspec],
        out_shape=[jax.ShapeDtypeStruct((m, d), BF16)] * 4,
        scratch_shapes=[pltpu.VMEM((seq, 128), F32), pltpu.VMEM((seq, 128), F32), pltpu.VMEM((tq, 128), F32),
                        pltpu.VMEM((2, tq, 1), F32)],
        compiler_params=_params("parallel", "parallel", "arbitrary"),
    )(hb, hb, hb, h, o, carries, da)


SUBLANES = 8


def _by_residue(taps):
    groups = []
    for res in range(SUBLANES):
        mine = [(off - res, k) for off, k in taps if off % SUBLANES == res]
        if mine:
            groups.append((res, max(a for a, _ in mine), mine))
    return groups


def _staged(src_ref, stage_ref, res, n):
    buf = stage_ref.at[res % 2]
    if res == 0:
        return src_ref
    buf[0:n, :] = src_ref[pl.ds(res, n), :]
    return buf


def _tap_sum(src_ref, stage_ref, w_ref, n_rows, taps, acc):
    for res, reach, mine in _by_residue(taps):
        shifted = _staged(src_ref, stage_ref, res, n_rows + reach)
        for a, k in mine:
            acc = acc + w_ref[k:k + 1, :] * shifted[a:a + n_rows, :]
    return acc


_CONV_TAPS = [(CV_HALO - CV_KERNEL + 1 + k, k) for k in range(CV_KERNEL)]
_CONV_TAPS_T = [(CV_KERNEL - 1 - k, k) for k in range(CV_KERNEL)]


def _conv_rows(pad_ref, stage_ref, w_ref, b_ref, n_rows):
    bias = jnp.broadcast_to(b_ref[...], (n_rows, b_ref.shape[1]))
    return _tap_sum(pad_ref, stage_ref, w_ref, n_rows, _CONV_TAPS, bias)


def _conv_fwd(h, conv_w, conv_b, ln_g, ln_b, nb, seq):
    m = nb * seq
    d = conv_b.shape[1]
    t = CONV_TILE
    tiles = seq // t
    hpt = t // CV_HALO

    def body(cv_ref, cg_ref, pv_ref, pg_ref, z_ref, w_ref, b_ref, g_ref, bb_ref, a_ref, pad_ref, stage_ref):
        first = pl.program_id(0) % tiles == 0
        pad_ref[0:CV_HALO, :] = jnp.where(first, 0.0, pv_ref[...] * _sigmoid(pg_ref[...]))
        pad_ref[CV_HALO:, :] = cv_ref[...] * _sigmoid(cg_ref[...])
        xhat, _ = _ln_stats(_conv_rows(pad_ref, stage_ref, w_ref, b_ref, t))
        s, _ = _silu_and_grad(xhat * g_ref[...] + bb_ref[...])
        sz, _ = _silu_and_grad(z_ref[...])
        a_ref[...] = (s * sz).astype(BF16)

    def main(c):
        return pl.BlockSpec((t, d), lambda i: (i, c))

    def prev(c):
        return pl.BlockSpec((CV_HALO, d), lambda i: (jnp.maximum(i * hpt - 1, 0), c))

    vec = pl.BlockSpec((1, d), lambda i: (0, 0))
    return pl.pallas_call(
        body, name="conv_fwd", grid=(m // t,),
        in_specs=[main(4), main(5), prev(4), prev(5), main(6),
                  pl.BlockSpec((CV_HALO, d), lambda i: (0, 0)), vec, vec, vec],
        out_specs=pl.BlockSpec((t, d), lambda i: (i, 0)),
        out_shape=jax.ShapeDtypeStruct((m, d), BF16),
        scratch_shapes=[pltpu.VMEM((CV_HALO + t, d), F32), pltpu.VMEM((2, CV_HALO + t, d), F32)],
        compiler_params=_params("parallel"),
    )(h, h, h, h, h, conv_w, conv_b, ln_g, ln_b)


def _conv_bwd(h, da, conv_w, conv_b, ln_g, ln_b, nb, seq):
    m = nb * seq
    d = conv_b.shape[1]
    t = CONV_TILE
    tiles = seq // t
    hpt = t // CV_HALO
    last_halo = m // CV_HALO - 1

    def body(cv_ref, cg_ref, pv_ref, pg_ref, nv_ref, ng_ref, z_ref, nz_ref, da_ref, nda_ref,
             w_ref, b_ref, g_ref, bb_ref, dh_ref, dw_ref, db_ref, dg_ref, dbb_ref, pad_ref, dc_ref, stage_ref):
        i = pl.program_id(0)

        @pl.when(i == 0)
        def _():
            dw_ref[...] = jnp.zeros_like(dw_ref)
            db_ref[...] = jnp.zeros_like(db_ref)
            dg_ref[...] = jnp.zeros_like(dg_ref)
            dbb_ref[...] = jnp.zeros_like(dbb_ref)

        first = i % tiles == 0
        last = i % tiles == tiles - 1
        cv = cv_ref[...]
        sg = _sigmoid(cg_ref[...])
        pad_ref[0:CV_HALO, :] = jnp.where(first, 0.0, pv_ref[...] * _sigmoid(pg_ref[...]))
        pad_ref[CV_HALO:CV_HALO + t, :] = cv * sg
        pad_ref[CV_HALO + t:, :] = nv_ref[...] * _sigmoid(ng_ref[...])

        def rows_bwd(conv_out, z, dav):
            xhat, rstd = _ln_stats(conv_out)
            s, ds = _silu_and_grad(xhat * g_ref[...] + bb_ref[...])
            sz, dsz = _silu_and_grad(z)
            dn = dav * sz * ds
            return _ln_bwd(dn * g_ref[...], xhat, rstd), dav * s * dsz, dn, xhat

        conv_all = _conv_rows(pad_ref, stage_ref, w_ref, b_ref, t + CV_HALO)
        dc, dz, dn, xhat = rows_bwd(conv_all[:t], z_ref[...], da_ref[...])
        dc_next, _, _, _ = rows_bwd(conv_all[t:], nz_ref[...], nda_ref[...])
        dc_ref[0:t, :] = dc
        dc_ref[t:, :] = jnp.where(last, 0.0, dc_next)
        dg_ref[...] += jnp.sum(dn * xhat, axis=0, keepdims=True)
        dbb_ref[...] += jnp.sum(dn, axis=0, keepdims=True)
        db_ref[...] += jnp.sum(dc, axis=0, keepdims=True)
        for res, reach, mine in _by_residue(_CONV_TAPS):
            shifted = _staged(pad_ref, stage_ref, res, t + reach)
            for a, k in mine:
                dw_ref[k:k + 1, :] += jnp.sum(dc * shifted[a:a + t, :], axis=0, keepdims=True)
        du = _tap_sum(dc_ref, stage_ref, w_ref, t, _CONV_TAPS_T, jnp.zeros((t, d), F32))
        dh_ref[:, 0:d] = (du * sg).astype(BF16)
        dh_ref[:, d:2 * d] = (du * cv * sg * (1.0 - sg)).astype(BF16)
        dh_ref[:, 2 * d:] = dz.astype(BF16)

    def main(c):
        return pl.BlockSpec((t, d), lambda i: (i, c))

    def prev(c):
        return pl.BlockSpec((CV_HALO, d), lambda i: (jnp.maximum(i * hpt - 1, 0), c))

    def nxt(c):
        return pl.BlockSpec((CV_HALO, d), lambda i: (jnp.minimum((i + 1) * hpt, last_halo), c))

    vec = pl.BlockSpec((1, d), lambda i: (0, 0))
    taps = pl.BlockSpec((CV_HALO, d), lambda i: (0, 0))
    return pl.pallas_call(
        body, name="conv_bwd", grid=(m // t,),
        in_specs=[main(4), main(5), prev(4), prev(5), nxt(4), nxt(5), main(6), nxt(6), main(0), nxt(0),
                  taps, vec, vec, vec],
        out_specs=[pl.BlockSpec((t, 3 * d), lambda i: (i, 0)), taps, vec, vec, vec],
        out_shape=[jax.ShapeDtypeStruct((m, 3 * d), BF16), jax.ShapeDtypeStruct((CV_HALO, d), F32),
                   jax.ShapeDtypeStruct((1, d), F32), jax.ShapeDtypeStruct((1, d), F32),
                   jax.ShapeDtypeStruct((1, d), F32)],
        scratch_shapes=[pltpu.VMEM((t + 2 * CV_HALO, d), F32), pltpu.VMEM((t + CV_HALO, d), F32),
                        pltpu.VMEM((2, t + 2 * CV_HALO, d), F32)],
        compiler_params=_params("arbitrary"),
    )(h, h, h, h, h, h, h, h, da, da, conv_w, conv_b, ln_g, ln_b)


_HBM = pl.BlockSpec(memory_space=pltpu.HBM)


def _place():
    return lax.axis_index("x"), lax.axis_index("y"), lax.axis_index("c")


def _slot(p):
    return 4 * p[0] + 2 * p[1] + p[2]


def _gather_mm_in_fwd(h0b, w_shard, riders):
    m, d = h0b.shape
    ns = w_shard.shape[1]
    n_i = m // IN_TILE
    n_steps = N_DEV
    n_arr = 1 + len(riders)

    def places():
        x, y, c = _place()
        chips = [(1 - x, y), (x, 1 - y), (1 - x, 1 - y)]
        return (x, y, c), (x, y, 1 - c), chips

    me, sibling, chips = places()
    order = [me, sibling] + [(*chip, me[2]) for chip in chips] + [(*chip, sibling[2]) for chip in chips]
    order = jnp.stack([_slot(p) for p in order]).astype(jnp.int32)

    def body(order_ref, h_ref, *rest):
        src_refs, rest = rest[:n_arr], rest[n_arr:]
        o32_ref, o16_ref = rest[:2]
        got_refs, rest = rest[2:2 + n_arr], rest[2 + n_arr:]
        wbuf, send_sems, recv_sems, own_sems, load_sems = rest
        s, i = pl.program_id(0), pl.program_id(1)
        me, sibling, chips = places()
        c = me[2]
        wg_ref = got_refs[0]

        def remote(t, k, block, to, own_src=False):
            dst = got_refs[t].at[_slot(block)]
            return pltpu.make_async_remote_copy(
                src_ref=src_refs[t] if own_src else dst, dst_ref=dst, send_sem=send_sems.at[7 * t + k],
                recv_sem=recv_sems.at[7 * t + k], device_id=to, device_id_type=MESH)

        def load(step):
            return pltpu.make_async_copy(wg_ref.at[order_ref[step]], wbuf.at[step % 2], load_sems.at[step % 2])

        own = [pltpu.make_async_copy(src_refs[t], got_refs[t].at[_slot(me)], own_sems.at[t]) for t in range(n_arr)]

        @pl.when((s == 0) & (i == 0))
        def _():
            for t in range(n_arr):
                own[t].start()
                remote(t, 0, me, sibling, own_src=True).start()
                for j, chip in enumerate(chips):
                    remote(t, 1 + j, me, (*chip, c), own_src=True).start()
            first = pltpu.make_async_copy(src_refs[0], wbuf.at[0], load_sems.at[0])
            first.start()
            first.wait()

        for nxt in range(1, n_steps):
            @pl.when((s == nxt - 1) & (i == n_i - 1))
            def _():
                if nxt == 1:
                    remote(0, 0, sibling, me).wait_recv()
                elif nxt <= 4:
                    j = nxt - 2
                    remote(0, 1 + j, (*chips[j], c), me).wait_recv()
                    remote(0, 4 + j, (*chips[j], c), sibling).start()
                else:
                    j = nxt - 5
                    remote(0, 4 + j, (*chips[j], 1 - c), me).wait_recv()
                if 3 <= nxt <= 5:
                    j = nxt - 3
                    for t in range(1, n_arr):
                        remote(t, 1 + j, (*chips[j], c), me).wait_recv()
                        remote(t, 4 + j, (*chips[j], c), sibling).start()
                load(nxt).start()

        for step in range(1, n_steps):
            @pl.when((s == step) & (i == 0))
            def _():
                load(step).wait()

        part = lax.dot_general(h_ref[...], wbuf[s % 2], _NN, preferred_element_type=F32)
        o32_ref[...] = part
        o16_ref[...] = part.astype(BF16)

        @pl.when((s == n_steps - 1) & (i == n_i - 1))
        def _():
            for t in range(n_arr):
                if t:
                    remote(t, 0, sibling, me).wait_recv()
                    for j, chip in enumerate(chips):
                        remote(t, 4 + j, (*chip, 1 - c), me).wait_recv()
                remote(t, 0, me, sibling, own_src=True).wait_send()
                for j, chip in enumerate(chips):
                    remote(t, 1 + j, me, (*chip, c), own_src=True).wait_send()
                    remote(t, 4 + j, (*chip, c), sibling).wait_send()
                own[t].wait()

    arrays = [w_shard] + list(riders)
    out = pl.BlockSpec((IN_TILE, ns), lambda s, i, order: (i, order[s]))
    grid_spec = pltpu.PrefetchScalarGridSpec(
        num_scalar_prefetch=1, grid=(n_steps, n_i),
        in_specs=[pl.BlockSpec((IN_TILE, d), lambda s, i, order: (i, 0))] + [_HBM] * n_arr,
        out_specs=[out, out] + [_HBM] * n_arr,
        scratch_shapes=[pltpu.VMEM((2, d, ns), BF16), pltpu.SemaphoreType.DMA((7 * n_arr,)),
                        pltpu.SemaphoreType.DMA((7 * n_arr,)), pltpu.SemaphoreType.DMA((n_arr,)),
                        pltpu.SemaphoreType.DMA((2,))])
    return pl.pallas_call(
        body, name="gather_mm_in_fwd", grid_spec=grid_spec,
        out_shape=[jax.ShapeDtypeStruct((m, N_DEV * ns), F32), jax.ShapeDtypeStruct((m, N_DEV * ns), BF16)]
        + [jax.ShapeDtypeStruct((N_DEV,) + a.shape, a.dtype) for a in arrays],
        compiler_params=_params("arbitrary", "arbitrary"),
    )(order, h0b, *arrays)


def _exchange_partials(parts, whole):
    n_parts = len(parts)
    arrays = list(parts) + list(whole)
    n = len(arrays)

    def body(*refs):
        ins, outs = refs[:n], refs[n:2 * n]
        send_sems, recv_sems, local_sems = refs[2 * n:]
        x, y, c = _place()
        me = (x, y, c)

        def src_for(a, p):
            return ins[a].at[_slot(p)] if a < n_parts else ins[a]

        def copy(a, k, peer):
            return pltpu.make_async_remote_copy(
                src_ref=src_for(a, peer), dst_ref=outs[a].at[_slot(me)], send_sem=send_sems.at[7 * a + k],
                recv_sem=recv_sems.at[7 * a + k], device_id=peer, device_id_type=MESH)

        def landed(a, k, peer):
            return pltpu.make_async_remote_copy(
                src_ref=src_for(a, peer), dst_ref=outs[a].at[_slot(peer)], send_sem=send_sems.at[7 * a + k],
                recv_sem=recv_sems.at[7 * a + k], device_id=peer, device_id_type=MESH)

        peers = []
        for k in range(1, N_DEV):
            fx, fy, fc = (k >> 2) & 1, (k >> 1) & 1, k & 1
            peers.append((1 - x if fx else x, 1 - y if fy else y, 1 - c if fc else c))
        mine = [pltpu.make_async_copy(src_for(a, me), outs[a].at[_slot(me)], local_sems.at[a]) for a in range(n)]
        for cp in mine:
            cp.start()
        sent = [copy(a, k, peer) for a in range(n) for k, peer in enumerate(peers)]
        for cp in sent:
            cp.start()
        for a in range(n):
            for k, peer in enumerate(peers):
                landed(a, k, peer).wait_recv()
        for cp in sent:
            cp.wait_send()
        for cp in mine:
            cp.wait()

    out_shape = [jax.ShapeDtypeStruct(p.shape, p.dtype) for p in parts]
    out_shape += [jax.ShapeDtypeStruct((N_DEV,) + w.shape, w.dtype) for w in whole]
    return pl.pallas_call(
        body, name="exchange_grad_partials",
        in_specs=[_HBM] * n, out_specs=[_HBM] * n, out_shape=out_shape,
        scratch_shapes=[pltpu.SemaphoreType.DMA((7 * n,)), pltpu.SemaphoreType.DMA((7 * n,)),
                        pltpu.SemaphoreType.DMA((n,))],
    )(*arrays)


def _peers_of(x, y, c):
    peers = []
    for k in range(1, N_DEV):
        fx, fy, fc = (k >> 2) & 1, (k >> 1) & 1, k & 1
        peers.append((1 - x if fx else x, 1 - y if fy else y, 1 - c if fc else c))
    return peers


def _with_own(landed, own, me):
    return lax.dynamic_update_index_in_dim(landed, own, me, 0)


def _adamw(name, slabs, w, mom, var, rows):
    r, c = w.shape
    n_slabs = len(slabs)
    per_slab = r // n_slabs // rows

    def body(*refs):
        p_refs = refs[:n_slabs]
        w_ref, m_ref, v_ref, g_ref, d_ref, nm_ref, nv_ref = refs[n_slabs:]
        slab = pl.program_id(0) // per_slab
        g = None
        for p in range(N_DEV):
            part = p_refs[0][p]
            for k in range(1, n_slabs):
                part = jnp.where(slab == k, p_refs[k][p], part)
            g = part.astype(F32) if g is None else g + part.astype(F32)
        m_new = ADAM_B1 * m_ref[...] + (1.0 - ADAM_B1) * g
        v_new = ADAM_B2 * v_ref[...] + (1.0 - ADAM_B2) * (g * g)
        m_hat = m_new / (1.0 - ADAM_B1 ** ADAM_STEP)
        v_hat = v_new / (1.0 - ADAM_B2 ** ADAM_STEP)
        g_ref[...] = g
        d_ref[...] = -ADAM_LR * (m_hat / (jnp.sqrt(v_hat) + ADAM_EPS) + ADAM_WD * w_ref[...])
        nm_ref[...] = m_new
        nv_ref[...] = v_new

    def slab_spec(k):
        return pl.BlockSpec((N_DEV, rows, c), lambda i: (0, jnp.clip(i - k * per_slab, 0, per_slab - 1), 0))

    blk = pl.BlockSpec((rows, c), lambda i: (i, 0))
    return pl.pallas_call(
        body, name=name, grid=(r // rows,),
        in_specs=[slab_spec(k) for k in range(n_slabs)] + [blk, blk, blk],
        out_specs=[blk] * 4, out_shape=[jax.ShapeDtypeStruct((r, c), F32)] * 4,
        compiler_params=_params("parallel"),
    )(*slabs, w, mom, var)


def kernel(x, ln_in_g, ln_in_b, w_in, w_sb_proj, conv_w, conv_b, conv_ln_g, conv_ln_b, w_cv_proj, w_out, ln_post_g, ln_post_b, loss_target, m_ln_in_g, m_ln_in_b, m_w_in, m_w_sb_proj, m_conv_w, m_conv_b, m_conv_ln_g, m_conv_ln_b, m_w_cv_proj, m_w_out, m_ln_post_g, m_ln_post_b, v_ln_in_g, v_ln_in_b, v_w_in, v_w_sb_proj, v_conv_w, v_conv_b, v_conv_ln_g, v_conv_ln_b, v_w_cv_proj, v_w_out, v_ln_post_g, v_ln_post_b):
    nb, seq, d = x.shape
    m = nb * seq
    x2d = x.reshape(m, d)
    target = loss_target.reshape(m, d)
    rs = d // N_DEV
    pad_taps = ((0, CV_HALO - CV_KERNEL), (0, 0))

    proj_shards = jnp.stack([w_sb_proj[0], w_cv_proj[0], w_out[0]]).astype(BF16)
    conv_w_shard = jnp.pad(conv_w[0], pad_taps)
    me = _slot(_place())
    own = lambda parts: lax.dynamic_index_in_dim(parts, me, 0, keepdims=False)

    h0, h0b = _ln_in_fwd(x2d, ln_in_g.reshape(1, d), ln_in_b.reshape(1, d))
    h, hb, w_in_g, proj_g, conv_w_g = _gather_mm_in_fwd(h0b, w_in[0].astype(BF16), [proj_shards, conv_w_shard])
    o, a_sb, carries = _attn_fwd(hb, h, nb, seq)
    w_sb_g = proj_g[:, 0].reshape(d, d)
    w_cv_g = proj_g[:, 1].reshape(d, d)
    w_out_g = proj_g[:, 2].reshape(d, d)
    conv_w_full = conv_w_g.transpose(1, 0, 2).reshape(CV_HALO, d)
    a_cv = _conv_fwd(h, conv_w_full, conv_b, conv_ln_g, conv_ln_b, nb, seq)
    y_sb = _mm_nn("mm_sb_fwd", a_sb, w_sb_g)
    y_cv = _mm_nn("mm_cv_fwd", a_cv, w_cv_g)
    merged = _merge_fwd(h, y_sb, y_cv)
    mo = _mm_nn("mm_out_fwd", merged, w_out_g)
    loss_part, d_pre, d_pre_b, dg_post, db_post = _post_ln_loss(h0, mo, target, ln_post_g, ln_post_b)

    dw_out = _mm_tn("mm_out_bwd_w", merged, d_pre_b, BF16)
    d_merged = _mm_nt("mm_out_bwd_x", d_pre_b, w_out_g)
    dy_sb, dy_cv, d_gates = _merge_bwd(h, y_sb, y_cv, d_merged)
    dw_sb = _mm_tn("mm_sb_bwd_w", a_sb, dy_sb, BF16)
    dw_cv = _mm_tn("mm_cv_bwd_w", a_cv, dy_cv, BF16)
    da_sb = _mm_nt("mm_sb_bwd_x", dy_sb, w_sb_g)
    da_cv = _mm_nt("mm_cv_bwd_x", dy_cv, w_cv_g)
    dproj = jnp.stack([dw_sb.reshape(N_DEV, rs, d), dw_cv.reshape(N_DEV, rs, d), dw_out.reshape(N_DEV, rs, d)], axis=1)
    dq, dk, dv, dz_sb = _attn_bwd(hb, h, o, carries, da_sb, nb, seq)
    d_conv3, dconv_w, dconv_b, dconv_ln_g, dconv_ln_b = _conv_bwd(
        h, da_cv, conv_w_full, conv_b, conv_ln_g, conv_ln_b, nb, seq)
    dhb = jnp.concatenate([dq, dk, dv, dz_sb, d_conv3, d_gates], axis=1)
    dconv_w_parts = dconv_w.reshape(CV_HALO, N_DEV, d // N_DEV).transpose(1, 0, 2)
    dw_in_a, r_proj, r_conv = _mm_in_bwd_w(h0b, dhb, 0, [dproj, dconv_w_parts])
    dw_in_b, r_in_a = _mm_in_bwd_w(h0b, dhb, 1, [dw_in_a])
    dh_mm, r_in_b = _mm_in_bwd_x(dhb, w_in_g, [dw_in_b])
    dx, dg_in, db_in = _ln_in_bwd(x2d, d_pre, dh_mm, ln_in_g.reshape(1, d))
    r_proj = _with_own(r_proj, own(dproj), me)
    r_conv = _with_own(r_conv, own(dconv_w_parts), me)
    r_in = [_with_own(r_in_a, own(dw_in_a), me), _with_own(r_in_b, own(dw_in_b), me)]

    small = jnp.concatenate([dg_in, db_in, dconv_b, dconv_ln_g, dconv_ln_b, dg_post, db_post,
                             jnp.zeros((1, d), F32)], axis=0)
    (r_small,) = _exchange_partials([], [small])

    g_in, d_in, nm_in, nv_in = _adamw("adamw_w_in", r_in, w_in[0], m_w_in[0], v_w_in[0], 128)
    stack3 = lambda a, b, c: jnp.concatenate([a[0], b[0], c[0]], axis=0)
    g_pr, d_pr, nm_pr, nv_pr = _adamw(
        "adamw_proj", [r_proj.reshape(N_DEV, 3 * rs, d)], stack3(w_sb_proj, w_cv_proj, w_out),
        stack3(m_w_sb_proj, m_w_cv_proj, m_w_out), stack3(v_w_sb_proj, v_w_cv_proj, v_w_out), 3 * rs)
    padc = lambda a: jnp.pad(a[0], pad_taps)
    g_cw, d_cw, nm_cw, nv_cw = _adamw("adamw_conv_w", [r_conv], padc(conv_w), padc(m_conv_w), padc(v_conv_w), CV_HALO)
    vecs = lambda *a: jnp.concatenate([t.reshape(1, d) for t in a] + [jnp.ones((1, d), F32)], axis=0)
    g_sm, d_sm, nm_sm, nv_sm = _adamw(
        "adamw_vectors", [r_small],
        vecs(ln_in_g, ln_in_b, conv_b, conv_ln_g, conv_ln_b, ln_post_g, ln_post_b),
        vecs(m_ln_in_g, m_ln_in_b, m_conv_b, m_conv_ln_g, m_conv_ln_b, m_ln_post_g, m_ln_post_b),
        vecs(v_ln_in_g, v_ln_in_b, v_conv_b, v_conv_ln_g, v_conv_ln_b, v_ln_post_g, v_ln_post_b), 8)

    loss = lax.psum(loss_part[0, 0], ("x", "y", "c"))

    def leaves(big, pr, cw, sm):
        return (sm[0], sm[1], big[None], pr[None, 0:rs], cw[None, :CV_KERNEL], sm[2:3], sm[3:4], sm[4:5],
                pr[None, rs:2 * rs], pr[None, 2 * rs:], sm[5:6], sm[6:7])

    return (loss, dx.reshape(nb, seq, d), *leaves(g_in, g_pr, g_cw, g_sm), *leaves(d_in, d_pr, d_cw, d_sm),
            *leaves(nm_in, nm_pr, nm_cw, nm_sm), *leaves(nv_in, nv_pr, nv_cw, nv_sm))
```

```python
import functools
import math

import jax
import jax.numpy as jnp
from jax import lax
from jax.experimental import pallas as pl
from jax.experimental.pallas import tpu as pltpu

F32 = jnp.float32
BF16 = jnp.bfloat16
MESH = pl.DeviceIdType.MESH

N_DEV = 8
SB_HEADS = 16
HEAD_DIM = 64
CV_KERNEL = 31
CV_HALO = 32
LN_EPS = 1e-5
DEEPNORM_ALPHA = 2.0 ** 0.25
ADAM_LR, ADAM_B1, ADAM_B2, ADAM_EPS, ADAM_WD, ADAM_STEP = 0.001, 0.9, 0.999, 1e-08, 0.01, 10

ATT_TK = 256
ATT_TQ = 1024
ROW_TILE = 512
IN_TILE = 1024
CONV_TILE = 256
VMEM_LIMIT = 56 * 1024 * 1024


def _params(*sem):
    return pltpu.CompilerParams(dimension_semantics=sem, vmem_limit_bytes=VMEM_LIMIT)


def _sigmoid(x):
    return 1.0 / (1.0 + jnp.exp(-x))


def _silu_and_grad(x):
    s = _sigmoid(x)
    return x * s, s * (1.0 + x * (1.0 - s))


def _ln_stats(x):
    mu = jnp.mean(x, axis=-1, keepdims=True)
    xc = x - mu
    var = jnp.mean(xc * xc, axis=-1, keepdims=True)
    rstd = lax.rsqrt(var + LN_EPS)
    return xc * rstd, rstd


def _ln_bwd(dxhat, xhat, rstd):
    m1 = jnp.mean(dxhat, axis=-1, keepdims=True)
    m2 = jnp.mean(dxhat * xhat, axis=-1, keepdims=True)
    return rstd * (dxhat - m1 - xhat * m2)


def _ln_in_fwd(x2d, g, b):
    m, d = x2d.shape

    def body(x_ref, g_ref, b_ref, h_ref, hb_ref):
        xhat, _ = _ln_stats(x_ref[...])
        y = xhat * g_ref[...] + b_ref[...]
        h_ref[...] = y
        hb_ref[...] = y.astype(BF16)

    row = pl.BlockSpec((ROW_TILE, d), lambda i: (i, 0))
    vec = pl.BlockSpec((1, d), lambda i: (0, 0))
    return pl.pallas_call(
        body, name="ln_in_fwd", grid=(m // ROW_TILE,),
        in_specs=[row, vec, vec], out_specs=[row, row],
        out_shape=[jax.ShapeDtypeStruct((m, d), F32), jax.ShapeDtypeStruct((m, d), BF16)],
        compiler_params=_params("parallel"),
    )(x2d, g, b)


def _ln_in_bwd(x2d, d_pre, dh_mm, g):
    m, d = x2d.shape

    def body(x_ref, dp_ref, dm_ref, g_ref, dx_ref, dg_ref, db_ref):
        @pl.when(pl.program_id(0) == 0)
        def _():
            dg_ref[...] = jnp.zeros_like(dg_ref)
            db_ref[...] = jnp.zeros_like(db_ref)

        xhat, rstd = _ln_stats(x_ref[...])
        dh = DEEPNORM_ALPHA * dp_ref[...] + dm_ref[...]
        dg_ref[...] += jnp.sum(dh * xhat, axis=0, keepdims=True)
        db_ref[...] += jnp.sum(dh, axis=0, keepdims=True)
        dx_ref[...] = _ln_bwd(dh * g_ref[...], xhat, rstd)

    row = pl.BlockSpec((ROW_TILE, d), lambda i: (i, 0))
    vec = pl.BlockSpec((1, d), lambda i: (0, 0))
    return pl.pallas_call(
        body, name="ln_in_bwd", grid=(m // ROW_TILE,),
        in_specs=[row, row, row, vec], out_specs=[row, vec, vec],
        out_shape=[jax.ShapeDtypeStruct((m, d), F32), jax.ShapeDtypeStruct((1, d), F32),
                   jax.ShapeDtypeStruct((1, d), F32)],
        compiler_params=_params("arbitrary"),
    )(x2d, d_pre, dh_mm, g)


def _merge_fwd(h, y_sb, y_cv):
    m, d = y_sb.shape

    def body(gs_ref, gc_ref, ys_ref, yc_ref, out_ref):
        out_ref[...] = (_sigmoid(gs_ref[...]) * ys_ref[...] + _sigmoid(gc_ref[...]) * yc_ref[...]).astype(BF16)

    row = pl.BlockSpec((ROW_TILE, d), lambda i: (i, 0))
    return pl.pallas_call(
        body, name="merge_fwd", grid=(m // ROW_TILE,),
        in_specs=[pl.BlockSpec((ROW_TILE, d), lambda i: (i, 7)), pl.BlockSpec((ROW_TILE, d), lambda i: (i, 8)), row, row],
        out_specs=row, out_shape=jax.ShapeDtypeStruct((m, d), BF16),
        compiler_params=_params("parallel"),
    )(h, h, y_sb, y_cv)


def _merge_bwd(h, y_sb, y_cv, dm):
    m, d = y_sb.shape

    def body(gs_ref, gc_ref, ys_ref, yc_ref, dm_ref, dys_ref, dyc_ref, dgate_ref):
        dmv = dm_ref[...]
        ss = _sigmoid(gs_ref[...])
        sc = _sigmoid(gc_ref[...])
        dys_ref[...] = (ss * dmv).astype(BF16)
        dyc_ref[...] = (sc * dmv).astype(BF16)
        dgate_ref[:, :d] = (dmv * ys_ref[...] * ss * (1.0 - ss)).astype(BF16)
        dgate_ref[:, d:] = (dmv * yc_ref[...] * sc * (1.0 - sc)).astype(BF16)

    row = pl.BlockSpec((ROW_TILE, d), lambda i: (i, 0))
    return pl.pallas_call(
        body, name="merge_bwd", grid=(m // ROW_TILE,),
        in_specs=[pl.BlockSpec((ROW_TILE, d), lambda i: (i, 7)), pl.BlockSpec((ROW_TILE, d), lambda i: (i, 8)), row, row, row],
        out_specs=[row, row, pl.BlockSpec((ROW_TILE, 2 * d), lambda i: (i, 0))],
        out_shape=[jax.ShapeDtypeStruct((m, d), BF16), jax.ShapeDtypeStruct((m, d), BF16),
                   jax.ShapeDtypeStruct((m, 2 * d), BF16)],
        compiler_params=_params("parallel"),
    )(h, h, y_sb, y_cv, dm)


def _post_ln_loss(h0, mo, target, g, b):
    m, d = h0.shape

    def body(h_ref, mo_ref, t_ref, g_ref, b_ref, loss_ref, dp_ref, dpb_ref, dg_ref, db_ref):
        @pl.when(pl.program_id(0) == 0)
        def _():
            loss_ref[...] = jnp.zeros_like(loss_ref)
            dg_ref[...] = jnp.zeros_like(dg_ref)
            db_ref[...] = jnp.zeros_like(db_ref)

        xhat, rstd = _ln_stats(DEEPNORM_ALPHA * h_ref[...] + mo_ref[...])
        err = xhat * g_ref[...] + b_ref[...] - t_ref[...]
        per_row = jnp.mean(err * err, axis=-1, keepdims=True)
        loss_ref[...] += 0.5 * jnp.sum(per_row, axis=0, keepdims=True)
        dy = err * (1.0 / d)
        dg_ref[...] += jnp.sum(dy * xhat, axis=0, keepdims=True)
        db_ref[...] += jnp.sum(dy, axis=0, keepdims=True)
        dp = _ln_bwd(dy * g_ref[...], xhat, rstd)
        dp_ref[...] = dp
        dpb_ref[...] = dp.astype(BF16)

    row = pl.BlockSpec((ROW_TILE, d), lambda i: (i, 0))
    vec = pl.BlockSpec((1, d), lambda i: (0, 0))
    one = pl.BlockSpec((1, 1), lambda i: (0, 0))
    return pl.pallas_call(
        body, name="post_ln_loss", grid=(m // ROW_TILE,),
        in_specs=[row, row, row, vec, vec], out_specs=[one, row, row, vec, vec],
        out_shape=[jax.ShapeDtypeStruct((1, 1), F32), jax.ShapeDtypeStruct((m, d), F32),
                   jax.ShapeDtypeStruct((m, d), BF16), jax.ShapeDtypeStruct((1, d), F32),
                   jax.ShapeDtypeStruct((1, d), F32)],
        compiler_params=_params("arbitrary"),
    )(h0, mo, target, g, b)


_NN = (((1,), (0,)), ((), ()))
_NT = (((1,), (1,)), ((), ()))
_TN = (((0,), (0,)), ((), ()))


def _grid_ends(grid):
    ids = [pl.program_id(ax) for ax in range(len(grid))]
    first = functools.reduce(jnp.logical_and, [i == 0 for i in ids])
    last = functools.reduce(jnp.logical_and, [i == n - 1 for i, n in zip(ids, grid)])
    return first, last


def _travel(grid, src_refs, land_refs, send_sems, recv_sems, whole):
    first, last = _grid_ends(grid)
    x, y, c = _place()
    n = len(src_refs)

    def copy(t, k, peer, lands_at):
        return pltpu.make_async_remote_copy(
            src_ref=src_refs[t] if whole else src_refs[t].at[_slot(peer)],
            dst_ref=land_refs[t].at[_slot(lands_at)], send_sem=send_sems.at[7 * t + k],
            recv_sem=recv_sems.at[7 * t + k], device_id=peer, device_id_type=MESH)

    @pl.when(first)
    def _():
        for k, peer in enumerate(_peers_of(x, y, c)):
            for t in range(n):
                copy(t, k, peer, (x, y, c)).start()

    def finish():
        @pl.when(last)
        def _():
            for k, peer in enumerate(_peers_of(x, y, c)):
                for t in range(n):
                    arrived = copy(t, k, peer, peer)
                    arrived.wait_send()
                    arrived.wait_recv()

    return finish


def _travel_scratch(n):
    return [pltpu.SemaphoreType.DMA((7 * n,)), pltpu.SemaphoreType.DMA((7 * n,))]


def _mm(name, a, b, dims, grid, a_spec, b_spec, out_specs, out_shape, acc_shape, k_axis, send=()):
    n_k = 1 if k_axis is None else grid[k_axis]
    n_out = len(out_shape)
    n_send = len(send)

    def body(a_ref, b_ref, *rest):
        src_refs, rest = rest[:n_send], rest[n_send:]
        outs, land_refs, acc_ref = rest[:n_out], rest[n_out:n_out + n_send], rest[n_out + n_send]
        if n_send:
            finish = _travel(grid, src_refs, land_refs, *rest[n_out + n_send + 1:], whole=False)

        part = lax.dot_general(a_ref[...].astype(BF16), b_ref[...].astype(BF16), dims, preferred_element_type=F32)
        if n_k == 1:
            for o in outs:
                o[...] = part.astype(o.dtype)
        else:
            k = pl.program_id(k_axis)

            @pl.when(k == 0)
            def _():
                acc_ref[...] = part

            @pl.when(k > 0)
            def _():
                acc_ref[...] += part

            @pl.when(k == n_k - 1)
            def _():
                for o in outs:
                    o[...] = acc_ref[...].astype(o.dtype)

        if n_send:
            finish()

    sem = tuple("arbitrary" for _ in grid) if n_send else tuple(
        "arbitrary" if ax == k_axis else "parallel" for ax in range(len(grid)))
    scratch = [pltpu.VMEM(acc_shape, F32)] + (_travel_scratch(n_send) if n_send else [])
    return pl.pallas_call(
        body, name=name, grid=grid, in_specs=[a_spec, b_spec] + [_HBM] * n_send,
        out_specs=list(out_specs) + [_HBM] * n_send,
        out_shape=list(out_shape) + [jax.ShapeDtypeStruct(s.shape, s.dtype) for s in send],
        scratch_shapes=scratch, compiler_params=_params(*sem),
    )(a, b, *send)


def _mm_nn(name, a, b, out_dtype=F32):
    m, k = a.shape
    n = b.shape[1]
    return _mm(name, a, b, _NN, (m // ROW_TILE,), pl.BlockSpec((ROW_TILE, k), lambda i: (i, 0)),
               pl.BlockSpec((k, n), lambda i: (0, 0)), [pl.BlockSpec((ROW_TILE, n), lambda i: (i, 0))],
               [jax.ShapeDtypeStruct((m, n), out_dtype)], (8, 128), None)[0]


def _mm_nt(name, a, b, out_dtype=F32):
    m, k = a.shape
    n = b.shape[0]
    return _mm(name, a, b, _NT, (m // ROW_TILE,), pl.BlockSpec((ROW_TILE, k), lambda i: (i, 0)),
               pl.BlockSpec((n, k), lambda i: (0, 0)), [pl.BlockSpec((ROW_TILE, n), lambda i: (i, 0))],
               [jax.ShapeDtypeStruct((m, n), out_dtype)], (8, 128), None)[0]


def _mm_tn(name, a, b, out_dtype):
    m, k = a.shape
    n = b.shape[1]
    return _mm(name, a, b, _TN, (m // ROW_TILE,), pl.BlockSpec((ROW_TILE, k), lambda i: (i, 0)),
               pl.BlockSpec((ROW_TILE, n), lambda i: (i, 0)), [pl.BlockSpec((k, n), lambda i: (0, 0))],
               [jax.ShapeDtypeStruct((k, n), out_dtype)], (k, n), 0)[0]


def _mm_in_bwd_x(dhb, w_g, send):
    m = dhb.shape[0]
    _, d, ns = w_g.shape
    return _mm("mm_in_bwd_x", dhb, w_g, _NT, (m // IN_TILE, N_DEV), pl.BlockSpec((IN_TILE, ns), lambda i, j: (i, j)),
               pl.BlockSpec((None, d, ns), lambda i, j: (j, 0, 0)), [pl.BlockSpec((IN_TILE, d), lambda i, j: (i, 0))],
               [jax.ShapeDtypeStruct((m, d), F32)], (IN_TILE, d), 1, send=send)


def _mm_in_bwd_w(h0b, dhb, half, send):
    m, d = h0b.shape
    ns = dhb.shape[1] // N_DEV
    dh = d // 2
    return _mm(f"mm_in_bwd_w{half}", h0b, dhb, _TN, (N_DEV, m // IN_TILE),
               pl.BlockSpec((IN_TILE, dh), lambda j, k: (k, half)),
               pl.BlockSpec((IN_TILE, ns), lambda j, k: (k, j)), [pl.BlockSpec((None, dh, ns), lambda j, k: (j, 0, 0))],
               [jax.ShapeDtypeStruct((N_DEV, dh, ns), BF16)], (dh, ns), 1, send=send)


LOG2_E = 1.4426950408889634
DEAD_AFTER = 110.0
NEVER_REACHED = 1e30


def _twice(u):
    return jnp.concatenate([u, u], axis=0)


def _dot2(x, u2):
    hi = x.astype(BF16)
    lo = (x - hi.astype(F32)).astype(BF16)
    return lax.dot_general(jnp.concatenate([hi, lo], axis=1), u2, _NN, preferred_element_type=F32)


def _softplus(l):
    e = jnp.exp2(jnp.abs(l) * (-LOG2_E))
    e1 = 1.0 + e
    return jnp.maximum(l, 0.0) + jnp.log(e1), e, e1


def _attn_fwd(hb, h, nb, seq, gather):
    m = nb * seq
    d = SB_HEADS * HEAD_DIM
    tq, tk = ATT_TQ, ATT_TK
    n_sub = tq // tk
    nq = seq // tq
    sec = d // 128
    scale = 1.0 / math.sqrt(HEAD_DIM)
    heads = [slice(hh * HEAD_DIM, (hh + 1) * HEAD_DIM) for hh in range(2)]
    grid = (nb, sec, nq)
    n_g = len(gather)

    def body(q_ref, k_ref, v_ref, z_ref, *rest):
        src_refs, rest = rest[:n_g], rest[n_g:]
        o_ref, a_ref, c_ref = rest[:3]
        land_refs, (carry_ref, acc_ref, send_sems, recv_sems) = rest[3:3 + n_g], rest[3 + n_g:]
        finish = _travel(grid, src_refs, land_refs, send_sems, recv_sems, whole=True)
        qi = pl.program_id(2)
        n_full = n_sub * qi
        u_after = _twice((lax.broadcasted_iota(jnp.int32, (tk, tk), 0)
                          > lax.broadcasted_iota(jnp.int32, (tk, tk), 1)).astype(BF16))
        qs = [(q_ref[:, cols].astype(F32) * scale).astype(BF16) for cols in heads]
        carry_ref[...] = jnp.zeros_like(carry_ref)
        acc_ref[...] = jnp.zeros_like(acc_ref)
        lane_t = lax.broadcasted_iota(jnp.int32, (tq, 128), 1)
        c_ref[...] = jnp.where(lane_t % HEAD_DIM < n_full, NEVER_REACHED, 0.0)

        def block(kb, r0, masked):
            n = tq - r0
            qrows = slice(r0, tq)
            krows = pl.ds(pl.multiple_of(kb * tk, tk), tk)
            lane = lax.broadcasted_iota(jnp.int32, (n, 128), 1)
            if masked:
                causal = lax.broadcasted_iota(jnp.int32, (n, tk), 1) < lax.broadcasted_iota(jnp.int32, (n, tk), 0)
            for hh, cols in enumerate(heads):
                carry = carry_ref[hh, qrows, :]
                l = lax.dot_general(qs[hh][qrows], k_ref[krows, cols], _NT, preferred_element_type=F32)
                sp, _, _ = _softplus(l)
                if masked:
                    sp = jnp.where(causal, sp, 0.0)
                after = carry + _dot2(sp, u_after)
                a = jnp.exp(l - (sp + after))
                if masked:
                    a = jnp.where(causal, a, 0.0)
                acc_ref[qrows, cols] += lax.dot_general(a.astype(BF16), v_ref[krows, cols], _NN,
                                                        preferred_element_type=F32)
                c_ref[qrows, :] = jnp.where(lane == hh * HEAD_DIM + kb, carry, c_ref[qrows, :])
                carry_ref[hh, qrows, :] = after[:, 0:1] + sp[:, 0:1]

        def still_alive():
            return jnp.min(carry_ref[...]) <= DEAD_AFTER

        for j in reversed(range(n_sub)):
            block(n_full + j, j * tk, True)

        def step(state):
            i, _ = state
            block(n_full - 1 - i, 0, False)
            return i + 1, still_alive()

        lax.while_loop(lambda state: (state[0] < n_full) & state[1], step, (0, still_alive()))
        o = acc_ref[...]
        o_ref[...] = o
        silu, _ = _silu_and_grad(z_ref[...])
        a_ref[...] = (o * silu).astype(BF16)
        finish()

    qspec = pl.BlockSpec((tq, 128), lambda b, hp, qi: (b * nq + qi, hp))
    return pl.pallas_call(
        body, name="attn_fwd", grid=grid,
        in_specs=[qspec,
                  pl.BlockSpec((seq, 128), lambda b, hp, qi: (b, sec + hp)),
                  pl.BlockSpec((seq, 128), lambda b, hp, qi: (b, 2 * sec + hp)),
                  pl.BlockSpec((tq, 128), lambda b, hp, qi: (b * nq + qi, 3 * sec + hp))] + [_HBM] * n_g,
        out_specs=[qspec, qspec, pl.BlockSpec((None, tq, 128), lambda b, hp, qi: (hp, b * nq + qi, 0))]
        + [_HBM] * n_g,
        out_shape=[jax.ShapeDtypeStruct((m, d), F32), jax.ShapeDtypeStruct((m, d), BF16),
                   jax.ShapeDtypeStruct((sec, m, 128), F32)]
        + [jax.ShapeDtypeStruct((N_DEV,) + g.shape, g.dtype) for g in gather],
        scratch_shapes=[pltpu.VMEM((2, tq, 1), F32), pltpu.VMEM((tq, 128), F32)] + _travel_scratch(n_g),
        compiler_params=_params("arbitrary", "arbitrary", "arbitrary"),
    )(hb, hb, hb, h, *gather)


def _attn_bwd(hb, h, o, carries, da, nb, seq, send):
    m = nb * seq
    d = SB_HEADS * HEAD_DIM
    tq, tk = ATT_TQ, ATT_TK
    n_sub = tq // tk
    nq = seq // tq
    sec = d // 128
    scale = 1.0 / math.sqrt(HEAD_DIM)
    heads = [slice(hh * HEAD_DIM, (hh + 1) * HEAD_DIM) for hh in range(2)]
    grid = (nb, sec, nq)
    n_send = len(send)

    def body(q_ref, k_ref, v_ref, z_ref, o_ref, c_ref, da_ref, *rest):
        src_refs, rest = rest[:n_send], rest[n_send:]
        dq_ref, dk_ref, dv_ref, dz_ref = rest[:4]
        land_refs, rest = rest[4:4 + n_send], rest[4 + n_send:]
        dk_acc, dv_acc, dq_acc, gcarry_ref, send_sems, recv_sems = rest
        finish = _travel(grid, src_refs, land_refs, send_sems, recv_sems, whole=False)
        qi = pl.program_id(2)

        @pl.when(qi == 0)
        def _():
            dk_acc[...] = jnp.zeros_like(dk_acc)
            dv_acc[...] = jnp.zeros_like(dv_acc)

        row = lax.broadcasted_iota(jnp.int32, (tk, tk), 0)
        col = lax.broadcasted_iota(jnp.int32, (tk, tk), 1)
        u_after = _twice((row > col).astype(BF16))
        u_before = _twice((row < col).astype(BF16))
        silu, dsilu = _silu_and_grad(z_ref[...])
        dav = da_ref[...]
        dz_ref[...] = (dav * o_ref[...] * dsilu).astype(BF16)
        do2 = (dav * silu).astype(BF16)
        qs = [(q_ref[:, cols].astype(F32) * scale).astype(BF16) for cols in heads]
        dos = [do2[:, cols] for cols in heads]
        dq_acc[...] = jnp.zeros_like(dq_acc)
        gcarry_ref[...] = jnp.zeros_like(gcarry_ref)

        def block(kb, r0, masked):
            n = tq - r0
            qrows = slice(r0, tq)
            krows = pl.ds(pl.multiple_of(kb * tk, tk), tk)
            lane = lax.broadcasted_iota(jnp.int32, (n, 128), 1)
            if masked:
                causal = lax.broadcasted_iota(jnp.int32, (n, tk), 1) < lax.broadcasted_iota(jnp.int32, (n, tk), 0)
            for hh, cols in enumerate(heads):
                q, do = qs[hh][qrows], dos[hh][qrows]
                ks = k_ref[krows, cols]
                l = lax.dot_general(q, ks, _NT, preferred_element_type=F32)
                sp, e, e1 = _softplus(l)
                if masked:
                    sp = jnp.where(causal, sp, 0.0)
                carry = jnp.sum(jnp.where(lane == hh * HEAD_DIM + kb, c_ref[qrows, :], 0.0), axis=1, keepdims=True)
                after = carry + _dot2(sp, u_after)
                a = jnp.exp(l - (sp + after))
                if masked:
                    a = jnp.where(causal, a, 0.0)
                dv_acc[krows, cols] += lax.dot_general(a.astype(BF16), do, _TN, preferred_element_type=F32)
                g = a * lax.dot_general(do, v_ref[krows, cols], _NT, preferred_element_type=F32)
                c = gcarry_ref[hh, qrows, :] + _dot2(g, u_before)
                r = pl.reciprocal(e1, approx=True)
                beta = jnp.where(l >= 0.0, r, e * r)
                dl = g - (g + c) * beta
                if masked:
                    dl = jnp.where(causal, dl, 0.0)
                dl_b = dl.astype(BF16)
                dq_acc[qrows, cols] += lax.dot_general(dl_b, ks, _NN, preferred_element_type=F32)
                dk_acc[krows, cols] += lax.dot_general(dl_b, q, _TN, preferred_element_type=F32)
                gcarry_ref[hh, qrows, :] = c[:, tk - 1:tk] + g[:, tk - 1:tk]

        def step(kb, _):
            lane = lax.broadcasted_iota(jnp.int32, (tq, 128), 1)
            mine = (lane == kb) | (lane == HEAD_DIM + kb)
            alive = jnp.min(jnp.where(mine, c_ref[...], jnp.inf)) <= DEAD_AFTER

            @pl.when(alive)
            def _():
                block(kb, 0, False)

            return 0

        lax.fori_loop(0, n_sub * qi, step, 0)
        for j in range(n_sub):
            block(n_sub * qi + j, j * tk, True)
        dq_ref[...] = (dq_acc[...] * scale).astype(BF16)

        @pl.when(qi == nq - 1)
        def _():
            dk_ref[...] = dk_acc[...].astype(BF16)
            dv_ref[...] = dv_acc[...].astype(BF16)

        finish()

    qspec = pl.BlockSpec((tq, 128), lambda b, hp, qi: (b * nq + qi, hp))
    kvspec = pl.BlockSpec((seq, 128), lambda b, hp, qi: (b, hp))
    return pl.pallas_call(
        body, name="attn_bwd", grid=grid,
        in_specs=[qspec,
                  pl.BlockSpec((seq, 128), lambda b, hp, qi: (b, sec + hp)),
                  pl.BlockSpec((seq, 128), lambda b, hp, qi: (b, 2 * sec + hp)),
                  pl.BlockSpec((tq, 128), lambda b, hp, qi: (b * nq + qi, 3 * sec + hp)),
                  qspec,
                  pl.BlockSpec((None, tq, 128), lambda b, hp, qi: (hp, b * nq + qi, 0)),
                  qspec] + [_HBM] * n_send,
        out_specs=[qspec, kvspec, kvspec, qspec] + [_HBM] * n_send,
        out_shape=[jax.ShapeDtypeStruct((m, d), BF16)] * 4 + [jax.ShapeDtypeStruct(s.shape, s.dtype) for s in send],
        scratch_shapes=[pltpu.VMEM((seq, 128), F32), pltpu.VMEM((seq, 128), F32), pltpu.VMEM((tq, 128), F32),
                        pltpu.VMEM((2, tq, 1), F32)] + _travel_scratch(n_send),
        compiler_params=_params("arbitrary", "arbitrary", "arbitrary"),
    )(hb, hb, hb, h, o, carries, da, *send)


SUBLANES = 8


def _by_residue(taps):
    groups = []
    for res in range(SUBLANES):
        mine = [(off - res, k) for off, k in taps if off % SUBLANES == res]
        if mine:
            groups.append((res, max(a for a, _ in mine), mine))
    return groups


def _staged(src_ref, stage_ref, res, n):
    buf = stage_ref.at[res % 2]
    if res == 0:
        return src_ref
    buf[0:n, :] = src_ref[pl.ds(res, n), :]
    return buf


def _tap_sum(src_ref, stage_ref, w_ref, n_rows, taps, acc):
    for res, reach, mine in _by_residue(taps):
        shifted = _staged(src_ref, stage_ref, res, n_rows + reach)
        for a, k in mine:
            acc = acc + w_ref[k:k + 1, :] * shifted[a:a + n_rows, :]
    return acc


_CONV_TAPS = [(CV_HALO - CV_KERNEL + 1 + k, k) for k in range(CV_KERNEL)]
_CONV_TAPS_T = [(CV_KERNEL - 1 - k, k) for k in range(CV_KERNEL)]


def _conv_rows(pad_ref, stage_ref, w_ref, b_ref, n_rows):
    bias = jnp.broadcast_to(b_ref[...], (n_rows, b_ref.shape[1]))
    return _tap_sum(pad_ref, stage_ref, w_ref, n_rows, _CONV_TAPS, bias)


def _conv_fwd(h, conv_w, conv_b, ln_g, ln_b, nb, seq):
    m = nb * seq
    d = conv_b.shape[1]
    t = CONV_TILE
    tiles = seq // t
    hpt = t // CV_HALO

    def body(cv_ref, cg_ref, pv_ref, pg_ref, z_ref, w_ref, b_ref, g_ref, bb_ref, a_ref, pad_ref, stage_ref):
        first = pl.program_id(0) % tiles == 0
        pad_ref[0:CV_HALO, :] = jnp.where(first, 0.0, pv_ref[...] * _sigmoid(pg_ref[...]))
        pad_ref[CV_HALO:, :] = cv_ref[...] * _sigmoid(cg_ref[...])
        xhat, _ = _ln_stats(_conv_rows(pad_ref, stage_ref, w_ref, b_ref, t))
        s, _ = _silu_and_grad(xhat * g_ref[...] + bb_ref[...])
        sz, _ = _silu_and_grad(z_ref[...])
        a_ref[...] = (s * sz).astype(BF16)

    def main(c):
        return pl.BlockSpec((t, d), lambda i: (i, c))

    def prev(c):
        return pl.BlockSpec((CV_HALO, d), lambda i: (jnp.maximum(i * hpt - 1, 0), c))

    vec = pl.BlockSpec((1, d), lambda i: (0, 0))
    return pl.pallas_call(
        body, name="conv_fwd", grid=(m // t,),
        in_specs=[main(4), main(5), prev(4), prev(5), main(6),
                  pl.BlockSpec((CV_HALO, d), lambda i: (0, 0)), vec, vec, vec],
        out_specs=pl.BlockSpec((t, d), lambda i: (i, 0)),
        out_shape=jax.ShapeDtypeStruct((m, d), BF16),
        scratch_shapes=[pltpu.VMEM((CV_HALO + t, d), F32), pltpu.VMEM((2, CV_HALO + t, d), F32)],
        compiler_params=_params("parallel"),
    )(h, h, h, h, h, conv_w, conv_b, ln_g, ln_b)


def _conv_bwd(h, da, conv_w, conv_b, ln_g, ln_b, nb, seq):
    m = nb * seq
    d = conv_b.shape[1]
    t = CONV_TILE
    tiles = seq // t
    hpt = t // CV_HALO
    last_halo = m // CV_HALO - 1

    def body(cv_ref, cg_ref, pv_ref, pg_ref, nv_ref, ng_ref, z_ref, nz_ref, da_ref, nda_ref,
             w_ref, b_ref, g_ref, bb_ref, dh_ref, dw_ref, db_ref, dg_ref, dbb_ref, pad_ref, dc_ref, stage_ref):
        i = pl.program_id(0)

        @pl.when(i == 0)
        def _():
            dw_ref[...] = jnp.zeros_like(dw_ref)
            db_ref[...] = jnp.zeros_like(db_ref)
            dg_ref[...] = jnp.zeros_like(dg_ref)
            dbb_ref[...] = jnp.zeros_like(dbb_ref)

        first = i % tiles == 0
        last = i % tiles == tiles - 1
        cv = cv_ref[...]
        sg = _sigmoid(cg_ref[...])
        pad_ref[0:CV_HALO, :] = jnp.where(first, 0.0, pv_ref[...] * _sigmoid(pg_ref[...]))
        pad_ref[CV_HALO:CV_HALO + t, :] = cv * sg
        pad_ref[CV_HALO + t:, :] = nv_ref[...] * _sigmoid(ng_ref[...])

        def rows_bwd(conv_out, z, dav):
            xhat, rstd = _ln_stats(conv_out)
            s, ds = _silu_and_grad(xhat * g_ref[...] + bb_ref[...])
            sz, dsz = _silu_and_grad(z)
            dn = dav * sz * ds
            return _ln_bwd(dn * g_ref[...], xhat, rstd), dav * s * dsz, dn, xhat

        conv_all = _conv_rows(pad_ref, stage_ref, w_ref, b_ref, t + CV_HALO)
        dc, dz, dn, xhat = rows_bwd(conv_all[:t], z_ref[...], da_ref[...])
        dc_next, _, _, _ = rows_bwd(conv_all[t:], nz_ref[...], nda_ref[...])
        dc_ref[0:t, :] = dc
        dc_ref[t:, :] = jnp.where(last, 0.0, dc_next)
        dg_ref[...] += jnp.sum(dn * xhat, axis=0, keepdims=True)
        dbb_ref[...] += jnp.sum(dn, axis=0, keepdims=True)
        db_ref[...] += jnp.sum(dc, axis=0, keepdims=True)
        for res, reach, mine in _by_residue(_CONV_TAPS):
            shifted = _staged(pad_ref, stage_ref, res, t + reach)
            for a, k in mine:
                dw_ref[k:k + 1, :] += jnp.sum(dc * shifted[a:a + t, :], axis=0, keepdims=True)
        du = _tap_sum(dc_ref, stage_ref, w_ref, t, _CONV_TAPS_T, jnp.zeros((t, d), F32))
        dh_ref[:, 0:d] = (du * sg).astype(BF16)
        dh_ref[:, d:2 * d] = (du * cv * sg * (1.0 - sg)).astype(BF16)
        dh_ref[:, 2 * d:] = dz.astype(BF16)

    def main(c):
        return pl.BlockSpec((t, d), lambda i: (i, c))

    def prev(c):
        return pl.BlockSpec((CV_HALO, d), lambda i: (jnp.maximum(i * hpt - 1, 0), c))

    def nxt(c):
        return pl.BlockSpec((CV_HALO, d), lambda i: (jnp.minimum((i + 1) * hpt, last_halo), c))

    vec = pl.BlockSpec((1, d), lambda i: (0, 0))
    taps = pl.BlockSpec((CV_HALO, d), lambda i: (0, 0))
    return pl.pallas_call(
        body, name="conv_bwd", grid=(m // t,),
        in_specs=[main(4), main(5), prev(4), prev(5), nxt(4), nxt(5), main(6), nxt(6), main(0), nxt(0),
                  taps, vec, vec, vec],
        out_specs=[pl.BlockSpec((t, 3 * d), lambda i: (i, 0)), taps, vec, vec, vec],
        out_shape=[jax.ShapeDtypeStruct((m, 3 * d), BF16), jax.ShapeDtypeStruct((CV_HALO, d), F32),
                   jax.ShapeDtypeStruct((1, d), F32), jax.ShapeDtypeStruct((1, d), F32),
                   jax.ShapeDtypeStruct((1, d), F32)],
        scratch_shapes=[pltpu.VMEM((t + 2 * CV_HALO, d), F32), pltpu.VMEM((t + CV_HALO, d), F32),
                        pltpu.VMEM((2, t + 2 * CV_HALO, d), F32)],
        compiler_params=_params("arbitrary"),
    )(h, h, h, h, h, h, h, h, da, da, conv_w, conv_b, ln_g, ln_b)


_HBM = pl.BlockSpec(memory_space=pltpu.HBM)


def _place():
    return lax.axis_index("x"), lax.axis_index("y"), lax.axis_index("c")


def _slot(p):
    return 4 * p[0] + 2 * p[1] + p[2]


def _gather_mm_in_fwd(h0b, w_shard, riders):
    m, d = h0b.shape
    ns = w_shard.shape[1]
    n_i = m // IN_TILE
    n_steps = N_DEV
    n_arr = 1 + len(riders)

    def places():
        x, y, c = _place()
        chips = [(1 - x, y), (x, 1 - y), (1 - x, 1 - y)]
        return (x, y, c), (x, y, 1 - c), chips

    me, sibling, chips = places()
    order = [me, sibling] + [(*chip, me[2]) for chip in chips] + [(*chip, sibling[2]) for chip in chips]
    order = jnp.stack([_slot(p) for p in order]).astype(jnp.int32)

    def body(order_ref, h_ref, *rest):
        src_refs, rest = rest[:n_arr], rest[n_arr:]
        o32_ref, o16_ref = rest[:2]
        got_refs, rest = rest[2:2 + n_arr], rest[2 + n_arr:]
        wbuf, send_sems, recv_sems, own_sems, load_sems = rest
        s, i = pl.program_id(0), pl.program_id(1)
        me, sibling, chips = places()
        c = me[2]
        wg_ref = got_refs[0]

        def remote(t, k, block, to, own_src=False):
            dst = got_refs[t].at[_slot(block)]
            return pltpu.make_async_remote_copy(
                src_ref=src_refs[t] if own_src else dst, dst_ref=dst, send_sem=send_sems.at[7 * t + k],
                recv_sem=recv_sems.at[7 * t + k], device_id=to, device_id_type=MESH)

        def load(step):
            return pltpu.make_async_copy(wg_ref.at[order_ref[step]], wbuf.at[step % 2], load_sems.at[step % 2])

        own = [pltpu.make_async_copy(src_refs[t], got_refs[t].at[_slot(me)], own_sems.at[t]) for t in range(n_arr)]

        @pl.when((s == 0) & (i == 0))
        def _():
            for t in range(n_arr):
                own[t].start()
                remote(t, 0, me, sibling, own_src=True).start()
                for j, chip in enumerate(chips):
                    remote(t, 1 + j, me, (*chip, c), own_src=True).start()
            first = pltpu.make_async_copy(src_refs[0], wbuf.at[0], load_sems.at[0])
            first.start()
            first.wait()

        for nxt in range(1, n_steps):
            @pl.when((s == nxt - 1) & (i == n_i - 1))
            def _():
                if nxt == 1:
                    remote(0, 0, sibling, me).wait_recv()
                elif nxt <= 4:
                    j = nxt - 2
                    remote(0, 1 + j, (*chips[j], c), me).wait_recv()
                    remote(0, 4 + j, (*chips[j], c), sibling).start()
                else:
                    j = nxt - 5
                    remote(0, 4 + j, (*chips[j], 1 - c), me).wait_recv()
                if 3 <= nxt <= 5:
                    j = nxt - 3
                    for t in range(1, n_arr):
                        remote(t, 1 + j, (*chips[j], c), me).wait_recv()
                        remote(t, 4 + j, (*chips[j], c), sibling).start()
                load(nxt).start()

        for step in range(1, n_steps):
            @pl.when((s == step) & (i == 0))
            def _():
                load(step).wait()

        part = lax.dot_general(h_ref[...], wbuf[s % 2], _NN, preferred_element_type=F32)
        o32_ref[...] = part
        o16_ref[...] = part.astype(BF16)

        @pl.when((s == n_steps - 1) & (i == n_i - 1))
        def _():
            for t in range(n_arr):
                if t:
                    remote(t, 0, sibling, me).wait_recv()
                    for j, chip in enumerate(chips):
                        remote(t, 4 + j, (*chip, 1 - c), me).wait_recv()
                remote(t, 0, me, sibling, own_src=True).wait_send()
                for j, chip in enumerate(chips):
                    remote(t, 1 + j, me, (*chip, c), own_src=True).wait_send()
                    remote(t, 4 + j, (*chip, c), sibling).wait_send()
                own[t].wait()

    arrays = [w_shard] + list(riders)
    out = pl.BlockSpec((IN_TILE, ns), lambda s, i, order: (i, order[s]))
    grid_spec = pltpu.PrefetchScalarGridSpec(
        num_scalar_prefetch=1, grid=(n_steps, n_i),
        in_specs=[pl.BlockSpec((IN_TILE, d), lambda s, i, order: (i, 0))] + [_HBM] * n_arr,
        out_specs=[out, out] + [_HBM] * n_arr,
        scratch_shapes=[pltpu.VMEM((2, d, ns), BF16), pltpu.SemaphoreType.DMA((7 * n_arr,)),
                        pltpu.SemaphoreType.DMA((7 * n_arr,)), pltpu.SemaphoreType.DMA((n_arr,)),
                        pltpu.SemaphoreType.DMA((2,))])
    return pl.pallas_call(
        body, name="gather_mm_in_fwd", grid_spec=grid_spec,
        out_shape=[jax.ShapeDtypeStruct((m, N_DEV * ns), F32), jax.ShapeDtypeStruct((m, N_DEV * ns), BF16)]
        + [jax.ShapeDtypeStruct((N_DEV,) + a.shape, a.dtype) for a in arrays],
        compiler_params=_params("arbitrary", "arbitrary"),
    )(order, h0b, *arrays)


def _exchange_partials(parts, whole):
    n_parts = len(parts)
    arrays = list(parts) + list(whole)
    n = len(arrays)

    def body(*refs):
        ins, outs = refs[:n], refs[n:2 * n]
        send_sems, recv_sems, local_sems = refs[2 * n:]
        x, y, c = _place()
        me = (x, y, c)

        def src_for(a, p):
            return ins[a].at[_slot(p)] if a < n_parts else ins[a]

        def copy(a, k, peer):
            return pltpu.make_async_remote_copy(
                src_ref=src_for(a, peer), dst_ref=outs[a].at[_slot(me)], send_sem=send_sems.at[7 * a + k],
                recv_sem=recv_sems.at[7 * a + k], device_id=peer, device_id_type=MESH)

        def landed(a, k, peer):
            return pltpu.make_async_remote_copy(
                src_ref=src_for(a, peer), dst_ref=outs[a].at[_slot(peer)], send_sem=send_sems.at[7 * a + k],
                recv_sem=recv_sems.at[7 * a + k], device_id=peer, device_id_type=MESH)

        peers = []
        for k in range(1, N_DEV):
            fx, fy, fc = (k >> 2) & 1, (k >> 1) & 1, k & 1
            peers.append((1 - x if fx else x, 1 - y if fy else y, 1 - c if fc else c))
        mine = [pltpu.make_async_copy(src_for(a, me), outs[a].at[_slot(me)], local_sems.at[a]) for a in range(n)]
        for cp in mine:
            cp.start()
        sent = [copy(a, k, peer) for a in range(n) for k, peer in enumerate(peers)]
        for cp in sent:
            cp.start()
        for a in range(n):
            for k, peer in enumerate(peers):
                landed(a, k, peer).wait_recv()
        for cp in sent:
            cp.wait_send()
        for cp in mine:
            cp.wait()

    out_shape = [jax.ShapeDtypeStruct(p.shape, p.dtype) for p in parts]
    out_shape += [jax.ShapeDtypeStruct((N_DEV,) + w.shape, w.dtype) for w in whole]
    return pl.pallas_call(
        body, name="exchange_grad_partials",
        in_specs=[_HBM] * n, out_specs=[_HBM] * n, out_shape=out_shape,
        scratch_shapes=[pltpu.SemaphoreType.DMA((7 * n,)), pltpu.SemaphoreType.DMA((7 * n,)),
                        pltpu.SemaphoreType.DMA((n,))],
    )(*arrays)


def _peers_of(x, y, c):
    peers = []
    for k in range(1, N_DEV):
        fx, fy, fc = (k >> 2) & 1, (k >> 1) & 1, k & 1
        peers.append((1 - x if fx else x, 1 - y if fy else y, 1 - c if fc else c))
    return peers


def _with_own(landed, own, me):
    return lax.dynamic_update_index_in_dim(landed, own, me, 0)


def _adamw(name, slabs, w, mom, var, rows):
    r, c = w.shape
    n_slabs = len(slabs)
    per_slab = r // n_slabs // rows

    def body(*refs):
        p_refs = refs[:n_slabs]
        w_ref, m_ref, v_ref, g_ref, d_ref, nm_ref, nv_ref = refs[n_slabs:]
        slab = pl.program_id(0) // per_slab
        g = None
        for p in range(N_DEV):
            part = p_refs[0][p]
            for k in range(1, n_slabs):
                part = jnp.where(slab == k, p_refs[k][p], part)
            g = part.astype(F32) if g is None else g + part.astype(F32)
        m_new = ADAM_B1 * m_ref[...] + (1.0 - ADAM_B1) * g
        v_new = ADAM_B2 * v_ref[...] + (1.0 - ADAM_B2) * (g * g)
        m_hat = m_new / (1.0 - ADAM_B1 ** ADAM_STEP)
        v_hat = v_new / (1.0 - ADAM_B2 ** ADAM_STEP)
        g_ref[...] = g
        d_ref[...] = -ADAM_LR * (m_hat / (jnp.sqrt(v_hat) + ADAM_EPS) + ADAM_WD * w_ref[...])
        nm_ref[...] = m_new
        nv_ref[...] = v_new

    def slab_spec(k):
        return pl.BlockSpec((N_DEV, rows, c), lambda i: (0, jnp.clip(i - k * per_slab, 0, per_slab - 1), 0))

    blk = pl.BlockSpec((rows, c), lambda i: (i, 0))
    return pl.pallas_call(
        body, name=name, grid=(r // rows,),
        in_specs=[slab_spec(k) for k in range(n_slabs)] + [blk, blk, blk],
        out_specs=[blk] * 4, out_shape=[jax.ShapeDtypeStruct((r, c), F32)] * 4,
        compiler_params=_params("parallel"),
    )(*slabs, w, mom, var)


def kernel(x, ln_in_g, ln_in_b, w_in, w_sb_proj, conv_w, conv_b, conv_ln_g, conv_ln_b, w_cv_proj, w_out, ln_post_g, ln_post_b, loss_target, m_ln_in_g, m_ln_in_b, m_w_in, m_w_sb_proj, m_conv_w, m_conv_b, m_conv_ln_g, m_conv_ln_b, m_w_cv_proj, m_w_out, m_ln_post_g, m_ln_post_b, v_ln_in_g, v_ln_in_b, v_w_in, v_w_sb_proj, v_conv_w, v_conv_b, v_conv_ln_g, v_conv_ln_b, v_w_cv_proj, v_w_out, v_ln_post_g, v_ln_post_b):
    nb, seq, d = x.shape
    m = nb * seq
    x2d = x.reshape(m, d)
    target = loss_target.reshape(m, d)
    rs = d // N_DEV
    pad_taps = ((0, CV_HALO - CV_KERNEL), (0, 0))

    proj_shards = jnp.stack([w_sb_proj[0], w_cv_proj[0], w_out[0]]).astype(BF16)
    conv_w_shard = jnp.pad(conv_w[0], pad_taps)
    me = _slot(_place())
    own = lambda parts: lax.dynamic_index_in_dim(parts, me, 0, keepdims=False)

    h0, h0b = _ln_in_fwd(x2d, ln_in_g.reshape(1, d), ln_in_b.reshape(1, d))
    h, hb, w_in_g = _gather_mm_in_fwd(h0b, w_in[0].astype(BF16), [])
    o, a_sb, carries, proj_g, conv_w_g = _attn_fwd(hb, h, nb, seq, [proj_shards, conv_w_shard])
    proj_g = _with_own(proj_g, proj_shards, me)
    conv_w_g = _with_own(conv_w_g, conv_w_shard, me)
    w_sb_g = proj_g[:, 0].reshape(d, d)
    w_cv_g = proj_g[:, 1].reshape(d, d)
    w_out_g = proj_g[:, 2].reshape(d, d)
    conv_w_full = conv_w_g.transpose(1, 0, 2).reshape(CV_HALO, d)
    a_cv = _conv_fwd(h, conv_w_full, conv_b, conv_ln_g, conv_ln_b, nb, seq)
    y_sb = _mm_nn("mm_sb_fwd", a_sb, w_sb_g)
    y_cv = _mm_nn("mm_cv_fwd", a_cv, w_cv_g)
    merged = _merge_fwd(h, y_sb, y_cv)
    mo = _mm_nn("mm_out_fwd", merged, w_out_g)
    loss_part, d_pre, d_pre_b, dg_post, db_post = _post_ln_loss(h0, mo, target, ln_post_g, ln_post_b)

    dw_out = _mm_tn("mm_out_bwd_w", merged, d_pre_b, BF16)
    d_merged = _mm_nt("mm_out_bwd_x", d_pre_b, w_out_g)
    dy_sb, dy_cv, d_gates = _merge_bwd(h, y_sb, y_cv, d_merged)
    dw_sb = _mm_tn("mm_sb_bwd_w", a_sb, dy_sb, BF16)
    dw_cv = _mm_tn("mm_cv_bwd_w", a_cv, dy_cv, BF16)
    da_sb = _mm_nt("mm_sb_bwd_x", dy_sb, w_sb_g)
    da_cv = _mm_nt("mm_cv_bwd_x", dy_cv, w_cv_g)
    dproj = jnp.stack([dw_sb.reshape(N_DEV, rs, d), dw_cv.reshape(N_DEV, rs, d), dw_out.reshape(N_DEV, rs, d)], axis=1)
    dq, dk, dv, dz_sb, r_proj = _attn_bwd(hb, h, o, carries, da_sb, nb, seq, [dproj])
    d_conv3, dconv_w, dconv_b, dconv_ln_g, dconv_ln_b = _conv_bwd(
        h, da_cv, conv_w_full, conv_b, conv_ln_g, conv_ln_b, nb, seq)
    dhb = jnp.concatenate([dq, dk, dv, dz_sb, d_conv3, d_gates], axis=1)
    dconv_w_parts = dconv_w.reshape(CV_HALO, N_DEV, d // N_DEV).transpose(1, 0, 2)
    dw_in_a, r_conv = _mm_in_bwd_w(h0b, dhb, 0, [dconv_w_parts])
    dw_in_b, r_in_a = _mm_in_bwd_w(h0b, dhb, 1, [dw_in_a])
    dh_mm, r_in_b = _mm_in_bwd_x(dhb, w_in_g, [dw_in_b])
    dx, dg_in, db_in = _ln_in_bwd(x2d, d_pre, dh_mm, ln_in_g.reshape(1, d))
    r_proj = _with_own(r_proj, own(dproj), me)
    r_conv = _with_own(r_conv, own(dconv_w_parts), me)
    r_in = [_with_own(r_in_a, own(dw_in_a), me), _with_own(r_in_b, own(dw_in_b), me)]

    small = jnp.concatenate([dg_in, db_in, dconv_b, dconv_ln_g, dconv_ln_b, dg_post, db_post,
                             jnp.zeros((1, d), F32)], axis=0)
    (r_small,) = _exchange_partials([], [small])

    g_in, d_in, nm_in, nv_in = _adamw("adamw_w_in", r_in, w_in[0], m_w_in[0], v_w_in[0], 128)
    stack3 = lambda a, b, c: jnp.concatenate([a[0], b[0], c[0]], axis=0)
    g_pr, d_pr, nm_pr, nv_pr = _adamw(
        "adamw_proj", [r_proj.reshape(N_DEV, 3 * rs, d)], stack3(w_sb_proj, w_cv_proj, w_out),
        stack3(m_w_sb_proj, m_w_cv_proj, m_w_out), stack3(v_w_sb_proj, v_w_cv_proj, v_w_out), 3 * rs)
    padc = lambda a: jnp.pad(a[0], pad_taps)
    g_cw, d_cw, nm_cw, nv_cw = _adamw("adamw_conv_w", [r_conv], padc(conv_w), padc(m_conv_w), padc(v_conv_w), CV_HALO)
    vecs = lambda *a: jnp.concatenate([t.reshape(1, d) for t in a] + [jnp.ones((1, d), F32)], axis=0)
    g_sm, d_sm, nm_sm, nv_sm = _adamw(
        "adamw_vectors", [r_small],
        vecs(ln_in_g, ln_in_b, conv_b, conv_ln_g, conv_ln_b, ln_post_g, ln_post_b),
        vecs(m_ln_in_g, m_ln_in_b, m_conv_b, m_conv_ln_g, m_conv_ln_b, m_ln_post_g, m_ln_post_b),
        vecs(v_ln_in_g, v_ln_in_b, v_conv_b, v_conv_ln_g, v_conv_ln_b, v_ln_post_g, v_ln_post_b), 8)

    loss = lax.psum(loss_part[0, 0], ("x", "y", "c"))

    def leaves(big, pr, cw, sm):
        return (sm[0], sm[1], big[None], pr[None, 0:rs], cw[None, :CV_KERNEL], sm[2:3], sm[3:4], sm[4:5],
                pr[None, rs:2 * rs], pr[None, 2 * rs:], sm[5:6], sm[6:7])

    return (loss, dx.reshape(nb, seq, d), *leaves(g_in, g_pr, g_cw, g_sm), *leaves(d_in, d_pr, d_cw, d_sm),
            *leaves(nm_in, nm_pr, nm_cw, nm_sm), *leaves(nv_in, nv_pr, nv_cw, nv_sm))
```

```python
import functools
import math

import jax
import jax.numpy as jnp
from jax import lax
from jax.experimental import pallas as pl
from jax.experimental.pallas import tpu as pltpu

F32 = jnp.float32
BF16 = jnp.bfloat16
MESH = pl.DeviceIdType.MESH

N_DEV = 8
SB_HEADS = 16
HEAD_DIM = 64
CV_KERNEL = 31
CV_HALO = 32
LN_EPS = 1e-5
DEEPNORM_ALPHA = 2.0 ** 0.25
ADAM_LR, ADAM_B1, ADAM_B2, ADAM_EPS, ADAM_WD, ADAM_STEP = 0.001, 0.9, 0.999, 1e-08, 0.01, 10

ATT_TK = 256
ATT_TQ = 1024
ATT_NEAR = 512
ROW_TILE = 512
IN_TILE = 1024
CONV_TILE = 256
VMEM_LIMIT = 56 * 1024 * 1024


def _params(*sem):
    return pltpu.CompilerParams(dimension_semantics=sem, vmem_limit_bytes=VMEM_LIMIT)


def _sigmoid(x):
    return 1.0 / (1.0 + jnp.exp(-x))


def _silu_and_grad(x):
    s = _sigmoid(x)
    return x * s, s * (1.0 + x * (1.0 - s))


def _ln_stats(x):
    mu = jnp.mean(x, axis=-1, keepdims=True)
    xc = x - mu
    var = jnp.mean(xc * xc, axis=-1, keepdims=True)
    rstd = lax.rsqrt(var + LN_EPS)
    return xc * rstd, rstd


def _ln_bwd(dxhat, xhat, rstd):
    m1 = jnp.mean(dxhat, axis=-1, keepdims=True)
    m2 = jnp.mean(dxhat * xhat, axis=-1, keepdims=True)
    return rstd * (dxhat - m1 - xhat * m2)


def _ln_in_fwd(x2d, g, b):
    m, d = x2d.shape

    def body(x_ref, g_ref, b_ref, h_ref, hb_ref):
        xhat, _ = _ln_stats(x_ref[...])
        y = xhat * g_ref[...] + b_ref[...]
        h_ref[...] = y
        hb_ref[...] = y.astype(BF16)

    row = pl.BlockSpec((ROW_TILE, d), lambda i: (i, 0))
    vec = pl.BlockSpec((1, d), lambda i: (0, 0))
    return pl.pallas_call(
        body, name="ln_in_fwd", grid=(m // ROW_TILE,),
        in_specs=[row, vec, vec], out_specs=[row, row],
        out_shape=[jax.ShapeDtypeStruct((m, d), F32), jax.ShapeDtypeStruct((m, d), BF16)],
        compiler_params=_params("parallel"),
    )(x2d, g, b)


def _ln_in_bwd(x2d, d_pre, dh_mm, g):
    m, d = x2d.shape

    def body(x_ref, dp_ref, dm_ref, g_ref, dx_ref, dg_ref, db_ref):
        @pl.when(pl.program_id(0) == 0)
        def _():
            dg_ref[...] = jnp.zeros_like(dg_ref)
            db_ref[...] = jnp.zeros_like(db_ref)

        xhat, rstd = _ln_stats(x_ref[...])
        dh = DEEPNORM_ALPHA * dp_ref[...] + dm_ref[...]
        dg_ref[...] += jnp.sum(dh * xhat, axis=0, keepdims=True)
        db_ref[...] += jnp.sum(dh, axis=0, keepdims=True)
        dx_ref[...] = _ln_bwd(dh * g_ref[...], xhat, rstd)

    row = pl.BlockSpec((ROW_TILE, d), lambda i: (i, 0))
    vec = pl.BlockSpec((1, d), lambda i: (0, 0))
    return pl.pallas_call(
        body, name="ln_in_bwd", grid=(m // ROW_TILE,),
        in_specs=[row, row, row, vec], out_specs=[row, vec, vec],
        out_shape=[jax.ShapeDtypeStruct((m, d), F32), jax.ShapeDtypeStruct((1, d), F32),
                   jax.ShapeDtypeStruct((1, d), F32)],
        compiler_params=_params("arbitrary"),
    )(x2d, d_pre, dh_mm, g)


def _merge_fwd(h, y_sb, y_cv):
    m, d = y_sb.shape

    def body(gs_ref, gc_ref, ys_ref, yc_ref, out_ref):
        out_ref[...] = (_sigmoid(gs_ref[...]) * ys_ref[...] + _sigmoid(gc_ref[...]) * yc_ref[...]).astype(BF16)

    row = pl.BlockSpec((ROW_TILE, d), lambda i: (i, 0))
    return pl.pallas_call(
        body, name="merge_fwd", grid=(m // ROW_TILE,),
        in_specs=[pl.BlockSpec((ROW_TILE, d), lambda i: (i, 7)), pl.BlockSpec((ROW_TILE, d), lambda i: (i, 8)), row, row],
        out_specs=row, out_shape=jax.ShapeDtypeStruct((m, d), BF16),
        compiler_params=_params("parallel"),
    )(h, h, y_sb, y_cv)


def _merge_bwd(h, y_sb, y_cv, dm):
    m, d = y_sb.shape

    def body(gs_ref, gc_ref, ys_ref, yc_ref, dm_ref, dys_ref, dyc_ref, dgate_ref):
        dmv = dm_ref[...]
        ss = _sigmoid(gs_ref[...])
        sc = _sigmoid(gc_ref[...])
        dys_ref[...] = (ss * dmv).astype(BF16)
        dyc_ref[...] = (sc * dmv).astype(BF16)
        dgate_ref[:, :d] = (dmv * ys_ref[...] * ss * (1.0 - ss)).astype(BF16)
        dgate_ref[:, d:] = (dmv * yc_ref[...] * sc * (1.0 - sc)).astype(BF16)

    row = pl.BlockSpec((ROW_TILE, d), lambda i: (i, 0))
    return pl.pallas_call(
        body, name="merge_bwd", grid=(m // ROW_TILE,),
        in_specs=[pl.BlockSpec((ROW_TILE, d), lambda i: (i, 7)), pl.BlockSpec((ROW_TILE, d), lambda i: (i, 8)), row, row, row],
        out_specs=[row, row, pl.BlockSpec((ROW_TILE, 2 * d), lambda i: (i, 0))],
        out_shape=[jax.ShapeDtypeStruct((m, d), BF16), jax.ShapeDtypeStruct((m, d), BF16),
                   jax.ShapeDtypeStruct((m, 2 * d), BF16)],
        compiler_params=_params("parallel"),
    )(h, h, y_sb, y_cv, dm)


def _post_ln_loss(h0, mo, target, g, b):
    m, d = h0.shape

    def body(h_ref, mo_ref, t_ref, g_ref, b_ref, loss_ref, dp_ref, dpb_ref, dg_ref, db_ref):
        @pl.when(pl.program_id(0) == 0)
        def _():
            loss_ref[...] = jnp.zeros_like(loss_ref)
            dg_ref[...] = jnp.zeros_like(dg_ref)
            db_ref[...] = jnp.zeros_like(db_ref)

        xhat, rstd = _ln_stats(DEEPNORM_ALPHA * h_ref[...] + mo_ref[...])
        err = xhat * g_ref[...] + b_ref[...] - t_ref[...]
        per_row = jnp.mean(err * err, axis=-1, keepdims=True)
        loss_ref[...] += 0.5 * jnp.sum(per_row, axis=0, keepdims=True)
        dy = err * (1.0 / d)
        dg_ref[...] += jnp.sum(dy * xhat, axis=0, keepdims=True)
        db_ref[...] += jnp.sum(dy, axis=0, keepdims=True)
        dp = _ln_bwd(dy * g_ref[...], xhat, rstd)
        dp_ref[...] = dp
        dpb_ref[...] = dp.astype(BF16)

    row = pl.BlockSpec((ROW_TILE, d), lambda i: (i, 0))
    vec = pl.BlockSpec((1, d), lambda i: (0, 0))
    one = pl.BlockSpec((1, 1), lambda i: (0, 0))
    return pl.pallas_call(
        body, name="post_ln_loss", grid=(m // ROW_TILE,),
        in_specs=[row, row, row, vec, vec], out_specs=[one, row, row, vec, vec],
        out_shape=[jax.ShapeDtypeStruct((1, 1), F32), jax.ShapeDtypeStruct((m, d), F32),
                   jax.ShapeDtypeStruct((m, d), BF16), jax.ShapeDtypeStruct((1, d), F32),
                   jax.ShapeDtypeStruct((1, d), F32)],
        compiler_params=_params("arbitrary"),
    )(h0, mo, target, g, b)


_NN = (((1,), (0,)), ((), ()))
_NT = (((1,), (1,)), ((), ()))
_TN = (((0,), (0,)), ((), ()))


def _grid_ends(grid):
    ids = [pl.program_id(ax) for ax in range(len(grid))]
    first = functools.reduce(jnp.logical_and, [i == 0 for i in ids])
    last = functools.reduce(jnp.logical_and, [i == n - 1 for i, n in zip(ids, grid)])
    return first, last


def _travel(grid, src_refs, land_refs, send_sems, recv_sems, whole):
    first, last = _grid_ends(grid)
    x, y, c = _place()
    n = len(src_refs)

    def copy(t, k, peer, lands_at):
        return pltpu.make_async_remote_copy(
            src_ref=src_refs[t] if whole else src_refs[t].at[_slot(peer)],
            dst_ref=land_refs[t].at[_slot(lands_at)], send_sem=send_sems.at[7 * t + k],
            recv_sem=recv_sems.at[7 * t + k], device_id=peer, device_id_type=MESH)

    @pl.when(first)
    def _():
        for k, peer in enumerate(_peers_of(x, y, c)):
            for t in range(n):
                copy(t, k, peer, (x, y, c)).start()

    def finish():
        @pl.when(last)
        def _():
            for k, peer in enumerate(_peers_of(x, y, c)):
                for t in range(n):
                    arrived = copy(t, k, peer, peer)
                    arrived.wait_send()
                    arrived.wait_recv()

    return finish


def _travel_scratch(n):
    return [pltpu.SemaphoreType.DMA((7 * n,)), pltpu.SemaphoreType.DMA((7 * n,))]


def _mm(name, a, b, dims, grid, a_spec, b_spec, out_specs, out_shape, acc_shape, k_axis, send=()):
    n_k = 1 if k_axis is None else grid[k_axis]
    n_out = len(out_shape)
    n_send = len(send)

    def body(a_ref, b_ref, *rest):
        src_refs, rest = rest[:n_send], rest[n_send:]
        outs, land_refs, acc_ref = rest[:n_out], rest[n_out:n_out + n_send], rest[n_out + n_send]
        if n_send:
            finish = _travel(grid, src_refs, land_refs, *rest[n_out + n_send + 1:], whole=False)

        part = lax.dot_general(a_ref[...].astype(BF16), b_ref[...].astype(BF16), dims, preferred_element_type=F32)
        if n_k == 1:
            for o in outs:
                o[...] = part.astype(o.dtype)
        else:
            k = pl.program_id(k_axis)

            @pl.when(k == 0)
            def _():
                acc_ref[...] = part

            @pl.when(k > 0)
            def _():
                acc_ref[...] += part

            @pl.when(k == n_k - 1)
            def _():
                for o in outs:
                    o[...] = acc_ref[...].astype(o.dtype)

        if n_send:
            finish()

    sem = tuple("arbitrary" for _ in grid) if n_send else tuple(
        "arbitrary" if ax == k_axis else "parallel" for ax in range(len(grid)))
    scratch = [pltpu.VMEM(acc_shape, F32)] + (_travel_scratch(n_send) if n_send else [])
    return pl.pallas_call(
        body, name=name, grid=grid, in_specs=[a_spec, b_spec] + [_HBM] * n_send,
        out_specs=list(out_specs) + [_HBM] * n_send,
        out_shape=list(out_shape) + [jax.ShapeDtypeStruct(s.shape, s.dtype) for s in send],
        scratch_shapes=scratch, compiler_params=_params(*sem),
    )(a, b, *send)


def _mm_nn(name, a, b, out_dtype=F32):
    m, k = a.shape
    n = b.shape[1]
    return _mm(name, a, b, _NN, (m // ROW_TILE,), pl.BlockSpec((ROW_TILE, k), lambda i: (i, 0)),
               pl.BlockSpec((k, n), lambda i: (0, 0)), [pl.BlockSpec((ROW_TILE, n), lambda i: (i, 0))],
               [jax.ShapeDtypeStruct((m, n), out_dtype)], (8, 128), None)[0]


def _mm_nt(name, a, b, out_dtype=F32):
    m, k = a.shape
    n = b.shape[0]
    return _mm(name, a, b, _NT, (m // ROW_TILE,), pl.BlockSpec((ROW_TILE, k), lambda i: (i, 0)),
               pl.BlockSpec((n, k), lambda i: (0, 0)), [pl.BlockSpec((ROW_TILE, n), lambda i: (i, 0))],
               [jax.ShapeDtypeStruct((m, n), out_dtype)], (8, 128), None)[0]


def _mm_tn(name, a, b, out_dtype):
    m, k = a.shape
    n = b.shape[1]
    return _mm(name, a, b, _TN, (m // ROW_TILE,), pl.BlockSpec((ROW_TILE, k), lambda i: (i, 0)),
               pl.BlockSpec((ROW_TILE, n), lambda i: (i, 0)), [pl.BlockSpec((k, n), lambda i: (0, 0))],
               [jax.ShapeDtypeStruct((k, n), out_dtype)], (k, n), 0)[0]


def _mm_in_bwd_x(dhb, w_g, send):
    m = dhb.shape[0]
    _, d, ns = w_g.shape
    return _mm("mm_in_bwd_x", dhb, w_g, _NT, (m // IN_TILE, N_DEV), pl.BlockSpec((IN_TILE, ns), lambda i, j: (i, j)),
               pl.BlockSpec((None, d, ns), lambda i, j: (j, 0, 0)), [pl.BlockSpec((IN_TILE, d), lambda i, j: (i, 0))],
               [jax.ShapeDtypeStruct((m, d), F32)], (IN_TILE, d), 1, send=send)


def _mm_in_bwd_w(h0b, dhb, half, send):
    m, d = h0b.shape
    ns = dhb.shape[1] // N_DEV
    dh = d // 2
    return _mm(f"mm_in_bwd_w{half}", h0b, dhb, _TN, (N_DEV, m // IN_TILE),
               pl.BlockSpec((IN_TILE, dh), lambda j, k: (k, half)),
               pl.BlockSpec((IN_TILE, ns), lambda j, k: (k, j)), [pl.BlockSpec((None, dh, ns), lambda j, k: (j, 0, 0))],
               [jax.ShapeDtypeStruct((N_DEV, dh, ns), BF16)], (dh, ns), 1, send=send)


LOG2_E = 1.4426950408889634
DEAD_AFTER = 110.0
NEVER_REACHED = 1e30


def _twice(u):
    return jnp.concatenate([u, u], axis=0)


def _dot2(x, u2):
    hi = x.astype(BF16)
    lo = (x - hi.astype(F32)).astype(BF16)
    return lax.dot_general(jnp.concatenate([hi, lo], axis=1), u2, _NN, preferred_element_type=F32)


def _softplus(l):
    e = jnp.exp2(jnp.abs(l) * (-LOG2_E))
    e1 = 1.0 + e
    return jnp.maximum(l, 0.0) + jnp.log(e1), e, e1


def _attn_fwd(hb, h, nb, seq, gather):
    m = nb * seq
    d = SB_HEADS * HEAD_DIM
    tq, tk = ATT_TQ, ATT_TK
    n_sub = tq // tk
    nq = seq // tq
    sec = d // 128
    scale = 1.0 / math.sqrt(HEAD_DIM)
    heads = [slice(hh * HEAD_DIM, (hh + 1) * HEAD_DIM) for hh in range(2)]
    grid = (nb, sec, nq)
    n_g = len(gather)

    def body(q_ref, k_ref, v_ref, z_ref, *rest):
        src_refs, rest = rest[:n_g], rest[n_g:]
        o_ref, a_ref, c_ref = rest[:3]
        land_refs, (carry_ref, acc_ref, send_sems, recv_sems) = rest[3:3 + n_g], rest[3 + n_g:]
        finish = _travel(grid, src_refs, land_refs, send_sems, recv_sems, whole=True)
        qi = pl.program_id(2)
        n_full = n_sub * qi
        u_after = _twice((lax.broadcasted_iota(jnp.int32, (tk, tk), 0)
                          > lax.broadcasted_iota(jnp.int32, (tk, tk), 1)).astype(BF16))
        qs = [(q_ref[:, cols].astype(F32) * scale).astype(BF16) for cols in heads]
        carry_ref[...] = jnp.zeros_like(carry_ref)
        acc_ref[...] = jnp.zeros_like(acc_ref)
        lane_t = lax.broadcasted_iota(jnp.int32, (tq, 128), 1)
        c_ref[...] = jnp.where(lane_t % HEAD_DIM < n_full + n_sub, NEVER_REACHED, 0.0)

        def block(kb, r0, n, masked):
            qrows = slice(r0, r0 + n)
            krows = pl.ds(pl.multiple_of(kb * tk, tk), tk)
            lane = lax.broadcasted_iota(jnp.int32, (n, 128), 1)
            if masked:
                causal = lax.broadcasted_iota(jnp.int32, (n, tk), 1) < lax.broadcasted_iota(jnp.int32, (n, tk), 0)
            for hh, cols in enumerate(heads):
                carry = carry_ref[hh, qrows, :]
                l = lax.dot_general(qs[hh][qrows], k_ref[krows, cols], _NT, preferred_element_type=F32)
                sp, _, _ = _softplus(l)
                if masked:
                    sp = jnp.where(causal, sp, 0.0)
                after = carry + _dot2(sp, u_after)
                a = jnp.exp(l - (sp + after))
                if masked:
                    a = jnp.where(causal, a, 0.0)
                acc_ref[qrows, cols] += lax.dot_general(a.astype(BF16), v_ref[krows, cols], _NN,
                                                        preferred_element_type=F32)
                c_ref[qrows, :] = jnp.where(lane == hh * HEAD_DIM + kb, carry, c_ref[qrows, :])
                carry_ref[hh, qrows, :] = after[:, 0:1] + sp[:, 0:1]

        def alive(r0, n):
            return jnp.min(carry_ref[:, r0:r0 + n, :]) <= DEAD_AFTER

        for j in reversed(range(n_sub)):
            r0 = j * tk
            near = min(ATT_NEAR, tq - r0)
            block(n_full + j, r0, near, True)
            if r0 + near < tq:
                @pl.when(alive(r0 + near, tq - r0 - near))
                def _():
                    block(n_full + j, r0 + near, tq - r0 - near, False)

        def step(state):
            i, _ = state
            for r0 in range(0, tq, ATT_NEAR):
                @pl.when(alive(r0, ATT_NEAR))
                def _():
                    block(n_full - 1 - i, r0, ATT_NEAR, False)
            return i + 1, alive(0, tq)

        lax.while_loop(lambda state: (state[0] < n_full) & state[1], step, (0, alive(0, tq)))
        o = acc_ref[...]
        o_ref[...] = o
        silu, _ = _silu_and_grad(z_ref[...])
        a_ref[...] = (o * silu).astype(BF16)
        finish()

    qspec = pl.BlockSpec((tq, 128), lambda b, hp, qi: (b * nq + qi, hp))
    return pl.pallas_call(
        body, name="attn_fwd", grid=grid,
        in_specs=[qspec,
                  pl.BlockSpec((seq, 128), lambda b, hp, qi: (b, sec + hp)),
                  pl.BlockSpec((seq, 128), lambda b, hp, qi: (b, 2 * sec + hp)),
                  pl.BlockSpec((tq, 128), lambda b, hp, qi: (b * nq + qi, 3 * sec + hp))] + [_HBM] * n_g,
        out_specs=[qspec, qspec, pl.BlockSpec((None, tq, 128), lambda b, hp, qi: (hp, b * nq + qi, 0))]
        + [_HBM] * n_g,
        out_shape=[jax.ShapeDtypeStruct((m, d), F32), jax.ShapeDtypeStruct((m, d), BF16),
                   jax.ShapeDtypeStruct((sec, m, 128), F32)]
        + [jax.ShapeDtypeStruct((N_DEV,) + g.shape, g.dtype) for g in gather],
        scratch_shapes=[pltpu.VMEM((2, tq, 1), F32), pltpu.VMEM((tq, 128), F32)] + _travel_scratch(n_g),
        compiler_params=_params("arbitrary", "arbitrary", "arbitrary"),
    )(hb, hb, hb, h, *gather)


def _attn_bwd(hb, h, o, carries, da, nb, seq, send):
    m = nb * seq
    d = SB_HEADS * HEAD_DIM
    tq, tk = ATT_TQ, ATT_TK
    n_sub = tq // tk
    nq = seq // tq
    sec = d // 128
    scale = 1.0 / math.sqrt(HEAD_DIM)
    heads = [slice(hh * HEAD_DIM, (hh + 1) * HEAD_DIM) for hh in range(2)]
    grid = (nb, sec, nq)
    n_send = len(send)

    def body(q_ref, k_ref, v_ref, z_ref, o_ref, c_ref, da_ref, *rest):
        src_refs, rest = rest[:n_send], rest[n_send:]
        dq_ref, dk_ref, dv_ref, dz_ref = rest[:4]
        land_refs, rest = rest[4:4 + n_send], rest[4 + n_send:]
        dk_acc, dv_acc, dq_acc, gcarry_ref, send_sems, recv_sems = rest
        finish = _travel(grid, src_refs, land_refs, send_sems, recv_sems, whole=False)
        qi = pl.program_id(2)

        @pl.when(qi == 0)
        def _():
            dk_acc[...] = jnp.zeros_like(dk_acc)
            dv_acc[...] = jnp.zeros_like(dv_acc)

        row = lax.broadcasted_iota(jnp.int32, (tk, tk), 0)
        col = lax.broadcasted_iota(jnp.int32, (tk, tk), 1)
        u_after = _twice((row > col).astype(BF16))
        u_before = _twice((row < col).astype(BF16))
        silu, dsilu = _silu_and_grad(z_ref[...])
        dav = da_ref[...]
        dz_ref[...] = (dav * o_ref[...] * dsilu).astype(BF16)
        do2 = (dav * silu).astype(BF16)
        qs = [(q_ref[:, cols].astype(F32) * scale).astype(BF16) for cols in heads]
        dos = [do2[:, cols] for cols in heads]
        dq_acc[...] = jnp.zeros_like(dq_acc)
        gcarry_ref[...] = jnp.zeros_like(gcarry_ref)

        def block(kb, r0, n, masked):
            qrows = slice(r0, r0 + n)
            krows = pl.ds(pl.multiple_of(kb * tk, tk), tk)
            lane = lax.broadcasted_iota(jnp.int32, (n, 128), 1)
            if masked:
                causal = lax.broadcasted_iota(jnp.int32, (n, tk), 1) < lax.broadcasted_iota(jnp.int32, (n, tk), 0)
            for hh, cols in enumerate(heads):
                q, do = qs[hh][qrows], dos[hh][qrows]
                ks = k_ref[krows, cols]
                l = lax.dot_general(q, ks, _NT, preferred_element_type=F32)
                sp, e, e1 = _softplus(l)
                if masked:
                    sp = jnp.where(causal, sp, 0.0)
                carry = jnp.sum(jnp.where(lane == hh * HEAD_DIM + kb, c_ref[qrows, :], 0.0), axis=1, keepdims=True)
                after = carry + _dot2(sp, u_after)
                a = jnp.exp(l - (sp + after))
                if masked:
                    a = jnp.where(causal, a, 0.0)
                dv_acc[krows, cols] += lax.dot_general(a.astype(BF16), do, _TN, preferred_element_type=F32)
                g = a * lax.dot_general(do, v_ref[krows, cols], _NT, preferred_element_type=F32)
                c = gcarry_ref[hh, qrows, :] + _dot2(g, u_before)
                r = pl.reciprocal(e1, approx=True)
                beta = jnp.where(l >= 0.0, r, e * r)
                dl = g - (g + c) * beta
                if masked:
                    dl = jnp.where(causal, dl, 0.0)
                dl_b = dl.astype(BF16)
                dq_acc[qrows, cols] += lax.dot_general(dl_b, ks, _NN, preferred_element_type=F32)
                dk_acc[krows, cols] += lax.dot_general(dl_b, q, _TN, preferred_element_type=F32)
                gcarry_ref[hh, qrows, :] = c[:, tk - 1:tk] + g[:, tk - 1:tk]

        def alive(kb, r0, n):
            lane = lax.broadcasted_iota(jnp.int32, (n, 128), 1)
            mine = (lane == kb) | (lane == HEAD_DIM + kb)
            return jnp.min(jnp.where(mine, c_ref[r0:r0 + n, :], jnp.inf)) <= DEAD_AFTER

        def step(kb, _):
            for r0 in range(0, tq, ATT_NEAR):
                @pl.when(alive(kb, r0, ATT_NEAR))
                def _():
                    block(kb, r0, ATT_NEAR, False)

            return 0

        lax.fori_loop(0, n_sub * qi, step, 0)
        for j in range(n_sub):
            kb, r0 = n_sub * qi + j, j * tk
            near = min(ATT_NEAR, tq - r0)
            if r0 + near < tq:
                @pl.when(alive(kb, r0 + near, tq - r0 - near))
                def _():
                    block(kb, r0 + near, tq - r0 - near, False)
            block(kb, r0, near, True)
        dq_ref[...] = (dq_acc[...] * scale).astype(BF16)

        @pl.when(qi == nq - 1)
        def _():
            dk_ref[...] = dk_acc[...].astype(BF16)
            dv_ref[...] = dv_acc[...].astype(BF16)

        finish()

    qspec = pl.BlockSpec((tq, 128), lambda b, hp, qi: (b * nq + qi, hp))
    kvspec = pl.BlockSpec((seq, 128), lambda b, hp, qi: (b, hp))
    return pl.pallas_call(
        body, name="attn_bwd", grid=grid,
        in_specs=[qspec,
                  pl.BlockSpec((seq, 128), lambda b, hp, qi: (b, sec + hp)),
                  pl.BlockSpec((seq, 128), lambda b, hp, qi: (b, 2 * sec + hp)),
                  pl.BlockSpec((tq, 128), lambda b, hp, qi: (b * nq + qi, 3 * sec + hp)),
                  qspec,
                  pl.BlockSpec((None, tq, 128), lambda b, hp, qi: (hp, b * nq + qi, 0)),
                  qspec] + [_HBM] * n_send,
        out_specs=[qspec, kvspec, kvspec, qspec] + [_HBM] * n_send,
        out_shape=[jax.ShapeDtypeStruct((m, d), BF16)] * 4 + [jax.ShapeDtypeStruct(s.shape, s.dtype) for s in send],
        scratch_shapes=[pltpu.VMEM((seq, 128), F32), pltpu.VMEM((seq, 128), F32), pltpu.VMEM((tq, 128), F32),
                        pltpu.VMEM((2, tq, 1), F32)] + _travel_scratch(n_send),
        compiler_params=_params("arbitrary", "arbitrary", "arbitrary"),
    )(hb, hb, hb, h, o, carries, da, *send)


SUBLANES = 8


def _by_residue(taps):
    groups = []
    for res in range(SUBLANES):
        mine = [(off - res, k) for off, k in taps if off % SUBLANES == res]
        if mine:
            groups.append((res, max(a for a, _ in mine), mine))
    return groups


def _staged(src_ref, stage_ref, res, n):
    buf = stage_ref.at[res % 2]
    if res == 0:
        return src_ref
    buf[0:n, :] = src_ref[pl.ds(res, n), :]
    return buf


def _tap_sum(src_ref, stage_ref, w_ref, n_rows, taps, acc):
    for res, reach, mine in _by_residue(taps):
        shifted = _staged(src_ref, stage_ref, res, n_rows + reach)
        for a, k in mine:
            acc = acc + w_ref[k:k + 1, :] * shifted[a:a + n_rows, :]
    return acc


_CONV_TAPS = [(CV_HALO - CV_KERNEL + 1 + k, k) for k in range(CV_KERNEL)]
_CONV_TAPS_T = [(CV_KERNEL - 1 - k, k) for k in range(CV_KERNEL)]


def _conv_rows(pad_ref, stage_ref, w_ref, b_ref, n_rows):
    bias = jnp.broadcast_to(b_ref[...], (n_rows, b_ref.shape[1]))
    return _tap_sum(pad_ref, stage_ref, w_ref, n_rows, _CONV_TAPS, bias)


def _conv_fwd(h, conv_w, conv_b, ln_g, ln_b, nb, seq):
    m = nb * seq
    d = conv_b.shape[1]
    t = CONV_TILE
    tiles = seq // t
    hpt = t // CV_HALO

    def body(cv_ref, cg_ref, pv_ref, pg_ref, z_ref, w_ref, b_ref, g_ref, bb_ref, a_ref, pad_ref, stage_ref):
        first = pl.program_id(0) % tiles == 0
        pad_ref[0:CV_HALO, :] = jnp.where(first, 0.0, pv_ref[...] * _sigmoid(pg_ref[...]))
        pad_ref[CV_HALO:, :] = cv_ref[...] * _sigmoid(cg_ref[...])
        xhat, _ = _ln_stats(_conv_rows(pad_ref, stage_ref, w_ref, b_ref, t))
        s, _ = _silu_and_grad(xhat * g_ref[...] + bb_ref[...])
        sz, _ = _silu_and_grad(z_ref[...])
        a_ref[...] = (s * sz).astype(BF16)

    def main(c):
        return pl.BlockSpec((t, d), lambda i: (i, c))

    def prev(c):
        return pl.BlockSpec((CV_HALO, d), lambda i: (jnp.maximum(i * hpt - 1, 0), c))

    vec = pl.BlockSpec((1, d), lambda i: (0, 0))
    return pl.pallas_call(
        body, name="conv_fwd", grid=(m // t,),
        in_specs=[main(4), main(5), prev(4), prev(5), main(6),
                  pl.BlockSpec((CV_HALO, d), lambda i: (0, 0)), vec, vec, vec],
        out_specs=pl.BlockSpec((t, d), lambda i: (i, 0)),
        out_shape=jax.ShapeDtypeStruct((m, d), BF16),
        scratch_shapes=[pltpu.VMEM((CV_HALO + t, d), F32), pltpu.VMEM((2, CV_HALO + t, d), F32)],
        compiler_params=_params("parallel"),
    )(h, h, h, h, h, conv_w, conv_b, ln_g, ln_b)


def _conv_bwd(h, da, conv_w, conv_b, ln_g, ln_b, nb, seq):
    m = nb * seq
    d = conv_b.shape[1]
    t = CONV_TILE
    tiles = seq // t
    hpt = t // CV_HALO
    last_halo = m // CV_HALO - 1

    def body(cv_ref, cg_ref, pv_ref, pg_ref, nv_ref, ng_ref, z_ref, nz_ref, da_ref, nda_ref,
             w_ref, b_ref, g_ref, bb_ref, dh_ref, dw_ref, db_ref, dg_ref, dbb_ref, pad_ref, dc_ref, stage_ref):
        i = pl.program_id(0)

        @pl.when(i == 0)
        def _():
            dw_ref[...] = jnp.zeros_like(dw_ref)
            db_ref[...] = jnp.zeros_like(db_ref)
            dg_ref[...] = jnp.zeros_like(dg_ref)
            dbb_ref[...] = jnp.zeros_like(dbb_ref)

        first = i % tiles == 0
        last = i % tiles == tiles - 1
        cv = cv_ref[...]
        sg = _sigmoid(cg_ref[...])
        pad_ref[0:CV_HALO, :] = jnp.where(first, 0.0, pv_ref[...] * _sigmoid(pg_ref[...]))
        pad_ref[CV_HALO:CV_HALO + t, :] = cv * sg
        pad_ref[CV_HALO + t:, :] = nv_ref[...] * _sigmoid(ng_ref[...])

        def rows_bwd(conv_out, z, dav):
            xhat, rstd = _ln_stats(conv_out)
            s, ds = _silu_and_grad(xhat * g_ref[...] + bb_ref[...])
            sz, dsz = _silu_and_grad(z)
            dn = dav * sz * ds
            return _ln_bwd(dn * g_ref[...], xhat, rstd), dav * s * dsz, dn, xhat

        conv_all = _conv_rows(pad_ref, stage_ref, w_ref, b_ref, t + CV_HALO)
        dc, dz, dn, xhat = rows_bwd(conv_all[:t], z_ref[...], da_ref[...])
        dc_next, _, _, _ = rows_bwd(conv_all[t:], nz_ref[...], nda_ref[...])
        dc_ref[0:t, :] = dc
        dc_ref[t:, :] = jnp.where(last, 0.0, dc_next)
        dg_ref[...] += jnp.sum(dn * xhat, axis=0, keepdims=True)
        dbb_ref[...] += jnp.sum(dn, axis=0, keepdims=True)
        db_ref[...] += jnp.sum(dc, axis=0, keepdims=True)
        for res, reach, mine in _by_residue(_CONV_TAPS):
            shifted = _staged(pad_ref, stage_ref, res, t + reach)
            for a, k in mine:
                dw_ref[k:k + 1, :] += jnp.sum(dc * shifted[a:a + t, :], axis=0, keepdims=True)
        du = _tap_sum(dc_ref, stage_ref, w_ref, t, _CONV_TAPS_T, jnp.zeros((t, d), F32))
        dh_ref[:, 0:d] = (du * sg).astype(BF16)
        dh_ref[:, d:2 * d] = (du * cv * sg * (1.0 - sg)).astype(BF16)
        dh_ref[:, 2 * d:] = dz.astype(BF16)

    def main(c):
        return pl.BlockSpec((t, d), lambda i: (i, c))

    def prev(c):
        return pl.BlockSpec((CV_HALO, d), lambda i: (jnp.maximum(i * hpt - 1, 0), c))

    def nxt(c):
        return pl.BlockSpec((CV_HALO, d), lambda i: (jnp.minimum((i + 1) * hpt, last_halo), c))

    vec = pl.BlockSpec((1, d), lambda i: (0, 0))
    taps = pl.BlockSpec((CV_HALO, d), lambda i: (0, 0))
    return pl.pallas_call(
        body, name="conv_bwd", grid=(m // t,),
        in_specs=[main(4), main(5), prev(4), prev(5), nxt(4), nxt(5), main(6), nxt(6), main(0), nxt(0),
                  taps, vec, vec, vec],
        out_specs=[pl.BlockSpec((t, 3 * d), lambda i: (i, 0)), taps, vec, vec, vec],
        out_shape=[jax.ShapeDtypeStruct((m, 3 * d), BF16), jax.ShapeDtypeStruct((CV_HALO, d), F32),
                   jax.ShapeDtypeStruct((1, d), F32), jax.ShapeDtypeStruct((1, d), F32),
                   jax.ShapeDtypeStruct((1, d), F32)],
        scratch_shapes=[pltpu.VMEM((t + 2 * CV_HALO, d), F32), pltpu.VMEM((t + CV_HALO, d), F32),
                        pltpu.VMEM((2, t + 2 * CV_HALO, d), F32)],
        compiler_params=_params("arbitrary"),
    )(h, h, h, h, h, h, h, h, da, da, conv_w, conv_b, ln_g, ln_b)


_HBM = pl.BlockSpec(memory_space=pltpu.HBM)


def _place():
    return lax.axis_index("x"), lax.axis_index("y"), lax.axis_index("c")


def _slot(p):
    return 4 * p[0] + 2 * p[1] + p[2]


def _gather_mm_in_fwd(h0b, w_shard, riders):
    m, d = h0b.shape
    ns = w_shard.shape[1]
    n_i = m // IN_TILE
    n_steps = N_DEV
    n_arr = 1 + len(riders)

    def places():
        x, y, c = _place()
        chips = [(1 - x, y), (x, 1 - y), (1 - x, 1 - y)]
        return (x, y, c), (x, y, 1 - c), chips

    me, sibling, chips = places()
    order = [me, sibling] + [(*chip, me[2]) for chip in chips] + [(*chip, sibling[2]) for chip in chips]
    order = jnp.stack([_slot(p) for p in order]).astype(jnp.int32)

    def body(order_ref, h_ref, *rest):
        src_refs, rest = rest[:n_arr], rest[n_arr:]
        o32_ref, o16_ref = rest[:2]
        got_refs, rest = rest[2:2 + n_arr], rest[2 + n_arr:]
        wbuf, send_sems, recv_sems, own_sems, load_sems = rest
        s, i = pl.program_id(0), pl.program_id(1)
        me, sibling, chips = places()
        c = me[2]
        wg_ref = got_refs[0]

        def remote(t, k, block, to, own_src=False):
            dst = got_refs[t].at[_slot(block)]
            return pltpu.make_async_remote_copy(
                src_ref=src_refs[t] if own_src else dst, dst_ref=dst, send_sem=send_sems.at[7 * t + k],
                recv_sem=recv_sems.at[7 * t + k], device_id=to, device_id_type=MESH)

        def load(step):
            return pltpu.make_async_copy(wg_ref.at[order_ref[step]], wbuf.at[step % 2], load_sems.at[step % 2])

        own = [pltpu.make_async_copy(src_refs[t], got_refs[t].at[_slot(me)], own_sems.at[t]) for t in range(n_arr)]

        @pl.when((s == 0) & (i == 0))
        def _():
            for t in range(n_arr):
                own[t].start()
                remote(t, 0, me, sibling, own_src=True).start()
                for j, chip in enumerate(chips):
                    remote(t, 1 + j, me, (*chip, c), own_src=True).start()
            first = pltpu.make_async_copy(src_refs[0], wbuf.at[0], load_sems.at[0])
            first.start()
            first.wait()

        for nxt in range(1, n_steps):
            @pl.when((s == nxt - 1) & (i == n_i - 1))
            def _():
                if nxt == 1:
                    remote(0, 0, sibling, me).wait_recv()
                elif nxt <= 4:
                    j = nxt - 2
                    remote(0, 1 + j, (*chips[j], c), me).wait_recv()
                    remote(0, 4 + j, (*chips[j], c), sibling).start()
                else:
                    j = nxt - 5
                    remote(0, 4 + j, (*chips[j], 1 - c), me).wait_recv()
                if 3 <= nxt <= 5:
                    j = nxt - 3
                    for t in range(1, n_arr):
                        remote(t, 1 + j, (*chips[j], c), me).wait_recv()
                        remote(t, 4 + j, (*chips[j], c), sibling).start()
                load(nxt).start()

        for step in range(1, n_steps):
            @pl.when((s == step) & (i == 0))
            def _():
                load(step).wait()

        part = lax.dot_general(h_ref[...], wbuf[s % 2], _NN, preferred_element_type=F32)
        o32_ref[...] = part
        o16_ref[...] = part.astype(BF16)

        @pl.when((s == n_steps - 1) & (i == n_i - 1))
        def _():
            for t in range(n_arr):
                if t:
                    remote(t, 0, sibling, me).wait_recv()
                    for j, chip in enumerate(chips):
                        remote(t, 4 + j, (*chip, 1 - c), me).wait_recv()
                remote(t, 0, me, sibling, own_src=True).wait_send()
                for j, chip in enumerate(chips):
                    remote(t, 1 + j, me, (*chip, c), own_src=True).wait_send()
                    remote(t, 4 + j, (*chip, c), sibling).wait_send()
                own[t].wait()

    arrays = [w_shard] + list(riders)
    out = pl.BlockSpec((IN_TILE, ns), lambda s, i, order: (i, order[s]))
    grid_spec = pltpu.PrefetchScalarGridSpec(
        num_scalar_prefetch=1, grid=(n_steps, n_i),
        in_specs=[pl.BlockSpec((IN_TILE, d), lambda s, i, order: (i, 0))] + [_HBM] * n_arr,
        out_specs=[out, out] + [_HBM] * n_arr,
        scratch_shapes=[pltpu.VMEM((2, d, ns), BF16), pltpu.SemaphoreType.DMA((7 * n_arr,)),
                        pltpu.SemaphoreType.DMA((7 * n_arr,)), pltpu.SemaphoreType.DMA((n_arr,)),
                        pltpu.SemaphoreType.DMA((2,))])
    return pl.pallas_call(
        body, name="gather_mm_in_fwd", grid_spec=grid_spec,
        out_shape=[jax.ShapeDtypeStruct((m, N_DEV * ns), F32), jax.ShapeDtypeStruct((m, N_DEV * ns), BF16)]
        + [jax.ShapeDtypeStruct((N_DEV,) + a.shape, a.dtype) for a in arrays],
        compiler_params=_params("arbitrary", "arbitrary"),
    )(order, h0b, *arrays)


def _exchange_partials(parts, whole):
    n_parts = len(parts)
    arrays = list(parts) + list(whole)
    n = len(arrays)

    def body(*refs):
        ins, outs = refs[:n], refs[n:2 * n]
        send_sems, recv_sems, local_sems = refs[2 * n:]
        x, y, c = _place()
        me = (x, y, c)

        def src_for(a, p):
            return ins[a].at[_slot(p)] if a < n_parts else ins[a]

        def copy(a, k, peer):
            return pltpu.make_async_remote_copy(
                src_ref=src_for(a, peer), dst_ref=outs[a].at[_slot(me)], send_sem=send_sems.at[7 * a + k],
                recv_sem=recv_sems.at[7 * a + k], device_id=peer, device_id_type=MESH)

        def landed(a, k, peer):
            return pltpu.make_async_remote_copy(
                src_ref=src_for(a, peer), dst_ref=outs[a].at[_slot(peer)], send_sem=send_sems.at[7 * a + k],
                recv_sem=recv_sems.at[7 * a + k], device_id=peer, device_id_type=MESH)

        peers = []
        for k in range(1, N_DEV):
            fx, fy, fc = (k >> 2) & 1, (k >> 1) & 1, k & 1
            peers.append((1 - x if fx else x, 1 - y if fy else y, 1 - c if fc else c))
        mine = [pltpu.make_async_copy(src_for(a, me), outs[a].at[_slot(me)], local_sems.at[a]) for a in range(n)]
        for cp in mine:
            cp.start()
        sent = [copy(a, k, peer) for a in range(n) for k, peer in enumerate(peers)]
        for cp in sent:
            cp.start()
        for a in range(n):
            for k, peer in enumerate(peers):
                landed(a, k, peer).wait_recv()
        for cp in sent:
            cp.wait_send()
        for cp in mine:
            cp.wait()

    out_shape = [jax.ShapeDtypeStruct(p.shape, p.dtype) for p in parts]
    out_shape += [jax.ShapeDtypeStruct((N_DEV,) + w.shape, w.dtype) for w in whole]
    return pl.pallas_call(
        body, name="exchange_grad_partials",
        in_specs=[_HBM] * n, out_specs=[_HBM] * n, out_shape=out_shape,
        scratch_shapes=[pltpu.SemaphoreType.DMA((7 * n,)), pltpu.SemaphoreType.DMA((7 * n,)),
                        pltpu.SemaphoreType.DMA((n,))],
    )(*arrays)


def _peers_of(x, y, c):
    peers = []
    for k in range(1, N_DEV):
        fx, fy, fc = (k >> 2) & 1, (k >> 1) & 1, k & 1
        peers.append((1 - x if fx else x, 1 - y if fy else y, 1 - c if fc else c))
    return peers


def _with_own(landed, own, me):
    return lax.dynamic_update_index_in_dim(landed, own, me, 0)


def _adamw(name, slabs, w, mom, var, rows):
    r, c = w.shape
    n_slabs = len(slabs)
    per_slab = r // n_slabs // rows

    def body(*refs):
        p_refs = refs[:n_slabs]
        w_ref, m_ref, v_ref, g_ref, d_ref, nm_ref, nv_ref = refs[n_slabs:]
        slab = pl.program_id(0) // per_slab
        g = None
        for p in range(N_DEV):
            part = p_refs[0][p]
            for k in range(1, n_slabs):
                part = jnp.where(slab == k, p_refs[k][p], part)
            g = part.astype(F32) if g is None else g + part.astype(F32)
        m_new = ADAM_B1 * m_ref[...] + (1.0 - ADAM_B1) * g
        v_new = ADAM_B2 * v_ref[...] + (1.0 - ADAM_B2) * (g * g)
        m_hat = m_new / (1.0 - ADAM_B1 ** ADAM_STEP)
        v_hat = v_new / (1.0 - ADAM_B2 ** ADAM_STEP)
        g_ref[...] = g
        d_ref[...] = -ADAM_LR * (m_hat / (jnp.sqrt(v_hat) + ADAM_EPS) + ADAM_WD * w_ref[...])
        nm_ref[...] = m_new
        nv_ref[...] = v_new

    def slab_spec(k):
        return pl.BlockSpec((N_DEV, rows, c), lambda i: (0, jnp.clip(i - k * per_slab, 0, per_slab - 1), 0))

    blk = pl.BlockSpec((rows, c), lambda i: (i, 0))
    return pl.pallas_call(
        body, name=name, grid=(r // rows,),
        in_specs=[slab_spec(k) for k in range(n_slabs)] + [blk, blk, blk],
        out_specs=[blk] * 4, out_shape=[jax.ShapeDtypeStruct((r, c), F32)] * 4,
        compiler_params=_params("parallel"),
    )(*slabs, w, mom, var)


def kernel(x, ln_in_g, ln_in_b, w_in, w_sb_proj, conv_w, conv_b, conv_ln_g, conv_ln_b, w_cv_proj, w_out, ln_post_g, ln_post_b, loss_target, m_ln_in_g, m_ln_in_b, m_w_in, m_w_sb_proj, m_conv_w, m_conv_b, m_conv_ln_g, m_conv_ln_b, m_w_cv_proj, m_w_out, m_ln_post_g, m_ln_post_b, v_ln_in_g, v_ln_in_b, v_w_in, v_w_sb_proj, v_conv_w, v_conv_b, v_conv_ln_g, v_conv_ln_b, v_w_cv_proj, v_w_out, v_ln_post_g, v_ln_post_b):
    nb, seq, d = x.shape
    m = nb * seq
    x2d = x.reshape(m, d)
    target = loss_target.reshape(m, d)
    rs = d // N_DEV
    pad_taps = ((0, CV_HALO - CV_KERNEL), (0, 0))

    proj_shards = jnp.stack([w_sb_proj[0], w_cv_proj[0], w_out[0]]).astype(BF16)
    conv_w_shard = jnp.pad(conv_w[0], pad_taps)
    me = _slot(_place())
    own = lambda parts: lax.dynamic_index_in_dim(parts, me, 0, keepdims=False)

    h0, h0b = _ln_in_fwd(x2d, ln_in_g.reshape(1, d), ln_in_b.reshape(1, d))
    h, hb, w_in_g = _gather_mm_in_fwd(h0b, w_in[0].astype(BF16), [])
    o, a_sb, carries, proj_g, conv_w_g = _attn_fwd(hb, h, nb, seq, [proj_shards, conv_w_shard])
    proj_g = _with_own(proj_g, proj_shards, me)
    conv_w_g = _with_own(conv_w_g, conv_w_shard, me)
    w_sb_g = proj_g[:, 0].reshape(d, d)
    w_cv_g = proj_g[:, 1].reshape(d, d)
    w_out_g = proj_g[:, 2].reshape(d, d)
    conv_w_full = conv_w_g.transpose(1, 0, 2).reshape(CV_HALO, d)
    a_cv = _conv_fwd(h, conv_w_full, conv_b, conv_ln_g, conv_ln_b, nb, seq)
    y_sb = _mm_nn("mm_sb_fwd", a_sb, w_sb_g)
    y_cv = _mm_nn("mm_cv_fwd", a_cv, w_cv_g)
    merged = _merge_fwd(h, y_sb, y_cv)
    mo = _mm_nn("mm_out_fwd", merged, w_out_g)
    loss_part, d_pre, d_pre_b, dg_post, db_post = _post_ln_loss(h0, mo, target, ln_post_g, ln_post_b)

    dw_out = _mm_tn("mm_out_bwd_w", merged, d_pre_b, BF16)
    d_merged = _mm_nt("mm_out_bwd_x", d_pre_b, w_out_g)
    dy_sb, dy_cv, d_gates = _merge_bwd(h, y_sb, y_cv, d_merged)
    dw_sb = _mm_tn("mm_sb_bwd_w", a_sb, dy_sb, BF16)
    dw_cv = _mm_tn("mm_cv_bwd_w", a_cv, dy_cv, BF16)
    da_sb = _mm_nt("mm_sb_bwd_x", dy_sb, w_sb_g)
    da_cv = _mm_nt("mm_cv_bwd_x", dy_cv, w_cv_g)
    dproj = jnp.stack([dw_sb.reshape(N_DEV, rs, d), dw_cv.reshape(N_DEV, rs, d), dw_out.reshape(N_DEV, rs, d)], axis=1)
    dq, dk, dv, dz_sb, r_proj = _attn_bwd(hb, h, o, carries, da_sb, nb, seq, [dproj])
    d_conv3, dconv_w, dconv_b, dconv_ln_g, dconv_ln_b = _conv_bwd(
        h, da_cv, conv_w_full, conv_b, conv_ln_g, conv_ln_b, nb, seq)
    dhb = jnp.concatenate([dq, dk, dv, dz_sb, d_conv3, d_gates], axis=1)
    dconv_w_parts = dconv_w.reshape(CV_HALO, N_DEV, d // N_DEV).transpose(1, 0, 2)
    dw_in_a, r_conv = _mm_in_bwd_w(h0b, dhb, 0, [dconv_w_parts])
    dw_in_b, r_in_a = _mm_in_bwd_w(h0b, dhb, 1, [dw_in_a])
    dh_mm, r_in_b = _mm_in_bwd_x(dhb, w_in_g, [dw_in_b])
    dx, dg_in, db_in = _ln_in_bwd(x2d, d_pre, dh_mm, ln_in_g.reshape(1, d))
    r_proj = _with_own(r_proj, own(dproj), me)
    r_conv = _with_own(r_conv, own(dconv_w_parts), me)
    r_in = [_with_own(r_in_a, own(dw_in_a), me), _with_own(r_in_b, own(dw_in_b), me)]

    small = jnp.concatenate([dg_in, db_in, dconv_b, dconv_ln_g, dconv_ln_b, dg_post, db_post,
                             jnp.zeros((1, d), F32)], axis=0)
    (r_small,) = _exchange_partials([], [small])

    g_in, d_in, nm_in, nv_in = _adamw("adamw_w_in", r_in, w_in[0], m_w_in[0], v_w_in[0], 128)
    stack3 = lambda a, b, c: jnp.concatenate([a[0], b[0], c[0]], axis=0)
    g_pr, d_pr, nm_pr, nv_pr = _adamw(
        "adamw_proj", [r_proj.reshape(N_DEV, 3 * rs, d)], stack3(w_sb_proj, w_cv_proj, w_out),
        stack3(m_w_sb_proj, m_w_cv_proj, m_w_out), stack3(v_w_sb_proj, v_w_cv_proj, v_w_out), 3 * rs)
    padc = lambda a: jnp.pad(a[0], pad_taps)
    g_cw, d_cw, nm_cw, nv_cw = _adamw("adamw_conv_w", [r_conv], padc(conv_w), padc(m_conv_w), padc(v_conv_w), CV_HALO)
    vecs = lambda *a: jnp.concatenate([t.reshape(1, d) for t in a] + [jnp.ones((1, d), F32)], axis=0)
    g_sm, d_sm, nm_sm, nv_sm = _adamw(
        "adamw_vectors", [r_small],
        vecs(ln_in_g, ln_in_b, conv_b, conv_ln_g, conv_ln_b, ln_post_g, ln_post_b),
        vecs(m_ln_in_g, m_ln_in_b, m_conv_b, m_conv_ln_g, m_conv_ln_b, m_ln_post_g, m_ln_post_b),
        vecs(v_ln_in_g, v_ln_in_b, v_conv_b, v_conv_ln_g, v_conv_ln_b, v_ln_post_g, v_ln_post_b), 8)

    loss = lax.psum(loss_part[0, 0], ("x", "y", "c"))

    def leaves(big, pr, cw, sm):
        return (sm[0], sm[1], big[None], pr[None, 0:rs], cw[None, :CV_KERNEL], sm[2:3], sm[3:4], sm[4:5],
                pr[None, rs:2 * rs], pr[None, 2 * rs:], sm[5:6], sm[6:7])

    return (loss, dx.reshape(nb, seq, d), *leaves(g_in, g_pr, g_cw, g_sm), *leaves(d_in, d_pr, d_cw, d_sm),
            *leaves(nm_in, nm_pr, nm_cw, nm_sm), *leaves(nv_in, nv_pr, nv_cw, nv_sm))
```

```python
import functools
import math

import jax
import jax.numpy as jnp
from jax import lax
from jax.experimental import pallas as pl
from jax.experimental.pallas import tpu as pltpu

F32 = jnp.float32
BF16 = jnp.bfloat16
MESH = pl.DeviceIdType.MESH

N_DEV = 8
SB_HEADS = 16
HEAD_DIM = 64
CV_KERNEL = 31
CV_HALO = 32
LN_EPS = 1e-5
DEEPNORM_ALPHA = 2.0 ** 0.25
ADAM_LR, ADAM_B1, ADAM_B2, ADAM_EPS, ADAM_WD, ADAM_STEP = 0.001, 0.9, 0.999, 1e-08, 0.01, 10

ATT_TK = 256
ATT_TQ = 1024
ATT_NEAR = 512
ROW_TILE = 512
IN_TILE = 1024
CONV_TILE = 256
VMEM_LIMIT = 56 * 1024 * 1024


def _params(*sem):
    return pltpu.CompilerParams(dimension_semantics=sem, vmem_limit_bytes=VMEM_LIMIT)


def _sigmoid(x):
    return 1.0 / (1.0 + jnp.exp(-x))


def _silu_and_grad(x):
    s = _sigmoid(x)
    return x * s, s * (1.0 + x * (1.0 - s))


def _ln_stats(x):
    mu = jnp.mean(x, axis=-1, keepdims=True)
    xc = x - mu
    var = jnp.mean(xc * xc, axis=-1, keepdims=True)
    rstd = lax.rsqrt(var + LN_EPS)
    return xc * rstd, rstd


def _ln_bwd(dxhat, xhat, rstd):
    m1 = jnp.mean(dxhat, axis=-1, keepdims=True)
    m2 = jnp.mean(dxhat * xhat, axis=-1, keepdims=True)
    return rstd * (dxhat - m1 - xhat * m2)


def _ln_in_fwd(x2d, g, b):
    m, d = x2d.shape

    def body(x_ref, g_ref, b_ref, h_ref, hb_ref):
        xhat, _ = _ln_stats(x_ref[...])
        y = xhat * g_ref[...] + b_ref[...]
        h_ref[...] = y
        hb_ref[...] = y.astype(BF16)

    row = pl.BlockSpec((ROW_TILE, d), lambda i: (i, 0))
    vec = pl.BlockSpec((1, d), lambda i: (0, 0))
    return pl.pallas_call(
        body, name="ln_in_fwd", grid=(m // ROW_TILE,),
        in_specs=[row, vec, vec], out_specs=[row, row],
        out_shape=[jax.ShapeDtypeStruct((m, d), F32), jax.ShapeDtypeStruct((m, d), BF16)],
        compiler_params=_params("parallel"),
    )(x2d, g, b)


def _ln_in_bwd(x2d, d_pre, dh_mm, g):
    m, d = x2d.shape

    def body(x_ref, dp_ref, dm_ref, g_ref, dx_ref, dg_ref, db_ref):
        @pl.when(pl.program_id(0) == 0)
        def _():
            dg_ref[...] = jnp.zeros_like(dg_ref)
            db_ref[...] = jnp.zeros_like(db_ref)

        xhat, rstd = _ln_stats(x_ref[...])
        dh = DEEPNORM_ALPHA * dp_ref[...] + dm_ref[...]
        dg_ref[...] += jnp.sum(dh * xhat, axis=0, keepdims=True)
        db_ref[...] += jnp.sum(dh, axis=0, keepdims=True)
        dx_ref[...] = _ln_bwd(dh * g_ref[...], xhat, rstd)

    row = pl.BlockSpec((ROW_TILE, d), lambda i: (i, 0))
    vec = pl.BlockSpec((1, d), lambda i: (0, 0))
    return pl.pallas_call(
        body, name="ln_in_bwd", grid=(m // ROW_TILE,),
        in_specs=[row, row, row, vec], out_specs=[row, vec, vec],
        out_shape=[jax.ShapeDtypeStruct((m, d), F32), jax.ShapeDtypeStruct((1, d), F32),
                   jax.ShapeDtypeStruct((1, d), F32)],
        compiler_params=_params("arbitrary"),
    )(x2d, d_pre, dh_mm, g)


def _merge_fwd(h, y_sb, y_cv):
    m, d = y_sb.shape

    def body(gs_ref, gc_ref, ys_ref, yc_ref, out_ref):
        out_ref[...] = (_sigmoid(gs_ref[...]) * ys_ref[...] + _sigmoid(gc_ref[...]) * yc_ref[...]).astype(BF16)

    row = pl.BlockSpec((ROW_TILE, d), lambda i: (i, 0))
    return pl.pallas_call(
        body, name="merge_fwd", grid=(m // ROW_TILE,),
        in_specs=[pl.BlockSpec((ROW_TILE, d), lambda i: (i, 7)), pl.BlockSpec((ROW_TILE, d), lambda i: (i, 8)), row, row],
        out_specs=row, out_shape=jax.ShapeDtypeStruct((m, d), BF16),
        compiler_params=_params("parallel"),
    )(h, h, y_sb, y_cv)


def _merge_bwd(h, y_sb, y_cv, dm):
    m, d = y_sb.shape

    def body(gs_ref, gc_ref, ys_ref, yc_ref, dm_ref, dys_ref, dyc_ref, dgate_ref):
        dmv = dm_ref[...]
        ss = _sigmoid(gs_ref[...])
        sc = _sigmoid(gc_ref[...])
        dys_ref[...] = (ss * dmv).astype(BF16)
        dyc_ref[...] = (sc * dmv).astype(BF16)
        dgate_ref[:, :d] = (dmv * ys_ref[...] * ss * (1.0 - ss)).astype(BF16)
        dgate_ref[:, d:] = (dmv * yc_ref[...] * sc * (1.0 - sc)).astype(BF16)

    row = pl.BlockSpec((ROW_TILE, d), lambda i: (i, 0))
    return pl.pallas_call(
        body, name="merge_bwd", grid=(m // ROW_TILE,),
        in_specs=[pl.BlockSpec((ROW_TILE, d), lambda i: (i, 7)), pl.BlockSpec((ROW_TILE, d), lambda i: (i, 8)), row, row, row],
        out_specs=[row, row, pl.BlockSpec((ROW_TILE, 2 * d), lambda i: (i, 0))],
        out_shape=[jax.ShapeDtypeStruct((m, d), BF16), jax.ShapeDtypeStruct((m, d), BF16),
                   jax.ShapeDtypeStruct((m, 2 * d), BF16)],
        compiler_params=_params("parallel"),
    )(h, h, y_sb, y_cv, dm)


def _post_ln_loss(h0, mo, target, g, b):
    m, d = h0.shape

    def body(h_ref, mo_ref, t_ref, g_ref, b_ref, loss_ref, dp_ref, dpb_ref, dg_ref, db_ref):
        @pl.when(pl.program_id(0) == 0)
        def _():
            loss_ref[...] = jnp.zeros_like(loss_ref)
            dg_ref[...] = jnp.zeros_like(dg_ref)
            db_ref[...] = jnp.zeros_like(db_ref)

        xhat, rstd = _ln_stats(DEEPNORM_ALPHA * h_ref[...] + mo_ref[...])
        err = xhat * g_ref[...] + b_ref[...] - t_ref[...]
        per_row = jnp.mean(err * err, axis=-1, keepdims=True)
        loss_ref[...] += 0.5 * jnp.sum(per_row, axis=0, keepdims=True)
        dy = err * (1.0 / d)
        dg_ref[...] += jnp.sum(dy * xhat, axis=0, keepdims=True)
        db_ref[...] += jnp.sum(dy, axis=0, keepdims=True)
        dp = _ln_bwd(dy * g_ref[...], xhat, rstd)
        dp_ref[...] = dp
        dpb_ref[...] = dp.astype(BF16)

    row = pl.BlockSpec((ROW_TILE, d), lambda i: (i, 0))
    vec = pl.BlockSpec((1, d), lambda i: (0, 0))
    one = pl.BlockSpec((1, 1), lambda i: (0, 0))
    return pl.pallas_call(
        body, name="post_ln_loss", grid=(m // ROW_TILE,),
        in_specs=[row, row, row, vec, vec], out_specs=[one, row, row, vec, vec],
        out_shape=[jax.ShapeDtypeStruct((1, 1), F32), jax.ShapeDtypeStruct((m, d), F32),
                   jax.ShapeDtypeStruct((m, d), BF16), jax.ShapeDtypeStruct((1, d), F32),
                   jax.ShapeDtypeStruct((1, d), F32)],
        compiler_params=_params("arbitrary"),
    )(h0, mo, target, g, b)


_NN = (((1,), (0,)), ((), ()))
_NT = (((1,), (1,)), ((), ()))
_TN = (((0,), (0,)), ((), ()))


def _grid_ends(grid):
    ids = [pl.program_id(ax) for ax in range(len(grid))]
    first = functools.reduce(jnp.logical_and, [i == 0 for i in ids])
    last = functools.reduce(jnp.logical_and, [i == n - 1 for i, n in zip(ids, grid)])
    return first, last


def _travel(grid, src_refs, land_refs, send_sems, recv_sems, whole):
    first, last = _grid_ends(grid)
    x, y, c = _place()
    n = len(src_refs)

    def copy(t, k, peer, lands_at):
        return pltpu.make_async_remote_copy(
            src_ref=src_refs[t] if whole else src_refs[t].at[_slot(peer)],
            dst_ref=land_refs[t].at[_slot(lands_at)], send_sem=send_sems.at[7 * t + k],
            recv_sem=recv_sems.at[7 * t + k], device_id=peer, device_id_type=MESH)

    @pl.when(first)
    def _():
        for k, peer in enumerate(_peers_of(x, y, c)):
            for t in range(n):
                copy(t, k, peer, (x, y, c)).start()

    def finish():
        @pl.when(last)
        def _():
            for k, peer in enumerate(_peers_of(x, y, c)):
                for t in range(n):
                    arrived = copy(t, k, peer, peer)
                    arrived.wait_send()
                    arrived.wait_recv()

    return finish


def _travel_scratch(n):
    return [pltpu.SemaphoreType.DMA((7 * n,)), pltpu.SemaphoreType.DMA((7 * n,))]


def _mm(name, a, b, dims, grid, a_spec, b_spec, out_specs, out_shape, acc_shape, k_axis, send=()):
    n_k = 1 if k_axis is None else grid[k_axis]
    n_out = len(out_shape)
    n_send = len(send)

    def body(a_ref, b_ref, *rest):
        src_refs, rest = rest[:n_send], rest[n_send:]
        outs, land_refs, acc_ref = rest[:n_out], rest[n_out:n_out + n_send], rest[n_out + n_send]
        if n_send:
            finish = _travel(grid, src_refs, land_refs, *rest[n_out + n_send + 1:], whole=False)

        part = lax.dot_general(a_ref[...].astype(BF16), b_ref[...].astype(BF16), dims, preferred_element_type=F32)
        if n_k == 1:
            for o in outs:
                o[...] = part.astype(o.dtype)
        else:
            k = pl.program_id(k_axis)

            @pl.when(k == 0)
            def _():
                acc_ref[...] = part

            @pl.when(k > 0)
            def _():
                acc_ref[...] += part

            @pl.when(k == n_k - 1)
            def _():
                for o in outs:
                    o[...] = acc_ref[...].astype(o.dtype)

        if n_send:
            finish()

    sem = tuple("arbitrary" for _ in grid) if n_send else tuple(
        "arbitrary" if ax == k_axis else "parallel" for ax in range(len(grid)))
    scratch = [pltpu.VMEM(acc_shape, F32)] + (_travel_scratch(n_send) if n_send else [])
    return pl.pallas_call(
        body, name=name, grid=grid, in_specs=[a_spec, b_spec] + [_HBM] * n_send,
        out_specs=list(out_specs) + [_HBM] * n_send,
        out_shape=list(out_shape) + [jax.ShapeDtypeStruct(s.shape, s.dtype) for s in send],
        scratch_shapes=scratch, compiler_params=_params(*sem),
    )(a, b, *send)


def _mm_nn(name, a, b, out_dtype=F32):
    m, k = a.shape
    n = b.shape[1]
    return _mm(name, a, b, _NN, (m // ROW_TILE,), pl.BlockSpec((ROW_TILE, k), lambda i: (i, 0)),
               pl.BlockSpec((k, n), lambda i: (0, 0)), [pl.BlockSpec((ROW_TILE, n), lambda i: (i, 0))],
               [jax.ShapeDtypeStruct((m, n), out_dtype)], (8, 128), None)[0]


def _mm_nt(name, a, b, out_dtype=F32):
    m, k = a.shape
    n = b.shape[0]
    return _mm(name, a, b, _NT, (m // ROW_TILE,), pl.BlockSpec((ROW_TILE, k), lambda i: (i, 0)),
               pl.BlockSpec((n, k), lambda i: (0, 0)), [pl.BlockSpec((ROW_TILE, n), lambda i: (i, 0))],
               [jax.ShapeDtypeStruct((m, n), out_dtype)], (8, 128), None)[0]


def _mm_tn(name, a, b, out_dtype):
    m, k = a.shape
    n = b.shape[1]
    return _mm(name, a, b, _TN, (m // ROW_TILE,), pl.BlockSpec((ROW_TILE, k), lambda i: (i, 0)),
               pl.BlockSpec((ROW_TILE, n), lambda i: (i, 0)), [pl.BlockSpec((k, n), lambda i: (0, 0))],
               [jax.ShapeDtypeStruct((k, n), out_dtype)], (k, n), 0)[0]


def _mm_in_bwd_x(dhb, w_g, send):
    m = dhb.shape[0]
    _, d, ns = w_g.shape
    return _mm("mm_in_bwd_x", dhb, w_g, _NT, (m // IN_TILE, N_DEV), pl.BlockSpec((IN_TILE, ns), lambda i, j: (i, j)),
               pl.BlockSpec((None, d, ns), lambda i, j: (j, 0, 0)), [pl.BlockSpec((IN_TILE, d), lambda i, j: (i, 0))],
               [jax.ShapeDtypeStruct((m, d), F32)], (IN_TILE, d), 1, send=send)


def _mm_in_bwd_w(h0b, dhb, half, send):
    m, d = h0b.shape
    ns = dhb.shape[1] // N_DEV
    dh = d // 2
    return _mm(f"mm_in_bwd_w{half}", h0b, dhb, _TN, (N_DEV, m // IN_TILE),
               pl.BlockSpec((IN_TILE, dh), lambda j, k: (k, half)),
               pl.BlockSpec((IN_TILE, ns), lambda j, k: (k, j)), [pl.BlockSpec((None, dh, ns), lambda j, k: (j, 0, 0))],
               [jax.ShapeDtypeStruct((N_DEV, dh, ns), BF16)], (dh, ns), 1, send=send)


LOG2_E = 1.4426950408889634
DEAD_AFTER = 110.0
NEVER_REACHED = 1e30


def _twice(u):
    return jnp.concatenate([u, u], axis=0)


def _dot2(x, u2):
    hi = x.astype(BF16)
    lo = (x - hi.astype(F32)).astype(BF16)
    return lax.dot_general(jnp.concatenate([hi, lo], axis=1), u2, _NN, preferred_element_type=F32)


def _softplus(l):
    e = jnp.exp2(jnp.abs(l) * (-LOG2_E))
    e1 = 1.0 + e
    return jnp.maximum(l, 0.0) + jnp.log(e1), e, e1


def _attn_fwd(hb, h, nb, seq, gather):
    m = nb * seq
    d = SB_HEADS * HEAD_DIM
    tq, tk = ATT_TQ, ATT_TK
    n_sub = tq // tk
    nq = seq // tq
    sec = d // 128
    scale = 1.0 / math.sqrt(HEAD_DIM)
    heads = [slice(hh * HEAD_DIM, (hh + 1) * HEAD_DIM) for hh in range(2)]
    grid = (nb, sec, nq)
    n_g = len(gather)

    def body(q_ref, k_ref, v_ref, z_ref, *rest):
        src_refs, rest = rest[:n_g], rest[n_g:]
        o_ref, a_ref, c_ref = rest[:3]
        land_refs, (carry_ref, acc_ref, send_sems, recv_sems) = rest[3:3 + n_g], rest[3 + n_g:]
        finish = _travel(grid, src_refs, land_refs, send_sems, recv_sems, whole=True)
        qi = pl.program_id(2)
        n_full = n_sub * qi
        u_after = _twice((lax.broadcasted_iota(jnp.int32, (tk, tk), 0)
                          > lax.broadcasted_iota(jnp.int32, (tk, tk), 1)).astype(BF16))
        qs = [(q_ref[:, cols].astype(F32) * scale).astype(BF16) for cols in heads]
        carry_ref[...] = jnp.zeros_like(carry_ref)
        acc_ref[...] = jnp.zeros_like(acc_ref)
        lane_t = lax.broadcasted_iota(jnp.int32, (tq, 128), 1)
        c_ref[...] = jnp.where(lane_t % HEAD_DIM < n_full + n_sub, NEVER_REACHED, 0.0)

        def block(kb, r0, n, masked):
            qrows = slice(r0, r0 + n)
            krows = pl.ds(pl.multiple_of(kb * tk, tk), tk)
            lane = lax.broadcasted_iota(jnp.int32, (n, 128), 1)
            if masked:
                causal = lax.broadcasted_iota(jnp.int32, (n, tk), 1) < lax.broadcasted_iota(jnp.int32, (n, tk), 0)
            for hh, cols in enumerate(heads):
                carry = carry_ref[hh, qrows, :]
                l = lax.dot_general(qs[hh][qrows], k_ref[krows, cols], _NT, preferred_element_type=F32)
                sp, _, _ = _softplus(l)
                if masked:
                    sp = jnp.where(causal, sp, 0.0)
                after = carry + _dot2(sp, u_after)
                a = jnp.exp(l - (sp + after))
                if masked:
                    a = jnp.where(causal, a, 0.0)
                acc_ref[qrows, cols] += lax.dot_general(a.astype(BF16), v_ref[krows, cols], _NN,
                                                        preferred_element_type=F32)
                c_ref[qrows, :] = jnp.where(lane == hh * HEAD_DIM + kb, carry, c_ref[qrows, :])
                carry_ref[hh, qrows, :] = after[:, 0:1] + sp[:, 0:1]

        def alive(r0, n):
            return jnp.min(carry_ref[:, r0:r0 + n, :]) <= DEAD_AFTER

        for j in reversed(range(n_sub)):
            r0 = j * tk
            near = min(ATT_NEAR, tq - r0)
            block(n_full + j, r0, near, True)
            if r0 + near < tq:
                @pl.when(alive(r0 + near, tq - r0 - near))
                def _():
                    block(n_full + j, r0 + near, tq - r0 - near, False)

        def step(state):
            i, _ = state
            for r0 in range(0, tq, ATT_NEAR):
                @pl.when(alive(r0, ATT_NEAR))
                def _():
                    block(n_full - 1 - i, r0, ATT_NEAR, False)
            return i + 1, alive(0, tq)

        lax.while_loop(lambda state: (state[0] < n_full) & state[1], step, (0, alive(0, tq)))
        o = acc_ref[...]
        o_ref[...] = o
        silu, _ = _silu_and_grad(z_ref[...])
        a_ref[...] = (o * silu).astype(BF16)
        finish()

    qspec = pl.BlockSpec((tq, 128), lambda b, hp, qi: (b * nq + qi, hp))
    return pl.pallas_call(
        body, name="attn_fwd", grid=grid,
        in_specs=[qspec,
                  pl.BlockSpec((seq, 128), lambda b, hp, qi: (b, sec + hp)),
                  pl.BlockSpec((seq, 128), lambda b, hp, qi: (b, 2 * sec + hp)),
                  pl.BlockSpec((tq, 128), lambda b, hp, qi: (b * nq + qi, 3 * sec + hp))] + [_HBM] * n_g,
        out_specs=[qspec, qspec, pl.BlockSpec((None, tq, 128), lambda b, hp, qi: (hp, b * nq + qi, 0))]
        + [_HBM] * n_g,
        out_shape=[jax.ShapeDtypeStruct((m, d), F32), jax.ShapeDtypeStruct((m, d), BF16),
                   jax.ShapeDtypeStruct((sec, m, 128), F32)]
        + [jax.ShapeDtypeStruct((N_DEV,) + g.shape, g.dtype) for g in gather],
        scratch_shapes=[pltpu.VMEM((2, tq, 1), F32), pltpu.VMEM((tq, 128), F32)] + _travel_scratch(n_g),
        compiler_params=_params("arbitrary", "arbitrary", "arbitrary"),
    )(hb, hb, hb, h, *gather)


def _attn_bwd(hb, h, o, carries, da, nb, seq, send):
    m = nb * seq
    d = SB_HEADS * HEAD_DIM
    tq, tk = ATT_TQ, ATT_TK
    n_sub = tq // tk
    nq = seq // tq
    sec = d // 128
    scale = 1.0 / math.sqrt(HEAD_DIM)
    heads = [slice(hh * HEAD_DIM, (hh + 1) * HEAD_DIM) for hh in range(2)]
    grid = (nb, sec, nq)
    n_send = len(send)

    def body(q_ref, k_ref, v_ref, z_ref, o_ref, c_ref, da_ref, *rest):
        src_refs, rest = rest[:n_send], rest[n_send:]
        dq_ref, dk_ref, dv_ref, dz_ref = rest[:4]
        land_refs, rest = rest[4:4 + n_send], rest[4 + n_send:]
        dk_acc, dv_acc, dq_acc, gcarry_ref, send_sems, recv_sems = rest
        finish = _travel(grid, src_refs, land_refs, send_sems, recv_sems, whole=False)
        qi = pl.program_id(2)

        @pl.when(qi == 0)
        def _():
            dk_acc[...] = jnp.zeros_like(dk_acc)
            dv_acc[...] = jnp.zeros_like(dv_acc)

        row = lax.broadcasted_iota(jnp.int32, (tk, tk), 0)
        col = lax.broadcasted_iota(jnp.int32, (tk, tk), 1)
        u_after = _twice((row > col).astype(BF16))
        u_before = _twice((row < col).astype(BF16))
        silu, dsilu = _silu_and_grad(z_ref[...])
        dav = da_ref[...]
        dz_ref[...] = (dav * o_ref[...] * dsilu).astype(BF16)
        do2 = (dav * silu).astype(BF16)
        qs = [(q_ref[:, cols].astype(F32) * scale).astype(BF16) for cols in heads]
        dos = [do2[:, cols] for cols in heads]
        dq_acc[...] = jnp.zeros_like(dq_acc)
        gcarry_ref[...] = jnp.zeros_like(gcarry_ref)

        def block(kb, r0, n, masked):
            qrows = slice(r0, r0 + n)
            krows = pl.ds(pl.multiple_of(kb * tk, tk), tk)
            lane = lax.broadcasted_iota(jnp.int32, (n, 128), 1)
            if masked:
                causal = lax.broadcasted_iota(jnp.int32, (n, tk), 1) < lax.broadcasted_iota(jnp.int32, (n, tk), 0)
            for hh, cols in enumerate(heads):
                q, do = qs[hh][qrows], dos[hh][qrows]
                ks = k_ref[krows, cols]
                l = lax.dot_general(q, ks, _NT, preferred_element_type=F32)
                sp, e, e1 = _softplus(l)
                if masked:
                    sp = jnp.where(causal, sp, 0.0)
                carry = jnp.sum(jnp.where(lane == hh * HEAD_DIM + kb, c_ref[qrows, :], 0.0), axis=1, keepdims=True)
                after = carry + _dot2(sp, u_after)
                a = jnp.exp(l - (sp + after))
                if masked:
                    a = jnp.where(causal, a, 0.0)
                dv_acc[krows, cols] += lax.dot_general(a.astype(BF16), do, _TN, preferred_element_type=F32)
                g = a * lax.dot_general(do, v_ref[krows, cols], _NT, preferred_element_type=F32)
                c = gcarry_ref[hh, qrows, :] + _dot2(g, u_before)
                r = pl.reciprocal(e1, approx=True)
                beta = jnp.where(l >= 0.0, r, e * r)
                dl = g - (g + c) * beta
                if masked:
                    dl = jnp.where(causal, dl, 0.0)
                dl_b = dl.astype(BF16)
                dq_acc[qrows, cols] += lax.dot_general(dl_b, ks, _NN, preferred_element_type=F32)
                dk_acc[krows, cols] += lax.dot_general(dl_b, q, _TN, preferred_element_type=F32)
                gcarry_ref[hh, qrows, :] = c[:, tk - 1:tk] + g[:, tk - 1:tk]

        def alive(kb, r0, n):
            lane = lax.broadcasted_iota(jnp.int32, (n, 128), 1)
            mine = (lane == kb) | (lane == HEAD_DIM + kb)
            return jnp.min(jnp.where(mine, c_ref[r0:r0 + n, :], jnp.inf)) <= DEAD_AFTER

        def step(kb, _):
            for r0 in range(0, tq, ATT_NEAR):
                @pl.when(alive(kb, r0, ATT_NEAR))
                def _():
                    block(kb, r0, ATT_NEAR, False)

            return 0

        lax.fori_loop(0, n_sub * qi, step, 0)
        for j in range(n_sub):
            kb, r0 = n_sub * qi + j, j * tk
            near = min(ATT_NEAR, tq - r0)
            if r0 + near < tq:
                @pl.when(alive(kb, r0 + near, tq - r0 - near))
                def _():
                    block(kb, r0 + near, tq - r0 - near, False)
            block(kb, r0, near, True)
        dq_ref[...] = (dq_acc[...] * scale).astype(BF16)

        @pl.when(qi == nq - 1)
        def _():
            dk_ref[...] = dk_acc[...].astype(BF16)
            dv_ref[...] = dv_acc[...].astype(BF16)

        finish()

    qspec = pl.BlockSpec((tq, 128), lambda b, hp, qi: (b * nq + qi, hp))
    kvspec = pl.BlockSpec((seq, 128), lambda b, hp, qi: (b, hp))
    return pl.pallas_call(
        body, name="attn_bwd", grid=grid,
        in_specs=[qspec,
                  pl.BlockSpec((seq, 128), lambda b, hp, qi: (b, sec + hp)),
                  pl.BlockSpec((seq, 128), lambda b, hp, qi: (b, 2 * sec + hp)),
                  pl.BlockSpec((tq, 128), lambda b, hp, qi: (b * nq + qi, 3 * sec + hp)),
                  qspec,
                  pl.BlockSpec((None, tq, 128), lambda b, hp, qi: (hp, b * nq + qi, 0)),
                  qspec] + [_HBM] * n_send,
        out_specs=[qspec, kvspec, kvspec, qspec] + [_HBM] * n_send,
        out_shape=[jax.ShapeDtypeStruct((m, d), BF16)] * 4 + [jax.ShapeDtypeStruct(s.shape, s.dtype) for s in send],
        scratch_shapes=[pltpu.VMEM((seq, 128), F32), pltpu.VMEM((seq, 128), F32), pltpu.VMEM((tq, 128), F32),
                        pltpu.VMEM((2, tq, 1), F32)] + _travel_scratch(n_send),
        compiler_params=_params("arbitrary", "arbitrary", "arbitrary"),
    )(hb, hb, hb, h, o, carries, da, *send)


SUBLANES = 8


def _by_residue(taps):
    groups = []
    for res in range(SUBLANES):
        mine = [(off - res, k) for off, k in taps if off % SUBLANES == res]
        if mine:
            groups.append((res, max(a for a, _ in mine), mine))
    return groups


def _staged(src_ref, stage_ref, res, n):
    buf = stage_ref.at[res % 2]
    if res == 0:
        return src_ref
    buf[0:n, :] = src_ref[pl.ds(res, n), :]
    return buf


def _tap_sum(src_ref, stage_ref, w_ref, n_rows, taps, acc):
    for res, reach, mine in _by_residue(taps):
        shifted = _staged(src_ref, stage_ref, res, n_rows + reach)
        for a, k in mine:
            acc = acc + w_ref[k:k + 1, :] * shifted[a:a + n_rows, :]
    return acc


_CONV_TAPS = [(CV_HALO - CV_KERNEL + 1 + k, k) for k in range(CV_KERNEL)]
_CONV_TAPS_T = [(CV_KERNEL - 1 - k, k) for k in range(CV_KERNEL)]


def _conv_rows(pad_ref, stage_ref, w_ref, b_ref, n_rows):
    bias = jnp.broadcast_to(b_ref[...], (n_rows, b_ref.shape[1]))
    return _tap_sum(pad_ref, stage_ref, w_ref, n_rows, _CONV_TAPS, bias)


def _conv_fwd(h, conv_w, conv_b, ln_g, ln_b, nb, seq):
    m = nb * seq
    d = conv_b.shape[1]
    t = CONV_TILE
    tiles = seq // t
    hpt = t // CV_HALO

    def body(cv_ref, cg_ref, pv_ref, pg_ref, z_ref, w_ref, b_ref, g_ref, bb_ref, a_ref, pad_ref, stage_ref):
        first = pl.program_id(0) % tiles == 0
        pad_ref[0:CV_HALO, :] = jnp.where(first, 0.0, pv_ref[...] * _sigmoid(pg_ref[...]))
        pad_ref[CV_HALO:, :] = cv_ref[...] * _sigmoid(cg_ref[...])
        xhat, _ = _ln_stats(_conv_rows(pad_ref, stage_ref, w_ref, b_ref, t))
        s, _ = _silu_and_grad(xhat * g_ref[...] + bb_ref[...])
        sz, _ = _silu_and_grad(z_ref[...])
        a_ref[...] = (s * sz).astype(BF16)

    def main(c):
        return pl.BlockSpec((t, d), lambda i: (i, c))

    def prev(c):
        return pl.BlockSpec((CV_HALO, d), lambda i: (jnp.maximum(i * hpt - 1, 0), c))

    vec = pl.BlockSpec((1, d), lambda i: (0, 0))
    return pl.pallas_call(
        body, name="conv_fwd", grid=(m // t,),
        in_specs=[main(4), main(5), prev(4), prev(5), main(6),
                  pl.BlockSpec((CV_HALO, d), lambda i: (0, 0)), vec, vec, vec],
        out_specs=pl.BlockSpec((t, d), lambda i: (i, 0)),
        out_shape=jax.ShapeDtypeStruct((m, d), BF16),
        scratch_shapes=[pltpu.VMEM((CV_HALO + t, d), F32), pltpu.VMEM((2, CV_HALO + t, d), F32)],
        compiler_params=_params("parallel"),
    )(h, h, h, h, h, conv_w, conv_b, ln_g, ln_b)


def _conv_bwd(h, da, conv_w, conv_b, ln_g, ln_b, nb, seq):
    m = nb * seq
    d = conv_b.shape[1]
    t = CONV_TILE
    tiles = seq // t
    hpt = t // CV_HALO
    last_halo = m // CV_HALO - 1

    def body(cv_ref, cg_ref, pv_ref, pg_ref, nv_ref, ng_ref, z_ref, nz_ref, da_ref, nda_ref,
             w_ref, b_ref, g_ref, bb_ref, dh_ref, dw_ref, db_ref, dg_ref, dbb_ref, pad_ref, dc_ref, stage_ref):
        i = pl.program_id(0)

        @pl.when(i == 0)
        def _():
            dw_ref[...] = jnp.zeros_like(dw_ref)
            db_ref[...] = jnp.zeros_like(db_ref)
            dg_ref[...] = jnp.zeros_like(dg_ref)
            dbb_ref[...] = jnp.zeros_like(dbb_ref)

        first = i % tiles == 0
        last = i % tiles == tiles - 1
        cv = cv_ref[...]
        sg = _sigmoid(cg_ref[...])
        pad_ref[0:CV_HALO, :] = jnp.where(first, 0.0, pv_ref[...] * _sigmoid(pg_ref[...]))
        pad_ref[CV_HALO:CV_HALO + t, :] = cv * sg
        pad_ref[CV_HALO + t:, :] = nv_ref[...] * _sigmoid(ng_ref[...])

        def rows_bwd(conv_out, z, dav):
            xhat, rstd = _ln_stats(conv_out)
            s, ds = _silu_and_grad(xhat * g_ref[...] + bb_ref[...])
            sz, dsz = _silu_and_grad(z)
            dn = dav * sz * ds
            return _ln_bwd(dn * g_ref[...], xhat, rstd), dav * s * dsz, dn, xhat

        conv_all = _conv_rows(pad_ref, stage_ref, w_ref, b_ref, t + CV_HALO)
        dc, dz, dn, xhat = rows_bwd(conv_all[:t], z_ref[...], da_ref[...])
        dc_next, _, _, _ = rows_bwd(conv_all[t:], nz_ref[...], nda_ref[...])
        dc_ref[0:t, :] = dc
        dc_ref[t:, :] = jnp.where(last, 0.0, dc_next)
        dg_ref[...] += jnp.sum(dn * xhat, axis=0, keepdims=True)
        dbb_ref[...] += jnp.sum(dn, axis=0, keepdims=True)
        db_ref[...] += jnp.sum(dc, axis=0, keepdims=True)
        for res, reach, mine in _by_residue(_CONV_TAPS):
            shifted = _staged(pad_ref, stage_ref, res, t + reach)
            for a, k in mine:
                dw_ref[k:k + 1, :] += jnp.sum(dc * shifted[a:a + t, :], axis=0, keepdims=True)
        du = _tap_sum(dc_ref, stage_ref, w_ref, t, _CONV_TAPS_T, jnp.zeros((t, d), F32))
        dh_ref[:, 0:d] = (du * sg).astype(BF16)
        dh_ref[:, d:2 * d] = (du * cv * sg * (1.0 - sg)).astype(BF16)
        dh_ref[:, 2 * d:] = dz.astype(BF16)

    def main(c):
        return pl.BlockSpec((t, d), lambda i: (i, c))

    def prev(c):
        return pl.BlockSpec((CV_HALO, d), lambda i: (jnp.maximum(i * hpt - 1, 0), c))

    def nxt(c):
        return pl.BlockSpec((CV_HALO, d), lambda i: (jnp.minimum((i + 1) * hpt, last_halo), c))

    vec = pl.BlockSpec((1, d), lambda i: (0, 0))
    taps = pl.BlockSpec((CV_HALO, d), lambda i: (0, 0))
    return pl.pallas_call(
        body, name="conv_bwd", grid=(m // t,),
        in_specs=[main(4), main(5), prev(4), prev(5), nxt(4), nxt(5), main(6), nxt(6), main(0), nxt(0),
                  taps, vec, vec, vec],
        out_specs=[pl.BlockSpec((t, 3 * d), lambda i: (i, 0)), taps, vec, vec, vec],
        out_shape=[jax.ShapeDtypeStruct((m, 3 * d), BF16), jax.ShapeDtypeStruct((CV_HALO, d), F32),
                   jax.ShapeDtypeStruct((1, d), F32), jax.ShapeDtypeStruct((1, d), F32),
                   jax.ShapeDtypeStruct((1, d), F32)],
        scratch_shapes=[pltpu.VMEM((t + 2 * CV_HALO, d), F32), pltpu.VMEM((t + CV_HALO, d), F32),
                        pltpu.VMEM((2, t + 2 * CV_HALO, d), F32)],
        compiler_params=_params("arbitrary"),
    )(h, h, h, h, h, h, h, h, da, da, conv_w, conv_b, ln_g, ln_b)


_HBM = pl.BlockSpec(memory_space=pltpu.HBM)


def _place():
    return lax.axis_index("x"), lax.axis_index("y"), lax.axis_index("c")


def _slot(p):
    return 4 * p[0] + 2 * p[1] + p[2]


def _gather_mm_in_fwd(h0b, w_shard, riders):
    m, d = h0b.shape
    ns = w_shard.shape[1]
    n_i = m // IN_TILE
    n_steps = N_DEV
    n_arr = 1 + len(riders)

    def places():
        x, y, c = _place()
        chips = [(1 - x, y), (x, 1 - y), (1 - x, 1 - y)]
        return (x, y, c), (x, y, 1 - c), chips

    me, sibling, chips = places()
    order = [me, sibling] + [(*chip, me[2]) for chip in chips] + [(*chip, sibling[2]) for chip in chips]
    order = jnp.stack([_slot(p) for p in order]).astype(jnp.int32)

    def body(order_ref, h_ref, *rest):
        src_refs, rest = rest[:n_arr], rest[n_arr:]
        o32_ref, o16_ref = rest[:2]
        got_refs, rest = rest[2:2 + n_arr], rest[2 + n_arr:]
        wbuf, send_sems, recv_sems, own_sems, load_sems = rest
        s, i = pl.program_id(0), pl.program_id(1)
        me, sibling, chips = places()
        c = me[2]
        wg_ref = got_refs[0]

        def remote(t, k, block, to, own_src=False):
            dst = got_refs[t].at[_slot(block)]
            return pltpu.make_async_remote_copy(
                src_ref=src_refs[t] if own_src else dst, dst_ref=dst, send_sem=send_sems.at[7 * t + k],
                recv_sem=recv_sems.at[7 * t + k], device_id=to, device_id_type=MESH)

        def load(step):
            return pltpu.make_async_copy(wg_ref.at[order_ref[step]], wbuf.at[step % 2], load_sems.at[step % 2])

        own = [pltpu.make_async_copy(src_refs[t], got_refs[t].at[_slot(me)], own_sems.at[t]) for t in range(n_arr)]

        @pl.when((s == 0) & (i == 0))
        def _():
            for t in range(n_arr):
                own[t].start()
                remote(t, 0, me, sibling, own_src=True).start()
                for j, chip in enumerate(chips):
                    remote(t, 1 + j, me, (*chip, c), own_src=True).start()
            first = pltpu.make_async_copy(src_refs[0], wbuf.at[0], load_sems.at[0])
            first.start()
            first.wait()

        for nxt in range(1, n_steps):
            @pl.when((s == nxt - 1) & (i == n_i - 1))
            def _():
                if nxt == 1:
                    remote(0, 0, sibling, me).wait_recv()
                elif nxt <= 4:
                    j = nxt - 2
                    remote(0, 1 + j, (*chips[j], c), me).wait_recv()
                    remote(0, 4 + j, (*chips[j], c), sibling).start()
                else:
                    j = nxt - 5
                    remote(0, 4 + j, (*chips[j], 1 - c), me).wait_recv()
                if 3 <= nxt <= 5:
                    j = nxt - 3
                    for t in range(1, n_arr):
                        remote(t, 1 + j, (*chips[j], c), me).wait_recv()
                        remote(t, 4 + j, (*chips[j], c), sibling).start()
                load(nxt).start()

        for step in range(1, n_steps):
            @pl.when((s == step) & (i == 0))
            def _():
                load(step).wait()

        part = lax.dot_general(h_ref[...], wbuf[s % 2], _NN, preferred_element_type=F32)
        o32_ref[...] = part
        o16_ref[...] = part.astype(BF16)

        @pl.when((s == n_steps - 1) & (i == n_i - 1))
        def _():
            for t in range(n_arr):
                if t:
                    remote(t, 0, sibling, me).wait_recv()
                    for j, chip in enumerate(chips):
                        remote(t, 4 + j, (*chip, 1 - c), me).wait_recv()
                remote(t, 0, me, sibling, own_src=True).wait_send()
                for j, chip in enumerate(chips):
                    remote(t, 1 + j, me, (*chip, c), own_src=True).wait_send()
                    remote(t, 4 + j, (*chip, c), sibling).wait_send()
                own[t].wait()

    arrays = [w_shard] + list(riders)
    out = pl.BlockSpec((IN_TILE, ns), lambda s, i, order: (i, order[s]))
    grid_spec = pltpu.PrefetchScalarGridSpec(
        num_scalar_prefetch=1, grid=(n_steps, n_i),
        in_specs=[pl.BlockSpec((IN_TILE, d), lambda s, i, order: (i, 0))] + [_HBM] * n_arr,
        out_specs=[out, out] + [_HBM] * n_arr,
        scratch_shapes=[pltpu.VMEM((2, d, ns), BF16), pltpu.SemaphoreType.DMA((7 * n_arr,)),
                        pltpu.SemaphoreType.DMA((7 * n_arr,)), pltpu.SemaphoreType.DMA((n_arr,)),
                        pltpu.SemaphoreType.DMA((2,))])
    return pl.pallas_call(
        body, name="gather_mm_in_fwd", grid_spec=grid_spec,
        out_shape=[jax.ShapeDtypeStruct((m, N_DEV * ns), F32), jax.ShapeDtypeStruct((m, N_DEV * ns), BF16)]
        + [jax.ShapeDtypeStruct((N_DEV,) + a.shape, a.dtype) for a in arrays],
        compiler_params=_params("arbitrary", "arbitrary"),
    )(order, h0b, *arrays)


def _exchange_partials(parts, whole):
    n_parts = len(parts)
    arrays = list(parts) + list(whole)
    n = len(arrays)

    def body(*refs):
        ins, outs = refs[:n], refs[n:2 * n]
        send_sems, recv_sems, local_sems = refs[2 * n:]
        x, y, c = _place()
        me = (x, y, c)

        def src_for(a, p):
            return ins[a].at[_slot(p)] if a < n_parts else ins[a]

        def copy(a, k, peer):
            return pltpu.make_async_remote_copy(
                src_ref=src_for(a, peer), dst_ref=outs[a].at[_slot(me)], send_sem=send_sems.at[7 * a + k],
                recv_sem=recv_sems.at[7 * a + k], device_id=peer, device_id_type=MESH)

        def landed(a, k, peer):
            return pltpu.make_async_remote_copy(
                src_ref=src_for(a, peer), dst_ref=outs[a].at[_slot(peer)], send_sem=send_sems.at[7 * a + k],
                recv_sem=recv_sems.at[7 * a + k], device_id=peer, device_id_type=MESH)

        peers = []
        for k in range(1, N_DEV):
            fx, fy, fc = (k >> 2) & 1, (k >> 1) & 1, k & 1
            peers.append((1 - x if fx else x, 1 - y if fy else y, 1 - c if fc else c))
        mine = [pltpu.make_async_copy(src_for(a, me), outs[a].at[_slot(me)], local_sems.at[a]) for a in range(n)]
        for cp in mine:
            cp.start()
        sent = [copy(a, k, peer) for a in range(n) for k, peer in enumerate(peers)]
        for cp in sent:
            cp.start()
        for a in range(n):
            for k, peer in enumerate(peers):
                landed(a, k, peer).wait_recv()
        for cp in sent:
            cp.wait_send()
        for cp in mine:
            cp.wait()

    out_shape = [jax.ShapeDtypeStruct(p.shape, p.dtype) for p in parts]
    out_shape += [jax.ShapeDtypeStruct((N_DEV,) + w.shape, w.dtype) for w in whole]
    return pl.pallas_call(
        body, name="exchange_grad_partials",
        in_specs=[_HBM] * n, out_specs=[_HBM] * n, out_shape=out_shape,
        scratch_shapes=[pltpu.SemaphoreType.DMA((7 * n,)), pltpu.SemaphoreType.DMA((7 * n,)),
                        pltpu.SemaphoreType.DMA((n,))],
    )(*arrays)


def _peers_of(x, y, c):
    peers = []
    for k in range(1, N_DEV):
        fx, fy, fc = (k >> 2) & 1, (k >> 1) & 1, k & 1
        peers.append((1 - x if fx else x, 1 - y if fy else y, 1 - c if fc else c))
    return peers


def _with_own(landed, own, me):
    return lax.dynamic_update_index_in_dim(landed, own, me, 0)


def _adamw(name, slabs, w, mom, var, rows):
    r, c = w.shape
    n_slabs = len(slabs)
    per_slab = r // n_slabs // rows

    def body(*refs):
        p_refs = refs[:n_slabs]
        w_ref, m_ref, v_ref, g_ref, d_ref, nm_ref, nv_ref = refs[n_slabs:]
        slab = pl.program_id(0) // per_slab
        g = None
        for p in range(N_DEV):
            part = p_refs[0][p]
            for k in range(1, n_slabs):
                part = jnp.where(slab == k, p_refs[k][p], part)
            g = part.astype(F32) if g is None else g + part.astype(F32)
        m_new = ADAM_B1 * m_ref[...] + (1.0 - ADAM_B1) * g
        v_new = ADAM_B2 * v_ref[...] + (1.0 - ADAM_B2) * (g * g)
        m_hat = m_new / (1.0 - ADAM_B1 ** ADAM_STEP)
        v_hat = v_new / (1.0 - ADAM_B2 ** ADAM_STEP)
        g_ref[...] = g
        d_ref[...] = -ADAM_LR * (m_hat / (jnp.sqrt(v_hat) + ADAM_EPS) + ADAM_WD * w_ref[...])
        nm_ref[...] = m_new
        nv_ref[...] = v_new

    def slab_spec(k):
        return pl.BlockSpec((N_DEV, rows, c), lambda i: (0, jnp.clip(i - k * per_slab, 0, per_slab - 1), 0))

    blk = pl.BlockSpec((rows, c), lambda i: (i, 0))
    return pl.pallas_call(
        body, name=name, grid=(r // rows,),
        in_specs=[slab_spec(k) for k in range(n_slabs)] + [blk, blk, blk],
        out_specs=[blk] * 4, out_shape=[jax.ShapeDtypeStruct((r, c), F32)] * 4,
        compiler_params=_params("parallel"),
    )(*slabs, w, mom, var)


def kernel(x, ln_in_g, ln_in_b, w_in, w_sb_proj, conv_w, conv_b, conv_ln_g, conv_ln_b, w_cv_proj, w_out, ln_post_g, ln_post_b, loss_target, m_ln_in_g, m_ln_in_b, m_w_in, m_w_sb_proj, m_conv_w, m_conv_b, m_conv_ln_g, m_conv_ln_b, m_w_cv_proj, m_w_out, m_ln_post_g, m_ln_post_b, v_ln_in_g, v_ln_in_b, v_w_in, v_w_sb_proj, v_conv_w, v_conv_b, v_conv_ln_g, v_conv_ln_b, v_w_cv_proj, v_w_out, v_ln_post_g, v_ln_post_b):
    nb, seq, d = x.shape
    m = nb * seq
    x2d = x.reshape(m, d)
    target = loss_target.reshape(m, d)
    rs = d // N_DEV
    pad_taps = ((0, CV_HALO - CV_KERNEL), (0, 0))

    proj_shards = jnp.stack([w_sb_proj[0], w_cv_proj[0], w_out[0]]).astype(BF16)
    conv_w_shard = jnp.pad(conv_w[0], pad_taps)
    me = _slot(_place())
    own = lambda parts: lax.dynamic_index_in_dim(parts, me, 0, keepdims=False)

    h0, h0b = _ln_in_fwd(x2d, ln_in_g.reshape(1, d), ln_in_b.reshape(1, d))
    h, hb, w_in_g = _gather_mm_in_fwd(h0b, w_in[0].astype(BF16), [])
    o, a_sb, carries, proj_g, conv_w_g = _attn_fwd(hb, h, nb, seq, [proj_shards, conv_w_shard])
    proj_g = _with_own(proj_g, proj_shards, me)
    conv_w_g = _with_own(conv_w_g, conv_w_shard, me)
    w_sb_g = proj_g[:, 0].reshape(d, d)
    w_cv_g = proj_g[:, 1].reshape(d, d)
    w_out_g = proj_g[:, 2].reshape(d, d)
    conv_w_full = conv_w_g.transpose(1, 0, 2).reshape(CV_HALO, d)
    a_cv = _conv_fwd(h, conv_w_full, conv_b, conv_ln_g, conv_ln_b, nb, seq)
    y_sb = _mm_nn("mm_sb_fwd", a_sb, w_sb_g)
    y_cv = _mm_nn("mm_cv_fwd", a_cv, w_cv_g)
    merged = _merge_fwd(h, y_sb, y_cv)
    mo = _mm_nn("mm_out_fwd", merged, w_out_g)
    loss_part, d_pre, d_pre_b, dg_post, db_post = _post_ln_loss(h0, mo, target, ln_post_g, ln_post_b)

    dw_out = _mm_tn("mm_out_bwd_w", merged, d_pre_b, BF16)
    d_merged = _mm_nt("mm_out_bwd_x", d_pre_b, w_out_g)
    dy_sb, dy_cv, d_gates = _merge_bwd(h, y_sb, y_cv, d_merged)
    dw_sb = _mm_tn("mm_sb_bwd_w", a_sb, dy_sb, BF16)
    dw_cv = _mm_tn("mm_cv_bwd_w", a_cv, dy_cv, BF16)
    da_sb = _mm_nt("mm_sb_bwd_x", dy_sb, w_sb_g)
    da_cv = _mm_nt("mm_cv_bwd_x", dy_cv, w_cv_g)
    dproj = jnp.stack([dw_sb.reshape(N_DEV, rs, d), dw_cv.reshape(N_DEV, rs, d), dw_out.reshape(N_DEV, rs, d)], axis=1)
    dq, dk, dv, dz_sb, r_proj = _attn_bwd(hb, h, o, carries, da_sb, nb, seq, [dproj])
    d_conv3, dconv_w, dconv_b, dconv_ln_g, dconv_ln_b = _conv_bwd(
        h, da_cv, conv_w_full, conv_b, conv_ln_g, conv_ln_b, nb, seq)
    dhb = jnp.concatenate([dq, dk, dv, dz_sb, d_conv3, d_gates], axis=1)
    dconv_w_parts = dconv_w.reshape(CV_HALO, N_DEV, d // N_DEV).transpose(1, 0, 2)
    dw_in_a, r_conv = _mm_in_bwd_w(h0b, dhb, 0, [dconv_w_parts])
    dw_in_b, r_in_a = _mm_in_bwd_w(h0b, dhb, 1, [dw_in_a])
    dh_mm, r_in_b = _mm_in_bwd_x(dhb, w_in_g, [dw_in_b])
    dx, dg_in, db_in = _ln_in_bwd(x2d, d_pre, dh_mm, ln_in_g.reshape(1, d))
    r_proj = _with_own(r_proj, own(dproj), me)
    r_conv = _with_own(r_conv, own(dconv_w_parts), me)
    r_in = [_with_own(r_in_a, own(dw_in_a), me), _with_own(r_in_b, own(dw_in_b), me)]

    small = jnp.concatenate([dg_in, db_in, dconv_b, dconv_ln_g, dconv_ln_b, dg_post, db_post,
                             jnp.broadcast_to(loss_part, (1, d))], axis=0)
    (r_small,) = _exchange_partials([], [small])
    loss = jnp.sum(r_small[:, -1, 0])

    g_in, d_in, nm_in, nv_in = _adamw("adamw_w_in", r_in, w_in[0], m_w_in[0], v_w_in[0], 128)
    stack3 = lambda a, b, c: jnp.concatenate([a[0], b[0], c[0]], axis=0)
    g_pr, d_pr, nm_pr, nv_pr = _adamw(
        "adamw_proj", [r_proj.reshape(N_DEV, 3 * rs, d)], stack3(w_sb_proj, w_cv_proj, w_out),
        stack3(m_w_sb_proj, m_w_cv_proj, m_w_out), stack3(v_w_sb_proj, v_w_cv_proj, v_w_out), 3 * rs)
    padc = lambda a: jnp.pad(a[0], pad_taps)
    g_cw, d_cw, nm_cw, nv_cw = _adamw("adamw_conv_w", [r_conv], padc(conv_w), padc(m_conv_w), padc(v_conv_w), CV_HALO)
    vecs = lambda *a: jnp.concatenate([t.reshape(1, d) for t in a] + [jnp.ones((1, d), F32)], axis=0)
    g_sm, d_sm, nm_sm, nv_sm = _adamw(
        "adamw_vectors", [r_small],
        vecs(ln_in_g, ln_in_b, conv_b, conv_ln_g, conv_ln_b, ln_post_g, ln_post_b),
        vecs(m_ln_in_g, m_ln_in_b, m_conv_b, m_conv_ln_g, m_conv_ln_b, m_ln_post_g, m_ln_post_b),
        vecs(v_ln_in_g, v_ln_in_b, v_conv_b, v_conv_ln_g, v_conv_ln_b, v_ln_post_g, v_ln_post_b), 8)


    def leaves(big, pr, cw, sm):
        return (sm[0], sm[1], big[None], pr[None, 0:rs], cw[None, :CV_KERNEL], sm[2:3], sm[3:4], sm[4:5],
                pr[None, rs:2 * rs], pr[None, 2 * rs:], sm[5:6], sm[6:7])

    return (loss, dx.reshape(nb, seq, d), *leaves(g_in, g_pr, g_cw, g_sm), *leaves(d_in, d_pr, d_cw, d_sm),
            *leaves(nm_in, nm_pr, nm_cw, nm_sm), *leaves(nv_in, nv_pr, nv_cw, nv_sm))
```

```python
import functools
import math

import jax
import jax.numpy as jnp
from jax import lax
from jax.experimental import pallas as pl
from jax.experimental.pallas import tpu as pltpu

F32 = jnp.float32
BF16 = jnp.bfloat16
MESH = pl.DeviceIdType.MESH

N_DEV = 8
SB_HEADS = 16
HEAD_DIM = 64
CV_KERNEL = 31
CV_HALO = 32
LN_EPS = 1e-5
DEEPNORM_ALPHA = 2.0 ** 0.25
ADAM_LR, ADAM_B1, ADAM_B2, ADAM_EPS, ADAM_WD, ADAM_STEP = 0.001, 0.9, 0.999, 1e-08, 0.01, 10

ATT_TK = 256
ATT_TQ = 1024
ATT_NEAR = 512
ROW_TILE = 512
IN_TILE = 1024
CONV_TILE = 256
VMEM_LIMIT = 56 * 1024 * 1024


def _params(*sem):
    return pltpu.CompilerParams(dimension_semantics=sem, vmem_limit_bytes=VMEM_LIMIT)


def _sigmoid(x):
    return 1.0 / (1.0 + jnp.exp(-x))


def _silu_and_grad(x):
    s = _sigmoid(x)
    return x * s, s * (1.0 + x * (1.0 - s))


def _ln_stats(x):
    mu = jnp.mean(x, axis=-1, keepdims=True)
    xc = x - mu
    var = jnp.mean(xc * xc, axis=-1, keepdims=True)
    rstd = lax.rsqrt(var + LN_EPS)
    return xc * rstd, rstd


def _ln_bwd(dxhat, xhat, rstd):
    m1 = jnp.mean(dxhat, axis=-1, keepdims=True)
    m2 = jnp.mean(dxhat * xhat, axis=-1, keepdims=True)
    return rstd * (dxhat - m1 - xhat * m2)


def _ln_in_fwd(x2d, g, b):
    m, d = x2d.shape

    def body(x_ref, g_ref, b_ref, h_ref, hb_ref):
        xhat, _ = _ln_stats(x_ref[...])
        y = xhat * g_ref[...] + b_ref[...]
        h_ref[...] = y
        hb_ref[...] = y.astype(BF16)

    row = pl.BlockSpec((ROW_TILE, d), lambda i: (i, 0))
    vec = pl.BlockSpec((1, d), lambda i: (0, 0))
    return pl.pallas_call(
        body, name="ln_in_fwd", grid=(m // ROW_TILE,),
        in_specs=[row, vec, vec], out_specs=[row, row],
        out_shape=[jax.ShapeDtypeStruct((m, d), F32), jax.ShapeDtypeStruct((m, d), BF16)],
        compiler_params=_params("parallel"),
    )(x2d, g, b)


def _ln_in_bwd(x2d, d_pre, dh_mm, g):
    m, d = x2d.shape

    def body(x_ref, dp_ref, dm_ref, g_ref, dx_ref, dg_ref, db_ref):
        @pl.when(pl.program_id(0) == 0)
        def _():
            dg_ref[...] = jnp.zeros_like(dg_ref)
            db_ref[...] = jnp.zeros_like(db_ref)

        xhat, rstd = _ln_stats(x_ref[...])
        dh = DEEPNORM_ALPHA * dp_ref[...] + dm_ref[...]
        dg_ref[...] += jnp.sum(dh * xhat, axis=0, keepdims=True)
        db_ref[...] += jnp.sum(dh, axis=0, keepdims=True)
        dx_ref[...] = _ln_bwd(dh * g_ref[...], xhat, rstd)

    row = pl.BlockSpec((ROW_TILE, d), lambda i: (i, 0))
    vec = pl.BlockSpec((1, d), lambda i: (0, 0))
    return pl.pallas_call(
        body, name="ln_in_bwd", grid=(m // ROW_TILE,),
        in_specs=[row, row, row, vec], out_specs=[row, vec, vec],
        out_shape=[jax.ShapeDtypeStruct((m, d), F32), jax.ShapeDtypeStruct((1, d), F32),
                   jax.ShapeDtypeStruct((1, d), F32)],
        compiler_params=_params("arbitrary"),
    )(x2d, d_pre, dh_mm, g)


def _merge_fwd(h, y_sb, y_cv):
    m, d = y_sb.shape

    def body(gs_ref, gc_ref, ys_ref, yc_ref, out_ref):
        out_ref[...] = (_sigmoid(gs_ref[...]) * ys_ref[...] + _sigmoid(gc_ref[...]) * yc_ref[...]).astype(BF16)

    row = pl.BlockSpec((ROW_TILE, d), lambda i: (i, 0))
    return pl.pallas_call(
        body, name="merge_fwd", grid=(m // ROW_TILE,),
        in_specs=[pl.BlockSpec((ROW_TILE, d), lambda i: (i, 7)), pl.BlockSpec((ROW_TILE, d), lambda i: (i, 8)), row, row],
        out_specs=row, out_shape=jax.ShapeDtypeStruct((m, d), BF16),
        compiler_params=_params("parallel"),
    )(h, h, y_sb, y_cv)


def _merge_bwd(h, y_sb, y_cv, dm):
    m, d = y_sb.shape

    def body(gs_ref, gc_ref, ys_ref, yc_ref, dm_ref, dys_ref, dyc_ref, dgate_ref):
        dmv = dm_ref[...]
        ss = _sigmoid(gs_ref[...])
        sc = _sigmoid(gc_ref[...])
        dys_ref[...] = (ss * dmv).astype(BF16)
        dyc_ref[...] = (sc * dmv).astype(BF16)
        dgate_ref[:, :d] = (dmv * ys_ref[...] * ss * (1.0 - ss)).astype(BF16)
        dgate_ref[:, d:] = (dmv * yc_ref[...] * sc * (1.0 - sc)).astype(BF16)

    row = pl.BlockSpec((ROW_TILE, d), lambda i: (i, 0))
    return pl.pallas_call(
        body, name="merge_bwd", grid=(m // ROW_TILE,),
        in_specs=[pl.BlockSpec((ROW_TILE, d), lambda i: (i, 7)), pl.BlockSpec((ROW_TILE, d), lambda i: (i, 8)), row, row, row],
        out_specs=[row, row, pl.BlockSpec((ROW_TILE, 2 * d), lambda i: (i, 0))],
        out_shape=[jax.ShapeDtypeStruct((m, d), BF16), jax.ShapeDtypeStruct((m, d), BF16),
                   jax.ShapeDtypeStruct((m, 2 * d), BF16)],
        compiler_params=_params("parallel"),
    )(h, h, y_sb, y_cv, dm)


def _post_ln_loss(h0, mo, target, g, b):
    m, d = h0.shape

    def body(h_ref, mo_ref, t_ref, g_ref, b_ref, loss_ref, dp_ref, dpb_ref, dg_ref, db_ref):
        @pl.when(pl.program_id(0) == 0)
        def _():
            loss_ref[...] = jnp.zeros_like(loss_ref)
            dg_ref[...] = jnp.zeros_like(dg_ref)
            db_ref[...] = jnp.zeros_like(db_ref)

        xhat, rstd = _ln_stats(DEEPNORM_ALPHA * h_ref[...] + mo_ref[...])
        err = xhat * g_ref[...] + b_ref[...] - t_ref[...]
        per_row = jnp.mean(err * err, axis=-1, keepdims=True)
        loss_ref[...] += 0.5 * jnp.sum(per_row, axis=0, keepdims=True)
        dy = err * (1.0 / d)
        dg_ref[...] += jnp.sum(dy * xhat, axis=0, keepdims=True)
        db_ref[...] += jnp.sum(dy, axis=0, keepdims=True)
        dp = _ln_bwd(dy * g_ref[...], xhat, rstd)
        dp_ref[...] = dp
        dpb_ref[...] = dp.astype(BF16)

    row = pl.BlockSpec((ROW_TILE, d), lambda i: (i, 0))
    vec = pl.BlockSpec((1, d), lambda i: (0, 0))
    one = pl.BlockSpec((1, 1), lambda i: (0, 0))
    return pl.pallas_call(
        body, name="post_ln_loss", grid=(m // ROW_TILE,),
        in_specs=[row, row, row, vec, vec], out_specs=[one, row, row, vec, vec],
        out_shape=[jax.ShapeDtypeStruct((1, 1), F32), jax.ShapeDtypeStruct((m, d), F32),
                   jax.ShapeDtypeStruct((m, d), BF16), jax.ShapeDtypeStruct((1, d), F32),
                   jax.ShapeDtypeStruct((1, d), F32)],
        compiler_params=_params("arbitrary"),
    )(h0, mo, target, g, b)


_NN = (((1,), (0,)), ((), ()))
_NT = (((1,), (1,)), ((), ()))
_TN = (((0,), (0,)), ((), ()))


def _grid_ends(grid):
    ids = [pl.program_id(ax) for ax in range(len(grid))]
    first = functools.reduce(jnp.logical_and, [i == 0 for i in ids])
    last = functools.reduce(jnp.logical_and, [i == n - 1 for i, n in zip(ids, grid)])
    return first, last


def _travel(grid, src_refs, land_refs, send_sems, recv_sems, whole):
    first, last = _grid_ends(grid)
    x, y, c = _place()
    n = len(src_refs)

    def copy(t, k, peer, lands_at):
        return pltpu.make_async_remote_copy(
            src_ref=src_refs[t] if whole else src_refs[t].at[_slot(peer)],
            dst_ref=land_refs[t].at[_slot(lands_at)], send_sem=send_sems.at[7 * t + k],
            recv_sem=recv_sems.at[7 * t + k], device_id=peer, device_id_type=MESH)

    @pl.when(first)
    def _():
        for k, peer in enumerate(_peers_of(x, y, c)):
            for t in range(n):
                copy(t, k, peer, (x, y, c)).start()

    def finish():
        @pl.when(last)
        def _():
            for k, peer in enumerate(_peers_of(x, y, c)):
                for t in range(n):
                    arrived = copy(t, k, peer, peer)
                    arrived.wait_send()
                    arrived.wait_recv()

    return finish


def _travel_scratch(n):
    return [pltpu.SemaphoreType.DMA((7 * n,)), pltpu.SemaphoreType.DMA((7 * n,))]


def _mm(name, a, b, dims, grid, a_spec, b_spec, out_specs, out_shape, acc_shape, k_axis, send=()):
    n_k = 1 if k_axis is None else grid[k_axis]
    n_out = len(out_shape)
    n_send = len(send)

    def body(a_ref, b_ref, *rest):
        src_refs, rest = rest[:n_send], rest[n_send:]
        outs, land_refs, acc_ref = rest[:n_out], rest[n_out:n_out + n_send], rest[n_out + n_send]
        if n_send:
            finish = _travel(grid, src_refs, land_refs, *rest[n_out + n_send + 1:], whole=False)

        part = lax.dot_general(a_ref[...].astype(BF16), b_ref[...].astype(BF16), dims, preferred_element_type=F32)
        if n_k == 1:
            for o in outs:
                o[...] = part.astype(o.dtype)
        else:
            k = pl.program_id(k_axis)

            @pl.when(k == 0)
            def _():
                acc_ref[...] = part

            @pl.when(k > 0)
            def _():
                acc_ref[...] += part

            @pl.when(k == n_k - 1)
            def _():
                for o in outs:
                    o[...] = acc_ref[...].astype(o.dtype)

        if n_send:
            finish()

    sem = tuple("arbitrary" for _ in grid) if n_send else tuple(
        "arbitrary" if ax == k_axis else "parallel" for ax in range(len(grid)))
    scratch = [pltpu.VMEM(acc_shape, F32)] + (_travel_scratch(n_send) if n_send else [])
    return pl.pallas_call(
        body, name=name, grid=grid, in_specs=[a_spec, b_spec] + [_HBM] * n_send,
        out_specs=list(out_specs) + [_HBM] * n_send,
        out_shape=list(out_shape) + [jax.ShapeDtypeStruct(s.shape, s.dtype) for s in send],
        scratch_shapes=scratch, compiler_params=_params(*sem),
    )(a, b, *send)


def _mm_nn(name, a, b, out_dtype=F32):
    m, k = a.shape
    n = b.shape[1]
    return _mm(name, a, b, _NN, (m // ROW_TILE,), pl.BlockSpec((ROW_TILE, k), lambda i: (i, 0)),
               pl.BlockSpec((k, n), lambda i: (0, 0)), [pl.BlockSpec((ROW_TILE, n), lambda i: (i, 0))],
               [jax.ShapeDtypeStruct((m, n), out_dtype)], (8, 128), None)[0]


def _mm_nt(name, a, b, out_dtype=F32):
    m, k = a.shape
    n = b.shape[0]
    return _mm(name, a, b, _NT, (m // ROW_TILE,), pl.BlockSpec((ROW_TILE, k), lambda i: (i, 0)),
               pl.BlockSpec((n, k), lambda i: (0, 0)), [pl.BlockSpec((ROW_TILE, n), lambda i: (i, 0))],
               [jax.ShapeDtypeStruct((m, n), out_dtype)], (8, 128), None)[0]


def _mm_tn(name, a, b, out_dtype):
    m, k = a.shape
    n = b.shape[1]
    return _mm(name, a, b, _TN, (m // ROW_TILE,), pl.BlockSpec((ROW_TILE, k), lambda i: (i, 0)),
               pl.BlockSpec((ROW_TILE, n), lambda i: (i, 0)), [pl.BlockSpec((k, n), lambda i: (0, 0))],
               [jax.ShapeDtypeStruct((k, n), out_dtype)], (k, n), 0)[0]


def _mm_in_bwd_x(dhb, w_g, send):
    m = dhb.shape[0]
    _, d, ns = w_g.shape
    return _mm("mm_in_bwd_x", dhb, w_g, _NT, (m // IN_TILE, N_DEV), pl.BlockSpec((IN_TILE, ns), lambda i, j: (i, j)),
               pl.BlockSpec((None, d, ns), lambda i, j: (j, 0, 0)), [pl.BlockSpec((IN_TILE, d), lambda i, j: (i, 0))],
               [jax.ShapeDtypeStruct((m, d), F32)], (IN_TILE, d), 1, send=send)


def _mm_in_bwd_w(h0b, dhb, half, send):
    m, d = h0b.shape
    ns = dhb.shape[1] // N_DEV
    dh = d // 2
    return _mm(f"mm_in_bwd_w{half}", h0b, dhb, _TN, (N_DEV, m // IN_TILE),
               pl.BlockSpec((IN_TILE, dh), lambda j, k: (k, half)),
               pl.BlockSpec((IN_TILE, ns), lambda j, k: (k, j)), [pl.BlockSpec((None, dh, ns), lambda j, k: (j, 0, 0))],
               [jax.ShapeDtypeStruct((N_DEV, dh, ns), BF16)], (dh, ns), 1, send=send)


LOG2_E = 1.4426950408889634
DEAD_AFTER = 110.0
NEVER_REACHED = 1e30


def _twice(u):
    return jnp.concatenate([u, u], axis=0)


def _dot2(x, u2):
    hi = x.astype(BF16)
    lo = (x - hi.astype(F32)).astype(BF16)
    return lax.dot_general(jnp.concatenate([hi, lo], axis=1), u2, _NN, preferred_element_type=F32)


def _softplus(l):
    e = jnp.exp2(jnp.abs(l) * (-LOG2_E))
    e1 = 1.0 + e
    return jnp.maximum(l, 0.0) + jnp.log(e1), e, e1


def _attn_fwd(hb, h, nb, seq, gather):
    m = nb * seq
    d = SB_HEADS * HEAD_DIM
    tq, tk = ATT_TQ, ATT_TK
    n_sub = tq // tk
    nq = seq // tq
    sec = d // 128
    scale = 1.0 / math.sqrt(HEAD_DIM)
    heads = [slice(hh * HEAD_DIM, (hh + 1) * HEAD_DIM) for hh in range(2)]
    grid = (nb, sec, nq)
    n_g = len(gather)

    def body(q_ref, k_ref, v_ref, z_ref, *rest):
        src_refs, rest = rest[:n_g], rest[n_g:]
        o_ref, a_ref, c_ref = rest[:3]
        land_refs, (carry_ref, acc_ref, send_sems, recv_sems) = rest[3:3 + n_g], rest[3 + n_g:]
        finish = _travel(grid, src_refs, land_refs, send_sems, recv_sems, whole=True)
        qi = pl.program_id(2)
        n_full = n_sub * qi
        u_after = _twice((lax.broadcasted_iota(jnp.int32, (tk, tk), 0)
                          > lax.broadcasted_iota(jnp.int32, (tk, tk), 1)).astype(BF16))
        qs = [(q_ref[:, cols].astype(F32) * scale).astype(BF16) for cols in heads]
        carry_ref[...] = jnp.zeros_like(carry_ref)
        acc_ref[...] = jnp.zeros_like(acc_ref)
        lane_t = lax.broadcasted_iota(jnp.int32, (tq, 128), 1)
        c_ref[...] = jnp.where(lane_t % HEAD_DIM < n_full + n_sub, NEVER_REACHED, 0.0)

        def block(kb, r0, n, masked):
            qrows = slice(r0, r0 + n)
            krows = pl.ds(pl.multiple_of(kb * tk, tk), tk)
            lane = lax.broadcasted_iota(jnp.int32, (n, 128), 1)
            if masked:
                causal = lax.broadcasted_iota(jnp.int32, (n, tk), 1) < lax.broadcasted_iota(jnp.int32, (n, tk), 0)
            for hh, cols in enumerate(heads):
                carry = carry_ref[hh, qrows, :]
                l = lax.dot_general(qs[hh][qrows], k_ref[krows, cols], _NT, preferred_element_type=F32)
                sp, _, _ = _softplus(l)
                if masked:
                    sp = jnp.where(causal, sp, 0.0)
                after = carry + _dot2(sp, u_after)
                a = jnp.exp(l - (sp + after))
                if masked:
                    a = jnp.where(causal, a, 0.0)
                acc_ref[qrows, cols] += lax.dot_general(a.astype(BF16), v_ref[krows, cols], _NN,
                                                        preferred_element_type=F32)
                c_ref[qrows, :] = jnp.where(lane == hh * HEAD_DIM + kb, carry, c_ref[qrows, :])
                carry_ref[hh, qrows, :] = after[:, 0:1] + sp[:, 0:1]

        def alive(r0, n):
            return jnp.min(carry_ref[:, r0:r0 + n, :]) <= DEAD_AFTER

        for j in reversed(range(n_sub)):
            r0 = j * tk
            near = min(ATT_NEAR, tq - r0)
            block(n_full + j, r0, near, True)
            if r0 + near < tq:
                @pl.when(alive(r0 + near, tq - r0 - near))
                def _():
                    block(n_full + j, r0 + near, tq - r0 - near, False)

        def step(state):
            i, _ = state
            for r0 in range(0, tq, ATT_NEAR):
                @pl.when(alive(r0, ATT_NEAR))
                def _():
                    block(n_full - 1 - i, r0, ATT_NEAR, False)
            return i + 1, alive(0, tq)

        lax.while_loop(lambda state: (state[0] < n_full) & state[1], step, (0, alive(0, tq)))
        o = acc_ref[...]
        o_ref[...] = o
        silu, _ = _silu_and_grad(z_ref[...])
        a_ref[...] = (o * silu).astype(BF16)
        finish()

    qspec = pl.BlockSpec((tq, 128), lambda b, hp, qi: (b * nq + qi, hp))
    return pl.pallas_call(
        body, name="attn_fwd", grid=grid,
        in_specs=[qspec,
                  pl.BlockSpec((seq, 128), lambda b, hp, qi: (b, sec + hp)),
                  pl.BlockSpec((seq, 128), lambda b, hp, qi: (b, 2 * sec + hp)),
                  pl.BlockSpec((tq, 128), lambda b, hp, qi: (b * nq + qi, 3 * sec + hp))] + [_HBM] * n_g,
        out_specs=[qspec, qspec, pl.BlockSpec((None, tq, 128), lambda b, hp, qi: (hp, b * nq + qi, 0))]
        + [_HBM] * n_g,
        out_shape=[jax.ShapeDtypeStruct((m, d), F32), jax.ShapeDtypeStruct((m, d), BF16),
                   jax.ShapeDtypeStruct((sec, m, 128), F32)]
        + [jax.ShapeDtypeStruct((N_DEV,) + g.shape, g.dtype) for g in gather],
        scratch_shapes=[pltpu.VMEM((2, tq, 1), F32), pltpu.VMEM((tq, 128), F32)] + _travel_scratch(n_g),
        compiler_params=_params("arbitrary", "arbitrary", "arbitrary"),
    )(hb, hb, hb, h, *gather)


def _attn_bwd(hb, h, o, carries, da, nb, seq, send):
    m = nb * seq
    d = SB_HEADS * HEAD_DIM
    tq, tk = ATT_TQ, ATT_TK
    n_sub = tq // tk
    nq = seq // tq
    sec = d // 128
    scale = 1.0 / math.sqrt(HEAD_DIM)
    heads = [slice(hh * HEAD_DIM, (hh + 1) * HEAD_DIM) for hh in range(2)]
    grid = (nb, sec, nq)
    n_send = len(send)

    def body(q_ref, k_ref, v_ref, z_ref, o_ref, c_ref, da_ref, *rest):
        src_refs, rest = rest[:n_send], rest[n_send:]
        dq_ref, dk_ref, dv_ref, dz_ref = rest[:4]
        land_refs, rest = rest[4:4 + n_send], rest[4 + n_send:]
        dk_acc, dv_acc, dq_acc, gcarry_ref, send_sems, recv_sems = rest
        finish = _travel(grid, src_refs, land_refs, send_sems, recv_sems, whole=False)
        qi = pl.program_id(2)

        @pl.when(qi == 0)
        def _():
            dk_acc[...] = jnp.zeros_like(dk_acc)
            dv_acc[...] = jnp.zeros_like(dv_acc)

        row = lax.broadcasted_iota(jnp.int32, (tk, tk), 0)
        col = lax.broadcasted_iota(jnp.int32, (tk, tk), 1)
        u_after = _twice((row > col).astype(BF16))
        u_before = (row < col).astype(BF16)
        silu, dsilu = _silu_and_grad(z_ref[...])
        dav = da_ref[...]
        dz_ref[...] = (dav * o_ref[...] * dsilu).astype(BF16)
        do2 = (dav * silu).astype(BF16)
        qs = [(q_ref[:, cols].astype(F32) * scale).astype(BF16) for cols in heads]
        dos = [do2[:, cols] for cols in heads]
        dq_acc[...] = jnp.zeros_like(dq_acc)
        gcarry_ref[...] = jnp.zeros_like(gcarry_ref)

        def block(kb, r0, n, masked):
            qrows = slice(r0, r0 + n)
            krows = pl.ds(pl.multiple_of(kb * tk, tk), tk)
            lane = lax.broadcasted_iota(jnp.int32, (n, 128), 1)
            if masked:
                causal = lax.broadcasted_iota(jnp.int32, (n, tk), 1) < lax.broadcasted_iota(jnp.int32, (n, tk), 0)
            for hh, cols in enumerate(heads):
                q, do = qs[hh][qrows], dos[hh][qrows]
                ks = k_ref[krows, cols]
                l = lax.dot_general(q, ks, _NT, preferred_element_type=F32)
                sp, e, e1 = _softplus(l)
                if masked:
                    sp = jnp.where(causal, sp, 0.0)
                carry = jnp.sum(jnp.where(lane == hh * HEAD_DIM + kb, c_ref[qrows, :], 0.0), axis=1, keepdims=True)
                after = carry + _dot2(sp, u_after)
                a = jnp.exp(l - (sp + after))
                if masked:
                    a = jnp.where(causal, a, 0.0)
                dv_acc[krows, cols] += lax.dot_general(a.astype(BF16), do, _TN, preferred_element_type=F32)
                g = a * lax.dot_general(do, v_ref[krows, cols], _NT, preferred_element_type=F32)
                c = gcarry_ref[hh, qrows, :] + lax.dot_general(g.astype(BF16), u_before, _NN,
                                                                preferred_element_type=F32)
                r = pl.reciprocal(e1, approx=True)
                beta = jnp.where(l >= 0.0, r, e * r)
                dl = g - (g + c) * beta
                if masked:
                    dl = jnp.where(causal, dl, 0.0)
                dl_b = dl.astype(BF16)
                dq_acc[qrows, cols] += lax.dot_general(dl_b, ks, _NN, preferred_element_type=F32)
                dk_acc[krows, cols] += lax.dot_general(dl_b, q, _TN, preferred_element_type=F32)
                gcarry_ref[hh, qrows, :] = c[:, tk - 1:tk] + g[:, tk - 1:tk]

        def alive(kb, r0, n):
            lane = lax.broadcasted_iota(jnp.int32, (n, 128), 1)
            mine = (lane == kb) | (lane == HEAD_DIM + kb)
            return jnp.min(jnp.where(mine, c_ref[r0:r0 + n, :], jnp.inf)) <= DEAD_AFTER

        def step(kb, _):
            for r0 in range(0, tq, ATT_NEAR):
                @pl.when(alive(kb, r0, ATT_NEAR))
                def _():
                    block(kb, r0, ATT_NEAR, False)

            return 0

        lax.fori_loop(0, n_sub * qi, step, 0)
        for j in range(n_sub):
            kb, r0 = n_sub * qi + j, j * tk
            near = min(ATT_NEAR, tq - r0)
            if r0 + near < tq:
                @pl.when(alive(kb, r0 + near, tq - r0 - near))
                def _():
                    block(kb, r0 + near, tq - r0 - near, False)
            block(kb, r0, near, True)
        dq_ref[...] = (dq_acc[...] * scale).astype(BF16)

        @pl.when(qi == nq - 1)
        def _():
            dk_ref[...] = dk_acc[...].astype(BF16)
            dv_ref[...] = dv_acc[...].astype(BF16)

        finish()

    qspec = pl.BlockSpec((tq, 128), lambda b, hp, qi: (b * nq + qi, hp))
    kvspec = pl.BlockSpec((seq, 128), lambda b, hp, qi: (b, hp))
    return pl.pallas_call(
        body, name="attn_bwd", grid=grid,
        in_specs=[qspec,
                  pl.BlockSpec((seq, 128), lambda b, hp, qi: (b, sec + hp)),
                  pl.BlockSpec((seq, 128), lambda b, hp, qi: (b, 2 * sec + hp)),
                  pl.BlockSpec((tq, 128), lambda b, hp, qi: (b * nq + qi, 3 * sec + hp)),
                  qspec,
                  pl.BlockSpec((None, tq, 128), lambda b, hp, qi: (hp, b * nq + qi, 0)),
                  qspec] + [_HBM] * n_send,
        out_specs=[qspec, kvspec, kvspec, qspec] + [_HBM] * n_send,
        out_shape=[jax.ShapeDtypeStruct((m, d), BF16)] * 4 + [jax.ShapeDtypeStruct(s.shape, s.dtype) for s in send],
        scratch_shapes=[pltpu.VMEM((seq, 128), F32), pltpu.VMEM((seq, 128), F32), pltpu.VMEM((tq, 128), F32),
                        pltpu.VMEM((2, tq, 1), F32)] + _travel_scratch(n_send),
        compiler_params=_params("arbitrary", "arbitrary", "arbitrary"),
    )(hb, hb, hb, h, o, carries, da, *send)


SUBLANES = 8


def _by_residue(taps):
    groups = []
    for res in range(SUBLANES):
        mine = [(off - res, k) for off, k in taps if off % SUBLANES == res]
        if mine:
            groups.append((res, max(a for a, _ in mine), mine))
    return groups


def _staged(src_ref, stage_ref, res, n):
    buf = stage_ref.at[res % 2]
    if res == 0:
        return src_ref
    buf[0:n, :] = src_ref[pl.ds(res, n), :]
    return buf


def _tap_sum(src_ref, stage_ref, w_ref, n_rows, taps, acc):
    for res, reach, mine in _by_residue(taps):
        shifted = _staged(src_ref, stage_ref, res, n_rows + reach)
        for a, k in mine:
            acc = acc + w_ref[k:k + 1, :] * shifted[a:a + n_rows, :]
    return acc


_CONV_TAPS = [(CV_HALO - CV_KERNEL + 1 + k, k) for k in range(CV_KERNEL)]
_CONV_TAPS_T = [(CV_KERNEL - 1 - k, k) for k in range(CV_KERNEL)]


def _conv_rows(pad_ref, stage_ref, w_ref, b_ref, n_rows):
    bias = jnp.broadcast_to(b_ref[...], (n_rows, b_ref.shape[1]))
    return _tap_sum(pad_ref, stage_ref, w_ref, n_rows, _CONV_TAPS, bias)


def _conv_fwd(h, conv_w, conv_b, ln_g, ln_b, nb, seq):
    m = nb * seq
    d = conv_b.shape[1]
    t = CONV_TILE
    tiles = seq // t
    hpt = t // CV_HALO

    def body(cv_ref, cg_ref, pv_ref, pg_ref, z_ref, w_ref, b_ref, g_ref, bb_ref, a_ref, pad_ref, stage_ref):
        first = pl.program_id(0) % tiles == 0
        pad_ref[0:CV_HALO, :] = jnp.where(first, 0.0, pv_ref[...] * _sigmoid(pg_ref[...]))
        pad_ref[CV_HALO:, :] = cv_ref[...] * _sigmoid(cg_ref[...])
        xhat, _ = _ln_stats(_conv_rows(pad_ref, stage_ref, w_ref, b_ref, t))
        s, _ = _silu_and_grad(xhat * g_ref[...] + bb_ref[...])
        sz, _ = _silu_and_grad(z_ref[...])
        a_ref[...] = (s * sz).astype(BF16)

    def main(c):
        return pl.BlockSpec((t, d), lambda i: (i, c))

    def prev(c):
        return pl.BlockSpec((CV_HALO, d), lambda i: (jnp.maximum(i * hpt - 1, 0), c))

    vec = pl.BlockSpec((1, d), lambda i: (0, 0))
    return pl.pallas_call(
        body, name="conv_fwd", grid=(m // t,),
        in_specs=[main(4), main(5), prev(4), prev(5), main(6),
                  pl.BlockSpec((CV_HALO, d), lambda i: (0, 0)), vec, vec, vec],
        out_specs=pl.BlockSpec((t, d), lambda i: (i, 0)),
        out_shape=jax.ShapeDtypeStruct((m, d), BF16),
        scratch_shapes=[pltpu.VMEM((CV_HALO + t, d), F32), pltpu.VMEM((2, CV_HALO + t, d), F32)],
        compiler_params=_params("parallel"),
    )(h, h, h, h, h, conv_w, conv_b, ln_g, ln_b)


def _conv_bwd(h, da, conv_w, conv_b, ln_g, ln_b, nb, seq):
    m = nb * seq
    d = conv_b.shape[1]
    t = CONV_TILE
    tiles = seq // t
    hpt = t // CV_HALO
    last_halo = m // CV_HALO - 1

    def body(cv_ref, cg_ref, pv_ref, pg_ref, nv_ref, ng_ref, z_ref, nz_ref, da_ref, nda_ref,
             w_ref, b_ref, g_ref, bb_ref, dh_ref, dw_ref, db_ref, dg_ref, dbb_ref, pad_ref, dc_ref, stage_ref):
        i = pl.program_id(0)

        @pl.when(i == 0)
        def _():
            dw_ref[...] = jnp.zeros_like(dw_ref)
            db_ref[...] = jnp.zeros_like(db_ref)
            dg_ref[...] = jnp.zeros_like(dg_ref)
            dbb_ref[...] = jnp.zeros_like(dbb_ref)

        first = i % tiles == 0
        last = i % tiles == tiles - 1
        cv = cv_ref[...]
        sg = _sigmoid(cg_ref[...])
        pad_ref[0:CV_HALO, :] = jnp.where(first, 0.0, pv_ref[...] * _sigmoid(pg_ref[...]))
        pad_ref[CV_HALO:CV_HALO + t, :] = cv * sg
        pad_ref[CV_HALO + t:, :] = nv_ref[...] * _sigmoid(ng_ref[...])

        def rows_bwd(conv_out, z, dav):
            xhat, rstd = _ln_stats(conv_out)
            s, ds = _silu_and_grad(xhat * g_ref[...] + bb_ref[...])
            sz, dsz = _silu_and_grad(z)
            dn = dav * sz * ds
            return _ln_bwd(dn * g_ref[...], xhat, rstd), dav * s * dsz, dn, xhat

        conv_all = _conv_rows(pad_ref, stage_ref, w_ref, b_ref, t + CV_HALO)
        dc, dz, dn, xhat = rows_bwd(conv_all[:t], z_ref[...], da_ref[...])
        dc_next, _, _, _ = rows_bwd(conv_all[t:], nz_ref[...], nda_ref[...])
        dc_ref[0:t, :] = dc
        dc_ref[t:, :] = jnp.where(last, 0.0, dc_next)
        dg_ref[...] += jnp.sum(dn * xhat, axis=0, keepdims=True)
        dbb_ref[...] += jnp.sum(dn, axis=0, keepdims=True)
        db_ref[...] += jnp.sum(dc, axis=0, keepdims=True)
        for res, reach, mine in _by_residue(_CONV_TAPS):
            shifted = _staged(pad_ref, stage_ref, res, t + reach)
            for a, k in mine:
                dw_ref[k:k + 1, :] += jnp.sum(dc * shifted[a:a + t, :], axis=0, keepdims=True)
        du = _tap_sum(dc_ref, stage_ref, w_ref, t, _CONV_TAPS_T, jnp.zeros((t, d), F32))
        dh_ref[:, 0:d] = (du * sg).astype(BF16)
        dh_ref[:, d:2 * d] = (du * cv * sg * (1.0 - sg)).astype(BF16)
        dh_ref[:, 2 * d:] = dz.astype(BF16)

    def main(c):
        return pl.BlockSpec((t, d), lambda i: (i, c))

    def prev(c):
        return pl.BlockSpec((CV_HALO, d), lambda i: (jnp.maximum(i * hpt - 1, 0), c))

    def nxt(c):
        return pl.BlockSpec((CV_HALO, d), lambda i: (jnp.minimum((i + 1) * hpt, last_halo), c))

    vec = pl.BlockSpec((1, d), lambda i: (0, 0))
    taps = pl.BlockSpec((CV_HALO, d), lambda i: (0, 0))
    return pl.pallas_call(
        body, name="conv_bwd", grid=(m // t,),
        in_specs=[main(4), main(5), prev(4), prev(5), nxt(4), nxt(5), main(6), nxt(6), main(0), nxt(0),
                  taps, vec, vec, vec],
        out_specs=[pl.BlockSpec((t, 3 * d), lambda i: (i, 0)), taps, vec, vec, vec],
        out_shape=[jax.ShapeDtypeStruct((m, 3 * d), BF16), jax.ShapeDtypeStruct((CV_HALO, d), F32),
                   jax.ShapeDtypeStruct((1, d), F32), jax.ShapeDtypeStruct((1, d), F32),
                   jax.ShapeDtypeStruct((1, d), F32)],
        scratch_shapes=[pltpu.VMEM((t + 2 * CV_HALO, d), F32), pltpu.VMEM((t + CV_HALO, d), F32),
                        pltpu.VMEM((2, t + 2 * CV_HALO, d), F32)],
        compiler_params=_params("arbitrary"),
    )(h, h, h, h, h, h, h, h, da, da, conv_w, conv_b, ln_g, ln_b)


_HBM = pl.BlockSpec(memory_space=pltpu.HBM)


def _place():
    return lax.axis_index("x"), lax.axis_index("y"), lax.axis_index("c")


def _slot(p):
    return 4 * p[0] + 2 * p[1] + p[2]


def _gather_mm_in_fwd(h0b, w_shard, riders):
    m, d = h0b.shape
    ns = w_shard.shape[1]
    n_i = m // IN_TILE
    n_steps = N_DEV
    n_arr = 1 + len(riders)

    def places():
        x, y, c = _place()
        chips = [(1 - x, y), (x, 1 - y), (1 - x, 1 - y)]
        return (x, y, c), (x, y, 1 - c), chips

    me, sibling, chips = places()
    order = [me, sibling] + [(*chip, me[2]) for chip in chips] + [(*chip, sibling[2]) for chip in chips]
    order = jnp.stack([_slot(p) for p in order]).astype(jnp.int32)

    def body(order_ref, h_ref, *rest):
        src_refs, rest = rest[:n_arr], rest[n_arr:]
        o32_ref, o16_ref = rest[:2]
        got_refs, rest = rest[2:2 + n_arr], rest[2 + n_arr:]
        wbuf, send_sems, recv_sems, own_sems, load_sems = rest
        s, i = pl.program_id(0), pl.program_id(1)
        me, sibling, chips = places()
        c = me[2]
        wg_ref = got_refs[0]

        def remote(t, k, block, to, own_src=False):
            dst = got_refs[t].at[_slot(block)]
            return pltpu.make_async_remote_copy(
                src_ref=src_refs[t] if own_src else dst, dst_ref=dst, send_sem=send_sems.at[7 * t + k],
                recv_sem=recv_sems.at[7 * t + k], device_id=to, device_id_type=MESH)

        def load(step):
            return pltpu.make_async_copy(wg_ref.at[order_ref[step]], wbuf.at[step % 2], load_sems.at[step % 2])

        own = [pltpu.make_async_copy(src_refs[t], got_refs[t].at[_slot(me)], own_sems.at[t]) for t in range(n_arr)]

        @pl.when((s == 0) & (i == 0))
        def _():
            for t in range(n_arr):
                own[t].start()
                remote(t, 0, me, sibling, own_src=True).start()
                for j, chip in enumerate(chips):
                    remote(t, 1 + j, me, (*chip, c), own_src=True).start()
            first = pltpu.make_async_copy(src_refs[0], wbuf.at[0], load_sems.at[0])
            first.start()
            first.wait()

        for nxt in range(1, n_steps):
            @pl.when((s == nxt - 1) & (i == n_i - 1))
            def _():
                if nxt == 1:
                    remote(0, 0, sibling, me).wait_recv()
                elif nxt <= 4:
                    j = nxt - 2
                    remote(0, 1 + j, (*chips[j], c), me).wait_recv()
                    remote(0, 4 + j, (*chips[j], c), sibling).start()
                else:
                    j = nxt - 5
                    remote(0, 4 + j, (*chips[j], 1 - c), me).wait_recv()
                if 3 <= nxt <= 5:
                    j = nxt - 3
                    for t in range(1, n_arr):
                        remote(t, 1 + j, (*chips[j], c), me).wait_recv()
                        remote(t, 4 + j, (*chips[j], c), sibling).start()
                load(nxt).start()

        for step in range(1, n_steps):
            @pl.when((s == step) & (i == 0))
            def _():
                load(step).wait()

        part = lax.dot_general(h_ref[...], wbuf[s % 2], _NN, preferred_element_type=F32)
        o32_ref[...] = part
        o16_ref[...] = part.astype(BF16)

        @pl.when((s == n_steps - 1) & (i == n_i - 1))
        def _():
            for t in range(n_arr):
                if t:
                    remote(t, 0, sibling, me).wait_recv()
                    for j, chip in enumerate(chips):
                        remote(t, 4 + j, (*chip, 1 - c), me).wait_recv()
                remote(t, 0, me, sibling, own_src=True).wait_send()
                for j, chip in enumerate(chips):
                    remote(t, 1 + j, me, (*chip, c), own_src=True).wait_send()
                    remote(t, 4 + j, (*chip, c), sibling).wait_send()
                own[t].wait()

    arrays = [w_shard] + list(riders)
    out = pl.BlockSpec((IN_TILE, ns), lambda s, i, order: (i, order[s]))
    grid_spec = pltpu.PrefetchScalarGridSpec(
        num_scalar_prefetch=1, grid=(n_steps, n_i),
        in_specs=[pl.BlockSpec((IN_TILE, d), lambda s, i, order: (i, 0))] + [_HBM] * n_arr,
        out_specs=[out, out] + [_HBM] * n_arr,
        scratch_shapes=[pltpu.VMEM((2, d, ns), BF16), pltpu.SemaphoreType.DMA((7 * n_arr,)),
                        pltpu.SemaphoreType.DMA((7 * n_arr,)), pltpu.SemaphoreType.DMA((n_arr,)),
                        pltpu.SemaphoreType.DMA((2,))])
    return pl.pallas_call(
        body, name="gather_mm_in_fwd", grid_spec=grid_spec,
        out_shape=[jax.ShapeDtypeStruct((m, N_DEV * ns), F32), jax.ShapeDtypeStruct((m, N_DEV * ns), BF16)]
        + [jax.ShapeDtypeStruct((N_DEV,) + a.shape, a.dtype) for a in arrays],
        compiler_params=_params("arbitrary", "arbitrary"),
    )(order, h0b, *arrays)


def _exchange_partials(parts, whole):
    n_parts = len(parts)
    arrays = list(parts) + list(whole)
    n = len(arrays)

    def body(*refs):
        ins, outs = refs[:n], refs[n:2 * n]
        send_sems, recv_sems, local_sems = refs[2 * n:]
        x, y, c = _place()
        me = (x, y, c)

        def src_for(a, p):
            return ins[a].at[_slot(p)] if a < n_parts else ins[a]

        def copy(a, k, peer):
            return pltpu.make_async_remote_copy(
                src_ref=src_for(a, peer), dst_ref=outs[a].at[_slot(me)], send_sem=send_sems.at[7 * a + k],
                recv_sem=recv_sems.at[7 * a + k], device_id=peer, device_id_type=MESH)

        def landed(a, k, peer):
            return pltpu.make_async_remote_copy(
                src_ref=src_for(a, peer), dst_ref=outs[a].at[_slot(peer)], send_sem=send_sems.at[7 * a + k],
                recv_sem=recv_sems.at[7 * a + k], device_id=peer, device_id_type=MESH)

        peers = []
        for k in range(1, N_DEV):
            fx, fy, fc = (k >> 2) & 1, (k >> 1) & 1, k & 1
            peers.append((1 - x if fx else x, 1 - y if fy else y, 1 - c if fc else c))
        mine = [pltpu.make_async_copy(src_for(a, me), outs[a].at[_slot(me)], local_sems.at[a]) for a in range(n)]
        for cp in mine:
            cp.start()
        sent = [copy(a, k, peer) for a in range(n) for k, peer in enumerate(peers)]
        for cp in sent:
            cp.start()
        for a in range(n):
            for k, peer in enumerate(peers):
                landed(a, k, peer).wait_recv()
        for cp in sent:
            cp.wait_send()
        for cp in mine:
            cp.wait()

    out_shape = [jax.ShapeDtypeStruct(p.shape, p.dtype) for p in parts]
    out_shape += [jax.ShapeDtypeStruct((N_DEV,) + w.shape, w.dtype) for w in whole]
    return pl.pallas_call(
        body, name="exchange_grad_partials",
        in_specs=[_HBM] * n, out_specs=[_HBM] * n, out_shape=out_shape,
        scratch_shapes=[pltpu.SemaphoreType.DMA((7 * n,)), pltpu.SemaphoreType.DMA((7 * n,)),
                        pltpu.SemaphoreType.DMA((n,))],
    )(*arrays)


def _peers_of(x, y, c):
    peers = []
    for k in range(1, N_DEV):
        fx, fy, fc = (k >> 2) & 1, (k >> 1) & 1, k & 1
        peers.append((1 - x if fx else x, 1 - y if fy else y, 1 - c if fc else c))
    return peers


def _with_own(landed, own, me):
    return lax.dynamic_update_index_in_dim(landed, own, me, 0)


def _adamw(name, slabs, w, mom, var, rows):
    r, c = w.shape
    n_slabs = len(slabs)
    per_slab = r // n_slabs // rows

    def body(*refs):
        p_refs = refs[:n_slabs]
        w_ref, m_ref, v_ref, g_ref, d_ref, nm_ref, nv_ref = refs[n_slabs:]
        slab = pl.program_id(0) // per_slab
        g = None
        for p in range(N_DEV):
            part = p_refs[0][p]
            for k in range(1, n_slabs):
                part = jnp.where(slab == k, p_refs[k][p], part)
            g = part.astype(F32) if g is None else g + part.astype(F32)
        m_new = ADAM_B1 * m_ref[...] + (1.0 - ADAM_B1) * g
        v_new = ADAM_B2 * v_ref[...] + (1.0 - ADAM_B2) * (g * g)
        m_hat = m_new / (1.0 - ADAM_B1 ** ADAM_STEP)
        v_hat = v_new / (1.0 - ADAM_B2 ** ADAM_STEP)
        g_ref[...] = g
        d_ref[...] = -ADAM_LR * (m_hat / (jnp.sqrt(v_hat) + ADAM_EPS) + ADAM_WD * w_ref[...])
        nm_ref[...] = m_new
        nv_ref[...] = v_new

    def slab_spec(k):
        return pl.BlockSpec((N_DEV, rows, c), lambda i: (0, jnp.clip(i - k * per_slab, 0, per_slab - 1), 0))

    blk = pl.BlockSpec((rows, c), lambda i: (i, 0))
    return pl.pallas_call(
        body, name=name, grid=(r // rows,),
        in_specs=[slab_spec(k) for k in range(n_slabs)] + [blk, blk, blk],
        out_specs=[blk] * 4, out_shape=[jax.ShapeDtypeStruct((r, c), F32)] * 4,
        compiler_params=_params("parallel"),
    )(*slabs, w, mom, var)


def kernel(x, ln_in_g, ln_in_b, w_in, w_sb_proj, conv_w, conv_b, conv_ln_g, conv_ln_b, w_cv_proj, w_out, ln_post_g, ln_post_b, loss_target, m_ln_in_g, m_ln_in_b, m_w_in, m_w_sb_proj, m_conv_w, m_conv_b, m_conv_ln_g, m_conv_ln_b, m_w_cv_proj, m_w_out, m_ln_post_g, m_ln_post_b, v_ln_in_g, v_ln_in_b, v_w_in, v_w_sb_proj, v_conv_w, v_conv_b, v_conv_ln_g, v_conv_ln_b, v_w_cv_proj, v_w_out, v_ln_post_g, v_ln_post_b):
    nb, seq, d = x.shape
    m = nb * seq
    x2d = x.reshape(m, d)
    target = loss_target.reshape(m, d)
    rs = d // N_DEV
    pad_taps = ((0, CV_HALO - CV_KERNEL), (0, 0))

    proj_shards = jnp.stack([w_sb_proj[0], w_cv_proj[0], w_out[0]]).astype(BF16)
    conv_w_shard = jnp.pad(conv_w[0], pad_taps)
    me = _slot(_place())
    own = lambda parts: lax.dynamic_index_in_dim(parts, me, 0, keepdims=False)

    h0, h0b = _ln_in_fwd(x2d, ln_in_g.reshape(1, d), ln_in_b.reshape(1, d))
    h, hb, w_in_g = _gather_mm_in_fwd(h0b, w_in[0].astype(BF16), [])
    o, a_sb, carries, proj_g, conv_w_g = _attn_fwd(hb, h, nb, seq, [proj_shards, conv_w_shard])
    proj_g = _with_own(proj_g, proj_shards, me)
    conv_w_g = _with_own(conv_w_g, conv_w_shard, me)
    w_sb_g = proj_g[:, 0].reshape(d, d)
    w_cv_g = proj_g[:, 1].reshape(d, d)
    w_out_g = proj_g[:, 2].reshape(d, d)
    conv_w_full = conv_w_g.transpose(1, 0, 2).reshape(CV_HALO, d)
    a_cv = _conv_fwd(h, conv_w_full, conv_b, conv_ln_g, conv_ln_b, nb, seq)
    y_sb = _mm_nn("mm_sb_fwd", a_sb, w_sb_g)
    y_cv = _mm_nn("mm_cv_fwd", a_cv, w_cv_g)
    merged = _merge_fwd(h, y_sb, y_cv)
    mo = _mm_nn("mm_out_fwd", merged, w_out_g)
    loss_part, d_pre, d_pre_b, dg_post, db_post = _post_ln_loss(h0, mo, target, ln_post_g, ln_post_b)

    dw_out = _mm_tn("mm_out_bwd_w", merged, d_pre_b, BF16)
    d_merged = _mm_nt("mm_out_bwd_x", d_pre_b, w_out_g)
    dy_sb, dy_cv, d_gates = _merge_bwd(h, y_sb, y_cv, d_merged)
    dw_sb = _mm_tn("mm_sb_bwd_w", a_sb, dy_sb, BF16)
    dw_cv = _mm_tn("mm_cv_bwd_w", a_cv, dy_cv, BF16)
    da_sb = _mm_nt("mm_sb_bwd_x", dy_sb, w_sb_g)
    da_cv = _mm_nt("mm_cv_bwd_x", dy_cv, w_cv_g)
    dproj = jnp.stack([dw_sb.reshape(N_DEV, rs, d), dw_cv.reshape(N_DEV, rs, d), dw_out.reshape(N_DEV, rs, d)], axis=1)
    dq, dk, dv, dz_sb, r_proj = _attn_bwd(hb, h, o, carries, da_sb, nb, seq, [dproj])
    d_conv3, dconv_w, dconv_b, dconv_ln_g, dconv_ln_b = _conv_bwd(
        h, da_cv, conv_w_full, conv_b, conv_ln_g, conv_ln_b, nb, seq)
    dhb = jnp.concatenate([dq, dk, dv, dz_sb, d_conv3, d_gates], axis=1)
    dconv_w_parts = dconv_w.reshape(CV_HALO, N_DEV, d // N_DEV).transpose(1, 0, 2)
    dw_in_a, r_conv = _mm_in_bwd_w(h0b, dhb, 0, [dconv_w_parts])
    dw_in_b, r_in_a = _mm_in_bwd_w(h0b, dhb, 1, [dw_in_a])
    dh_mm, r_in_b = _mm_in_bwd_x(dhb, w_in_g, [dw_in_b])
    dx, dg_in, db_in = _ln_in_bwd(x2d, d_pre, dh_mm, ln_in_g.reshape(1, d))
    r_proj = _with_own(r_proj, own(dproj), me)
    r_conv = _with_own(r_conv, own(dconv_w_parts), me)
    r_in = [_with_own(r_in_a, own(dw_in_a), me), _with_own(r_in_b, own(dw_in_b), me)]

    small = jnp.concatenate([dg_in, db_in, dconv_b, dconv_ln_g, dconv_ln_b, dg_post, db_post,
                             jnp.broadcast_to(loss_part, (1, d))], axis=0)
    (r_small,) = _exchange_partials([], [small])
    loss = jnp.sum(r_small[:, -1, 0])

    g_in, d_in, nm_in, nv_in = _adamw("adamw_w_in", r_in, w_in[0], m_w_in[0], v_w_in[0], 128)
    stack3 = lambda a, b, c: jnp.concatenate([a[0], b[0], c[0]], axis=0)
    g_pr, d_pr, nm_pr, nv_pr = _adamw(
        "adamw_proj", [r_proj.reshape(N_DEV, 3 * rs, d)], stack3(w_sb_proj, w_cv_proj, w_out),
        stack3(m_w_sb_proj, m_w_cv_proj, m_w_out), stack3(v_w_sb_proj, v_w_cv_proj, v_w_out), 3 * rs)
    padc = lambda a: jnp.pad(a[0], pad_taps)
    g_cw, d_cw, nm_cw, nv_cw = _adamw("adamw_conv_w", [r_conv], padc(conv_w), padc(m_conv_w), padc(v_conv_w), CV_HALO)
    vecs = lambda *a: jnp.concatenate([t.reshape(1, d) for t in a] + [jnp.ones((1, d), F32)], axis=0)
    g_sm, d_sm, nm_sm, nv_sm = _adamw(
        "adamw_vectors", [r_small],
        vecs(ln_in_g, ln_in_b, conv_b, conv_ln_g, conv_ln_b, ln_post_g, ln_post_b),
        vecs(m_ln_in_g, m_ln_in_b, m_conv_b, m_conv_ln_g, m_conv_ln_b, m_ln_post_g, m_ln_post_b),
        vecs(v_ln_in_g, v_ln_in_b, v_conv_b, v_conv_ln_g, v_conv_ln_b, v_ln_post_g, v_ln_post_b), 8)


    def leaves(big, pr, cw, sm):
        return (sm[0], sm[1], big[None], pr[None, 0:rs], cw[None, :CV_KERNEL], sm[2:3], sm[3:4], sm[4:5],
                pr[None, rs:2 * rs], pr[None, 2 * rs:], sm[5:6], sm[6:7])

    return (loss, dx.reshape(nb, seq, d), *leaves(g_in, g_pr, g_cw, g_sm), *leaves(d_in, d_pr, d_cw, d_sm),
            *leaves(nm_in, nm_pr, nm_cw, nm_sm), *leaves(nv_in, nv_pr, nv_cw, nv_sm))
```

```python
import functools
import math

import jax
import jax.numpy as jnp
from jax import lax
from jax.experimental import pallas as pl
from jax.experimental.pallas import tpu as pltpu

F32 = jnp.float32
BF16 = jnp.bfloat16
MESH = pl.DeviceIdType.MESH

N_DEV = 8
SB_HEADS = 16
HEAD_DIM = 64
CV_KERNEL = 31
CV_HALO = 32
LN_EPS = 1e-5
DEEPNORM_ALPHA = 2.0 ** 0.25
ADAM_LR, ADAM_B1, ADAM_B2, ADAM_EPS, ADAM_WD, ADAM_STEP = 0.001, 0.9, 0.999, 1e-08, 0.01, 10

ATT_TK = 256
ATT_TQ = 1024
ATT_NEAR = 512
ROW_TILE = 512
IN_TILE = 1024
CONV_TILE = 256
VMEM_LIMIT = 56 * 1024 * 1024


def _params(*sem):
    return pltpu.CompilerParams(dimension_semantics=sem, vmem_limit_bytes=VMEM_LIMIT)


def _sigmoid(x):
    return 1.0 / (1.0 + jnp.exp(-x))


def _silu_and_grad(x):
    s = _sigmoid(x)
    return x * s, s * (1.0 + x * (1.0 - s))


def _ln_stats(x):
    mu = jnp.mean(x, axis=-1, keepdims=True)
    xc = x - mu
    var = jnp.mean(xc * xc, axis=-1, keepdims=True)
    rstd = lax.rsqrt(var + LN_EPS)
    return xc * rstd, rstd


def _ln_bwd(dxhat, xhat, rstd):
    m1 = jnp.mean(dxhat, axis=-1, keepdims=True)
    m2 = jnp.mean(dxhat * xhat, axis=-1, keepdims=True)
    return rstd * (dxhat - m1 - xhat * m2)


def _ln_in_fwd(x2d, g, b):
    m, d = x2d.shape

    def body(x_ref, g_ref, b_ref, h_ref, hb_ref):
        xhat, _ = _ln_stats(x_ref[...])
        y = xhat * g_ref[...] + b_ref[...]
        h_ref[...] = y
        hb_ref[...] = y.astype(BF16)

    row = pl.BlockSpec((ROW_TILE, d), lambda i: (i, 0))
    vec = pl.BlockSpec((1, d), lambda i: (0, 0))
    return pl.pallas_call(
        body, name="ln_in_fwd", grid=(m // ROW_TILE,),
        in_specs=[row, vec, vec], out_specs=[row, row],
        out_shape=[jax.ShapeDtypeStruct((m, d), F32), jax.ShapeDtypeStruct((m, d), BF16)],
        compiler_params=_params("parallel"),
    )(x2d, g, b)


def _ln_in_bwd(x2d, d_pre, dh_mm, g):
    m, d = x2d.shape

    def body(x_ref, dp_ref, dm_ref, g_ref, dx_ref, dg_ref, db_ref):
        @pl.when(pl.program_id(0) == 0)
        def _():
            dg_ref[...] = jnp.zeros_like(dg_ref)
            db_ref[...] = jnp.zeros_like(db_ref)

        xhat, rstd = _ln_stats(x_ref[...])
        dh = DEEPNORM_ALPHA * dp_ref[...] + dm_ref[...]
        dg_ref[...] += jnp.sum(dh * xhat, axis=0, keepdims=True)
        db_ref[...] += jnp.sum(dh, axis=0, keepdims=True)
        dx_ref[...] = _ln_bwd(dh * g_ref[...], xhat, rstd)

    row = pl.BlockSpec((ROW_TILE, d), lambda i: (i, 0))
    vec = pl.BlockSpec((1, d), lambda i: (0, 0))
    return pl.pallas_call(
        body, name="ln_in_bwd", grid=(m // ROW_TILE,),
        in_specs=[row, row, row, vec], out_specs=[row, vec, vec],
        out_shape=[jax.ShapeDtypeStruct((m, d), F32), jax.ShapeDtypeStruct((1, d), F32),
                   jax.ShapeDtypeStruct((1, d), F32)],
        compiler_params=_params("arbitrary"),
    )(x2d, d_pre, dh_mm, g)


def _merge_out_fwd(h, y_sb, y_cv, w_out):
    m, d = y_sb.shape

    def body(gs_ref, gc_ref, ys_ref, yc_ref, w_ref, merged_ref, mo_ref):
        merged = (_sigmoid(gs_ref[...]) * ys_ref[...] + _sigmoid(gc_ref[...]) * yc_ref[...]).astype(BF16)
        merged_ref[...] = merged
        mo_ref[...] = lax.dot_general(merged, w_ref[...], (((1,), (0,)), ((), ())), preferred_element_type=F32)

    row = pl.BlockSpec((ROW_TILE, d), lambda i: (i, 0))
    return pl.pallas_call(
        body, name="merge_out_fwd", grid=(m // ROW_TILE,),
        in_specs=[pl.BlockSpec((ROW_TILE, d), lambda i: (i, 7)), pl.BlockSpec((ROW_TILE, d), lambda i: (i, 8)), row, row,
                  pl.BlockSpec((d, d), lambda i: (0, 0))],
        out_specs=[row, row],
        out_shape=[jax.ShapeDtypeStruct((m, d), BF16), jax.ShapeDtypeStruct((m, d), F32)],
        compiler_params=_params("parallel"),
    )(h, h, y_sb, y_cv, w_out)


def _merge_bwd(h, y_sb, y_cv, dm):
    m, d = y_sb.shape

    def body(gs_ref, gc_ref, ys_ref, yc_ref, dm_ref, dys_ref, dyc_ref, dgate_ref):
        dmv = dm_ref[...]
        ss = _sigmoid(gs_ref[...])
        sc = _sigmoid(gc_ref[...])
        dys_ref[...] = (ss * dmv).astype(BF16)
        dyc_ref[...] = (sc * dmv).astype(BF16)
        dgate_ref[:, :d] = (dmv * ys_ref[...] * ss * (1.0 - ss)).astype(BF16)
        dgate_ref[:, d:] = (dmv * yc_ref[...] * sc * (1.0 - sc)).astype(BF16)

    row = pl.BlockSpec((ROW_TILE, d), lambda i: (i, 0))
    return pl.pallas_call(
        body, name="merge_bwd", grid=(m // ROW_TILE,),
        in_specs=[pl.BlockSpec((ROW_TILE, d), lambda i: (i, 7)), pl.BlockSpec((ROW_TILE, d), lambda i: (i, 8)), row, row, row],
        out_specs=[row, row, pl.BlockSpec((ROW_TILE, 2 * d), lambda i: (i, 0))],
        out_shape=[jax.ShapeDtypeStruct((m, d), BF16), jax.ShapeDtypeStruct((m, d), BF16),
                   jax.ShapeDtypeStruct((m, 2 * d), BF16)],
        compiler_params=_params("parallel"),
    )(h, h, y_sb, y_cv, dm)


def _post_ln_loss(h0, mo, target, g, b):
    m, d = h0.shape

    def body(h_ref, mo_ref, t_ref, g_ref, b_ref, loss_ref, dp_ref, dpb_ref, dg_ref, db_ref):
        @pl.when(pl.program_id(0) == 0)
        def _():
            loss_ref[...] = jnp.zeros_like(loss_ref)
            dg_ref[...] = jnp.zeros_like(dg_ref)
            db_ref[...] = jnp.zeros_like(db_ref)

        xhat, rstd = _ln_stats(DEEPNORM_ALPHA * h_ref[...] + mo_ref[...])
        err = xhat * g_ref[...] + b_ref[...] - t_ref[...]
        per_row = jnp.mean(err * err, axis=-1, keepdims=True)
        loss_ref[...] += 0.5 * jnp.sum(per_row, axis=0, keepdims=True)
        dy = err * (1.0 / d)
        dg_ref[...] += jnp.sum(dy * xhat, axis=0, keepdims=True)
        db_ref[...] += jnp.sum(dy, axis=0, keepdims=True)
        dp = _ln_bwd(dy * g_ref[...], xhat, rstd)
        dp_ref[...] = dp
        dpb_ref[...] = dp.astype(BF16)

    row = pl.BlockSpec((ROW_TILE, d), lambda i: (i, 0))
    vec = pl.BlockSpec((1, d), lambda i: (0, 0))
    one = pl.BlockSpec((1, 1), lambda i: (0, 0))
    return pl.pallas_call(
        body, name="post_ln_loss", grid=(m // ROW_TILE,),
        in_specs=[row, row, row, vec, vec], out_specs=[one, row, row, vec, vec],
        out_shape=[jax.ShapeDtypeStruct((1, 1), F32), jax.ShapeDtypeStruct((m, d), F32),
                   jax.ShapeDtypeStruct((m, d), BF16), jax.ShapeDtypeStruct((1, d), F32),
                   jax.ShapeDtypeStruct((1, d), F32)],
        compiler_params=_params("arbitrary"),
    )(h0, mo, target, g, b)


_NN = (((1,), (0,)), ((), ()))
_NT = (((1,), (1,)), ((), ()))
_TN = (((0,), (0,)), ((), ()))


def _grid_ends(grid):
    ids = [pl.program_id(ax) for ax in range(len(grid))]
    first = functools.reduce(jnp.logical_and, [i == 0 for i in ids])
    last = functools.reduce(jnp.logical_and, [i == n - 1 for i, n in zip(ids, grid)])
    return first, last


def _travel(grid, src_refs, land_refs, send_sems, recv_sems, whole):
    first, last = _grid_ends(grid)
    x, y, c = _place()
    n = len(src_refs)

    def copy(t, k, peer, lands_at):
        return pltpu.make_async_remote_copy(
            src_ref=src_refs[t] if whole else src_refs[t].at[_slot(peer)],
            dst_ref=land_refs[t].at[_slot(lands_at)], send_sem=send_sems.at[7 * t + k],
            recv_sem=recv_sems.at[7 * t + k], device_id=peer, device_id_type=MESH)

    @pl.when(first)
    def _():
        for k, peer in enumerate(_peers_of(x, y, c)):
            for t in range(n):
                copy(t, k, peer, (x, y, c)).start()

    def finish():
        @pl.when(last)
        def _():
            for k, peer in enumerate(_peers_of(x, y, c)):
                for t in range(n):
                    arrived = copy(t, k, peer, peer)
                    arrived.wait_send()
                    arrived.wait_recv()

    return finish


def _travel_scratch(n):
    return [pltpu.SemaphoreType.DMA((7 * n,)), pltpu.SemaphoreType.DMA((7 * n,))]


def _mm(name, a, b, dims, grid, a_spec, b_spec, out_specs, out_shape, acc_shape, k_axis, send=()):
    n_k = 1 if k_axis is None else grid[k_axis]
    n_out = len(out_shape)
    n_send = len(send)

    def body(a_ref, b_ref, *rest):
        src_refs, rest = rest[:n_send], rest[n_send:]
        outs, land_refs, acc_ref = rest[:n_out], rest[n_out:n_out + n_send], rest[n_out + n_send]
        if n_send:
            finish = _travel(grid, src_refs, land_refs, *rest[n_out + n_send + 1:], whole=False)

        part = lax.dot_general(a_ref[...].astype(BF16), b_ref[...].astype(BF16), dims, preferred_element_type=F32)
        if n_k == 1:
            for o in outs:
                o[...] = part.astype(o.dtype)
        else:
            k = pl.program_id(k_axis)

            @pl.when(k == 0)
            def _():
                acc_ref[...] = part

            @pl.when(k > 0)
            def _():
                acc_ref[...] += part

            @pl.when(k == n_k - 1)
            def _():
                for o in outs:
                    o[...] = acc_ref[...].astype(o.dtype)

        if n_send:
            finish()

    sem = tuple("arbitrary" for _ in grid) if n_send else tuple(
        "arbitrary" if ax == k_axis else "parallel" for ax in range(len(grid)))
    scratch = [pltpu.VMEM(acc_shape, F32)] + (_travel_scratch(n_send) if n_send else [])
    return pl.pallas_call(
        body, name=name, grid=grid, in_specs=[a_spec, b_spec] + [_HBM] * n_send,
        out_specs=list(out_specs) + [_HBM] * n_send,
        out_shape=list(out_shape) + [jax.ShapeDtypeStruct(s.shape, s.dtype) for s in send],
        scratch_shapes=scratch, compiler_params=_params(*sem),
    )(a, b, *send)


def _mm_nn(name, a, b, out_dtype=F32):
    m, k = a.shape
    n = b.shape[1]
    return _mm(name, a, b, _NN, (m // ROW_TILE,), pl.BlockSpec((ROW_TILE, k), lambda i: (i, 0)),
               pl.BlockSpec((k, n), lambda i: (0, 0)), [pl.BlockSpec((ROW_TILE, n), lambda i: (i, 0))],
               [jax.ShapeDtypeStruct((m, n), out_dtype)], (8, 128), None)[0]


def _mm_nt(name, a, b, out_dtype=F32):
    m, k = a.shape
    n = b.shape[0]
    return _mm(name, a, b, _NT, (m // ROW_TILE,), pl.BlockSpec((ROW_TILE, k), lambda i: (i, 0)),
               pl.BlockSpec((n, k), lambda i: (0, 0)), [pl.BlockSpec((ROW_TILE, n), lambda i: (i, 0))],
               [jax.ShapeDtypeStruct((m, n), out_dtype)], (8, 128), None)[0]


def _mm_tn(name, a, b, out_dtype):
    m, k = a.shape
    n = b.shape[1]
    return _mm(name, a, b, _TN, (m // ROW_TILE,), pl.BlockSpec((ROW_TILE, k), lambda i: (i, 0)),
               pl.BlockSpec((ROW_TILE, n), lambda i: (i, 0)), [pl.BlockSpec((k, n), lambda i: (0, 0))],
               [jax.ShapeDtypeStruct((k, n), out_dtype)], (k, n), 0)[0]


def _mm_in_bwd_x(dhb, w_g, send):
    m = dhb.shape[0]
    _, d, ns = w_g.shape
    return _mm("mm_in_bwd_x", dhb, w_g, _NT, (m // IN_TILE, N_DEV), pl.BlockSpec((IN_TILE, ns), lambda i, j: (i, j)),
               pl.BlockSpec((None, d, ns), lambda i, j: (j, 0, 0)), [pl.BlockSpec((IN_TILE, d), lambda i, j: (i, 0))],
               [jax.ShapeDtypeStruct((m, d), F32)], (IN_TILE, d), 1, send=send)


def _mm_in_bwd_w(h0b, dhb, half, send):
    m, d = h0b.shape
    ns = dhb.shape[1] // N_DEV
    dh = d // 2
    return _mm(f"mm_in_bwd_w{half}", h0b, dhb, _TN, (N_DEV, m // IN_TILE),
               pl.BlockSpec((IN_TILE, dh), lambda j, k: (k, half)),
               pl.BlockSpec((IN_TILE, ns), lambda j, k: (k, j)), [pl.BlockSpec((None, dh, ns), lambda j, k: (j, 0, 0))],
               [jax.ShapeDtypeStruct((N_DEV, dh, ns), BF16)], (dh, ns), 1, send=send)


LOG2_E = 1.4426950408889634
DEAD_AFTER = 110.0
NEVER_REACHED = 1e30


def _twice(u):
    return jnp.concatenate([u, u], axis=0)


def _dot2(x, u2):
    hi = x.astype(BF16)
    lo = (x - hi.astype(F32)).astype(BF16)
    return lax.dot_general(jnp.concatenate([hi, lo], axis=1), u2, _NN, preferred_element_type=F32)


def _softplus(l):
    e = jnp.exp2(jnp.abs(l) * (-LOG2_E))
    e1 = 1.0 + e
    return jnp.maximum(l, 0.0) + jnp.log(e1), e, e1


def _attn_fwd(hb, h, nb, seq, gather):
    m = nb * seq
    d = SB_HEADS * HEAD_DIM
    tq, tk = ATT_TQ, ATT_TK
    n_sub = tq // tk
    nq = seq // tq
    sec = d // 128
    scale = 1.0 / math.sqrt(HEAD_DIM)
    heads = [slice(hh * HEAD_DIM, (hh + 1) * HEAD_DIM) for hh in range(2)]
    grid = (nb, sec, nq)
    n_g = len(gather)

    def body(q_ref, k_ref, v_ref, z_ref, *rest):
        src_refs, rest = rest[:n_g], rest[n_g:]
        o_ref, a_ref, c_ref = rest[:3]
        land_refs, (carry_ref, acc_ref, send_sems, recv_sems) = rest[3:3 + n_g], rest[3 + n_g:]
        finish = _travel(grid, src_refs, land_refs, send_sems, recv_sems, whole=True)
        qi = pl.program_id(2)
        n_full = n_sub * qi
        u_after = _twice((lax.broadcasted_iota(jnp.int32, (tk, tk), 0)
                          > lax.broadcasted_iota(jnp.int32, (tk, tk), 1)).astype(BF16))
        qs = [(q_ref[:, cols].astype(F32) * scale).astype(BF16) for cols in heads]
        carry_ref[...] = jnp.zeros_like(carry_ref)
        acc_ref[...] = jnp.zeros_like(acc_ref)
        lane_t = lax.broadcasted_iota(jnp.int32, (tq, 128), 1)
        c_ref[...] = jnp.where(lane_t % HEAD_DIM < n_full + n_sub, NEVER_REACHED, 0.0)

        def block(kb, r0, n, masked):
            qrows = slice(r0, r0 + n)
            krows = pl.ds(pl.multiple_of(kb * tk, tk), tk)
            lane = lax.broadcasted_iota(jnp.int32, (n, 128), 1)
            if masked:
                causal = lax.broadcasted_iota(jnp.int32, (n, tk), 1) < lax.broadcasted_iota(jnp.int32, (n, tk), 0)
            for hh, cols in enumerate(heads):
                carry = carry_ref[hh, qrows, :]
                l = lax.dot_general(qs[hh][qrows], k_ref[krows, cols], _NT, preferred_element_type=F32)
                sp, _, _ = _softplus(l)
                if masked:
                    sp = jnp.where(causal, sp, 0.0)
                after = carry + _dot2(sp, u_after)
                a = jnp.exp(l - (sp + after))
                if masked:
                    a = jnp.where(causal, a, 0.0)
                acc_ref[qrows, cols] += lax.dot_general(a.astype(BF16), v_ref[krows, cols], _NN,
                                                        preferred_element_type=F32)
                c_ref[qrows, :] = jnp.where(lane == hh * HEAD_DIM + kb, carry, c_ref[qrows, :])
                carry_ref[hh, qrows, :] = after[:, 0:1] + sp[:, 0:1]

        def alive(r0, n):
            return jnp.min(carry_ref[:, r0:r0 + n, :]) <= DEAD_AFTER

        for j in reversed(range(n_sub)):
            r0 = j * tk
            near = min(ATT_NEAR, tq - r0)
            block(n_full + j, r0, near, True)
            if r0 + near < tq:
                @pl.when(alive(r0 + near, tq - r0 - near))
                def _():
                    block(n_full + j, r0 + near, tq - r0 - near, False)

        def step(state):
            i, _ = state
            for r0 in range(0, tq, ATT_NEAR):
                @pl.when(alive(r0, ATT_NEAR))
                def _():
                    block(n_full - 1 - i, r0, ATT_NEAR, False)
            return i + 1, alive(0, tq)

        lax.while_loop(lambda state: (state[0] < n_full) & state[1], step, (0, alive(0, tq)))
        o = acc_ref[...]
        o_ref[...] = o
        silu, _ = _silu_and_grad(z_ref[...])
        a_ref[...] = (o * silu).astype(BF16)
        finish()

    qspec = pl.BlockSpec((tq, 128), lambda b, hp, qi: (b * nq + qi, hp))
    return pl.pallas_call(
        body, name="attn_fwd", grid=grid,
        in_specs=[qspec,
                  pl.BlockSpec((seq, 128), lambda b, hp, qi: (b, sec + hp)),
                  pl.BlockSpec((seq, 128), lambda b, hp, qi: (b, 2 * sec + hp)),
                  pl.BlockSpec((tq, 128), lambda b, hp, qi: (b * nq + qi, 3 * sec + hp))] + [_HBM] * n_g,
        out_specs=[qspec, qspec, pl.BlockSpec((None, tq, 128), lambda b, hp, qi: (hp, b * nq + qi, 0))]
        + [_HBM] * n_g,
        out_shape=[jax.ShapeDtypeStruct((m, d), F32), jax.ShapeDtypeStruct((m, d), BF16),
                   jax.ShapeDtypeStruct((sec, m, 128), F32)]
        + [jax.ShapeDtypeStruct((N_DEV,) + g.shape, g.dtype) for g in gather],
        scratch_shapes=[pltpu.VMEM((2, tq, 1), F32), pltpu.VMEM((tq, 128), F32)] + _travel_scratch(n_g),
        compiler_params=_params("arbitrary", "arbitrary", "arbitrary"),
    )(hb, hb, hb, h, *gather)


def _attn_bwd(hb, h, o, carries, da, nb, seq, send):
    m = nb * seq
    d = SB_HEADS * HEAD_DIM
    tq, tk = ATT_TQ, ATT_TK
    n_sub = tq // tk
    nq = seq // tq
    sec = d // 128
    scale = 1.0 / math.sqrt(HEAD_DIM)
    heads = [slice(hh * HEAD_DIM, (hh + 1) * HEAD_DIM) for hh in range(2)]
    grid = (nb, sec, nq)
    n_send = len(send)

    def body(q_ref, k_ref, v_ref, z_ref, o_ref, c_ref, da_ref, *rest):
        src_refs, rest = rest[:n_send], rest[n_send:]
        dq_ref, dk_ref, dv_ref, dz_ref = rest[:4]
        land_refs, rest = rest[4:4 + n_send], rest[4 + n_send:]
        dk_acc, dv_acc, dq_acc, gcarry_ref, send_sems, recv_sems = rest
        finish = _travel(grid, src_refs, land_refs, send_sems, recv_sems, whole=False)
        qi = pl.program_id(2)

        @pl.when(qi == 0)
        def _():
            dk_acc[...] = jnp.zeros_like(dk_acc)
            dv_acc[...] = jnp.zeros_like(dv_acc)

        row = lax.broadcasted_iota(jnp.int32, (tk, tk), 0)
        col = lax.broadcasted_iota(jnp.int32, (tk, tk), 1)
        u_after = _twice((row > col).astype(BF16))
        u_before = (row < col).astype(BF16)
        silu, dsilu = _silu_and_grad(z_ref[...])
        dav = da_ref[...]
        dz_ref[...] = (dav * o_ref[...] * dsilu).astype(BF16)
        do2 = (dav * silu).astype(BF16)
        qs = [(q_ref[:, cols].astype(F32) * scale).astype(BF16) for cols in heads]
        dos = [do2[:, cols] for cols in heads]
        dq_acc[...] = jnp.zeros_like(dq_acc)
        gcarry_ref[...] = jnp.zeros_like(gcarry_ref)

        def block(kb, r0, n, masked):
            qrows = slice(r0, r0 + n)
            krows = pl.ds(pl.multiple_of(kb * tk, tk), tk)
            lane = lax.broadcasted_iota(jnp.int32, (n, 128), 1)
            if masked:
                causal = lax.broadcasted_iota(jnp.int32, (n, tk), 1) < lax.broadcasted_iota(jnp.int32, (n, tk), 0)
            for hh, cols in enumerate(heads):
                q, do = qs[hh][qrows], dos[hh][qrows]
                ks = k_ref[krows, cols]
                l = lax.dot_general(q, ks, _NT, preferred_element_type=F32)
                sp, e, e1 = _softplus(l)
                if masked:
                    sp = jnp.where(causal, sp, 0.0)
                carry = jnp.sum(jnp.where(lane == hh * HEAD_DIM + kb, c_ref[qrows, :], 0.0), axis=1, keepdims=True)
                after = carry + _dot2(sp, u_after)
                a = jnp.exp(l - (sp + after))
                if masked:
                    a = jnp.where(causal, a, 0.0)
                dv_acc[krows, cols] += lax.dot_general(a.astype(BF16), do, _TN, preferred_element_type=F32)
                g = a * lax.dot_general(do, v_ref[krows, cols], _NT, preferred_element_type=F32)
                c = gcarry_ref[hh, qrows, :] + lax.dot_general(g.astype(BF16), u_before, _NN,
                                                                preferred_element_type=F32)
                r = pl.reciprocal(e1, approx=True)
                beta = jnp.where(l >= 0.0, r, e * r)
                dl = g - (g + c) * beta
                if masked:
                    dl = jnp.where(causal, dl, 0.0)
                dl_b = dl.astype(BF16)
                dq_acc[qrows, cols] += lax.dot_general(dl_b, ks, _NN, preferred_element_type=F32)
                dk_acc[krows, cols] += lax.dot_general(dl_b, q, _TN, preferred_element_type=F32)
                gcarry_ref[hh, qrows, :] = c[:, tk - 1:tk] + g[:, tk - 1:tk]

        def alive(kb, r0, n):
            lane = lax.broadcasted_iota(jnp.int32, (n, 128), 1)
            mine = (lane == kb) | (lane == HEAD_DIM + kb)
            return jnp.min(jnp.where(mine, c_ref[r0:r0 + n, :], jnp.inf)) <= DEAD_AFTER

        def step(kb, _):
            for r0 in range(0, tq, ATT_NEAR):
                @pl.when(alive(kb, r0, ATT_NEAR))
                def _():
                    block(kb, r0, ATT_NEAR, False)

            return 0

        lax.fori_loop(0, n_sub * qi, step, 0)
        for j in range(n_sub):
            kb, r0 = n_sub * qi + j, j * tk
            near = min(ATT_NEAR, tq - r0)
            if r0 + near < tq:
                @pl.when(alive(kb, r0 + near, tq - r0 - near))
                def _():
                    block(kb, r0 + near, tq - r0 - near, False)
            block(kb, r0, near, True)
        dq_ref[...] = (dq_acc[...] * scale).astype(BF16)

        @pl.when(qi == nq - 1)
        def _():
            dk_ref[...] = dk_acc[...].astype(BF16)
            dv_ref[...] = dv_acc[...].astype(BF16)

        finish()

    qspec = pl.BlockSpec((tq, 128), lambda b, hp, qi: (b * nq + qi, hp))
    kvspec = pl.BlockSpec((seq, 128), lambda b, hp, qi: (b, hp))
    return pl.pallas_call(
        body, name="attn_bwd", grid=grid,
        in_specs=[qspec,
                  pl.BlockSpec((seq, 128), lambda b, hp, qi: (b, sec + hp)),
                  pl.BlockSpec((seq, 128), lambda b, hp, qi: (b, 2 * sec + hp)),
                  pl.BlockSpec((tq, 128), lambda b, hp, qi: (b * nq + qi, 3 * sec + hp)),
                  qspec,
                  pl.BlockSpec((None, tq, 128), lambda b, hp, qi: (hp, b * nq + qi, 0)),
                  qspec] + [_HBM] * n_send,
        out_specs=[qspec, kvspec, kvspec, qspec] + [_HBM] * n_send,
        out_shape=[jax.ShapeDtypeStruct((m, d), BF16)] * 4 + [jax.ShapeDtypeStruct(s.shape, s.dtype) for s in send],
        scratch_shapes=[pltpu.VMEM((seq, 128), F32), pltpu.VMEM((seq, 128), F32), pltpu.VMEM((tq, 128), F32),
                        pltpu.VMEM((2, tq, 1), F32)] + _travel_scratch(n_send),
        compiler_params=_params("arbitrary", "arbitrary", "arbitrary"),
    )(hb, hb, hb, h, o, carries, da, *send)


SUBLANES = 8


def _by_residue(taps):
    groups = []
    for res in range(SUBLANES):
        mine = [(off - res, k) for off, k in taps if off % SUBLANES == res]
        if mine:
            groups.append((res, max(a for a, _ in mine), mine))
    return groups


def _staged(src_ref, stage_ref, res, n):
    buf = stage_ref.at[res % 2]
    if res == 0:
        return src_ref
    buf[0:n, :] = src_ref[pl.ds(res, n), :]
    return buf


def _tap_sum(src_ref, stage_ref, w_ref, n_rows, taps, acc):
    for res, reach, mine in _by_residue(taps):
        shifted = _staged(src_ref, stage_ref, res, n_rows + reach)
        for a, k in mine:
            acc = acc + w_ref[k:k + 1, :] * shifted[a:a + n_rows, :]
    return acc


_CONV_TAPS = [(CV_HALO - CV_KERNEL + 1 + k, k) for k in range(CV_KERNEL)]
_CONV_TAPS_T = [(CV_KERNEL - 1 - k, k) for k in range(CV_KERNEL)]


def _conv_rows(pad_ref, stage_ref, w_ref, b_ref, n_rows):
    bias = jnp.broadcast_to(b_ref[...], (n_rows, b_ref.shape[1]))
    return _tap_sum(pad_ref, stage_ref, w_ref, n_rows, _CONV_TAPS, bias)


def _conv_fwd(h, conv_w, conv_b, ln_g, ln_b, nb, seq):
    m = nb * seq
    d = conv_b.shape[1]
    t = CONV_TILE
    tiles = seq // t
    hpt = t // CV_HALO

    def body(cv_ref, cg_ref, pv_ref, pg_ref, z_ref, w_ref, b_ref, g_ref, bb_ref, a_ref, pad_ref, stage_ref):
        first = pl.program_id(0) % tiles == 0
        pad_ref[0:CV_HALO, :] = jnp.where(first, 0.0, pv_ref[...] * _sigmoid(pg_ref[...]))
        pad_ref[CV_HALO:, :] = cv_ref[...] * _sigmoid(cg_ref[...])
        xhat, _ = _ln_stats(_conv_rows(pad_ref, stage_ref, w_ref, b_ref, t))
        s, _ = _silu_and_grad(xhat * g_ref[...] + bb_ref[...])
        sz, _ = _silu_and_grad(z_ref[...])
        a_ref[...] = (s * sz).astype(BF16)

    def main(c):
        return pl.BlockSpec((t, d), lambda i: (i, c))

    def prev(c):
        return pl.BlockSpec((CV_HALO, d), lambda i: (jnp.maximum(i * hpt - 1, 0), c))

    vec = pl.BlockSpec((1, d), lambda i: (0, 0))
    return pl.pallas_call(
        body, name="conv_fwd", grid=(m // t,),
        in_specs=[main(4), main(5), prev(4), prev(5), main(6),
                  pl.BlockSpec((CV_HALO, d), lambda i: (0, 0)), vec, vec, vec],
        out_specs=pl.BlockSpec((t, d), lambda i: (i, 0)),
        out_shape=jax.ShapeDtypeStruct((m, d), BF16),
        scratch_shapes=[pltpu.VMEM((CV_HALO + t, d), F32), pltpu.VMEM((2, CV_HALO + t, d), F32)],
        compiler_params=_params("parallel"),
    )(h, h, h, h, h, conv_w, conv_b, ln_g, ln_b)


def _conv_bwd(h, da, conv_w, conv_b, ln_g, ln_b, nb, seq):
    m = nb * seq
    d = conv_b.shape[1]
    t = CONV_TILE
    tiles = seq // t
    hpt = t // CV_HALO
    last_halo = m // CV_HALO - 1

    def body(cv_ref, cg_ref, pv_ref, pg_ref, nv_ref, ng_ref, z_ref, nz_ref, da_ref, nda_ref,
             w_ref, b_ref, g_ref, bb_ref, dh_ref, dw_ref, db_ref, dg_ref, dbb_ref, pad_ref, dc_ref, stage_ref):
        i = pl.program_id(0)

        @pl.when(i == 0)
        def _():
            dw_ref[...] = jnp.zeros_like(dw_ref)
            db_ref[...] = jnp.zeros_like(db_ref)
            dg_ref[...] = jnp.zeros_like(dg_ref)
            dbb_ref[...] = jnp.zeros_like(dbb_ref)

        first = i % tiles == 0
        last = i % tiles == tiles - 1
        cv = cv_ref[...]
        sg = _sigmoid(cg_ref[...])
        pad_ref[0:CV_HALO, :] = jnp.where(first, 0.0, pv_ref[...] * _sigmoid(pg_ref[...]))
        pad_ref[CV_HALO:CV_HALO + t, :] = cv * sg
        pad_ref[CV_HALO + t:, :] = nv_ref[...] * _sigmoid(ng_ref[...])

        def rows_bwd(conv_out, z, dav):
            xhat, rstd = _ln_stats(conv_out)
            s, ds = _silu_and_grad(xhat * g_ref[...] + bb_ref[...])
            sz, dsz = _silu_and_grad(z)
            dn = dav * sz * ds
            return _ln_bwd(dn * g_ref[...], xhat, rstd), dav * s * dsz, dn, xhat

        conv_all = _conv_rows(pad_ref, stage_ref, w_ref, b_ref, t + CV_HALO)
        dc, dz, dn, xhat = rows_bwd(conv_all[:t], z_ref[...], da_ref[...])
        dc_next, _, _, _ = rows_bwd(conv_all[t:], nz_ref[...], nda_ref[...])
        dc_ref[0:t, :] = dc
        dc_ref[t:, :] = jnp.where(last, 0.0, dc_next)
        dg_ref[...] += jnp.sum(dn * xhat, axis=0, keepdims=True)
        dbb_ref[...] += jnp.sum(dn, axis=0, keepdims=True)
        db_ref[...] += jnp.sum(dc, axis=0, keepdims=True)
        for res, reach, mine in _by_residue(_CONV_TAPS):
            shifted = _staged(pad_ref, stage_ref, res, t + reach)
            for a, k in mine:
                dw_ref[k:k + 1, :] += jnp.sum(dc * shifted[a:a + t, :], axis=0, keepdims=True)
        du = _tap_sum(dc_ref, stage_ref, w_ref, t, _CONV_TAPS_T, jnp.zeros((t, d), F32))
        dh_ref[:, 0:d] = (du * sg).astype(BF16)
        dh_ref[:, d:2 * d] = (du * cv * sg * (1.0 - sg)).astype(BF16)
        dh_ref[:, 2 * d:] = dz.astype(BF16)

    def main(c):
        return pl.BlockSpec((t, d), lambda i: (i, c))

    def prev(c):
        return pl.BlockSpec((CV_HALO, d), lambda i: (jnp.maximum(i * hpt - 1, 0), c))

    def nxt(c):
        return pl.BlockSpec((CV_HALO, d), lambda i: (jnp.minimum((i + 1) * hpt, last_halo), c))

    vec = pl.BlockSpec((1, d), lambda i: (0, 0))
    taps = pl.BlockSpec((CV_HALO, d), lambda i: (0, 0))
    return pl.pallas_call(
        body, name="conv_bwd", grid=(m // t,),
        in_specs=[main(4), main(5), prev(4), prev(5), nxt(4), nxt(5), main(6), nxt(6), main(0), nxt(0),
                  taps, vec, vec, vec],
        out_specs=[pl.BlockSpec((t, 3 * d), lambda i: (i, 0)), taps, vec, vec, vec],
        out_shape=[jax.ShapeDtypeStruct((m, 3 * d), BF16), jax.ShapeDtypeStruct((CV_HALO, d), F32),
                   jax.ShapeDtypeStruct((1, d), F32), jax.ShapeDtypeStruct((1, d), F32),
                   jax.ShapeDtypeStruct((1, d), F32)],
        scratch_shapes=[pltpu.VMEM((t + 2 * CV_HALO, d), F32), pltpu.VMEM((t + CV_HALO, d), F32),
                        pltpu.VMEM((2, t + 2 * CV_HALO, d), F32)],
        compiler_params=_params("arbitrary"),
    )(h, h, h, h, h, h, h, h, da, da, conv_w, conv_b, ln_g, ln_b)


_HBM = pl.BlockSpec(memory_space=pltpu.HBM)


def _place():
    return lax.axis_index("x"), lax.axis_index("y"), lax.axis_index("c")


def _slot(p):
    return 4 * p[0] + 2 * p[1] + p[2]


def _gather_mm_in_fwd(h0b, w_shard, riders):
    m, d = h0b.shape
    ns = w_shard.shape[1]
    n_i = m // IN_TILE
    n_steps = N_DEV
    n_arr = 1 + len(riders)

    def places():
        x, y, c = _place()
        chips = [(1 - x, y), (x, 1 - y), (1 - x, 1 - y)]
        return (x, y, c), (x, y, 1 - c), chips

    me, sibling, chips = places()
    order = [me, sibling] + [(*chip, me[2]) for chip in chips] + [(*chip, sibling[2]) for chip in chips]
    order = jnp.stack([_slot(p) for p in order]).astype(jnp.int32)

    def body(order_ref, h_ref, *rest):
        src_refs, rest = rest[:n_arr], rest[n_arr:]
        o32_ref, o16_ref = rest[:2]
        got_refs, rest = rest[2:2 + n_arr], rest[2 + n_arr:]
        wbuf, send_sems, recv_sems, own_sems, load_sems = rest
        s, i = pl.program_id(0), pl.program_id(1)
        me, sibling, chips = places()
        c = me[2]
        wg_ref = got_refs[0]

        def remote(t, k, block, to, own_src=False):
            dst = got_refs[t].at[_slot(block)]
            return pltpu.make_async_remote_copy(
                src_ref=src_refs[t] if own_src else dst, dst_ref=dst, send_sem=send_sems.at[7 * t + k],
                recv_sem=recv_sems.at[7 * t + k], device_id=to, device_id_type=MESH)

        def load(step):
            return pltpu.make_async_copy(wg_ref.at[order_ref[step]], wbuf.at[step % 2], load_sems.at[step % 2])

        own = [pltpu.make_async_copy(src_refs[t], got_refs[t].at[_slot(me)], own_sems.at[t]) for t in range(n_arr)]

        @pl.when((s == 0) & (i == 0))
        def _():
            for t in range(n_arr):
                own[t].start()
                remote(t, 0, me, sibling, own_src=True).start()
                for j, chip in enumerate(chips):
                    remote(t, 1 + j, me, (*chip, c), own_src=True).start()
            first = pltpu.make_async_copy(src_refs[0], wbuf.at[0], load_sems.at[0])
            first.start()
            first.wait()

        for nxt in range(1, n_steps):
            @pl.when((s == nxt - 1) & (i == n_i - 1))
            def _():
                if nxt == 1:
                    remote(0, 0, sibling, me).wait_recv()
                elif nxt <= 4:
                    j = nxt - 2
                    remote(0, 1 + j, (*chips[j], c), me).wait_recv()
                    remote(0, 4 + j, (*chips[j], c), sibling).start()
                else:
                    j = nxt - 5
                    remote(0, 4 + j, (*chips[j], 1 - c), me).wait_recv()
                if 3 <= nxt <= 5:
                    j = nxt - 3
                    for t in range(1, n_arr):
                        remote(t, 1 + j, (*chips[j], c), me).wait_recv()
                        remote(t, 4 + j, (*chips[j], c), sibling).start()
                load(nxt).start()

        for step in range(1, n_steps):
            @pl.when((s == step) & (i == 0))
            def _():
                load(step).wait()

        part = lax.dot_general(h_ref[...], wbuf[s % 2], _NN, preferred_element_type=F32)
        o32_ref[...] = part
        o16_ref[...] = part.astype(BF16)

        @pl.when((s == n_steps - 1) & (i == n_i - 1))
        def _():
            for t in range(n_arr):
                if t:
                    remote(t, 0, sibling, me).wait_recv()
                    for j, chip in enumerate(chips):
                        remote(t, 4 + j, (*chip, 1 - c), me).wait_recv()
                remote(t, 0, me, sibling, own_src=True).wait_send()
                for j, chip in enumerate(chips):
                    remote(t, 1 + j, me, (*chip, c), own_src=True).wait_send()
                    remote(t, 4 + j, (*chip, c), sibling).wait_send()
                own[t].wait()

    arrays = [w_shard] + list(riders)
    out = pl.BlockSpec((IN_TILE, ns), lambda s, i, order: (i, order[s]))
    grid_spec = pltpu.PrefetchScalarGridSpec(
        num_scalar_prefetch=1, grid=(n_steps, n_i),
        in_specs=[pl.BlockSpec((IN_TILE, d), lambda s, i, order: (i, 0))] + [_HBM] * n_arr,
        out_specs=[out, out] + [_HBM] * n_arr,
        scratch_shapes=[pltpu.VMEM((2, d, ns), BF16), pltpu.SemaphoreType.DMA((7 * n_arr,)),
                        pltpu.SemaphoreType.DMA((7 * n_arr,)), pltpu.SemaphoreType.DMA((n_arr,)),
                        pltpu.SemaphoreType.DMA((2,))])
    return pl.pallas_call(
        body, name="gather_mm_in_fwd", grid_spec=grid_spec,
        out_shape=[jax.ShapeDtypeStruct((m, N_DEV * ns), F32), jax.ShapeDtypeStruct((m, N_DEV * ns), BF16)]
        + [jax.ShapeDtypeStruct((N_DEV,) + a.shape, a.dtype) for a in arrays],
        compiler_params=_params("arbitrary", "arbitrary"),
    )(order, h0b, *arrays)


def _exchange_partials(parts, whole):
    n_parts = len(parts)
    arrays = list(parts) + list(whole)
    n = len(arrays)

    def body(*refs):
        ins, outs = refs[:n], refs[n:2 * n]
        send_sems, recv_sems, local_sems = refs[2 * n:]
        x, y, c = _place()
        me = (x, y, c)

        def src_for(a, p):
            return ins[a].at[_slot(p)] if a < n_parts else ins[a]

        def copy(a, k, peer):
            return pltpu.make_async_remote_copy(
                src_ref=src_for(a, peer), dst_ref=outs[a].at[_slot(me)], send_sem=send_sems.at[7 * a + k],
                recv_sem=recv_sems.at[7 * a + k], device_id=peer, device_id_type=MESH)

        def landed(a, k, peer):
            return pltpu.make_async_remote_copy(
                src_ref=src_for(a, peer), dst_ref=outs[a].at[_slot(peer)], send_sem=send_sems.at[7 * a + k],
                recv_sem=recv_sems.at[7 * a + k], device_id=peer, device_id_type=MESH)

        peers = []
        for k in range(1, N_DEV):
            fx, fy, fc = (k >> 2) & 1, (k >> 1) & 1, k & 1
            peers.append((1 - x if fx else x, 1 - y if fy else y, 1 - c if fc else c))
        mine = [pltpu.make_async_copy(src_for(a, me), outs[a].at[_slot(me)], local_sems.at[a]) for a in range(n)]
        for cp in mine:
            cp.start()
        sent = [copy(a, k, peer) for a in range(n) for k, peer in enumerate(peers)]
        for cp in sent:
            cp.start()
        for a in range(n):
            for k, peer in enumerate(peers):
                landed(a, k, peer).wait_recv()
        for cp in sent:
            cp.wait_send()
        for cp in mine:
            cp.wait()

    out_shape = [jax.ShapeDtypeStruct(p.shape, p.dtype) for p in parts]
    out_shape += [jax.ShapeDtypeStruct((N_DEV,) + w.shape, w.dtype) for w in whole]
    return pl.pallas_call(
        body, name="exchange_grad_partials",
        in_specs=[_HBM] * n, out_specs=[_HBM] * n, out_shape=out_shape,
        scratch_shapes=[pltpu.SemaphoreType.DMA((7 * n,)), pltpu.SemaphoreType.DMA((7 * n,)),
                        pltpu.SemaphoreType.DMA((n,))],
    )(*arrays)


def _peers_of(x, y, c):
    peers = []
    for k in range(1, N_DEV):
        fx, fy, fc = (k >> 2) & 1, (k >> 1) & 1, k & 1
        peers.append((1 - x if fx else x, 1 - y if fy else y, 1 - c if fc else c))
    return peers


def _with_own(landed, own, me):
    return lax.dynamic_update_index_in_dim(landed, own, me, 0)


def _adamw(name, slabs, w, mom, var, rows):
    r, c = w.shape
    n_slabs = len(slabs)
    per_slab = r // n_slabs // rows

    def body(*refs):
        p_refs = refs[:n_slabs]
        w_ref, m_ref, v_ref, g_ref, d_ref, nm_ref, nv_ref = refs[n_slabs:]
        slab = pl.program_id(0) // per_slab
        g = None
        for p in range(N_DEV):
            part = p_refs[0][p]
            for k in range(1, n_slabs):
                part = jnp.where(slab == k, p_refs[k][p], part)
            g = part.astype(F32) if g is None else g + part.astype(F32)
        m_new = ADAM_B1 * m_ref[...] + (1.0 - ADAM_B1) * g
        v_new = ADAM_B2 * v_ref[...] + (1.0 - ADAM_B2) * (g * g)
        m_hat = m_new / (1.0 - ADAM_B1 ** ADAM_STEP)
        v_hat = v_new / (1.0 - ADAM_B2 ** ADAM_STEP)
        g_ref[...] = g
        d_ref[...] = -ADAM_LR * (m_hat / (jnp.sqrt(v_hat) + ADAM_EPS) + ADAM_WD * w_ref[...])
        nm_ref[...] = m_new
        nv_ref[...] = v_new

    def slab_spec(k):
        return pl.BlockSpec((N_DEV, rows, c), lambda i: (0, jnp.clip(i - k * per_slab, 0, per_slab - 1), 0))

    blk = pl.BlockSpec((rows, c), lambda i: (i, 0))
    return pl.pallas_call(
        body, name=name, grid=(r // rows,),
        in_specs=[slab_spec(k) for k in range(n_slabs)] + [blk, blk, blk],
        out_specs=[blk] * 4, out_shape=[jax.ShapeDtypeStruct((r, c), F32)] * 4,
        compiler_params=_params("parallel"),
    )(*slabs, w, mom, var)


def kernel(x, ln_in_g, ln_in_b, w_in, w_sb_proj, conv_w, conv_b, conv_ln_g, conv_ln_b, w_cv_proj, w_out, ln_post_g, ln_post_b, loss_target, m_ln_in_g, m_ln_in_b, m_w_in, m_w_sb_proj, m_conv_w, m_conv_b, m_conv_ln_g, m_conv_ln_b, m_w_cv_proj, m_w_out, m_ln_post_g, m_ln_post_b, v_ln_in_g, v_ln_in_b, v_w_in, v_w_sb_proj, v_conv_w, v_conv_b, v_conv_ln_g, v_conv_ln_b, v_w_cv_proj, v_w_out, v_ln_post_g, v_ln_post_b):
    nb, seq, d = x.shape
    m = nb * seq
    x2d = x.reshape(m, d)
    target = loss_target.reshape(m, d)
    rs = d // N_DEV
    pad_taps = ((0, CV_HALO - CV_KERNEL), (0, 0))

    proj_shards = jnp.stack([w_sb_proj[0], w_cv_proj[0], w_out[0]]).astype(BF16)
    conv_w_shard = jnp.pad(conv_w[0], pad_taps)
    me = _slot(_place())
    own = lambda parts: lax.dynamic_index_in_dim(parts, me, 0, keepdims=False)

    h0, h0b = _ln_in_fwd(x2d, ln_in_g.reshape(1, d), ln_in_b.reshape(1, d))
    h, hb, w_in_g = _gather_mm_in_fwd(h0b, w_in[0].astype(BF16), [])
    o, a_sb, carries, proj_g, conv_w_g = _attn_fwd(hb, h, nb, seq, [proj_shards, conv_w_shard])
    proj_g = _with_own(proj_g, proj_shards, me)
    conv_w_g = _with_own(conv_w_g, conv_w_shard, me)
    w_sb_g = proj_g[:, 0].reshape(d, d)
    w_cv_g = proj_g[:, 1].reshape(d, d)
    w_out_g = proj_g[:, 2].reshape(d, d)
    conv_w_full = conv_w_g.transpose(1, 0, 2).reshape(CV_HALO, d)
    a_cv = _conv_fwd(h, conv_w_full, conv_b, conv_ln_g, conv_ln_b, nb, seq)
    y_sb = _mm_nn("mm_sb_fwd", a_sb, w_sb_g)
    y_cv = _mm_nn("mm_cv_fwd", a_cv, w_cv_g)
    merged, mo = _merge_out_fwd(h, y_sb, y_cv, w_out_g)
    loss_part, d_pre, d_pre_b, dg_post, db_post = _post_ln_loss(h0, mo, target, ln_post_g, ln_post_b)

    dw_out = _mm_tn("mm_out_bwd_w", merged, d_pre_b, BF16)
    d_merged = _mm_nt("mm_out_bwd_x", d_pre_b, w_out_g)
    dy_sb, dy_cv, d_gates = _merge_bwd(h, y_sb, y_cv, d_merged)
    dw_sb = _mm_tn("mm_sb_bwd_w", a_sb, dy_sb, BF16)
    dw_cv = _mm_tn("mm_cv_bwd_w", a_cv, dy_cv, BF16)
    da_sb = _mm_nt("mm_sb_bwd_x", dy_sb, w_sb_g)
    da_cv = _mm_nt("mm_cv_bwd_x", dy_cv, w_cv_g)
    dproj = jnp.stack([dw_sb.reshape(N_DEV, rs, d), dw_cv.reshape(N_DEV, rs, d), dw_out.reshape(N_DEV, rs, d)], axis=1)
    dq, dk, dv, dz_sb, r_proj = _attn_bwd(hb, h, o, carries, da_sb, nb, seq, [dproj])
    d_conv3, dconv_w, dconv_b, dconv_ln_g, dconv_ln_b = _conv_bwd(
        h, da_cv, conv_w_full, conv_b, conv_ln_g, conv_ln_b, nb, seq)
    dhb = jnp.concatenate([dq, dk, dv, dz_sb, d_conv3, d_gates], axis=1)
    dconv_w_parts = dconv_w.reshape(CV_HALO, N_DEV, d // N_DEV).transpose(1, 0, 2)
    dw_in_a, r_conv = _mm_in_bwd_w(h0b, dhb, 0, [dconv_w_parts])
    dw_in_b, r_in_a = _mm_in_bwd_w(h0b, dhb, 1, [dw_in_a])
    dh_mm, r_in_b = _mm_in_bwd_x(dhb, w_in_g, [dw_in_b])
    dx, dg_in, db_in = _ln_in_bwd(x2d, d_pre, dh_mm, ln_in_g.reshape(1, d))
    r_proj = _with_own(r_proj, own(dproj), me)
    r_conv = _with_own(r_conv, own(dconv_w_parts), me)
    r_in = [_with_own(r_in_a, own(dw_in_a), me), _with_own(r_in_b, own(dw_in_b), me)]

    small = jnp.concatenate([dg_in, db_in, dconv_b, dconv_ln_g, dconv_ln_b, dg_post, db_post,
                             jnp.broadcast_to(loss_part, (1, d))], axis=0)
    (r_small,) = _exchange_partials([], [small])
    loss = jnp.sum(r_small[:, -1, 0])

    g_in, d_in, nm_in, nv_in = _adamw("adamw_w_in", r_in, w_in[0], m_w_in[0], v_w_in[0], 128)
    stack3 = lambda a, b, c: jnp.concatenate([a[0], b[0], c[0]], axis=0)
    g_pr, d_pr, nm_pr, nv_pr = _adamw(
        "adamw_proj", [r_proj.reshape(N_DEV, 3 * rs, d)], stack3(w_sb_proj, w_cv_proj, w_out),
        stack3(m_w_sb_proj, m_w_cv_proj, m_w_out), stack3(v_w_sb_proj, v_w_cv_proj, v_w_out), 3 * rs)
    padc = lambda a: jnp.pad(a[0], pad_taps)
    g_cw, d_cw, nm_cw, nv_cw = _adamw("adamw_conv_w", [r_conv], padc(conv_w), padc(m_conv_w), padc(v_conv_w), CV_HALO)
    vecs = lambda *a: jnp.concatenate([t.reshape(1, d) for t in a] + [jnp.ones((1, d), F32)], axis=0)
    g_sm, d_sm, nm_sm, nv_sm = _adamw(
        "adamw_vectors", [r_small],
        vecs(ln_in_g, ln_in_b, conv_b, conv_ln_g, conv_ln_b, ln_post_g, ln_post_b),
        vecs(m_ln_in_g, m_ln_in_b, m_conv_b, m_conv_ln_g, m_conv_ln_b, m_ln_post_g, m_ln_post_b),
        vecs(v_ln_in_g, v_ln_in_b, v_conv_b, v_conv_ln_g, v_conv_ln_b, v_ln_post_g, v_ln_post_b), 8)


    def leaves(big, pr, cw, sm):
        return (sm[0], sm[1], big[None], pr[None, 0:rs], cw[None, :CV_KERNEL], sm[2:3], sm[3:4], sm[4:5],
                pr[None, rs:2 * rs], pr[None, 2 * rs:], sm[5:6], sm[6:7])

    return (loss, dx.reshape(nb, seq, d), *leaves(g_in, g_pr, g_cw, g_sm), *leaves(d_in, d_pr, d_cw, d_sm),
            *leaves(nm_in, nm_pr, nm_cw, nm_sm), *leaves(nv_in, nv_pr, nv_cw, nv_sm))
```
